```python
import math
import jax, jax.numpy as jnp
from jax import lax
import numpy as np

D_MODEL = 1024
BATCH = 8
SEQ = 4096
DEPTH = 2

PLE_DIM = 256
GDN_HEADS = 8
GDN_HEAD_DIM = 128
GDN_WIDTH = GDN_HEADS * GDN_HEAD_DIM
GDN_CHUNK = 64
CONV_WIDTH = 4
LRU_WIDTH = D_MODEL
LRU_BLOCKS = 8
LRU_BLOCK_DIM = LRU_WIDTH // LRU_BLOCKS
LRU_C = 8.0
SWA_Q_HEADS = 16
SWA_KV_HEADS = 4
SWA_HEAD_DIM = 64
SWA_GROUP = SWA_Q_HEADS // SWA_KV_HEADS
SWA_Q_WIDTH = SWA_Q_HEADS * SWA_HEAD_DIM
SWA_KV_WIDTH = SWA_KV_HEADS * SWA_HEAD_DIM
WINDOW = 128
REL_BUCKETS = 32
REL_MAX_DISTANCE = 128
N_EXPERTS = 32
TOP_K = 4
D_EXPERT = D_MODEL
SWIGLU_LIMIT = 7.0
SWIGLU_ALPHA = 1.702
EXPERT_BLOCK = 128
N_BRANCHES = 3
IN_SPLITS = (GDN_WIDTH, GDN_WIDTH, GDN_WIDTH, GDN_WIDTH, GDN_HEADS, GDN_HEADS,
             LRU_WIDTH, LRU_WIDTH,
             SWA_Q_WIDTH, SWA_KV_WIDTH, SWA_KV_WIDTH,
             N_BRANCHES * D_MODEL)
IN_COLS = sum(IN_SPLITS)
DEEPNORM_ALPHA = (2.0 * DEPTH) ** 0.25
DEEPNORM_BETA = (8.0 * DEPTH) ** -0.25
LN_EPS = 1e-5
NORM_EPS = 1e-6

kernel_name = "hybrid_gdn_rglru_swa_moe_deepnorm"


def layer_norm(x, g, b):
    x32 = x.astype(jnp.float32)
    mu = jnp.mean(x32, axis=-1, keepdims=True)
    var = jnp.mean(jnp.square(x32 - mu), axis=-1, keepdims=True)
    return ((x32 - mu) * lax.rsqrt(var + LN_EPS) * g + b).astype(x.dtype)


def l2norm(t):
    return t * lax.rsqrt(jnp.sum(jnp.square(t), axis=-1, keepdims=True) + NORM_EPS)


def split_cols(t, sizes):
    offsets = []
    acc = 0
    for s in sizes[:-1]:
        acc += s
        offsets.append(acc)
    return jnp.split(t, offsets, axis=-1)


def causal_dwconv(x, w):
    K, S = w.shape[0], x.shape[1]
    xp = jnp.pad(x, ((0, 0), (K - 1, 0), (0, 0)))
    return sum(xp[:, j:j + S] * w[j] for j in range(K))


def gated_delta_rule_chunked(q, k, v, g, beta):
    B, S, H, dk = q.shape
    dv = v.shape[-1]
    C = GDN_CHUNK
    N = S // C

    def chunks(t):
        return jnp.moveaxis(t.reshape((B, N, C, H) + t.shape[3:]), 3, 1)

    q = chunks(q * (dk ** -0.5))
    k, v, g, beta = chunks(k), chunks(v), chunks(g), chunks(beta)
    G = jnp.cumsum(g, axis=-1)
    idx = jnp.arange(C)
    causal = idx[:, None] >= idx[None, :]
    strict = idx[:, None] > idx[None, :]
    decay = jnp.exp(jnp.where(causal, G[..., :, None] - G[..., None, :], -jnp.inf))
    kb = k * beta[..., None]
    L = jnp.where(strict, jnp.einsum('bhnid,bhnjd->bhnij', kb, k) * decay, 0.0)
    eye = jnp.eye(C, dtype=L.dtype)
    T = lax.linalg.triangular_solve(eye + L, jnp.broadcast_to(eye, L.shape), left_side=True, lower=True)
    u = T @ (v * beta[..., None])
    w = T @ (kb * jnp.exp(G)[..., None])
    intra = jnp.einsum('bhnid,bhnjd->bhnij', q, k) * decay
    q_dec = q * jnp.exp(G)[..., None]
    G_last = G[..., -1:]
    k_dec = k * jnp.exp(G_last - G)[..., None]
    chunk_decay = jnp.exp(G_last[..., 0])
    xs = tuple(jnp.moveaxis(t, 2, 0) for t in (w, u, q_dec, k_dec, intra, chunk_decay))

    def step(state, inp):
        w_n, u_n, qd, kd, a_n, cd = inp
        v_new = u_n - jnp.einsum('bhcd,bhde->bhce', w_n, state)
        o = jnp.einsum('bhcd,bhde->bhce', qd, state) + jnp.einsum('bhij,bhje->bhie', a_n, v_new)
        state = state * cd[..., None, None] + jnp.einsum('bhcd,bhce->bhde', kd, v_new)
        return state, o

    _, o = lax.scan(step, jnp.zeros((B, H, dk, dv), q.dtype), xs)
    return jnp.transpose(o, (1, 0, 3, 2, 4)).reshape(B, S, H, dv)


def gdn_branch(q, k, v, z, a, b, conv_w, a_log, dt_bias, norm_w):
    B, S, _ = q.shape
    qkv = jax.nn.silu(causal_dwconv(jnp.concatenate([q, k, v], axis=-1), conv_w))
    q, k, v = jnp.split(qkv, 3, axis=-1)

    def heads(t):
        return t.reshape(B, S, GDN_HEADS, GDN_HEAD_DIM).astype(jnp.float32)

    qh, kh, vh = l2norm(heads(q)), l2norm(heads(k)), heads(v)
    g = -jnp.exp(a_log.astype(jnp.float32)) * jax.nn.softplus(a.astype(jnp.float32) + dt_bias)
    beta = jax.nn.sigmoid(b.astype(jnp.float32))
    o = gated_delta_rule_chunked(qh, kh, vh, g, beta)
    o = o * lax.rsqrt(jnp.mean(jnp.square(o), axis=-1, keepdims=True) + NORM_EPS) * norm_w * jax.nn.silu(heads(z))
    return o.reshape(B, S, GDN_WIDTH).astype(q.dtype)


def lru_combine(c1, c2):
    a1, b1 = c1
    a2, b2 = c2
    return a1 * a2, a2 * b1 + b2


def lru_branch(xb, gate, conv_w, conv_b, w_a, b_a, w_x, b_x, lam):
    B, S, _ = xb.shape
    xc = (causal_dwconv(xb, conv_w) + conv_b).astype(jnp.float32)
    blocks = xc.reshape(B, S, LRU_BLOCKS, LRU_BLOCK_DIM)
    r = jax.nn.sigmoid(jnp.einsum('bshi,hij->bshj', blocks, w_a).reshape(B, S, LRU_WIDTH) + b_a)
    i = jax.nn.sigmoid(jnp.einsum('bshi,hij->bshj', blocks, w_x).reshape(B, S, LRU_WIDTH) + b_x)
    log_a = -LRU_C * r * jax.nn.softplus(-lam.astype(jnp.float32))
    a = jnp.exp(log_a)
    u = jnp.sqrt(-jnp.expm1(2.0 * log_a)) * (i * xc)
    _, h = lax.associative_scan(lru_combine, (a, u), axis=1)
    return (h * jax.nn.gelu(gate.astype(jnp.float32))).astype(xb.dtype)


def t5_bucket(dist):
    max_exact = REL_BUCKETS // 2
    d = jnp.maximum(dist.astype(jnp.float32), 1.0)
    large = max_exact + (jnp.log(d / max_exact) / math.log(REL_MAX_DISTANCE / max_exact)
                         * (REL_BUCKETS - max_exact)).astype(jnp.int32)
    large = jnp.minimum(large, REL_BUCKETS - 1)
    return jnp.where(dist < max_exact, dist, large)


def swa_branch(q, k, v, sinks, rel_bias):
    B, S, _ = q.shape
    NB = S // WINDOW
    qb = q.reshape(B, NB, WINDOW, SWA_KV_HEADS, SWA_GROUP, SWA_HEAD_DIM)

    def band(t):
        t = t.reshape(B, NB, WINDOW, SWA_KV_HEADS, SWA_HEAD_DIM)
        prev = jnp.pad(t, ((0, 0), (1, 0), (0, 0), (0, 0), (0, 0)))[:, :-1]
        return jnp.concatenate([prev, t], axis=2)

    kb, vb = band(k), band(v)
    scores = jnp.einsum('bnqhgd,bnshd->bnhgqs', qb, kb,
                        preferred_element_type=jnp.float32) * (SWA_HEAD_DIM ** -0.5)
    kj = jnp.arange(2 * WINDOW)[None, :]
    dist = (jnp.arange(WINDOW)[:, None] + WINDOW) - kj
    in_window = (dist >= 0) & (dist < WINDOW)
    bias = rel_bias[t5_bucket(jnp.maximum(dist, 0))].astype(jnp.float32)
    bias = jnp.transpose(bias, (2, 0, 1)).reshape(SWA_KV_HEADS, SWA_GROUP, WINDOW, 2 * WINDOW)
    key_exists = (jnp.arange(NB)[:, None] * WINDOW + kj - WINDOW) >= 0
    mask = in_window[None] & key_exists[:, None, :]
    scores = jnp.where(mask[None, :, None, None], scores + bias, -jnp.inf)
    sink = sinks.astype(jnp.float32).reshape(SWA_KV_HEADS, SWA_GROUP)[None, None, :, :, None, None]
    m = jnp.maximum(jnp.max(scores, axis=-1, keepdims=True), sink)
    pexp = jnp.exp(scores - m)
    probs = pexp / (jnp.sum(pexp, axis=-1, keepdims=True) + jnp.exp(sink - m))
    out = jnp.einsum('bnhgqs,bnshd->bnqhgd', probs.astype(v.dtype), vb)
    return out.reshape(B, S, SWA_Q_WIDTH)


def token_mixer(h, w_in, conv_qkv_w, gdn_a_log, gdn_dt_bias, gdn_norm_w,
                rg_conv_w, rg_conv_b, rg_w_a, rg_b_a, rg_w_x, rg_b_x, rg_lambda,
                attn_sinks, rel_bias, w_o_gdn, w_o_lru, w_o_swa, w_out):
    proj = h @ w_in
    (qa, ka, va, za, aa, ba, xl, gl, qc, kc, vc, gate_logits) = split_cols(proj, IN_SPLITS)
    y_a = gdn_branch(qa, ka, va, za, aa, ba, conv_qkv_w, gdn_a_log, gdn_dt_bias, gdn_norm_w) @ w_o_gdn
    y_b = lru_branch(xl, gl, rg_conv_w, rg_conv_b, rg_w_a, rg_b_a, rg_w_x, rg_b_x, rg_lambda) @ w_o_lru
    y_c = swa_branch(qc, kc, vc, attn_sinks, rel_bias) @ w_o_swa
    g_a, g_b, g_c = jnp.split(jax.nn.sigmoid(gate_logits), N_BRANCHES, axis=-1)
    return (g_a * y_a + g_b * y_b + g_c * y_c) @ w_out


def clamped_swiglu(hgu):
    gate = jnp.minimum(hgu[..., ::2], SWIGLU_LIMIT)
    lin = jnp.clip(hgu[..., 1::2], -SWIGLU_LIMIT, SWIGLU_LIMIT)
    return gate * jax.nn.sigmoid(SWIGLU_ALPHA * gate) * (lin + 1.0)


def moe(h, router_w, router_b, w_gu, b_gu, w_down, b_down):
    B, S, D = h.shape
    T = B * S
    A = T * TOP_K
    xt = h.reshape(T, D)
    logits = (xt @ router_w + router_b).astype(jnp.float32)
    top_vals, top_idx = lax.top_k(logits, TOP_K)
    gates = jax.nn.softmax(top_vals, axis=-1)
    e_flat = top_idx.reshape(A)
    order = jnp.argsort(e_flat)
    sorted_e = e_flat[order]
    counts = jnp.bincount(e_flat, length=N_EXPERTS)
    starts = jnp.cumsum(counts) - counts
    padded = ((counts + EXPERT_BLOCK - 1) // EXPERT_BLOCK) * EXPERT_BLOCK
    pad_ends = jnp.cumsum(padded)
    pad_starts = pad_ends - padded
    dest_sorted = pad_starts[sorted_e] + (jnp.arange(A) - starts[sorted_e])
    P = A + N_EXPERTS * EXPERT_BLOCK
    n_blocks = P // EXPERT_BLOCK
    buf = jnp.zeros((P, D), h.dtype).at[dest_sorted].set(xt[order // TOP_K])
    blk_e = jnp.minimum(jnp.searchsorted(pad_ends, jnp.arange(n_blocks) * EXPERT_BLOCK, side='right'),
                        N_EXPERTS - 1)

    def expert_block(args):
        xb, e = args
        return clamped_swiglu(xb @ w_gu[e] + b_gu[e]) @ w_down[e] + b_down[e]

    out_buf = lax.map(expert_block, (buf.reshape(n_blocks, EXPERT_BLOCK, D), blk_e)).reshape(P, D)
    dest = jnp.zeros((A,), dest_sorted.dtype).at[order].set(dest_sorted)
    y = jnp.einsum('tk,tkd->td', gates.astype(out_buf.dtype), out_buf[dest].reshape(T, TOP_K, D))
    return y.reshape(B, S, D).astype(h.dtype)


def setup_inputs(seed: int = 0) -> dict:
    key = jax.random.key(seed)
    ks = iter(jax.random.split(key, 48))
    f32 = jnp.float32

    def nrm(shape, scale):
        return jax.random.normal(next(ks), shape, f32) * scale

    def gain(shape):
        return 1.0 + nrm(shape, 0.02)

    x = nrm((BATCH, SEQ, D_MODEL), 1.0)
    p = nrm((DEPTH, BATCH, SEQ, PLE_DIM), 1.0)
    w_in = nrm((DEPTH, D_MODEL, IN_COLS), D_MODEL ** -0.5)
    conv_qkv_w = nrm((DEPTH, CONV_WIDTH, 3 * GDN_WIDTH), CONV_WIDTH ** -0.5)
    gdn_a_log = jnp.log(jax.random.uniform(next(ks), (DEPTH, GDN_HEADS), f32, 1.0, 16.0))
    dt = jnp.exp(jax.random.uniform(next(ks), (DEPTH, GDN_HEADS), f32, math.log(1e-3), math.log(1e-1)))
    gdn_dt_bias = dt + jnp.log(-jnp.expm1(-dt))
    gdn_norm_w = gain((DEPTH, GDN_HEAD_DIM))
    rg_conv_w = nrm((DEPTH, CONV_WIDTH, LRU_WIDTH), CONV_WIDTH ** -0.5)
    rg_conv_b = nrm((DEPTH, LRU_WIDTH), 0.02)
    rg_w_a = nrm((DEPTH, LRU_BLOCKS, LRU_BLOCK_DIM, LRU_BLOCK_DIM), LRU_BLOCK_DIM ** -0.5)
    rg_b_a = nrm((DEPTH, LRU_WIDTH), 0.02)
    rg_w_x = nrm((DEPTH, LRU_BLOCKS, LRU_BLOCK_DIM, LRU_BLOCK_DIM), LRU_BLOCK_DIM ** -0.5)
    rg_b_x = nrm((DEPTH, LRU_WIDTH), 0.02)
    s = jax.random.uniform(next(ks), (DEPTH, LRU_WIDTH), f32, 0.9, 0.999) ** (1.0 / LRU_C)
    rg_lambda = jnp.log(s) - jnp.log1p(-s)
    attn_sinks = nrm((DEPTH, SWA_Q_HEADS), 0.5)
    rel_bias = nrm((REL_BUCKETS, SWA_Q_HEADS), 0.5)
    w_o_gdn = nrm((DEPTH, GDN_WIDTH, D_MODEL), GDN_WIDTH ** -0.5)
    w_o_lru = nrm((DEPTH, LRU_WIDTH, D_MODEL), LRU_WIDTH ** -0.5)
    w_o_swa = nrm((DEPTH, SWA_Q_WIDTH, D_MODEL), SWA_Q_WIDTH ** -0.5)
    w_out = nrm((DEPTH, D_MODEL, D_MODEL), D_MODEL ** -0.5 * DEEPNORM_BETA)
    ln1_g = gain((DEPTH, D_MODEL))
    ln1_b = nrm((DEPTH, D_MODEL), 0.02)
    router_w = nrm((DEPTH, D_MODEL, N_EXPERTS), D_MODEL ** -0.5)
    router_b = nrm((DEPTH, N_EXPERTS), 0.01)
    w_gu = nrm((DEPTH, N_EXPERTS, D_MODEL, 2 * D_EXPERT), D_MODEL ** -0.5)
    b_gu = nrm((DEPTH, N_EXPERTS, 2 * D_EXPERT), 0.02)
    w_down = nrm((DEPTH, N_EXPERTS, D_EXPERT, D_MODEL), D_EXPERT ** -0.5 * DEEPNORM_BETA)
    b_down = nrm((DEPTH, N_EXPERTS, D_MODEL), 0.02)
    ln2_g = gain((DEPTH, D_MODEL))
    ln2_b = nrm((DEPTH, D_MODEL), 0.02)
    ple_w_gate = nrm((DEPTH, D_MODEL, D_MODEL), D_MODEL ** -0.5)
    ple_w_proj = nrm((DEPTH, PLE_DIM, D_MODEL), PLE_DIM ** -0.5 * DEEPNORM_BETA)
    ln3_g = gain((DEPTH, D_MODEL))
    ln3_b = nrm((DEPTH, D_MODEL), 0.02)
    return {"x": x, "p": p, "w_in": w_in, "conv_qkv_w": conv_qkv_w, "gdn_a_log": gdn_a_log,
            "gdn_dt_bias": gdn_dt_bias, "gdn_norm_w": gdn_norm_w, "rg_conv_w": rg_conv_w,
            "rg_conv_b": rg_conv_b, "rg_w_a": rg_w_a, "rg_b_a": rg_b_a, "rg_w_x": rg_w_x,
            "rg_b_x": rg_b_x, "rg_lambda": rg_lambda, "attn_sinks": attn_sinks, "rel_bias": rel_bias,
            "w_o_gdn": w_o_gdn, "w_o_lru": w_o_lru, "w_o_swa": w_o_swa, "w_out": w_out,
            "ln1_g": ln1_g, "ln1_b": ln1_b, "router_w": router_w, "router_b": router_b,
            "w_gu": w_gu, "b_gu": b_gu, "w_down": w_down, "b_down": b_down,
            "ln2_g": ln2_g, "ln2_b": ln2_b, "ple_w_gate": ple_w_gate, "ple_w_proj": ple_w_proj,
            "ln3_g": ln3_g, "ln3_b": ln3_b}


def reference(x, p, w_in, conv_qkv_w, gdn_a_log, gdn_dt_bias, gdn_norm_w, rg_conv_w, rg_conv_b,
              rg_w_a, rg_b_a, rg_w_x, rg_b_x, rg_lambda, attn_sinks, rel_bias,
              w_o_gdn, w_o_lru, w_o_swa, w_out, ln1_g, ln1_b, router_w, router_b,
              w_gu, b_gu, w_down, b_down, ln2_g, ln2_b, ple_w_gate, ple_w_proj, ln3_g, ln3_b):
    for i in range(DEPTH):
        y = token_mixer(x, w_in[i], conv_qkv_w[i], gdn_a_log[i], gdn_dt_bias[i], gdn_norm_w[i],
                        rg_conv_w[i], rg_conv_b[i], rg_w_a[i], rg_b_a[i], rg_w_x[i], rg_b_x[i], rg_lambda[i],
                        attn_sinks[i], rel_bias, w_o_gdn[i], w_o_lru[i], w_o_swa[i], w_out[i])
        x = layer_norm(DEEPNORM_ALPHA * x + y, ln1_g[i], ln1_b[i])
        y = moe(x, router_w[i], router_b[i], w_gu[i], b_gu[i], w_down[i], b_down[i])
        x = layer_norm(DEEPNORM_ALPHA * x + y, ln2_g[i], ln2_b[i])
        ple = jax.nn.sigmoid(x @ ple_w_gate[i]) * (p[i] @ ple_w_proj[i])
        x = layer_norm(DEEPNORM_ALPHA * x + ple, ln3_g[i], ln3_b[i])
    return x
```

```python
import functools
import math

import numpy as np
import jax
import jax.numpy as jnp
from jax import lax
from jax.experimental import pallas as pl
from jax.experimental.pallas import tpu as pltpu

f32 = jnp.float32
bf16 = jnp.bfloat16
i32 = jnp.int32

D_MODEL = 1024
PLE_DIM = 256
GDN_HEADS = 8
GDN_HEAD_DIM = 128
GDN_CHUNK = 64
CONV_WIDTH = 4
LRU_BLOCKS = 8
LRU_BLOCK_DIM = 128
LRU_C = 8.0
SWA_Q_HEADS = 16
SWA_KV_HEADS = 4
SWA_HEAD_DIM = 64
SWA_GROUP = 4
WINDOW = 128
REL_BUCKETS = 32
REL_MAX_DISTANCE = 128
N_EXPERTS = 32
TOP_K = 4
SWIGLU_LIMIT = 7.0
SWIGLU_ALPHA = 1.702
LN_EPS = 1e-5
NORM_EPS = 1e-6
NEG_BIG = -1e30

LANES = 128
VMEM_LIMIT = 56 * 1024 * 1024

C_GQ, C_GK, C_GV, C_GZ = 0, 1024, 2048, 3072
C_LX, C_LG = 4096, 5120
C_SQ = 6144
C_MA, C_MB, C_MC = 7168, 8192, 9216
C_SK, C_SV = 10240, 10496
C_AB = 10752
FAT_W = 10880
FAT_TN = 2176

EXPERT_BLK = 256


def _params(sem):
    return pltpu.CompilerParams(dimension_semantics=sem, vmem_limit_bytes=VMEM_LIMIT)


def _sigmoid(x):
    return 1.0 / (1.0 + jnp.exp(-x))


def _softplus(x):
    return jnp.maximum(x, 0.0) + jnp.log(1.0 + jnp.exp(-jnp.abs(x)))


def _layer_norm(z, g, b):
    mu = jnp.mean(z, axis=-1, keepdims=True)
    zc = z - mu
    var = jnp.mean(zc * zc, axis=-1, keepdims=True)
    return zc * lax.rsqrt(var + LN_EPS) * g + b


def _inproj_kernel(x_ref, w_ref, o_ref, xb_ref):
    @pl.when(pl.program_id(1) == 0)
    def _():
        xb_ref[...] = x_ref[...].astype(bf16)

    o_ref[...] = jnp.dot(xb_ref[...], w_ref[...], preferred_element_type=f32).astype(o_ref.dtype)


def inproj(x2d, wcat, tm=1024):
    T, K = x2d.shape
    N = wcat.shape[1]
    tn = FAT_TN
    return pl.pallas_call(
        _inproj_kernel,
        grid=(T // tm, N // tn),
        in_specs=[pl.BlockSpec((tm, K), lambda i, j: (i, 0)),
                  pl.BlockSpec((K, tn), lambda i, j: (0, j))],
        out_specs=pl.BlockSpec((tm, tn), lambda i, j: (i, j)),
        out_shape=jax.ShapeDtypeStruct((T, N), bf16),
        scratch_shapes=[pltpu.VMEM((tm, K), bf16)],
        compiler_params=_params(("arbitrary", "arbitrary")),
        name="inproj",
    )(x2d, wcat)


def _causal_conv_silu(src_ref, dst_ref, xpad_ref, carry_ref, w, ts):
    x = src_ref[...].astype(f32)
    xpad_ref[0:8, :] = carry_ref[...]
    xpad_ref[8:8 + ts, :] = x
    carry_ref[...] = x[ts - 8:ts, :]
    y = (w[0:1, :] * xpad_ref[5:5 + ts, :] + w[1:2, :] * xpad_ref[6:6 + ts, :]
         + w[2:3, :] * xpad_ref[7:7 + ts, :] + w[3:4, :] * x)
    dst_ref[...] = y * _sigmoid(y)


def _gdn_kernel(q_ref, k_ref, v_ref, z_ref, ab_ref, cw_ref, alog_ref, dtb_ref, nw_ref, o_ref,
                xpad_ref, qs_ref, ks_ref, vs_ref, carry_ref, state_ref, g_ref, beta_ref, *, ts):
    C = GDN_CHUNK
    D = GDN_HEAD_DIM
    P = 2 * C

    @pl.when(pl.program_id(1) == 0)
    def _():
        carry_ref[...] = jnp.zeros_like(carry_ref)
        state_ref[...] = jnp.zeros_like(state_ref)

    for p, (src, dst) in enumerate(((q_ref, qs_ref), (k_ref, ks_ref), (v_ref, vs_ref))):
        _causal_conv_silu(src, dst, xpad_ref, carry_ref.at[p], cw_ref[:, p * 1024:(p + 1) * 1024], ts)

    ab = ab_ref[...].astype(f32)
    g = -jnp.exp(alog_ref[...]) * _softplus(ab + dtb_ref[...])
    rin = lax.broadcasted_iota(i32, (ts, LANES), 0) & (C - 1)
    gsum = g
    for d in (1, 2, 4, 8, 16, 32):
        gsum = gsum + jnp.where(rin >= d, pltpu.roll(gsum, d, 0), 0.0)
    g_ref[...] = gsum
    beta_ref[...] = _sigmoid(ab)

    ri = lax.broadcasted_iota(i32, (P, P), 0)
    ci = lax.broadcasted_iota(i32, (P, P), 1)
    same = (ri >= C) == (ci >= C)
    eye = ri == ci
    causal = same & (ri >= ci)
    strict = same & (ri > ci)
    eye_f = jnp.where(eye, 1.0, 0.0).astype(f32)
    first_cols = ci < C
    nw = nw_ref[...]

    def stack(a, b):
        return jnp.concatenate([a, b], axis=0)

    def chunk_body(c, carry):
        r0 = pl.multiple_of(c * C, C)
        rows = pl.ds(r0, C)
        gc = g_ref[rows, :]
        bc = beta_ref[rows, :]
        for hp in range(GDN_HEADS // 2):
            h0, h1 = 2 * hp, 2 * hp + 1
            c0 = slice(h0 * D, (h0 + 1) * D)
            c1 = slice(h1 * D, (h1 + 1) * D)
            q2 = stack(qs_ref[rows, c0], qs_ref[rows, c1])
            k2 = stack(ks_ref[rows, c0], ks_ref[rows, c1])
            v2 = stack(vs_ref[rows, c0], vs_ref[rows, c1])
            qn = q2 * (lax.rsqrt(jnp.sum(q2 * q2, axis=-1, keepdims=True) + NORM_EPS) * (D ** -0.5))
            kn = k2 * lax.rsqrt(jnp.sum(k2 * k2, axis=-1, keepdims=True) + NORM_EPS)
            gcol = stack(gc[:, h0:h0 + 1], gc[:, h1:h1 + 1])
            bcol = stack(bc[:, 8 + h0:9 + h0], bc[:, 8 + h1:9 + h1])
            gl0 = gc[C - 1:C, h0:h0 + 1]
            gl1 = gc[C - 1:C, h1:h1 + 1]
            glast = stack(jnp.broadcast_to(gl0, (C, 1)), jnp.broadcast_to(gl1, (C, 1)))
            eg = jnp.exp(gcol)
            egl = jnp.exp(glast - gcol)
            kb = kn * bcol
            knb = kn.astype(bf16)
            a2 = lax.dot_general(stack(kb, qn).astype(bf16), knb, (((1,), (1,)), ((), ())),
                                 preferred_element_type=f32)
            kk = a2[:P]
            qk = a2[P:]
            gm = jnp.broadcast_to(gcol, (P, P))
            grow = jnp.sum(jnp.where(eye, gm, 0.0), axis=0, keepdims=True)
            decay = jnp.where(causal, jnp.exp(jnp.minimum(gm - grow, 0.0)), 0.0)
            lmat = jnp.where(strict, kk * decay, 0.0)
            intra = qk * decay
            lb = lmat.astype(bf16)
            xm = eye_f - lmat
            pm = jnp.dot(lb, lb, preferred_element_type=f32)
            for it in range(5):
                pb = pm.astype(bf16)
                xm = xm + jnp.dot(xm.astype(bf16), pb, preferred_element_type=f32)
                if it < 4:
                    pm = jnp.dot(pb, pb, preferred_element_type=f32)
            rhs = jnp.concatenate([v2 * bcol, kb * eg], axis=1).astype(bf16)
            uw = jnp.dot(xm.astype(bf16), rhs, preferred_element_type=f32)
            u2 = uw[:, :D]
            w2 = uw[:, D:]
            qd = qn * eg
            kd = kn * egl
            s0 = state_ref[h0]
            s1 = state_ref[h1]
            wq0 = jnp.dot(stack(w2[:C], qd[:C]).astype(bf16), s0.astype(bf16), preferred_element_type=f32)
            wq1 = jnp.dot(stack(w2[C:], qd[C:]).astype(bf16), s1.astype(bf16), preferred_element_type=f32)
            vnew = u2 - stack(wq0[:C], wq1[:C])
            vnb = vnew.astype(bf16)
            o2 = stack(wq0[C:], wq1[C:]) + jnp.dot(intra.astype(bf16), vnb, preferred_element_type=f32)
            kdt = kd.T
            kdt0 = jnp.where(first_cols, kdt, 0.0).astype(bf16)
            kdt1 = jnp.where(first_cols, 0.0, kdt).astype(bf16)
            state_ref[h0] = s0 * jnp.exp(gl0) + jnp.dot(kdt0, vnb, preferred_element_type=f32)
            state_ref[h1] = s1 * jnp.exp(gl1) + jnp.dot(kdt1, vnb, preferred_element_type=f32)
            z2 = stack(z_ref[rows, c0], z_ref[rows, c1]).astype(f32)
            on = (o2 * lax.rsqrt(jnp.mean(o2 * o2, axis=-1, keepdims=True) + NORM_EPS) * nw
                  * (z2 * _sigmoid(z2))).astype(o_ref.dtype)
            o_ref[rows, c0] = on[:C]
            o_ref[rows, c1] = on[C:]
        return carry

    lax.fori_loop(0, ts // C, chunk_body, 0)


def gdn_branch(fat, conv_w, a_log_row, dt_bias_row, norm_w_row, B, S, ts=256):
    T = B * S
    ns = S // ts
    row = lambda b, s: b * ns + s
    blk = lambda cb: pl.BlockSpec((ts, 1024), lambda b, s: (row(b, s), cb))
    full = lambda shp: pl.BlockSpec(shp, lambda b, s: (0,) * len(shp))
    return pl.pallas_call(
        functools.partial(_gdn_kernel, ts=ts),
        grid=(B, ns),
        in_specs=[blk(C_GQ // 1024), blk(C_GK // 1024), blk(C_GV // 1024), blk(C_GZ // 1024),
                  pl.BlockSpec((ts, LANES), lambda b, s: (row(b, s), C_AB // LANES)),
                  full((CONV_WIDTH, 3072)), full((1, LANES)), full((1, LANES)), full((1, LANES))],
        out_specs=pl.BlockSpec((ts, 1024), lambda b, s: (row(b, s), 0)),
        out_shape=jax.ShapeDtypeStruct((T, 1024), bf16),
        scratch_shapes=[pltpu.VMEM((ts + 8, 1024), f32),
                        pltpu.VMEM((ts, 1024), f32), pltpu.VMEM((ts, 1024), f32), pltpu.VMEM((ts, 1024), f32),
                        pltpu.VMEM((3, 8, 1024), f32),
                        pltpu.VMEM((GDN_HEADS, GDN_HEAD_DIM, GDN_HEAD_DIM), f32),
                        pltpu.VMEM((ts, LANES), f32), pltpu.VMEM((ts, LANES), f32)],
        compiler_params=_params(("arbitrary", "arbitrary")),
        name="gdn",
    )(fat, fat, fat, fat, fat, conv_w, a_log_row, dt_bias_row, norm_w_row)


def _lru_kernel(x_ref, gate_ref, cw_ref, cb_ref, wax_ref, ba_ref, bx_ref, lam_ref, o_ref,
                xpad_ref, a_ref, u_ref, h_ref, carry_ref, hc_ref, *, ts):
    @pl.when(pl.program_id(1) == 0)
    def _():
        carry_ref[...] = jnp.zeros_like(carry_ref)
        hc_ref[...] = jnp.zeros_like(hc_ref)

    x = x_ref[...].astype(f32)
    xpad_ref[0:8, :] = carry_ref[...]
    xpad_ref[8:8 + ts, :] = x
    carry_ref[...] = x[ts - 8:ts, :]
    w = cw_ref[...]
    xpad_ref[8:8 + ts, :] = (w[0:1, :] * xpad_ref[5:5 + ts, :] + w[1:2, :] * xpad_ref[6:6 + ts, :]
                             + w[2:3, :] * xpad_ref[7:7 + ts, :] + w[3:4, :] * x + cb_ref[...])
    nsp = _softplus(-lam_ref[...])
    for blk in range(LRU_BLOCKS):
        cs = slice(blk * LRU_BLOCK_DIM, (blk + 1) * LRU_BLOCK_DIM)
        xc = xpad_ref[8:8 + ts, cs]
        ri = jnp.dot(xc.astype(bf16), wax_ref[blk], preferred_element_type=f32)
        r = _sigmoid(ri[:, :LRU_BLOCK_DIM] + ba_ref[:, cs])
        i = _sigmoid(ri[:, LRU_BLOCK_DIM:] + bx_ref[:, cs])
        log_a = -LRU_C * r * nsp[:, cs]
        a_ref[:, cs] = jnp.exp(log_a)
        u_ref[:, cs] = jnp.sqrt(1.0 - jnp.exp(2.0 * log_a)) * (i * xc)

    rowi = lax.broadcasted_iota(i32, (8, D_MODEL), 0)

    def group(gi, h):
        r0 = pl.multiple_of(gi * 8, 8)
        a = a_ref[pl.ds(r0, 8), :]
        b = u_ref[pl.ds(r0, 8), :]
        for d in (1, 2, 4):
            m = rowi >= d
            a_s = pltpu.roll(a, d, 0)
            b_s = pltpu.roll(b, d, 0)
            b = jnp.where(m, a * b_s + b, b)
            a = jnp.where(m, a * a_s, a)
        hh = a * h + b
        h_ref[pl.ds(r0, 8), :] = hh
        return hh[7:8, :]

    hc_ref[...] = lax.fori_loop(0, ts // 8, group, hc_ref[...])
    gt = gate_ref[...].astype(f32)
    o_ref[...] = (h_ref[...] * jax.nn.gelu(gt)).astype(o_ref.dtype)


def lru_branch(fat, conv_w, conv_b, wax, b_a, b_x, lam, B, S, ts=512):
    T = B * S
    ns = S // ts
    row = lambda b, s: b * ns + s
    full = lambda shp: pl.BlockSpec(shp, lambda b, s: (0,) * len(shp))
    return pl.pallas_call(
        functools.partial(_lru_kernel, ts=ts),
        grid=(B, ns),
        in_specs=[pl.BlockSpec((ts, 1024), lambda b, s: (row(b, s), C_LX // 1024)),
                  pl.BlockSpec((ts, 1024), lambda b, s: (row(b, s), C_LG // 1024)),
                  full((CONV_WIDTH, 1024)), full((1, 1024)), full((LRU_BLOCKS, LRU_BLOCK_DIM, 2 * LRU_BLOCK_DIM)),
                  full((1, 1024)), full((1, 1024)), full((1, 1024))],
        out_specs=pl.BlockSpec((ts, 1024), lambda b, s: (row(b, s), 0)),
        out_shape=jax.ShapeDtypeStruct((T, 1024), bf16),
        scratch_shapes=[pltpu.VMEM((ts + 8, 1024), f32), pltpu.VMEM((ts, 1024), f32),
                        pltpu.VMEM((ts, 1024), f32), pltpu.VMEM((ts, 1024), f32),
                        pltpu.VMEM((8, 1024), f32), pltpu.VMEM((1, 1024), f32)],
        compiler_params=_params(("arbitrary", "arbitrary")),
        name="lru",
    )(fat, fat, conv_w, conv_b, wax, b_a, b_x, lam)


def _swa_kernel(q_ref, kc_ref, kp_ref, vc_ref, vp_ref, bias_ref, sink_ref, o_ref, kb_ref, vb_ref, *, tq):
    W = WINDOW
    hd = SWA_HEAD_DIM
    kb_ref[0:W, :] = kp_ref[...]
    kb_ref[W:W + tq, :] = kc_ref[...]
    vb_ref[0:W, :] = vp_ref[...]
    vb_ref[W:W + tq, :] = vc_ref[...]
    first_tile = pl.program_id(1) == 0
    col = lax.broadcasted_iota(i32, (W, 2 * W), 1)
    scale = hd ** -0.5

    def qblock(n, carry):
        r0 = pl.multiple_of(n * W, W)
        pen = jnp.where(jnp.logical_and(first_tile, n == 0), NEG_BIG, 0.0).astype(f32)
        penm = jnp.where(col < W, pen, 0.0)
        for hk in range(SWA_KV_HEADS):
            kk = kb_ref[pl.ds(r0, 2 * W), hk * hd:(hk + 1) * hd]
            vv = vb_ref[pl.ds(r0, 2 * W), hk * hd:(hk + 1) * hd]
            qg = q_ref[pl.ds(r0, W), hk * SWA_GROUP * hd:(hk + 1) * SWA_GROUP * hd]
            outs = []
            for gq in range(SWA_GROUP):
                h = hk * SWA_GROUP + gq
                qh = qg[:, gq * hd:(gq + 1) * hd]
                s = lax.dot_general(qh, kk, (((1,), (1,)), ((), ())), preferred_element_type=f32)
                s = s * scale + bias_ref[h] + penm
                sink = sink_ref[h:h + 1, 0:1]
                m = jnp.maximum(jnp.max(s, axis=-1, keepdims=True), sink)
                pe = jnp.exp(s - m)
                den = jnp.sum(pe, axis=-1, keepdims=True) + jnp.exp(sink - m)
                o = jnp.dot(pe.astype(bf16), vv, preferred_element_type=f32)
                outs.append(o / den)
            o_ref[pl.ds(r0, W), hk * SWA_GROUP * hd:(hk + 1) * SWA_GROUP * hd] = (
                jnp.concatenate(outs, axis=1).astype(o_ref.dtype))
        return carry

    lax.fori_loop(0, tq // W, qblock, 0)


def swa_branch(fat, bias_tab, sinks_tab, B, S, tq=512):
    T = B * S
    ns = S // tq
    nb = S // WINDOW
    per = tq // WINDOW
    row = lambda b, s: b * ns + s
    prev = lambda b, s: b * nb + jnp.maximum(s * per - 1, 0)
    kvw = SWA_KV_HEADS * SWA_HEAD_DIM
    full = lambda shp: pl.BlockSpec(shp, lambda b, s: (0,) * len(shp))
    return pl.pallas_call(
        functools.partial(_swa_kernel, tq=tq),
        grid=(B, ns),
        in_specs=[pl.BlockSpec((tq, 1024), lambda b, s: (row(b, s), C_SQ // 1024)),
                  pl.BlockSpec((tq, kvw), lambda b, s: (row(b, s), C_SK // kvw)),
                  pl.BlockSpec((WINDOW, kvw), lambda b, s: (prev(b, s), C_SK // kvw)),
                  pl.BlockSpec((tq, kvw), lambda b, s: (row(b, s), C_SV // kvw)),
                  pl.BlockSpec((WINDOW, kvw), lambda b, s: (prev(b, s), C_SV // kvw)),
                  full((SWA_Q_HEADS, WINDOW, 2 * WINDOW)), full((SWA_Q_HEADS, LANES))],
        out_specs=pl.BlockSpec((tq, 1024), lambda b, s: (row(b, s), 0)),
        out_shape=jax.ShapeDtypeStruct((T, 1024), bf16),
        scratch_shapes=[pltpu.VMEM((tq + WINDOW, kvw), bf16), pltpu.VMEM((tq + WINDOW, kvw), bf16)],
        compiler_params=_params(("arbitrary", "arbitrary")),
        name="swa",
    )(fat, fat, fat, fat, fat, bias_tab, sinks_tab)


def _merge_kernel(oa_ref, ob_ref, oc_ref, ga_ref, gb_ref, gc_ref, x_ref, wa_ref, wb_ref, wc_ref, wo_ref,
                  g_ref, b_ref, o_ref, *, alpha):
    ya = jnp.dot(oa_ref[...], wa_ref[...], preferred_element_type=f32)
    yb = jnp.dot(ob_ref[...], wb_ref[...], preferred_element_type=f32)
    yc = jnp.dot(oc_ref[...], wc_ref[...], preferred_element_type=f32)
    mix = (_sigmoid(ga_ref[...].astype(f32)) * ya + _sigmoid(gb_ref[...].astype(f32)) * yb
           + _sigmoid(gc_ref[...].astype(f32)) * yc)
    y = jnp.dot(mix.astype(bf16), wo_ref[...], preferred_element_type=f32)
    o_ref[...] = _layer_norm(alpha * x_ref[...] + y, g_ref[...], b_ref[...])


def merge_ln(oa, ob, oc, fat, x2d, wa, wb, wc, wo, g, b, alpha, tm=512):
    T = x2d.shape[0]
    act = pl.BlockSpec((tm, 1024), lambda i: (i, 0))
    fatb = lambda cb: pl.BlockSpec((tm, 1024), lambda i: (i, cb))
    wsp = pl.BlockSpec((1024, 1024), lambda i: (0, 0))
    vec = pl.BlockSpec((1, 1024), lambda i: (0, 0))
    return pl.pallas_call(
        functools.partial(_merge_kernel, alpha=alpha),
        grid=(T // tm,),
        in_specs=[act, act, act, fatb(C_MA // 1024), fatb(C_MB // 1024), fatb(C_MC // 1024), act,
                  wsp, wsp, wsp, wsp, vec, vec],
        out_specs=act,
        out_shape=jax.ShapeDtypeStruct((T, 1024), f32),
        compiler_params=_params(("arbitrary",)),
        name="merge_ln",
    )(oa, ob, oc, fat, fat, fat, x2d, wa, wb, wc, wo, g, b)


def _route_kernel(x_ref, rw_ref, rb_ref, gates_ref, eidx_ref, rank_ref, cnt_ref, run_ref, *, tm):
    @pl.when(pl.program_id(0) == 0)
    def _():
        run_ref[...] = jnp.zeros_like(run_ref)

    logits = jnp.dot(x_ref[...], rw_ref[...], preferred_element_type=f32,
                     precision=lax.Precision.HIGHEST) + rb_ref[...]
    lane = lax.broadcasted_iota(i32, (tm, LANES), 1)
    lane_f = lane.astype(f32)
    work = logits
    vals, idxs, hots = [], [], []
    for _ in range(TOP_K):
        m = jnp.max(work, axis=-1, keepdims=True)
        idx = jnp.min(jnp.where(work == m, lane_f, float(LANES)), axis=-1, keepdims=True)
        hot = lane_f == idx
        vals.append(m)
        idxs.append(idx)
        hots.append(hot)
        work = jnp.where(hot, -jnp.inf, work)
    es = [jnp.exp(v - vals[0]) for v in vals]
    den = es[0] + es[1] + es[2] + es[3]
    sel = jnp.zeros((tm, LANES), f32)
    for hot in hots:
        sel = sel + jnp.where(hot, 1.0, 0.0)
    ri = lax.broadcasted_iota(i32, (tm, tm), 0)
    ci = lax.broadcasted_iota(i32, (tm, tm), 1)
    tril = jnp.where(ri > ci, 1.0, 0.0).astype(bf16)
    before = jnp.dot(tril, sel.astype(bf16), preferred_element_type=f32) + run_ref[...]
    run_ref[...] = run_ref[...] + jnp.sum(sel, axis=0, keepdims=True)
    cnt_ref[...] = run_ref[...]
    gates = jnp.zeros((tm, LANES), f32)
    eidx = jnp.zeros((tm, LANES), f32)
    rank = jnp.zeros((tm, LANES), f32)
    for k in range(TOP_K):
        rk = jnp.sum(jnp.where(hots[k], before, 0.0), axis=-1, keepdims=True)
        gates = jnp.where(lane == k, es[k] / den, gates)
        eidx = jnp.where(lane == k, idxs[k], eidx)
        rank = jnp.where(lane == k, rk, rank)
    gates_ref[...] = gates
    eidx_ref[...] = eidx.astype(i32)
    rank_ref[...] = rank.astype(i32)


def route(x2d, rw_pad, rb_pad, tm=512):
    T = x2d.shape[0]
    outb = pl.BlockSpec((tm, LANES), lambda i: (i, 0))
    return pl.pallas_call(
        functools.partial(_route_kernel, tm=tm),
        grid=(T // tm,),
        in_specs=[pl.BlockSpec((tm, 1024), lambda i: (i, 0)),
                  pl.BlockSpec((1024, LANES), lambda i: (0, 0)),
                  pl.BlockSpec((1, LANES), lambda i: (0, 0))],
        out_specs=[outb, outb, outb, pl.BlockSpec((1, LANES), lambda i: (0, 0))],
        out_shape=[jax.ShapeDtypeStruct((T, LANES), f32), jax.ShapeDtypeStruct((T, LANES), i32),
                   jax.ShapeDtypeStruct((T, LANES), i32), jax.ShapeDtypeStruct((1, LANES), f32)],
        scratch_shapes=[pltpu.VMEM((1, LANES), f32)],
        compiler_params=_params(("arbitrary",)),
        name="route",
    )(x2d, rw_pad, rb_pad)


ROW_UNROLL = 8


def _row_copy(src_ref, src_row, dst_ref, dst_row, sem):
    return pltpu.make_async_copy(src_ref.at[pl.ds(src_row, 1)], dst_ref.at[pl.ds(dst_row, 1)], sem)


def _dispatch_kernel(dest_ref, x_ref, buf_in_ref, buf_ref, sem, *, tm):
    del buf_in_ref

    def issue(g, carry):
        for j in range(ROW_UNROLL):
            r = g * ROW_UNROLL + j
            for k in range(TOP_K):
                _row_copy(x_ref, r, buf_ref, dest_ref[r * TOP_K + k], sem).start()
        return carry

    lax.fori_loop(0, tm // ROW_UNROLL, issue, 0)

    def drain(g, carry):
        for j in range(ROW_UNROLL * TOP_K):
            _row_copy(x_ref, 0, buf_ref, 0, sem).wait()
        return carry

    lax.fori_loop(0, tm // ROW_UNROLL, drain, 0)


def dispatch(dest_flat, x2d, buf0, tm=512):
    T = x2d.shape[0]
    return pl.pallas_call(
        functools.partial(_dispatch_kernel, tm=tm),
        grid=(T // tm,),
        in_specs=[pl.BlockSpec((tm * TOP_K,), lambda i: (i,), memory_space=pltpu.SMEM),
                  pl.BlockSpec((tm, 1024), lambda i: (i, 0)),
                  pl.BlockSpec(memory_space=pl.ANY)],
        out_specs=pl.BlockSpec(memory_space=pl.ANY),
        out_shape=jax.ShapeDtypeStruct(buf0.shape, buf0.dtype),
        scratch_shapes=[pltpu.SemaphoreType.DMA(())],
        input_output_aliases={2: 0},
        compiler_params=_params(("arbitrary",)),
        name="dispatch",
    )(dest_flat, x2d, buf0)


def _expert_kernel(blk_e_ref, nused_ref, x_ref, wgu_ref, bgu_ref, wd_ref, bd_ref, o_ref):
    i = pl.program_id(0)

    @pl.when(i < nused_ref[0])
    def _():
        de = wd_ref.shape[1]
        xb = x_ref[...].astype(bf16)
        hgu = jnp.dot(xb, wgu_ref[0], preferred_element_type=f32) + bgu_ref[0]
        gate = jnp.minimum(hgu[:, :de], SWIGLU_LIMIT)
        lin = jnp.clip(hgu[:, de:], -SWIGLU_LIMIT, SWIGLU_LIMIT)
        act = gate * _sigmoid(SWIGLU_ALPHA * gate) * (lin + 1.0)
        o_ref[...] = jnp.dot(act.astype(bf16), wd_ref[0], preferred_element_type=f32) + bd_ref[0]

    @pl.when(i >= nused_ref[0])
    def _():
        o_ref[...] = jnp.zeros_like(o_ref)


def experts(blk_e, nused, buf, wgu, bgu, wd, bd):
    P, D = buf.shape
    nblk = P // EXPERT_BLK
    de = wd.shape[1]
    live = lambda i, be, nu: jnp.minimum(i, nu[0] - 1)
    grid_spec = pltpu.PrefetchScalarGridSpec(
        num_scalar_prefetch=2,
        grid=(nblk,),
        in_specs=[pl.BlockSpec((EXPERT_BLK, D), lambda i, be, nu: (live(i, be, nu), 0)),
                  pl.BlockSpec((1, D, 2 * de), lambda i, be, nu: (be[i], 0, 0)),
                  pl.BlockSpec((1, 1, 2 * de), lambda i, be, nu: (be[i], 0, 0)),
                  pl.BlockSpec((1, de, D), lambda i, be, nu: (be[i], 0, 0)),
                  pl.BlockSpec((1, 1, D), lambda i, be, nu: (be[i], 0, 0))],
        out_specs=pl.BlockSpec((EXPERT_BLK, D), lambda i, be, nu: (i, 0)),
    )
    return pl.pallas_call(
        _expert_kernel,
        grid_spec=grid_spec,
        out_shape=jax.ShapeDtypeStruct((P, D), f32),
        compiler_params=_params(("arbitrary",)),
        name="experts",
    )(blk_e, nused, buf, wgu, bgu, wd, bd)


def _combine_kernel(dest_ref, gates_ref, x_ref, p_ref, obuf_ref, wg_ref, wp_ref,
                    g2_ref, b2_ref, g3_ref, b3_ref, o_ref, rows_ref, sem, *, tm, alpha):
    def issue(g, carry):
        for j in range(ROW_UNROLL):
            r = g * ROW_UNROLL + j
            for k in range(TOP_K):
                _row_copy(obuf_ref, dest_ref[r * TOP_K + k], rows_ref.at[k], r, sem).start()
        return carry

    lax.fori_loop(0, tm // ROW_UNROLL, issue, 0)

    def drain(g, carry):
        for j in range(ROW_UNROLL * TOP_K):
            _row_copy(obuf_ref, 0, rows_ref.at[0], 0, sem).wait()
        return carry

    lax.fori_loop(0, tm // ROW_UNROLL, drain, 0)

    gates = gates_ref[...]
    y = gates[:, 0:1] * rows_ref[0]
    for k in range(1, TOP_K):
        y = y + gates[:, k:k + 1] * rows_ref[k]
    x2 = _layer_norm(alpha * x_ref[...] + y, g2_ref[...], b2_ref[...])
    gate = _sigmoid(jnp.dot(x2.astype(bf16), wg_ref[...], preferred_element_type=f32))
    proj = jnp.dot(p_ref[...].astype(bf16), wp_ref[...], preferred_element_type=f32)
    o_ref[...] = _layer_norm(alpha * x2 + gate * proj, g3_ref[...], b3_ref[...])


def combine_ple(dest_flat, gates, x2d, p2d, obuf, wg, wp, g2, b2, g3, b3, alpha, tm=256):
    T = x2d.shape[0]
    act = pl.BlockSpec((tm, 1024), lambda i: (i, 0))
    vec = pl.BlockSpec((1, 1024), lambda i: (0, 0))
    return pl.pallas_call(
        functools.partial(_combine_kernel, tm=tm, alpha=alpha),
        grid=(T // tm,),
        in_specs=[pl.BlockSpec((tm * TOP_K,), lambda i: (i,), memory_space=pltpu.SMEM),
                  pl.BlockSpec((tm, LANES), lambda i: (i, 0)),
                  act,
                  pl.BlockSpec((tm, PLE_DIM), lambda i: (i, 0)),
                  pl.BlockSpec(memory_space=pl.ANY),
                  pl.BlockSpec((1024, 1024), lambda i: (0, 0)),
                  pl.BlockSpec((PLE_DIM, 1024), lambda i: (0, 0)),
                  vec, vec, vec, vec],
        out_specs=act,
        out_shape=jax.ShapeDtypeStruct((T, 1024), f32),
        scratch_shapes=[pltpu.VMEM((TOP_K, tm, 1024), f32), pltpu.SemaphoreType.DMA(())],
        compiler_params=_params(("arbitrary",)),
        name="combine_ple",
    )(dest_flat, gates, x2d, p2d, obuf, wg, wp, g2, b2, g3, b3)


def _t5_bucket_np(dist):
    max_exact = REL_BUCKETS // 2
    d = np.maximum(dist.astype(np.float32), np.float32(1.0))
    large = max_exact + (np.log(d / np.float32(max_exact)) / np.float32(math.log(REL_MAX_DISTANCE / max_exact))
                         * np.float32(REL_BUCKETS - max_exact)).astype(np.int32)
    large = np.minimum(large, REL_BUCKETS - 1)
    return np.where(dist < max_exact, dist, large)


def _swa_bias_table(rel_bias):
    kj = np.arange(2 * WINDOW)[None, :]
    dist = (np.arange(WINDOW)[:, None] + WINDOW) - kj
    in_window = (dist >= 0) & (dist < WINDOW)
    bucket = _t5_bucket_np(np.maximum(dist, 0))
    bias = jnp.transpose(rel_bias[bucket].astype(f32), (2, 0, 1))
    return jnp.where(jnp.asarray(in_window)[None], bias, NEG_BIG)


def _pad_row(v, width=LANES, fill=0.0):
    v = v.astype(f32).reshape(1, -1)
    return jnp.pad(v, ((0, 0), (0, width - v.shape[1])), constant_values=fill)


def _wcat(w_in):
    cols = [w_in[:, 0:4096], w_in[:, 4112:6160], w_in[:, 6160:7184], w_in[:, 7696:10768],
            w_in[:, 7184:7440], w_in[:, 7440:7696], w_in[:, 4096:4112],
            jnp.zeros((w_in.shape[0], FAT_W - 10768), w_in.dtype)]
    return jnp.concatenate(cols, axis=1).astype(bf16)


def kernel(x, p, w_in, conv_qkv_w, gdn_a_log, gdn_dt_bias, gdn_norm_w, rg_conv_w, rg_conv_b, rg_w_a, rg_b_a, rg_w_x, rg_b_x, rg_lambda, attn_sinks, rel_bias, w_o_gdn, w_o_lru, w_o_swa, w_out, ln1_g, ln1_b, router_w, router_b, w_gu, b_gu, w_down, b_down, ln2_g, ln2_b, ple_w_gate, ple_w_proj, ln3_g, ln3_b):
    B, S, D = x.shape
    depth = w_in.shape[0]
    T = B * S
    A = T * TOP_K
    alpha = (2.0 * depth) ** 0.25
    P = A + N_EXPERTS * EXPERT_BLK
    nblk = P // EXPERT_BLK
    row = lambda v: v.astype(f32).reshape(1, -1)

    bias_tab = _swa_bias_table(rel_bias)
    xc = x.reshape(T, D)
    for i in range(depth):
        fat = inproj(xc, _wcat(w_in[i]))
        o_gdn = gdn_branch(fat, conv_qkv_w[i], _pad_row(gdn_a_log[i]), _pad_row(gdn_dt_bias[i]),
                           row(gdn_norm_w[i]), B, S)
        wax = jnp.concatenate([rg_w_a[i], rg_w_x[i]], axis=-1).astype(bf16)
        o_lru = lru_branch(fat, rg_conv_w[i], row(rg_conv_b[i]), wax, row(rg_b_a[i]), row(rg_b_x[i]),
                           row(rg_lambda[i]), B, S)
        sinks_tab = jnp.broadcast_to(attn_sinks[i].astype(f32)[:, None], (SWA_Q_HEADS, LANES))
        o_swa = swa_branch(fat, bias_tab, sinks_tab, B, S)
        x1 = merge_ln(o_gdn, o_lru, o_swa, fat, xc, w_o_gdn[i].astype(bf16), w_o_lru[i].astype(bf16),
                      w_o_swa[i].astype(bf16), w_out[i].astype(bf16), row(ln1_g[i]), row(ln1_b[i]), alpha)

        rw_pad = jnp.pad(router_w[i], ((0, 0), (0, LANES - N_EXPERTS)))
        rb_pad = _pad_row(router_b[i], fill=NEG_BIG)
        gates, eidx, rank, cnt = route(x1, rw_pad, rb_pad)
        counts = cnt[0, :N_EXPERTS].astype(i32)
        padded = ((counts + EXPERT_BLK - 1) // EXPERT_BLK) * EXPERT_BLK
        pad_ends = jnp.cumsum(padded)
        pad_starts = pad_ends - padded
        dest = (pad_starts[eidx[:, :TOP_K]] + rank[:, :TOP_K]).reshape(A)
        blk_e = jnp.minimum(jnp.searchsorted(pad_ends, jnp.arange(nblk, dtype=i32) * EXPERT_BLK, side='right'),
                            N_EXPERTS - 1).astype(i32)
        nused = (pad_ends[-1:] // EXPERT_BLK).astype(i32)

        buf = dispatch(dest, x1, jnp.zeros((P, D), f32))
        wgu = jnp.concatenate([w_gu[i][:, :, 0::2], w_gu[i][:, :, 1::2]], axis=-1).astype(bf16)
        bgu = jnp.concatenate([b_gu[i][:, 0::2], b_gu[i][:, 1::2]], axis=-1)[:, None, :]
        obuf = experts(blk_e, nused, buf, wgu, bgu, w_down[i].astype(bf16), b_down[i][:, None, :])
        xc = combine_ple(dest, gates, x1, p[i].reshape(T, PLE_DIM), obuf, ple_w_gate[i].astype(bf16),
                         ple_w_proj[i].astype(bf16), row(ln2_g[i]), row(ln2_b[i]), row(ln3_g[i]), row(ln3_b[i]),
                         alpha)
    return xc.reshape(B, S, D)
```

```python
import functools
import math

import numpy as np
import jax
import jax.numpy as jnp
from jax import lax
from jax.experimental import pallas as pl
from jax.experimental.pallas import tpu as pltpu

f32 = jnp.float32
bf16 = jnp.bfloat16
i32 = jnp.int32

D_MODEL = 1024
PLE_DIM = 256
GDN_HEADS = 8
GDN_HEAD_DIM = 128
GDN_CHUNK = 64
CONV_WIDTH = 4
LRU_BLOCKS = 8
LRU_BLOCK_DIM = 128
LRU_C = 8.0
SWA_Q_HEADS = 16
SWA_KV_HEADS = 4
SWA_HEAD_DIM = 64
SWA_GROUP = 4
WINDOW = 128
REL_BUCKETS = 32
REL_MAX_DISTANCE = 128
N_EXPERTS = 32
TOP_K = 4
SWIGLU_LIMIT = 7.0
SWIGLU_ALPHA = 1.702
LN_EPS = 1e-5
NORM_EPS = 1e-6
NEG_BIG = -1e30

LANES = 128
VMEM_LIMIT = 56 * 1024 * 1024

C_GQ, C_GK, C_GV, C_GZ = 0, 1024, 2048, 3072
C_LX, C_LG = 4096, 5120
C_SQ = 6144
C_MA, C_MB, C_MC = 7168, 8192, 9216
C_SK, C_SV = 10240, 10496
C_AB = 10752
FAT_W = 10880
FAT_TN = 2176

EXPERT_BLK = 256
CHUNKS_IN_FLIGHT = 2


def _params(sem):
    return pltpu.CompilerParams(dimension_semantics=sem, vmem_limit_bytes=VMEM_LIMIT)


def _sigmoid(x):
    return 1.0 / (1.0 + jnp.exp(-x))


def _softplus(x):
    return jnp.maximum(x, 0.0) + jnp.log(1.0 + jnp.exp(-jnp.abs(x)))


def _layer_norm(z, g, b):
    mu = jnp.mean(z, axis=-1, keepdims=True)
    zc = z - mu
    var = jnp.mean(zc * zc, axis=-1, keepdims=True)
    return zc * lax.rsqrt(var + LN_EPS) * g + b


def _inproj_kernel(x_ref, w_ref, o_ref, xb_ref):
    @pl.when(pl.program_id(1) == 0)
    def _():
        xb_ref[...] = x_ref[...].astype(bf16)

    o_ref[...] = jnp.dot(xb_ref[...], w_ref[...], preferred_element_type=f32).astype(o_ref.dtype)


def inproj(x2d, wcat, tm=1024):
    T, K = x2d.shape
    N = wcat.shape[1]
    tn = FAT_TN
    return pl.pallas_call(
        _inproj_kernel,
        grid=(T // tm, N // tn),
        in_specs=[pl.BlockSpec((tm, K), lambda i, j: (i, 0)),
                  pl.BlockSpec((K, tn), lambda i, j: (0, j))],
        out_specs=pl.BlockSpec((tm, tn), lambda i, j: (i, j)),
        out_shape=jax.ShapeDtypeStruct((T, N), bf16),
        scratch_shapes=[pltpu.VMEM((tm, K), bf16)],
        compiler_params=_params(("arbitrary", "arbitrary")),
        name="inproj",
    )(x2d, wcat)


def _causal_conv_silu(src_ref, dst_ref, xpad_ref, carry_ref, w, ts):
    x = src_ref[...].astype(f32)
    xpad_ref[0:8, :] = carry_ref[...]
    xpad_ref[8:8 + ts, :] = x
    carry_ref[...] = x[ts - 8:ts, :]
    y = (w[0:1, :] * xpad_ref[5:5 + ts, :] + w[1:2, :] * xpad_ref[6:6 + ts, :]
         + w[2:3, :] * xpad_ref[7:7 + ts, :] + w[3:4, :] * x)
    dst_ref[...] = y * _sigmoid(y)


def _gdn_kernel(q_ref, k_ref, v_ref, z_ref, ab_ref, cw_ref, alog_ref, dtb_ref, nw_ref, o_ref,
                xpad_ref, qs_ref, ks_ref, vs_ref, carry_ref, state_ref, g_ref, beta_ref,
                u_s, lhs_s, intra_s, kdt_s, *, ts):
    C = GDN_CHUNK
    D = GDN_HEAD_DIM
    P = 2 * C

    @pl.when(pl.program_id(1) == 0)
    def _():
        carry_ref[...] = jnp.zeros_like(carry_ref)
        state_ref[...] = jnp.zeros_like(state_ref)

    for p, (src, dst) in enumerate(((q_ref, qs_ref), (k_ref, ks_ref), (v_ref, vs_ref))):
        _causal_conv_silu(src, dst, xpad_ref, carry_ref.at[p], cw_ref[:, p * 1024:(p + 1) * 1024], ts)

    ab = ab_ref[...].astype(f32)
    g = -jnp.exp(alog_ref[...]) * _softplus(ab + dtb_ref[...])
    rin = lax.broadcasted_iota(i32, (ts, LANES), 0) & (C - 1)
    gsum = g
    for d in (1, 2, 4, 8, 16, 32):
        gsum = gsum + jnp.where(rin >= d, pltpu.roll(gsum, d, 0), 0.0)
    g_ref[...] = gsum
    beta_ref[...] = _sigmoid(ab)

    ri = lax.broadcasted_iota(i32, (P, P), 0)
    ci = lax.broadcasted_iota(i32, (P, P), 1)
    same = (ri >= C) == (ci >= C)
    eye = ri == ci
    causal = same & (ri >= ci)
    strict = same & (ri > ci)
    eye_f = jnp.where(eye, 1.0, 0.0).astype(f32)
    first_cols = ci < C
    nw = nw_ref[...]

    def stack(a, b):
        return jnp.concatenate([a, b], axis=0)

    def mm(a, b):
        return jnp.dot(a, b, preferred_element_type=f32)

    npair = GDN_HEADS // 2
    nchunk = ts // C
    hcols = [slice(h * D, (h + 1) * D) for h in range(GDN_HEADS)]

    for cg in range(0, nchunk, CHUNKS_IN_FLIGHT):
        probs = [(c, hp) for c in range(cg, cg + CHUNKS_IN_FLIGHT) for hp in range(npair)]
        qn, kn, vb, gcol, eg, egl, kb = [], [], [], [], [], [], []
        for c, hp in probs:
            rows = slice(c * C, (c + 1) * C)
            c0, c1 = hcols[2 * hp], hcols[2 * hp + 1]
            gc = g_ref[rows, :]
            bc = beta_ref[rows, :]
            q2 = stack(qs_ref[rows, c0], qs_ref[rows, c1])
            k2 = stack(ks_ref[rows, c0], ks_ref[rows, c1])
            v2 = stack(vs_ref[rows, c0], vs_ref[rows, c1])
            qn.append(q2 * (lax.rsqrt(jnp.sum(q2 * q2, axis=-1, keepdims=True) + NORM_EPS) * (D ** -0.5)))
            kn.append(k2 * lax.rsqrt(jnp.sum(k2 * k2, axis=-1, keepdims=True) + NORM_EPS))
            h0, h1 = 2 * hp, 2 * hp + 1
            gcl = stack(gc[:, h0:h0 + 1], gc[:, h1:h1 + 1])
            bcl = stack(bc[:, 8 + h0:9 + h0], bc[:, 8 + h1:9 + h1])
            glast = stack(jnp.broadcast_to(gc[C - 1:C, h0:h0 + 1], (C, 1)),
                          jnp.broadcast_to(gc[C - 1:C, h1:h1 + 1], (C, 1)))
            gcol.append(gcl)
            eg.append(jnp.exp(gcl))
            egl.append(jnp.exp(glast - gcl))
            kb.append(kn[-1] * bcl)
            vb.append(v2 * bcl)
        n = len(probs)
        a2 = [lax.dot_general(stack(kb[i], qn[i]).astype(bf16), kn[i].astype(bf16), (((1,), (1,)), ((), ())),
                              preferred_element_type=f32) for i in range(n)]
        lmat, intra = [], []
        for i in range(n):
            gm = jnp.broadcast_to(gcol[i], (P, P))
            grow = jnp.sum(jnp.where(eye, gm, 0.0), axis=0, keepdims=True)
            decay = jnp.where(causal, jnp.exp(jnp.minimum(gm - grow, 0.0)), 0.0)
            lmat.append(jnp.where(strict, a2[i][:P] * decay, 0.0))
            intra.append(a2[i][P:] * decay)
        lb = [l.astype(bf16) for l in lmat]
        xm = [eye_f - l for l in lmat]
        pm = [mm(b, b) for b in lb]
        for it in range(5):
            pb = [p.astype(bf16) for p in pm]
            xm = [x + mm(x.astype(bf16), b) for x, b in zip(xm, pb)]
            if it < 4:
                pm = [mm(b, b) for b in pb]
        uw = [mm(xm[i].astype(bf16), jnp.concatenate([vb[i], kb[i] * eg[i]], axis=1).astype(bf16))
              for i in range(n)]
        for i, (c, hp) in enumerate(probs):
            j = c * npair + hp
            w2 = uw[i][:, D:]
            qd = qn[i] * eg[i]
            kdt = (kn[i] * egl[i]).T
            u_s[j] = uw[i][:, :D]
            lhs_s[j, 0] = stack(w2[:C], qd[:C]).astype(bf16)
            lhs_s[j, 1] = stack(w2[C:], qd[C:]).astype(bf16)
            intra_s[j] = intra[i].astype(bf16)
            kdt_s[j, 0] = jnp.where(first_cols, kdt, 0.0).astype(bf16)
            kdt_s[j, 1] = jnp.where(first_cols, 0.0, kdt).astype(bf16)

    for c in range(nchunk):
        rows = slice(c * C, (c + 1) * C)
        gl = g_ref[(c + 1) * C - 1:(c + 1) * C, :]
        st = [state_ref[h] for h in range(GDN_HEADS)]
        wq = [mm(lhs_s[c * npair + h // 2, h % 2], st[h].astype(bf16)) for h in range(GDN_HEADS)]
        vnb = [(u_s[c * npair + hp] - stack(wq[2 * hp][:C], wq[2 * hp + 1][:C])).astype(bf16)
               for hp in range(npair)]
        o2 = [stack(wq[2 * hp][C:], wq[2 * hp + 1][C:]) + mm(intra_s[c * npair + hp], vnb[hp])
              for hp in range(npair)]
        for h in range(GDN_HEADS):
            state_ref[h] = st[h] * jnp.exp(gl[:, h:h + 1]) + mm(kdt_s[c * npair + h // 2, h % 2], vnb[h // 2])
        for hp in range(npair):
            c0, c1 = hcols[2 * hp], hcols[2 * hp + 1]
            z2 = stack(z_ref[rows, c0], z_ref[rows, c1]).astype(f32)
            on = (o2[hp] * lax.rsqrt(jnp.mean(o2[hp] * o2[hp], axis=-1, keepdims=True) + NORM_EPS) * nw
                  * (z2 * _sigmoid(z2))).astype(o_ref.dtype)
            o_ref[rows, c0] = on[:C]
            o_ref[rows, c1] = on[C:]


def gdn_branch(fat, conv_w, a_log_row, dt_bias_row, norm_w_row, B, S, ts=256):
    T = B * S
    ns = S // ts
    nprob = (ts // GDN_CHUNK) * (GDN_HEADS // 2)
    row = lambda b, s: b * ns + s
    blk = lambda cb: pl.BlockSpec((ts, 1024), lambda b, s: (row(b, s), cb))
    full = lambda shp: pl.BlockSpec(shp, lambda b, s: (0,) * len(shp))
    return pl.pallas_call(
        functools.partial(_gdn_kernel, ts=ts),
        grid=(B, ns),
        in_specs=[blk(C_GQ // 1024), blk(C_GK // 1024), blk(C_GV // 1024), blk(C_GZ // 1024),
                  pl.BlockSpec((ts, LANES), lambda b, s: (row(b, s), C_AB // LANES)),
                  full((CONV_WIDTH, 3072)), full((1, LANES)), full((1, LANES)), full((1, LANES))],
        out_specs=pl.BlockSpec((ts, 1024), lambda b, s: (row(b, s), 0)),
        out_shape=jax.ShapeDtypeStruct((T, 1024), bf16),
        scratch_shapes=[pltpu.VMEM((ts + 8, 1024), f32),
                        pltpu.VMEM((ts, 1024), f32), pltpu.VMEM((ts, 1024), f32), pltpu.VMEM((ts, 1024), f32),
                        pltpu.VMEM((3, 8, 1024), f32),
                        pltpu.VMEM((GDN_HEADS, GDN_HEAD_DIM, GDN_HEAD_DIM), f32),
                        pltpu.VMEM((ts, LANES), f32), pltpu.VMEM((ts, LANES), f32),
                        pltpu.VMEM((nprob, 128, GDN_HEAD_DIM), f32),
                        pltpu.VMEM((nprob, 2, 128, GDN_HEAD_DIM), bf16),
                        pltpu.VMEM((nprob, 128, 128), bf16),
                        pltpu.VMEM((nprob, 2, GDN_HEAD_DIM, 128), bf16)],
        compiler_params=_params(("arbitrary", "arbitrary")),
        name="gdn",
    )(fat, fat, fat, fat, fat, conv_w, a_log_row, dt_bias_row, norm_w_row)


def _lru_kernel(x_ref, gate_ref, cw_ref, cb_ref, wax_ref, ba_ref, bx_ref, lam_ref, o_ref,
                xpad_ref, a_ref, u_ref, h_ref, carry_ref, hc_ref, *, ts):
    @pl.when(pl.program_id(1) == 0)
    def _():
        carry_ref[...] = jnp.zeros_like(carry_ref)
        hc_ref[...] = jnp.zeros_like(hc_ref)

    x = x_ref[...].astype(f32)
    xpad_ref[0:8, :] = carry_ref[...]
    xpad_ref[8:8 + ts, :] = x
    carry_ref[...] = x[ts - 8:ts, :]
    w = cw_ref[...]
    xpad_ref[8:8 + ts, :] = (w[0:1, :] * xpad_ref[5:5 + ts, :] + w[1:2, :] * xpad_ref[6:6 + ts, :]
                             + w[2:3, :] * xpad_ref[7:7 + ts, :] + w[3:4, :] * x + cb_ref[...])
    nsp = _softplus(-lam_ref[...])
    for blk in range(LRU_BLOCKS):
        cs = slice(blk * LRU_BLOCK_DIM, (blk + 1) * LRU_BLOCK_DIM)
        xc = xpad_ref[8:8 + ts, cs]
        ri = jnp.dot(xc.astype(bf16), wax_ref[blk], preferred_element_type=f32)
        r = _sigmoid(ri[:, :LRU_BLOCK_DIM] + ba_ref[:, cs])
        i = _sigmoid(ri[:, LRU_BLOCK_DIM:] + bx_ref[:, cs])
        log_a = -LRU_C * r * nsp[:, cs]
        a_ref[:, cs] = jnp.exp(log_a)
        u_ref[:, cs] = jnp.sqrt(1.0 - jnp.exp(2.0 * log_a)) * (i * xc)

    rowi = lax.broadcasted_iota(i32, (8, D_MODEL), 0)

    def group(gi, h):
        r0 = pl.multiple_of(gi * 8, 8)
        a = a_ref[pl.ds(r0, 8), :]
        b = u_ref[pl.ds(r0, 8), :]
        for d in (1, 2, 4):
            m = rowi >= d
            a_s = pltpu.roll(a, d, 0)
            b_s = pltpu.roll(b, d, 0)
            b = jnp.where(m, a * b_s + b, b)
            a = jnp.where(m, a * a_s, a)
        hh = a * h + b
        h_ref[pl.ds(r0, 8), :] = hh
        return hh[7:8, :]

    hc_ref[...] = lax.fori_loop(0, ts // 8, group, hc_ref[...])
    gt = gate_ref[...].astype(f32)
    o_ref[...] = (h_ref[...] * jax.nn.gelu(gt)).astype(o_ref.dtype)


def lru_branch(fat, conv_w, conv_b, wax, b_a, b_x, lam, B, S, ts=512):
    T = B * S
    ns = S // ts
    row = lambda b, s: b * ns + s
    full = lambda shp: pl.BlockSpec(shp, lambda b, s: (0,) * len(shp))
    return pl.pallas_call(
        functools.partial(_lru_kernel, ts=ts),
        grid=(B, ns),
        in_specs=[pl.BlockSpec((ts, 1024), lambda b, s: (row(b, s), C_LX // 1024)),
                  pl.BlockSpec((ts, 1024), lambda b, s: (row(b, s), C_LG // 1024)),
                  full((CONV_WIDTH, 1024)), full((1, 1024)), full((LRU_BLOCKS, LRU_BLOCK_DIM, 2 * LRU_BLOCK_DIM)),
                  full((1, 1024)), full((1, 1024)), full((1, 1024))],
        out_specs=pl.BlockSpec((ts, 1024), lambda b, s: (row(b, s), 0)),
        out_shape=jax.ShapeDtypeStruct((T, 1024), bf16),
        scratch_shapes=[pltpu.VMEM((ts + 8, 1024), f32), pltpu.VMEM((ts, 1024), f32),
                        pltpu.VMEM((ts, 1024), f32), pltpu.VMEM((ts, 1024), f32),
                        pltpu.VMEM((8, 1024), f32), pltpu.VMEM((1, 1024), f32)],
        compiler_params=_params(("arbitrary", "arbitrary")),
        name="lru",
    )(fat, fat, conv_w, conv_b, wax, b_a, b_x, lam)


def _swa_kernel(q_ref, kc_ref, kp_ref, vc_ref, vp_ref, bias_ref, sink_ref, o_ref, kb_ref, vb_ref, *, tq):
    W = WINDOW
    hd = SWA_HEAD_DIM
    kb_ref[0:W, :] = kp_ref[...]
    kb_ref[W:W + tq, :] = kc_ref[...]
    vb_ref[0:W, :] = vp_ref[...]
    vb_ref[W:W + tq, :] = vc_ref[...]
    first_tile = pl.program_id(1) == 0
    col = lax.broadcasted_iota(i32, (W, 2 * W), 1)
    scale = hd ** -0.5

    def qblock(n, carry):
        r0 = pl.multiple_of(n * W, W)
        pen = jnp.where(jnp.logical_and(first_tile, n == 0), NEG_BIG, 0.0).astype(f32)
        penm = jnp.where(col < W, pen, 0.0)
        heads = range(SWA_Q_HEADS)
        kk = [kb_ref[pl.ds(r0, 2 * W), hk * hd:(hk + 1) * hd] for hk in range(SWA_KV_HEADS)]
        vv = [vb_ref[pl.ds(r0, 2 * W), hk * hd:(hk + 1) * hd] for hk in range(SWA_KV_HEADS)]
        qs = q_ref[pl.ds(r0, W), :] * scale
        s = [lax.dot_general(qs[:, h * hd:(h + 1) * hd], kk[h // SWA_GROUP], (((1,), (1,)), ((), ())),
                             preferred_element_type=f32) + (bias_ref[h] + penm) for h in heads]
        sink = [sink_ref[h:h + 1, 0:1] for h in heads]
        m = [jnp.maximum(jnp.max(s[h], axis=-1, keepdims=True), sink[h]) for h in heads]
        pe = [jnp.exp(s[h] - m[h]) for h in heads]
        o = [jnp.dot(pe[h].astype(bf16), vv[h // SWA_GROUP], preferred_element_type=f32) for h in heads]
        den = [jnp.sum(pe[h], axis=-1, keepdims=True) + jnp.exp(sink[h] - m[h]) for h in heads]
        o_ref[pl.ds(r0, W), :] = jnp.concatenate([o[h] / den[h] for h in heads], axis=1).astype(o_ref.dtype)
        return carry

    lax.fori_loop(0, tq // W, qblock, 0)


def swa_branch(fat, bias_tab, sinks_tab, B, S, tq=512):
    T = B * S
    ns = S // tq
    nb = S // WINDOW
    per = tq // WINDOW
    row = lambda b, s: b * ns + s
    prev = lambda b, s: b * nb + jnp.maximum(s * per - 1, 0)
    kvw = SWA_KV_HEADS * SWA_HEAD_DIM
    full = lambda shp: pl.BlockSpec(shp, lambda b, s: (0,) * len(shp))
    return pl.pallas_call(
        functools.partial(_swa_kernel, tq=tq),
        grid=(B, ns),
        in_specs=[pl.BlockSpec((tq, 1024), lambda b, s: (row(b, s), C_SQ // 1024)),
                  pl.BlockSpec((tq, kvw), lambda b, s: (row(b, s), C_SK // kvw)),
                  pl.BlockSpec((WINDOW, kvw), lambda b, s: (prev(b, s), C_SK // kvw)),
                  pl.BlockSpec((tq, kvw), lambda b, s: (row(b, s), C_SV // kvw)),
                  pl.BlockSpec((WINDOW, kvw), lambda b, s: (prev(b, s), C_SV // kvw)),
                  full((SWA_Q_HEADS, WINDOW, 2 * WINDOW)), full((SWA_Q_HEADS, LANES))],
        out_specs=pl.BlockSpec((tq, 1024), lambda b, s: (row(b, s), 0)),
        out_shape=jax.ShapeDtypeStruct((T, 1024), bf16),
        scratch_shapes=[pltpu.VMEM((tq + WINDOW, kvw), bf16), pltpu.VMEM((tq + WINDOW, kvw), bf16)],
        compiler_params=_params(("arbitrary", "arbitrary")),
        name="swa",
    )(fat, fat, fat, fat, fat, bias_tab, sinks_tab)


def _merge_kernel(oa_ref, ob_ref, oc_ref, ga_ref, gb_ref, gc_ref, x_ref, wa_ref, wb_ref, wc_ref, wo_ref,
                  g_ref, b_ref, o_ref, *, alpha):
    ya = jnp.dot(oa_ref[...], wa_ref[...], preferred_element_type=f32)
    yb = jnp.dot(ob_ref[...], wb_ref[...], preferred_element_type=f32)
    yc = jnp.dot(oc_ref[...], wc_ref[...], preferred_element_type=f32)
    mix = (_sigmoid(ga_ref[...].astype(f32)) * ya + _sigmoid(gb_ref[...].astype(f32)) * yb
           + _sigmoid(gc_ref[...].astype(f32)) * yc)
    y = jnp.dot(mix.astype(bf16), wo_ref[...], preferred_element_type=f32)
    o_ref[...] = _layer_norm(alpha * x_ref[...] + y, g_ref[...], b_ref[...])


def merge_ln(oa, ob, oc, fat, x2d, wa, wb, wc, wo, g, b, alpha, tm=512):
    T = x2d.shape[0]
    act = pl.BlockSpec((tm, 1024), lambda i: (i, 0))
    fatb = lambda cb: pl.BlockSpec((tm, 1024), lambda i: (i, cb))
    wsp = pl.BlockSpec((1024, 1024), lambda i: (0, 0))
    vec = pl.BlockSpec((1, 1024), lambda i: (0, 0))
    return pl.pallas_call(
        functools.partial(_merge_kernel, alpha=alpha),
        grid=(T // tm,),
        in_specs=[act, act, act, fatb(C_MA // 1024), fatb(C_MB // 1024), fatb(C_MC // 1024), act,
                  wsp, wsp, wsp, wsp, vec, vec],
        out_specs=act,
        out_shape=jax.ShapeDtypeStruct((T, 1024), f32),
        compiler_params=_params(("arbitrary",)),
        name="merge_ln",
    )(oa, ob, oc, fat, fat, fat, x2d, wa, wb, wc, wo, g, b)


def _route_kernel(x_ref, rw_ref, rb_ref, gates_ref, eidx_ref, rank_ref, cnt_ref, run_ref, *, tm):
    @pl.when(pl.program_id(0) == 0)
    def _():
        run_ref[...] = jnp.zeros_like(run_ref)

    logits = jnp.dot(x_ref[...], rw_ref[...], preferred_element_type=f32,
                     precision=lax.Precision.HIGHEST) + rb_ref[...]
    lane = lax.broadcasted_iota(i32, (tm, LANES), 1)
    lane_f = lane.astype(f32)
    work = logits
    vals, idxs, hots = [], [], []
    for _ in range(TOP_K):
        m = jnp.max(work, axis=-1, keepdims=True)
        idx = jnp.min(jnp.where(work == m, lane_f, float(LANES)), axis=-1, keepdims=True)
        hot = lane_f == idx
        vals.append(m)
        idxs.append(idx)
        hots.append(hot)
        work = jnp.where(hot, -jnp.inf, work)
    es = [jnp.exp(v - vals[0]) for v in vals]
    den = es[0] + es[1] + es[2] + es[3]
    sel = jnp.zeros((tm, LANES), f32)
    for hot in hots:
        sel = sel + jnp.where(hot, 1.0, 0.0)
    ri = lax.broadcasted_iota(i32, (tm, tm), 0)
    ci = lax.broadcasted_iota(i32, (tm, tm), 1)
    tril = jnp.where(ri > ci, 1.0, 0.0).astype(bf16)
    before = jnp.dot(tril, sel.astype(bf16), preferred_element_type=f32) + run_ref[...]
    run_ref[...] = run_ref[...] + jnp.sum(sel, axis=0, keepdims=True)
    cnt_ref[...] = run_ref[...]
    gates = jnp.zeros((tm, LANES), f32)
    eidx = jnp.zeros((tm, LANES), f32)
    rank = jnp.zeros((tm, LANES), f32)
    for k in range(TOP_K):
        rk = jnp.sum(jnp.where(hots[k], before, 0.0), axis=-1, keepdims=True)
        gates = jnp.where(lane == k, es[k] / den, gates)
        eidx = jnp.where(lane == k, idxs[k], eidx)
        rank = jnp.where(lane == k, rk, rank)
    gates_ref[...] = gates
    eidx_ref[...] = eidx.astype(i32)
    rank_ref[...] = rank.astype(i32)


def route(x2d, rw_pad, rb_pad, tm=512):
    T = x2d.shape[0]
    outb = pl.BlockSpec((tm, LANES), lambda i: (i, 0))
    return pl.pallas_call(
        functools.partial(_route_kernel, tm=tm),
        grid=(T // tm,),
        in_specs=[pl.BlockSpec((tm, 1024), lambda i: (i, 0)),
                  pl.BlockSpec((1024, LANES), lambda i: (0, 0)),
                  pl.BlockSpec((1, LANES), lambda i: (0, 0))],
        out_specs=[outb, outb, outb, pl.BlockSpec((1, LANES), lambda i: (0, 0))],
        out_shape=[jax.ShapeDtypeStruct((T, LANES), f32), jax.ShapeDtypeStruct((T, LANES), i32),
                   jax.ShapeDtypeStruct((T, LANES), i32), jax.ShapeDtypeStruct((1, LANES), f32)],
        scratch_shapes=[pltpu.VMEM((1, LANES), f32)],
        compiler_params=_params(("arbitrary",)),
        name="route",
    )(x2d, rw_pad, rb_pad)


ROW_UNROLL = 8


def _row_copy(src_ref, src_row, dst_ref, dst_row, sem):
    return pltpu.make_async_copy(src_ref.at[pl.ds(src_row, 1)], dst_ref.at[pl.ds(dst_row, 1)], sem)


def _dispatch_kernel(dest_ref, x_ref, buf_in_ref, buf_ref, sem, *, tm):
    del buf_in_ref

    def issue(g, carry):
        for j in range(ROW_UNROLL):
            r = g * ROW_UNROLL + j
            for k in range(TOP_K):
                _row_copy(x_ref, r, buf_ref, dest_ref[r * TOP_K + k], sem).start()
        return carry

    lax.fori_loop(0, tm // ROW_UNROLL, issue, 0)

    def drain(g, carry):
        for j in range(ROW_UNROLL * TOP_K):
            _row_copy(x_ref, 0, buf_ref, 0, sem).wait()
        return carry

    lax.fori_loop(0, tm // ROW_UNROLL, drain, 0)


def dispatch(dest_flat, x2d, buf0, tm=512):
    T = x2d.shape[0]
    return pl.pallas_call(
        functools.partial(_dispatch_kernel, tm=tm),
        grid=(T // tm,),
        in_specs=[pl.BlockSpec((tm * TOP_K,), lambda i: (i,), memory_space=pltpu.SMEM),
                  pl.BlockSpec((tm, 1024), lambda i: (i, 0)),
                  pl.BlockSpec(memory_space=pl.ANY)],
        out_specs=pl.BlockSpec(memory_space=pl.ANY),
        out_shape=jax.ShapeDtypeStruct(buf0.shape, buf0.dtype),
        scratch_shapes=[pltpu.SemaphoreType.DMA(())],
        input_output_aliases={2: 0},
        compiler_params=_params(("arbitrary",)),
        name="dispatch",
    )(dest_flat, x2d, buf0)


GU_GROUP = 2 * LANES


def _gu_prep_kernel(w_ref, o_ref):
    ri = lax.broadcasted_iota(i32, (GU_GROUP, GU_GROUP), 0)
    ci = lax.broadcasted_iota(i32, (GU_GROUP, GU_GROUP), 1)
    src = jnp.where(ci < LANES, 2 * ci, 2 * (ci - LANES) + 1)
    perm = jnp.where(ri == src, 1.0, 0.0).astype(bf16)
    for g in range(w_ref.shape[2] // GU_GROUP):
        cs = slice(g * GU_GROUP, (g + 1) * GU_GROUP)
        o_ref[0, :, cs] = jnp.dot(w_ref[0, :, cs].astype(bf16), perm, preferred_element_type=f32).astype(bf16)


def gu_prep(w_gu, tk=512):
    E, D, N = w_gu.shape
    return pl.pallas_call(
        _gu_prep_kernel,
        grid=(E, D // tk),
        in_specs=[pl.BlockSpec((1, tk, N), lambda e, k: (e, k, 0))],
        out_specs=pl.BlockSpec((1, tk, N), lambda e, k: (e, k, 0)),
        out_shape=jax.ShapeDtypeStruct((E, D, N), bf16),
        compiler_params=_params(("arbitrary", "arbitrary")),
        name="gu_prep",
    )(w_gu)


def _expert_kernel(blk_e_ref, nused_ref, x_ref, wgu_ref, bgu_ref, wd_ref, bd_ref, o_ref):
    i = pl.program_id(0)

    @pl.when(i < nused_ref[0])
    def _():
        de = wd_ref.shape[1]
        xb = x_ref[...].astype(bf16)
        hgu = jnp.dot(xb, wgu_ref[0], preferred_element_type=f32) + bgu_ref[0]
        acts = []
        for g in range(2 * de // GU_GROUP):
            gate = jnp.minimum(hgu[:, g * GU_GROUP:g * GU_GROUP + LANES], SWIGLU_LIMIT)
            lin = jnp.clip(hgu[:, g * GU_GROUP + LANES:(g + 1) * GU_GROUP], -SWIGLU_LIMIT, SWIGLU_LIMIT)
            acts.append((gate * _sigmoid(SWIGLU_ALPHA * gate) * (lin + 1.0)).astype(bf16))
        act = jnp.concatenate(acts, axis=1)
        o_ref[...] = jnp.dot(act, wd_ref[0], preferred_element_type=f32) + bd_ref[0]

    @pl.when(i >= nused_ref[0])
    def _():
        o_ref[...] = jnp.zeros_like(o_ref)


def experts(blk_e, nused, buf, wgu, bgu, wd, bd):
    P, D = buf.shape
    nblk = P // EXPERT_BLK
    de = wd.shape[1]
    live = lambda i, be, nu: jnp.minimum(i, nu[0] - 1)
    grid_spec = pltpu.PrefetchScalarGridSpec(
        num_scalar_prefetch=2,
        grid=(nblk,),
        in_specs=[pl.BlockSpec((EXPERT_BLK, D), lambda i, be, nu: (live(i, be, nu), 0)),
                  pl.BlockSpec((1, D, 2 * de), lambda i, be, nu: (be[i], 0, 0)),
                  pl.BlockSpec((1, 1, 2 * de), lambda i, be, nu: (be[i], 0, 0)),
                  pl.BlockSpec((1, de, D), lambda i, be, nu: (be[i], 0, 0)),
                  pl.BlockSpec((1, 1, D), lambda i, be, nu: (be[i], 0, 0))],
        out_specs=pl.BlockSpec((EXPERT_BLK, D), lambda i, be, nu: (i, 0)),
    )
    return pl.pallas_call(
        _expert_kernel,
        grid_spec=grid_spec,
        out_shape=jax.ShapeDtypeStruct((P, D), f32),
        compiler_params=_params(("arbitrary",)),
        name="experts",
    )(blk_e, nused, buf, wgu, bgu, wd, bd)


def _combine_kernel(dest_ref, gates_ref, x_ref, p_ref, obuf_ref, wg_ref, wp_ref,
                    g2_ref, b2_ref, g3_ref, b3_ref, o_ref, rows_ref, sem, *, tm, alpha):
    def issue(g, carry):
        for j in range(ROW_UNROLL):
            r = g * ROW_UNROLL + j
            for k in range(TOP_K):
                _row_copy(obuf_ref, dest_ref[r * TOP_K + k], rows_ref.at[k], r, sem).start()
        return carry

    lax.fori_loop(0, tm // ROW_UNROLL, issue, 0)

    def drain(g, carry):
        for j in range(ROW_UNROLL * TOP_K):
            _row_copy(obuf_ref, 0, rows_ref.at[0], 0, sem).wait()
        return carry

    lax.fori_loop(0, tm // ROW_UNROLL, drain, 0)

    gates = gates_ref[...]
    y = gates[:, 0:1] * rows_ref[0]
    for k in range(1, TOP_K):
        y = y + gates[:, k:k + 1] * rows_ref[k]
    x2 = _layer_norm(alpha * x_ref[...] + y, g2_ref[...], b2_ref[...])
    gate = _sigmoid(jnp.dot(x2.astype(bf16), wg_ref[...], preferred_element_type=f32))
    proj = jnp.dot(p_ref[...].astype(bf16), wp_ref[...], preferred_element_type=f32)
    o_ref[...] = _layer_norm(alpha * x2 + gate * proj, g3_ref[...], b3_ref[...])


def combine_ple(dest_flat, gates, x2d, p2d, obuf, wg, wp, g2, b2, g3, b3, alpha, tm=256):
    T = x2d.shape[0]
    act = pl.BlockSpec((tm, 1024), lambda i: (i, 0))
    vec = pl.BlockSpec((1, 1024), lambda i: (0, 0))
    return pl.pallas_call(
        functools.partial(_combine_kernel, tm=tm, alpha=alpha),
        grid=(T // tm,),
        in_specs=[pl.BlockSpec((tm * TOP_K,), lambda i: (i,), memory_space=pltpu.SMEM),
                  pl.BlockSpec((tm, LANES), lambda i: (i, 0)),
                  act,
                  pl.BlockSpec((tm, PLE_DIM), lambda i: (i, 0)),
                  pl.BlockSpec(memory_space=pl.ANY),
                  pl.BlockSpec((1024, 1024), lambda i: (0, 0)),
                  pl.BlockSpec((PLE_DIM, 1024), lambda i: (0, 0)),
                  vec, vec, vec, vec],
        out_specs=act,
        out_shape=jax.ShapeDtypeStruct((T, 1024), f32),
        scratch_shapes=[pltpu.VMEM((TOP_K, tm, 1024), f32), pltpu.SemaphoreType.DMA(())],
        compiler_params=_params(("arbitrary",)),
        name="combine_ple",
    )(dest_flat, gates, x2d, p2d, obuf, wg, wp, g2, b2, g3, b3)


def _t5_bucket_np(dist):
    max_exact = REL_BUCKETS // 2
    d = np.maximum(dist.astype(np.float32), np.float32(1.0))
    large = max_exact + (np.log(d / np.float32(max_exact)) / np.float32(math.log(REL_MAX_DISTANCE / max_exact))
                         * np.float32(REL_BUCKETS - max_exact)).astype(np.int32)
    large = np.minimum(large, REL_BUCKETS - 1)
    return np.where(dist < max_exact, dist, large)


def _swa_bias_table(rel_bias):
    kj = np.arange(2 * WINDOW)[None, :]
    dist = (np.arange(WINDOW)[:, None] + WINDOW) - kj
    in_window = (dist >= 0) & (dist < WINDOW)
    bucket = _t5_bucket_np(np.maximum(dist, 0))
    bias = jnp.transpose(rel_bias[bucket].astype(f32), (2, 0, 1))
    return jnp.where(jnp.asarray(in_window)[None], bias, NEG_BIG)


def _pad_row(v, width=LANES, fill=0.0):
    v = v.astype(f32).reshape(1, -1)
    return jnp.pad(v, ((0, 0), (0, width - v.shape[1])), constant_values=fill)


def _wcat(w_in):
    cols = [w_in[:, 0:4096], w_in[:, 4112:6160], w_in[:, 6160:7184], w_in[:, 7696:10768],
            w_in[:, 7184:7440], w_in[:, 7440:7696], w_in[:, 4096:4112],
            jnp.zeros((w_in.shape[0], FAT_W - 10768), w_in.dtype)]
    return jnp.concatenate(cols, axis=1).astype(bf16)


def kernel(x, p, w_in, conv_qkv_w, gdn_a_log, gdn_dt_bias, gdn_norm_w, rg_conv_w, rg_conv_b, rg_w_a, rg_b_a, rg_w_x, rg_b_x, rg_lambda, attn_sinks, rel_bias, w_o_gdn, w_o_lru, w_o_swa, w_out, ln1_g, ln1_b, router_w, router_b, w_gu, b_gu, w_down, b_down, ln2_g, ln2_b, ple_w_gate, ple_w_proj, ln3_g, ln3_b):
    B, S, D = x.shape
    depth = w_in.shape[0]
    T = B * S
    A = T * TOP_K
    alpha = (2.0 * depth) ** 0.25
    P = A + N_EXPERTS * EXPERT_BLK
    nblk = P // EXPERT_BLK
    row = lambda v: v.astype(f32).reshape(1, -1)

    bias_tab = _swa_bias_table(rel_bias)
    xc = x.reshape(T, D)
    for i in range(depth):
        fat = inproj(xc, _wcat(w_in[i]))
        o_gdn = gdn_branch(fat, conv_qkv_w[i], _pad_row(gdn_a_log[i]), _pad_row(gdn_dt_bias[i]),
                           row(gdn_norm_w[i]), B, S)
        wax = jnp.concatenate([rg_w_a[i], rg_w_x[i]], axis=-1).astype(bf16)
        o_lru = lru_branch(fat, rg_conv_w[i], row(rg_conv_b[i]), wax, row(rg_b_a[i]), row(rg_b_x[i]),
                           row(rg_lambda[i]), B, S)
        sinks_tab = jnp.broadcast_to(attn_sinks[i].astype(f32)[:, None], (SWA_Q_HEADS, LANES))
        o_swa = swa_branch(fat, bias_tab, sinks_tab, B, S)
        x1 = merge_ln(o_gdn, o_lru, o_swa, fat, xc, w_o_gdn[i].astype(bf16), w_o_lru[i].astype(bf16),
                      w_o_swa[i].astype(bf16), w_out[i].astype(bf16), row(ln1_g[i]), row(ln1_b[i]), alpha)

        rw_pad = jnp.pad(router_w[i], ((0, 0), (0, LANES - N_EXPERTS)))
        rb_pad = _pad_row(router_b[i], fill=NEG_BIG)
        gates, eidx, rank, cnt = route(x1, rw_pad, rb_pad)
        counts = cnt[0, :N_EXPERTS].astype(i32)
        padded = ((counts + EXPERT_BLK - 1) // EXPERT_BLK) * EXPERT_BLK
        pad_ends = jnp.cumsum(padded)
        pad_starts = pad_ends - padded
        dest = (pad_starts[eidx[:, :TOP_K]] + rank[:, :TOP_K]).reshape(A)
        blk_start = jnp.arange(nblk, dtype=i32) * EXPERT_BLK
        blk_e = jnp.minimum(jnp.sum((pad_ends[None, :] <= blk_start[:, None]).astype(i32), axis=1),
                            N_EXPERTS - 1).astype(i32)
        nused = (pad_ends[-1:] // EXPERT_BLK).astype(i32)

        buf = dispatch(dest, x1, jnp.zeros((P, D), f32))
        bgu = jnp.transpose(b_gu[i].reshape(N_EXPERTS, -1, LANES, 2), (0, 1, 3, 2)).reshape(N_EXPERTS, 1, -1)
        obuf = experts(blk_e, nused, buf, gu_prep(w_gu[i]), bgu, w_down[i].astype(bf16), b_down[i][:, None, :])
        xc = combine_ple(dest, gates, x1, p[i].reshape(T, PLE_DIM), obuf, ple_w_gate[i].astype(bf16),
                         ple_w_proj[i].astype(bf16), row(ln2_g[i]), row(ln2_b[i]), row(ln3_g[i]), row(ln3_b[i]),
                         alpha)
    return xc.reshape(B, S, D)
```

```python
import functools
import math

import numpy as np
import jax
import jax.numpy as jnp
from jax import lax
from jax.experimental import pallas as pl
from jax.experimental.pallas import tpu as pltpu

f32 = jnp.float32
bf16 = jnp.bfloat16
i32 = jnp.int32

D_MODEL = 1024
PLE_DIM = 256
GDN_HEADS = 8
GDN_HEAD_DIM = 128
GDN_CHUNK = 64
CONV_WIDTH = 4
LRU_BLOCKS = 8
LRU_BLOCK_DIM = 128
LRU_C = 8.0
SWA_Q_HEADS = 16
SWA_KV_HEADS = 4
SWA_HEAD_DIM = 64
SWA_GROUP = 4
WINDOW = 128
REL_BUCKETS = 32
REL_MAX_DISTANCE = 128
N_EXPERTS = 32
TOP_K = 4
SWIGLU_LIMIT = 7.0
SWIGLU_ALPHA = 1.702
LN_EPS = 1e-5
NORM_EPS = 1e-6
NEG_BIG = -1e30

LANES = 128
VMEM_LIMIT = 56 * 1024 * 1024

C_GQ, C_GK, C_GV, C_GZ = 0, 1024, 2048, 3072
C_LX, C_LG = 4096, 5120
C_SQ = 6144
C_MA, C_MB, C_MC = 7168, 8192, 9216
C_SK, C_SV = 10240, 10496
C_AB = 10752
FAT_W = 10880
FAT_TN = 2176

EXPERT_BLK = 512
CHUNKS_IN_FLIGHT = 2


def _params(sem):
    return pltpu.CompilerParams(dimension_semantics=sem, vmem_limit_bytes=VMEM_LIMIT)


def _sigmoid(x):
    return 1.0 / (1.0 + jnp.exp(-x))


def _softplus(x):
    return jnp.maximum(x, 0.0) + jnp.log(1.0 + jnp.exp(-jnp.abs(x)))


def _layer_norm(z, g, b):
    mu = jnp.mean(z, axis=-1, keepdims=True)
    zc = z - mu
    var = jnp.mean(zc * zc, axis=-1, keepdims=True)
    return zc * lax.rsqrt(var + LN_EPS) * g + b


def _inproj_kernel(x_ref, w_ref, o_ref, xb_ref):
    @pl.when(pl.program_id(1) == 0)
    def _():
        xb_ref[...] = x_ref[...].astype(bf16)

    o_ref[...] = jnp.dot(xb_ref[...], w_ref[...], preferred_element_type=f32).astype(o_ref.dtype)


def inproj(x2d, wcat, tm=1024):
    T, K = x2d.shape
    N = wcat.shape[1]
    tn = FAT_TN
    return pl.pallas_call(
        _inproj_kernel,
        grid=(T // tm, N // tn),
        in_specs=[pl.BlockSpec((tm, K), lambda i, j: (i, 0)),
                  pl.BlockSpec((K, tn), lambda i, j: (0, j))],
        out_specs=pl.BlockSpec((tm, tn), lambda i, j: (i, j)),
        out_shape=jax.ShapeDtypeStruct((T, N), bf16),
        scratch_shapes=[pltpu.VMEM((tm, K), bf16)],
        compiler_params=_params(("arbitrary", "arbitrary")),
        name="inproj",
    )(x2d, wcat)


def _shift_matrices(ts):
    ri = lax.broadcasted_iota(i32, (ts, ts), 0)
    ci = lax.broadcasted_iota(i32, (ts, ts), 1)
    return [jnp.where(ri - ci == d, 1.0, 0.0).astype(bf16) for d in (3, 2, 1)]


def _causal_conv_silu(src_ref, dst_ref, cz_ref, carry_ref, w, smats, ts, head_scale=None):
    xb = src_ref[...]
    x = xb.astype(f32)
    y = w[3:4, :] * x
    for j, sm in enumerate(smats):
        y = y + w[j:j + 1, :] * jnp.dot(sm, xb, preferred_element_type=f32)
    cz_ref[0:8, :] = carry_ref[...]
    cz_ref[8:16, :] = jnp.zeros((8, x.shape[1]), f32)
    corr = w[0:1, :] * cz_ref[5:13, :] + w[1:2, :] * cz_ref[6:14, :] + w[2:3, :] * cz_ref[7:15, :]
    carry_ref[...] = x[ts - 8:ts, :]

    def post(rows, yv):
        a = yv * _sigmoid(yv)
        if head_scale is None:
            dst_ref[rows, :] = a
            return
        for h in range(a.shape[1] // LANES):
            cs = slice(h * LANES, (h + 1) * LANES)
            ah = a[:, cs]
            dst_ref[rows, cs] = ah * (lax.rsqrt(jnp.sum(ah * ah, axis=-1, keepdims=True) + NORM_EPS) * head_scale)

    post(slice(0, 8), y[0:8] + corr)
    post(slice(8, ts), y[8:ts])


def _gdn_kernel(q_ref, k_ref, v_ref, z_ref, ab_ref, cw_ref, alog_ref, dtb_ref, nw_ref, o_ref,
                xpad_ref, qs_ref, ks_ref, vs_ref, carry_ref, state_ref, g_ref, beta_ref,
                u_s, lhs_s, intra_s, kdt_s, *, ts):
    C = GDN_CHUNK
    D = GDN_HEAD_DIM
    P = 2 * C

    @pl.when(pl.program_id(1) == 0)
    def _():
        carry_ref[...] = jnp.zeros_like(carry_ref)
        state_ref[...] = jnp.zeros_like(state_ref)

    smats = _shift_matrices(ts)
    for p, (src, dst, scale) in enumerate(((q_ref, qs_ref, D ** -0.5), (k_ref, ks_ref, 1.0), (v_ref, vs_ref, None))):
        _causal_conv_silu(src, dst, xpad_ref, carry_ref.at[p], cw_ref[:, p * 1024:(p + 1) * 1024], smats, ts, scale)

    ab = ab_ref[...].astype(f32)
    g = -jnp.exp(alog_ref[...]) * _softplus(ab + dtb_ref[...])
    rin = lax.broadcasted_iota(i32, (ts, LANES), 0) & (C - 1)
    gsum = g
    for d in (1, 2, 4, 8, 16, 32):
        gsum = gsum + jnp.where(rin >= d, pltpu.roll(gsum, d, 0), 0.0)
    g_ref[...] = gsum
    beta_ref[...] = _sigmoid(ab)

    ri = lax.broadcasted_iota(i32, (P, P), 0)
    ci = lax.broadcasted_iota(i32, (P, P), 1)
    same = (ri >= C) == (ci >= C)
    eye = ri == ci
    causal = same & (ri >= ci)
    strict = same & (ri > ci)
    eye_f = jnp.where(eye, 1.0, 0.0).astype(f32)
    first_cols = ci < C
    nw = nw_ref[...]

    def stack(a, b):
        return jnp.concatenate([a, b], axis=0)

    def mm(a, b):
        return jnp.dot(a, b, preferred_element_type=f32)

    npair = GDN_HEADS // 2
    nchunk = ts // C
    hcols = [slice(h * D, (h + 1) * D) for h in range(GDN_HEADS)]

    for cg in range(0, nchunk, CHUNKS_IN_FLIGHT):
        probs = [(c, hp) for c in range(cg, cg + CHUNKS_IN_FLIGHT) for hp in range(npair)]
        qn, kn, vb, gcol, eg, egl, kb = [], [], [], [], [], [], []
        for c, hp in probs:
            rows = slice(c * C, (c + 1) * C)
            c0, c1 = hcols[2 * hp], hcols[2 * hp + 1]
            gc = g_ref[rows, :]
            bc = beta_ref[rows, :]
            qn.append(stack(qs_ref[rows, c0], qs_ref[rows, c1]))
            kn.append(stack(ks_ref[rows, c0], ks_ref[rows, c1]))
            v2 = stack(vs_ref[rows, c0], vs_ref[rows, c1])
            h0, h1 = 2 * hp, 2 * hp + 1
            gcl = stack(gc[:, h0:h0 + 1], gc[:, h1:h1 + 1])
            bcl = stack(bc[:, 8 + h0:9 + h0], bc[:, 8 + h1:9 + h1])
            glast = stack(jnp.broadcast_to(gc[C - 1:C, h0:h0 + 1], (C, 1)),
                          jnp.broadcast_to(gc[C - 1:C, h1:h1 + 1], (C, 1)))
            gcol.append(gcl)
            eg.append(jnp.exp(gcl))
            egl.append(jnp.exp(glast - gcl))
            kb.append(kn[-1] * bcl)
            vb.append(v2 * bcl)
        n = len(probs)
        a2 = [lax.dot_general(stack(kb[i], qn[i]).astype(bf16), kn[i].astype(bf16), (((1,), (1,)), ((), ())),
                              preferred_element_type=f32) for i in range(n)]
        lmat, intra = [], []
        for i in range(n):
            gm = jnp.broadcast_to(gcol[i], (P, P))
            grow = jnp.sum(jnp.where(eye, gm, 0.0), axis=0, keepdims=True)
            decay = jnp.where(causal, jnp.exp(jnp.minimum(gm - grow, 0.0)), 0.0)
            lmat.append(jnp.where(strict, a2[i][:P] * decay, 0.0))
            intra.append(a2[i][P:] * decay)
        lb = [l.astype(bf16) for l in lmat]
        xm = [eye_f - l for l in lmat]
        pm = [mm(b, b) for b in lb]
        for it in range(5):
            pb = [p.astype(bf16) for p in pm]
            xm = [x + mm(x.astype(bf16), b) for x, b in zip(xm, pb)]
            if it < 4:
                pm = [mm(b, b) for b in pb]
        uw = [mm(xm[i].astype(bf16), jnp.concatenate([vb[i], kb[i] * eg[i]], axis=1).astype(bf16))
              for i in range(n)]
        for i, (c, hp) in enumerate(probs):
            j = c * npair + hp
            w2 = uw[i][:, D:]
            qd = qn[i] * eg[i]
            kdt = (kn[i] * egl[i]).T
            u_s[j] = uw[i][:, :D]
            lhs_s[j, 0] = stack(w2[:C], qd[:C]).astype(bf16)
            lhs_s[j, 1] = stack(w2[C:], qd[C:]).astype(bf16)
            intra_s[j] = intra[i].astype(bf16)
            kdt_s[j, 0] = jnp.where(first_cols, kdt, 0.0).astype(bf16)
            kdt_s[j, 1] = jnp.where(first_cols, 0.0, kdt).astype(bf16)

    for c in range(nchunk):
        rows = slice(c * C, (c + 1) * C)
        gl = g_ref[(c + 1) * C - 1:(c + 1) * C, :]
        st = [state_ref[h] for h in range(GDN_HEADS)]
        wq = [mm(lhs_s[c * npair + h // 2, h % 2], st[h].astype(bf16)) for h in range(GDN_HEADS)]
        vnb = [(u_s[c * npair + hp] - stack(wq[2 * hp][:C], wq[2 * hp + 1][:C])).astype(bf16)
               for hp in range(npair)]
        o2 = [stack(wq[2 * hp][C:], wq[2 * hp + 1][C:]) + mm(intra_s[c * npair + hp], vnb[hp])
              for hp in range(npair)]
        for h in range(GDN_HEADS):
            state_ref[h] = st[h] * jnp.exp(gl[:, h:h + 1]) + mm(kdt_s[c * npair + h // 2, h % 2], vnb[h // 2])
        for hp in range(npair):
            c0, c1 = hcols[2 * hp], hcols[2 * hp + 1]
            z2 = stack(z_ref[rows, c0], z_ref[rows, c1]).astype(f32)
            on = (o2[hp] * lax.rsqrt(jnp.mean(o2[hp] * o2[hp], axis=-1, keepdims=True) + NORM_EPS) * nw
                  * (z2 * _sigmoid(z2))).astype(o_ref.dtype)
            o_ref[rows, c0] = on[:C]
            o_ref[rows, c1] = on[C:]


def gdn_branch(fat, conv_w, a_log_row, dt_bias_row, norm_w_row, B, S, ts=256):
    T = B * S
    ns = S // ts
    nprob = (ts // GDN_CHUNK) * (GDN_HEADS // 2)
    row = lambda b, s: b * ns + s
    blk = lambda cb: pl.BlockSpec((ts, 1024), lambda b, s: (row(b, s), cb))
    full = lambda shp: pl.BlockSpec(shp, lambda b, s: (0,) * len(shp))
    return pl.pallas_call(
        functools.partial(_gdn_kernel, ts=ts),
        grid=(B, ns),
        in_specs=[blk(C_GQ // 1024), blk(C_GK // 1024), blk(C_GV // 1024), blk(C_GZ // 1024),
                  pl.BlockSpec((ts, LANES), lambda b, s: (row(b, s), C_AB // LANES)),
                  full((CONV_WIDTH, 3072)), full((1, LANES)), full((1, LANES)), full((1, LANES))],
        out_specs=pl.BlockSpec((ts, 1024), lambda b, s: (row(b, s), 0)),
        out_shape=jax.ShapeDtypeStruct((T, 1024), bf16),
        scratch_shapes=[pltpu.VMEM((16, 1024), f32),
                        pltpu.VMEM((ts, 1024), f32), pltpu.VMEM((ts, 1024), f32), pltpu.VMEM((ts, 1024), f32),
                        pltpu.VMEM((3, 8, 1024), f32),
                        pltpu.VMEM((GDN_HEADS, GDN_HEAD_DIM, GDN_HEAD_DIM), f32),
                        pltpu.VMEM((ts, LANES), f32), pltpu.VMEM((ts, LANES), f32),
                        pltpu.VMEM((nprob, 128, GDN_HEAD_DIM), f32),
                        pltpu.VMEM((nprob, 2, 128, GDN_HEAD_DIM), bf16),
                        pltpu.VMEM((nprob, 128, 128), bf16),
                        pltpu.VMEM((nprob, 2, GDN_HEAD_DIM, 128), bf16)],
        compiler_params=_params(("arbitrary", "arbitrary")),
        name="gdn",
    )(fat, fat, fat, fat, fat, conv_w, a_log_row, dt_bias_row, norm_w_row)


def _lru_kernel(x_ref, gate_ref, cw_ref, cb_ref, wax_ref, ba_ref, bx_ref, lam_ref, o_ref,
                xpad_ref, a_ref, u_ref, h_ref, carry_ref, hc_ref, *, ts):
    @pl.when(pl.program_id(1) == 0)
    def _():
        carry_ref[...] = jnp.zeros_like(carry_ref)
        hc_ref[...] = jnp.zeros_like(hc_ref)

    x = x_ref[...].astype(f32)
    xpad_ref[0:8, :] = carry_ref[...]
    xpad_ref[8:8 + ts, :] = x
    carry_ref[...] = x[ts - 8:ts, :]
    w = cw_ref[...]
    xpad_ref[8:8 + ts, :] = (w[0:1, :] * xpad_ref[5:5 + ts, :] + w[1:2, :] * xpad_ref[6:6 + ts, :]
                             + w[2:3, :] * xpad_ref[7:7 + ts, :] + w[3:4, :] * x + cb_ref[...])
    nsp = _softplus(-lam_ref[...])
    for blk in range(LRU_BLOCKS):
        cs = slice(blk * LRU_BLOCK_DIM, (blk + 1) * LRU_BLOCK_DIM)
        xc = xpad_ref[8:8 + ts, cs]
        ri = jnp.dot(xc.astype(bf16), wax_ref[blk], preferred_element_type=f32)
        r = _sigmoid(ri[:, :LRU_BLOCK_DIM] + ba_ref[:, cs])
        i = _sigmoid(ri[:, LRU_BLOCK_DIM:] + bx_ref[:, cs])
        log_a = -LRU_C * r * nsp[:, cs]
        a_ref[:, cs] = jnp.exp(log_a)
        u_ref[:, cs] = jnp.sqrt(1.0 - jnp.exp(2.0 * log_a)) * (i * xc)

    rowi = lax.broadcasted_iota(i32, (8, D_MODEL), 0)

    def group(gi, h):
        r0 = pl.multiple_of(gi * 8, 8)
        a = a_ref[pl.ds(r0, 8), :]
        b = u_ref[pl.ds(r0, 8), :]
        for d in (1, 2, 4):
            m = rowi >= d
            a_s = pltpu.roll(a, d, 0)
            b_s = pltpu.roll(b, d, 0)
            b = jnp.where(m, a * b_s + b, b)
            a = jnp.where(m, a * a_s, a)
        hh = a * h + b
        h_ref[pl.ds(r0, 8), :] = hh
        return hh[7:8, :]

    hc_ref[...] = lax.fori_loop(0, ts // 8, group, hc_ref[...])
    gt = gate_ref[...].astype(f32)
    o_ref[...] = (h_ref[...] * jax.nn.gelu(gt)).astype(o_ref.dtype)


def lru_branch(fat, conv_w, conv_b, wax, b_a, b_x, lam, B, S, ts=512):
    T = B * S
    ns = S // ts
    row = lambda b, s: b * ns + s
    full = lambda shp: pl.BlockSpec(shp, lambda b, s: (0,) * len(shp))
    return pl.pallas_call(
        functools.partial(_lru_kernel, ts=ts),
        grid=(B, ns),
        in_specs=[pl.BlockSpec((ts, 1024), lambda b, s: (row(b, s), C_LX // 1024)),
                  pl.BlockSpec((ts, 1024), lambda b, s: (row(b, s), C_LG // 1024)),
                  full((CONV_WIDTH, 1024)), full((1, 1024)), full((LRU_BLOCKS, LRU_BLOCK_DIM, 2 * LRU_BLOCK_DIM)),
                  full((1, 1024)), full((1, 1024)), full((1, 1024))],
        out_specs=pl.BlockSpec((ts, 1024), lambda b, s: (row(b, s), 0)),
        out_shape=jax.ShapeDtypeStruct((T, 1024), bf16),
        scratch_shapes=[pltpu.VMEM((ts + 8, 1024), f32), pltpu.VMEM((ts, 1024), f32),
                        pltpu.VMEM((ts, 1024), f32), pltpu.VMEM((ts, 1024), f32),
                        pltpu.VMEM((8, 1024), f32), pltpu.VMEM((1, 1024), f32)],
        compiler_params=_params(("arbitrary", "arbitrary")),
        name="lru",
    )(fat, fat, conv_w, conv_b, wax, b_a, b_x, lam)


def _swa_kernel(q_ref, kc_ref, kp_ref, vc_ref, vp_ref, bias_ref, sink_ref, o_ref, kb_ref, vb_ref, *, tq):
    W = WINDOW
    hd = SWA_HEAD_DIM
    kb_ref[0:W, :] = kp_ref[...]
    kb_ref[W:W + tq, :] = kc_ref[...]
    vb_ref[0:W, :] = vp_ref[...]
    vb_ref[W:W + tq, :] = vc_ref[...]
    first_tile = pl.program_id(1) == 0
    col = lax.broadcasted_iota(i32, (W, 2 * W), 1)
    scale = hd ** -0.5

    def qblock(n, carry):
        r0 = pl.multiple_of(n * W, W)
        pen = jnp.where(jnp.logical_and(first_tile, n == 0), NEG_BIG, 0.0).astype(f32)
        penm = jnp.where(col < W, pen, 0.0)
        heads = range(SWA_Q_HEADS)
        kk = [kb_ref[pl.ds(r0, 2 * W), hk * hd:(hk + 1) * hd] for hk in range(SWA_KV_HEADS)]
        vv = [vb_ref[pl.ds(r0, 2 * W), hk * hd:(hk + 1) * hd] for hk in range(SWA_KV_HEADS)]
        qs = q_ref[pl.ds(r0, W), :] * scale
        s = [lax.dot_general(qs[:, h * hd:(h + 1) * hd], kk[h // SWA_GROUP], (((1,), (1,)), ((), ())),
                             preferred_element_type=f32) + (bias_ref[h] + penm) for h in heads]
        sink = [sink_ref[h:h + 1, 0:1] for h in heads]
        m = [jnp.maximum(jnp.max(s[h], axis=-1, keepdims=True), sink[h]) for h in heads]
        pe = [jnp.exp(s[h] - m[h]) for h in heads]
        o = [jnp.dot(pe[h].astype(bf16), vv[h // SWA_GROUP], preferred_element_type=f32) for h in heads]
        den = [jnp.sum(pe[h], axis=-1, keepdims=True) + jnp.exp(sink[h] - m[h]) for h in heads]
        o_ref[pl.ds(r0, W), :] = jnp.concatenate([o[h] / den[h] for h in heads], axis=1).astype(o_ref.dtype)
        return carry

    lax.fori_loop(0, tq // W, qblock, 0)


def swa_branch(fat, bias_tab, sinks_tab, B, S, tq=512):
    T = B * S
    ns = S // tq
    nb = S // WINDOW
    per = tq // WINDOW
    row = lambda b, s: b * ns + s
    prev = lambda b, s: b * nb + jnp.maximum(s * per - 1, 0)
    kvw = SWA_KV_HEADS * SWA_HEAD_DIM
    full = lambda shp: pl.BlockSpec(shp, lambda b, s: (0,) * len(shp))
    return pl.pallas_call(
        functools.partial(_swa_kernel, tq=tq),
        grid=(B, ns),
        in_specs=[pl.BlockSpec((tq, 1024), lambda b, s: (row(b, s), C_SQ // 1024)),
                  pl.BlockSpec((tq, kvw), lambda b, s: (row(b, s), C_SK // kvw)),
                  pl.BlockSpec((WINDOW, kvw), lambda b, s: (prev(b, s), C_SK // kvw)),
                  pl.BlockSpec((tq, kvw), lambda b, s: (row(b, s), C_SV // kvw)),
                  pl.BlockSpec((WINDOW, kvw), lambda b, s: (prev(b, s), C_SV // kvw)),
                  full((SWA_Q_HEADS, WINDOW, 2 * WINDOW)), full((SWA_Q_HEADS, LANES))],
        out_specs=pl.BlockSpec((tq, 1024), lambda b, s: (row(b, s), 0)),
        out_shape=jax.ShapeDtypeStruct((T, 1024), bf16),
        scratch_shapes=[pltpu.VMEM((tq + WINDOW, kvw), bf16), pltpu.VMEM((tq + WINDOW, kvw), bf16)],
        compiler_params=_params(("arbitrary", "arbitrary")),
        name="swa",
    )(fat, fat, fat, fat, fat, bias_tab, sinks_tab)


def _merge_kernel(oa_ref, ob_ref, oc_ref, ga_ref, gb_ref, gc_ref, x_ref, wa_ref, wb_ref, wc_ref, wo_ref,
                  g_ref, b_ref, o_ref, *, alpha):
    ya = jnp.dot(oa_ref[...], wa_ref[...], preferred_element_type=f32)
    yb = jnp.dot(ob_ref[...], wb_ref[...], preferred_element_type=f32)
    yc = jnp.dot(oc_ref[...], wc_ref[...], preferred_element_type=f32)
    mix = (_sigmoid(ga_ref[...].astype(f32)) * ya + _sigmoid(gb_ref[...].astype(f32)) * yb
           + _sigmoid(gc_ref[...].astype(f32)) * yc)
    y = jnp.dot(mix.astype(bf16), wo_ref[...], preferred_element_type=f32)
    o_ref[...] = _layer_norm(alpha * x_ref[...] + y, g_ref[...], b_ref[...])


def merge_ln(oa, ob, oc, fat, x2d, wa, wb, wc, wo, g, b, alpha, tm=512):
    T = x2d.shape[0]
    act = pl.BlockSpec((tm, 1024), lambda i: (i, 0))
    fatb = lambda cb: pl.BlockSpec((tm, 1024), lambda i: (i, cb))
    wsp = pl.BlockSpec((1024, 1024), lambda i: (0, 0))
    vec = pl.BlockSpec((1, 1024), lambda i: (0, 0))
    return pl.pallas_call(
        functools.partial(_merge_kernel, alpha=alpha),
        grid=(T // tm,),
        in_specs=[act, act, act, fatb(C_MA // 1024), fatb(C_MB // 1024), fatb(C_MC // 1024), act,
                  wsp, wsp, wsp, wsp, vec, vec],
        out_specs=act,
        out_shape=jax.ShapeDtypeStruct((T, 1024), f32),
        compiler_params=_params(("arbitrary",)),
        name="merge_ln",
    )(oa, ob, oc, fat, fat, fat, x2d, wa, wb, wc, wo, g, b)


def _route_kernel(x_ref, rw_ref, rb_ref, gates_ref, eidx_ref, rank_ref, cnt_ref, run_ref, *, tm):
    @pl.when(pl.program_id(0) == 0)
    def _():
        run_ref[...] = jnp.zeros_like(run_ref)

    logits = jnp.dot(x_ref[...], rw_ref[...], preferred_element_type=f32,
                     precision=lax.Precision.HIGHEST) + rb_ref[...]
    lane = lax.broadcasted_iota(i32, (tm, LANES), 1)
    lane_f = lane.astype(f32)
    work = logits
    vals, idxs, hots = [], [], []
    for _ in range(TOP_K):
        m = jnp.max(work, axis=-1, keepdims=True)
        idx = jnp.min(jnp.where(work == m, lane_f, float(LANES)), axis=-1, keepdims=True)
        hot = lane_f == idx
        vals.append(m)
        idxs.append(idx)
        hots.append(hot)
        work = jnp.where(hot, -jnp.inf, work)
    es = [jnp.exp(v - vals[0]) for v in vals]
    den = es[0] + es[1] + es[2] + es[3]
    sel = jnp.zeros((tm, LANES), f32)
    for hot in hots:
        sel = sel + jnp.where(hot, 1.0, 0.0)
    ri = lax.broadcasted_iota(i32, (tm, tm), 0)
    ci = lax.broadcasted_iota(i32, (tm, tm), 1)
    tril = jnp.where(ri > ci, 1.0, 0.0).astype(bf16)
    before = jnp.dot(tril, sel.astype(bf16), preferred_element_type=f32) + run_ref[...]
    run_ref[...] = run_ref[...] + jnp.sum(sel, axis=0, keepdims=True)
    cnt_ref[...] = run_ref[...]
    gates = jnp.zeros((tm, LANES), f32)
    eidx = jnp.zeros((tm, LANES), f32)
    rank = jnp.zeros((tm, LANES), f32)
    for k in range(TOP_K):
        rk = jnp.sum(jnp.where(hots[k], before, 0.0), axis=-1, keepdims=True)
        gates = jnp.where(lane == k, es[k] / den, gates)
        eidx = jnp.where(lane == k, idxs[k], eidx)
        rank = jnp.where(lane == k, rk, rank)
    gates_ref[...] = gates
    eidx_ref[...] = eidx.astype(i32)
    rank_ref[...] = rank.astype(i32)


def route(x2d, rw_pad, rb_pad, tm=512):
    T = x2d.shape[0]
    outb = pl.BlockSpec((tm, LANES), lambda i: (i, 0))
    return pl.pallas_call(
        functools.partial(_route_kernel, tm=tm),
        grid=(T // tm,),
        in_specs=[pl.BlockSpec((tm, 1024), lambda i: (i, 0)),
                  pl.BlockSpec((1024, LANES), lambda i: (0, 0)),
                  pl.BlockSpec((1, LANES), lambda i: (0, 0))],
        out_specs=[outb, outb, outb, pl.BlockSpec((1, LANES), lambda i: (0, 0))],
        out_shape=[jax.ShapeDtypeStruct((T, LANES), f32), jax.ShapeDtypeStruct((T, LANES), i32),
                   jax.ShapeDtypeStruct((T, LANES), i32), jax.ShapeDtypeStruct((1, LANES), f32)],
        scratch_shapes=[pltpu.VMEM((1, LANES), f32)],
        compiler_params=_params(("arbitrary",)),
        name="route",
    )(x2d, rw_pad, rb_pad)


ROW_UNROLL = 8


def _row_copy(src_ref, src_row, dst_ref, dst_row, sem):
    return pltpu.make_async_copy(src_ref.at[pl.ds(src_row, 1)], dst_ref.at[pl.ds(dst_row, 1)], sem)


def _dispatch_kernel(dest_ref, x_ref, buf_in_ref, buf_ref, sem, *, tm):
    del buf_in_ref

    def issue(g, carry):
        for j in range(ROW_UNROLL):
            r = g * ROW_UNROLL + j
            for k in range(TOP_K):
                _row_copy(x_ref, r, buf_ref, dest_ref[r * TOP_K + k], sem).start(priority=k % 2)
        return carry

    lax.fori_loop(0, tm // ROW_UNROLL, issue, 0)

    def drain(g, carry):
        for j in range(ROW_UNROLL * TOP_K):
            _row_copy(x_ref, 0, buf_ref, 0, sem).wait()
        return carry

    lax.fori_loop(0, tm // ROW_UNROLL, drain, 0)


def dispatch(dest_flat, x2d, buf0, tm=512):
    T = x2d.shape[0]
    return pl.pallas_call(
        functools.partial(_dispatch_kernel, tm=tm),
        grid=(T // tm,),
        in_specs=[pl.BlockSpec((tm * TOP_K,), lambda i: (i,), memory_space=pltpu.SMEM),
                  pl.BlockSpec((tm, 1024), lambda i: (i, 0)),
                  pl.BlockSpec(memory_space=pl.ANY)],
        out_specs=pl.BlockSpec(memory_space=pl.ANY),
        out_shape=jax.ShapeDtypeStruct(buf0.shape, buf0.dtype),
        scratch_shapes=[pltpu.SemaphoreType.DMA(())],
        input_output_aliases={2: 0},
        compiler_params=_params(("arbitrary",)),
        name="dispatch",
    )(dest_flat, x2d, buf0)


GU_GROUP = 2 * LANES


def _gu_prep_kernel(w_ref, o_ref):
    ri = lax.broadcasted_iota(i32, (GU_GROUP, GU_GROUP), 0)
    ci = lax.broadcasted_iota(i32, (GU_GROUP, GU_GROUP), 1)
    src = jnp.where(ci < LANES, 2 * ci, 2 * (ci - LANES) + 1)
    perm = jnp.where(ri == src, 1.0, 0.0).astype(bf16)
    for g in range(w_ref.shape[2] // GU_GROUP):
        cs = slice(g * GU_GROUP, (g + 1) * GU_GROUP)
        o_ref[0, :, cs] = jnp.dot(w_ref[0, :, cs].astype(bf16), perm, preferred_element_type=f32).astype(bf16)


def gu_prep(w_gu, tk=512):
    E, D, N = w_gu.shape
    return pl.pallas_call(
        _gu_prep_kernel,
        grid=(E, D // tk),
        in_specs=[pl.BlockSpec((1, tk, N), lambda e, k: (e, k, 0))],
        out_specs=pl.BlockSpec((1, tk, N), lambda e, k: (e, k, 0)),
        out_shape=jax.ShapeDtypeStruct((E, D, N), bf16),
        compiler_params=_params(("arbitrary", "arbitrary")),
        name="gu_prep",
    )(w_gu)


def _expert_kernel(blk_e_ref, nused_ref, x_ref, wgu_ref, bgu_ref, wd_ref, bd_ref, o_ref):
    i = pl.program_id(0)

    @pl.when(i < nused_ref[0])
    def _():
        de = wd_ref.shape[1]
        xb = x_ref[...].astype(bf16)
        hgu = jnp.dot(xb, wgu_ref[0], preferred_element_type=f32) + bgu_ref[0]
        acts = []
        for g in range(2 * de // GU_GROUP):
            gate = jnp.minimum(hgu[:, g * GU_GROUP:g * GU_GROUP + LANES], SWIGLU_LIMIT)
            lin = jnp.clip(hgu[:, g * GU_GROUP + LANES:(g + 1) * GU_GROUP], -SWIGLU_LIMIT, SWIGLU_LIMIT)
            acts.append((gate * _sigmoid(SWIGLU_ALPHA * gate) * (lin + 1.0)).astype(bf16))
        act = jnp.concatenate(acts, axis=1)
        o_ref[...] = jnp.dot(act, wd_ref[0], preferred_element_type=f32) + bd_ref[0]

    @pl.when(i >= nused_ref[0])
    def _():
        o_ref[...] = jnp.zeros_like(o_ref)


def experts(blk_e, nused, buf, wgu, bgu, wd, bd):
    P, D = buf.shape
    nblk = P // EXPERT_BLK
    de = wd.shape[1]
    live = lambda i, be, nu: jnp.minimum(i, nu[0] - 1)
    grid_spec = pltpu.PrefetchScalarGridSpec(
        num_scalar_prefetch=2,
        grid=(nblk,),
        in_specs=[pl.BlockSpec((EXPERT_BLK, D), lambda i, be, nu: (live(i, be, nu), 0)),
                  pl.BlockSpec((1, D, 2 * de), lambda i, be, nu: (be[i], 0, 0)),
                  pl.BlockSpec((1, 1, 2 * de), lambda i, be, nu: (be[i], 0, 0)),
                  pl.BlockSpec((1, de, D), lambda i, be, nu: (be[i], 0, 0)),
                  pl.BlockSpec((1, 1, D), lambda i, be, nu: (be[i], 0, 0))],
        out_specs=pl.BlockSpec((EXPERT_BLK, D), lambda i, be, nu: (i, 0)),
    )
    return pl.pallas_call(
        _expert_kernel,
        grid_spec=grid_spec,
        out_shape=jax.ShapeDtypeStruct((P, D), f32),
        compiler_params=_params(("arbitrary",)),
        name="experts",
    )(blk_e, nused, buf, wgu, bgu, wd, bd)


def _combine_kernel(dest_ref, dest_next_ref, gates_ref, x_ref, p_ref, obuf_ref, wg_ref, wp_ref,
                    g2_ref, b2_ref, g3_ref, b3_ref, o_ref, rows_ref, sems, *, tm, alpha):
    i = pl.program_id(0)
    n = pl.num_programs(0)
    slot = i % 2

    def gather(idx_ref, s):
        def issue(g, carry):
            for j in range(ROW_UNROLL):
                r = g * ROW_UNROLL + j
                for k in range(TOP_K):
                    _row_copy(obuf_ref, idx_ref[r * TOP_K + k], rows_ref.at[s, k], r, sems.at[s]).start(priority=k % 2)
            return carry

        lax.fori_loop(0, tm // ROW_UNROLL, issue, 0)

    @pl.when(i == 0)
    def _():
        gather(dest_ref, 0)

    @pl.when(i + 1 < n)
    def _():
        gather(dest_next_ref, 1 - slot)

    def drain(g, carry):
        for j in range(ROW_UNROLL * TOP_K):
            _row_copy(obuf_ref, 0, rows_ref.at[slot, 0], 0, sems.at[slot]).wait()
        return carry

    lax.fori_loop(0, tm // ROW_UNROLL, drain, 0)

    gates = gates_ref[...]
    y = gates[:, 0:1] * rows_ref[slot, 0]
    for k in range(1, TOP_K):
        y = y + gates[:, k:k + 1] * rows_ref[slot, k]
    x2 = _layer_norm(alpha * x_ref[...] + y, g2_ref[...], b2_ref[...])
    gate = _sigmoid(jnp.dot(x2.astype(bf16), wg_ref[...], preferred_element_type=f32))
    proj = jnp.dot(p_ref[...].astype(bf16), wp_ref[...], preferred_element_type=f32)
    o_ref[...] = _layer_norm(alpha * x2 + gate * proj, g3_ref[...], b3_ref[...])


def combine_ple(dest_flat, gates, x2d, p2d, layer, obuf, wg, wp, g2, b2, g3, b3, alpha, tm=256):
    T = x2d.shape[0]
    nt = T // tm
    act = pl.BlockSpec((tm, 1024), lambda i: (i, 0))
    vec = pl.BlockSpec((1, 1024), lambda i: (0, 0))
    return pl.pallas_call(
        functools.partial(_combine_kernel, tm=tm, alpha=alpha),
        grid=(nt,),
        in_specs=[pl.BlockSpec((tm * TOP_K,), lambda i: (i,), memory_space=pltpu.SMEM),
                  pl.BlockSpec((tm * TOP_K,), lambda i: (jnp.minimum(i + 1, nt - 1),), memory_space=pltpu.SMEM),
                  pl.BlockSpec((tm, LANES), lambda i: (i, 0)),
                  act,
                  pl.BlockSpec((tm, PLE_DIM), lambda i: (layer * nt + i, 0)),
                  pl.BlockSpec(memory_space=pl.ANY),
                  pl.BlockSpec((1024, 1024), lambda i: (0, 0)),
                  pl.BlockSpec((PLE_DIM, 1024), lambda i: (0, 0)),
                  vec, vec, vec, vec],
        out_specs=act,
        out_shape=jax.ShapeDtypeStruct((T, 1024), f32),
        scratch_shapes=[pltpu.VMEM((2, TOP_K, tm, 1024), f32), pltpu.SemaphoreType.DMA((2,))],
        compiler_params=_params(("arbitrary",)),
        name="combine_ple",
    )(dest_flat, dest_flat, gates, x2d, p2d, obuf, wg, wp, g2, b2, g3, b3)


def _t5_bucket_np(dist):
    max_exact = REL_BUCKETS // 2
    d = np.maximum(dist.astype(np.float32), np.float32(1.0))
    large = max_exact + (np.log(d / np.float32(max_exact)) / np.float32(math.log(REL_MAX_DISTANCE / max_exact))
                         * np.float32(REL_BUCKETS - max_exact)).astype(np.int32)
    large = np.minimum(large, REL_BUCKETS - 1)
    return np.where(dist < max_exact, dist, large)


def _swa_bias_table(rel_bias):
    kj = np.arange(2 * WINDOW)[None, :]
    dist = (np.arange(WINDOW)[:, None] + WINDOW) - kj
    in_window = (dist >= 0) & (dist < WINDOW)
    bucket = _t5_bucket_np(np.maximum(dist, 0))
    bias = jnp.transpose(rel_bias[bucket].astype(f32), (2, 0, 1))
    return jnp.where(jnp.asarray(in_window)[None], bias, NEG_BIG)


def _pad_row(v, width=LANES, fill=0.0):
    v = v.astype(f32).reshape(1, -1)
    return jnp.pad(v, ((0, 0), (0, width - v.shape[1])), constant_values=fill)


def _wcat(w_in):
    cols = [w_in[:, 0:4096], w_in[:, 4112:6160], w_in[:, 6160:7184], w_in[:, 7696:10768],
            w_in[:, 7184:7440], w_in[:, 7440:7696], w_in[:, 4096:4112],
            jnp.zeros((w_in.shape[0], FAT_W - 10768), w_in.dtype)]
    return jnp.concatenate(cols, axis=1).astype(bf16)


def kernel(x, p, w_in, conv_qkv_w, gdn_a_log, gdn_dt_bias, gdn_norm_w, rg_conv_w, rg_conv_b, rg_w_a, rg_b_a, rg_w_x, rg_b_x, rg_lambda, attn_sinks, rel_bias, w_o_gdn, w_o_lru, w_o_swa, w_out, ln1_g, ln1_b, router_w, router_b, w_gu, b_gu, w_down, b_down, ln2_g, ln2_b, ple_w_gate, ple_w_proj, ln3_g, ln3_b):
    B, S, D = x.shape
    depth = w_in.shape[0]
    T = B * S
    A = T * TOP_K
    alpha = (2.0 * depth) ** 0.25
    P = A + N_EXPERTS * EXPERT_BLK
    nblk = P // EXPERT_BLK
    row = lambda v: v.astype(f32).reshape(1, -1)

    bias_tab = _swa_bias_table(rel_bias)
    p2d = p.reshape(depth * T, PLE_DIM)
    xc = x.reshape(T, D)
    for i in range(depth):
        fat = inproj(xc, _wcat(w_in[i]))
        o_gdn = gdn_branch(fat, conv_qkv_w[i], _pad_row(gdn_a_log[i]), _pad_row(gdn_dt_bias[i]),
                           row(gdn_norm_w[i]), B, S)
        wax = jnp.concatenate([rg_w_a[i], rg_w_x[i]], axis=-1).astype(bf16)
        o_lru = lru_branch(fat, rg_conv_w[i], row(rg_conv_b[i]), wax, row(rg_b_a[i]), row(rg_b_x[i]),
                           row(rg_lambda[i]), B, S)
        sinks_tab = jnp.broadcast_to(attn_sinks[i].astype(f32)[:, None], (SWA_Q_HEADS, LANES))
        o_swa = swa_branch(fat, bias_tab, sinks_tab, B, S)
        x1 = merge_ln(o_gdn, o_lru, o_swa, fat, xc, w_o_gdn[i].astype(bf16), w_o_lru[i].astype(bf16),
                      w_o_swa[i].astype(bf16), w_out[i].astype(bf16), row(ln1_g[i]), row(ln1_b[i]), alpha)

        rw_pad = jnp.pad(router_w[i], ((0, 0), (0, LANES - N_EXPERTS)))
        rb_pad = _pad_row(router_b[i], fill=NEG_BIG)
        gates, eidx, rank, cnt = route(x1, rw_pad, rb_pad)
        counts = cnt[0, :N_EXPERTS].astype(i32)
        padded = ((counts + EXPERT_BLK - 1) // EXPERT_BLK) * EXPERT_BLK
        pad_ends = jnp.cumsum(padded)
        pad_starts = pad_ends - padded
        dest = (pad_starts[eidx[:, :TOP_K]] + rank[:, :TOP_K]).reshape(A)
        blk_start = jnp.arange(nblk, dtype=i32) * EXPERT_BLK
        blk_e = jnp.minimum(jnp.sum((pad_ends[None, :] <= blk_start[:, None]).astype(i32), axis=1),
                            N_EXPERTS - 1).astype(i32)
        nused = (pad_ends[-1:] // EXPERT_BLK).astype(i32)

        buf = dispatch(dest, x1, jnp.zeros((P, D), f32))
        bgu = jnp.transpose(b_gu[i].reshape(N_EXPERTS, -1, LANES, 2), (0, 1, 3, 2)).reshape(N_EXPERTS, 1, -1)
        obuf = experts(blk_e, nused, buf, gu_prep(w_gu[i]), bgu, w_down[i].astype(bf16), b_down[i][:, None, :])
        xc = combine_ple(dest, gates, x1, p2d, i, obuf, ple_w_gate[i].astype(bf16),
                         ple_w_proj[i].astype(bf16), row(ln2_g[i]), row(ln2_b[i]), row(ln3_g[i]), row(ln3_b[i]),
                         alpha)
    return xc.reshape(B, S, D)
```

```python
import functools
import math

import numpy as np
import jax
import jax.numpy as jnp
from jax import lax
from jax.experimental import pallas as pl
from jax.experimental.pallas import tpu as pltpu

f32 = jnp.float32
bf16 = jnp.bfloat16
i32 = jnp.int32

D_MODEL = 1024
PLE_DIM = 256
GDN_HEADS = 8
GDN_HEAD_DIM = 128
GDN_CHUNK = 64
CONV_WIDTH = 4
LRU_BLOCKS = 8
LRU_BLOCK_DIM = 128
LRU_C = 8.0
SWA_Q_HEADS = 16
SWA_KV_HEADS = 4
SWA_HEAD_DIM = 64
SWA_GROUP = 4
WINDOW = 128
REL_BUCKETS = 32
REL_MAX_DISTANCE = 128
N_EXPERTS = 32
TOP_K = 4
SWIGLU_LIMIT = 7.0
SWIGLU_ALPHA = 1.702
LN_EPS = 1e-5
NORM_EPS = 1e-6
NEG_BIG = -1e30

LANES = 128
VMEM_LIMIT = 56 * 1024 * 1024

C_GQ, C_GK, C_GV, C_GZ = 0, 1024, 2048, 3072
C_LX, C_LG = 4096, 5120
C_SQ = 6144
C_MA, C_MB, C_MC = 7168, 8192, 9216
C_SK, C_SV = 10240, 10496
C_AB = 10752
FAT_W = 10880
FAT_TN = 2176

EXPERT_BLK = 512
CHUNKS_IN_FLIGHT = 2


def _params(sem):
    return pltpu.CompilerParams(dimension_semantics=sem, vmem_limit_bytes=VMEM_LIMIT)


def _sigmoid(x):
    return 1.0 / (1.0 + jnp.exp(-x))


def _softplus(x):
    return jnp.maximum(x, 0.0) + jnp.log(1.0 + jnp.exp(-jnp.abs(x)))


def _layer_norm(z, g, b):
    mu = jnp.mean(z, axis=-1, keepdims=True)
    zc = z - mu
    var = jnp.mean(zc * zc, axis=-1, keepdims=True)
    return zc * lax.rsqrt(var + LN_EPS) * g + b


def _inproj_kernel(x_ref, w_ref, o_ref, xb_ref):
    @pl.when(pl.program_id(1) == 0)
    def _():
        xb_ref[...] = x_ref[...].astype(bf16)

    o_ref[...] = jnp.dot(xb_ref[...], w_ref[...], preferred_element_type=f32).astype(o_ref.dtype)


def inproj(x2d, wcat, tm=1024):
    T, K = x2d.shape
    N = wcat.shape[1]
    tn = FAT_TN
    return pl.pallas_call(
        _inproj_kernel,
        grid=(T // tm, N // tn),
        in_specs=[pl.BlockSpec((tm, K), lambda i, j: (i, 0)),
                  pl.BlockSpec((K, tn), lambda i, j: (0, j))],
        out_specs=pl.BlockSpec((tm, tn), lambda i, j: (i, j)),
        out_shape=jax.ShapeDtypeStruct((T, N), bf16),
        scratch_shapes=[pltpu.VMEM((tm, K), bf16)],
        compiler_params=_params(("arbitrary", "arbitrary")),
        name="inproj",
    )(x2d, wcat)


def _shift_matrices(ts):
    ri = lax.broadcasted_iota(i32, (ts, ts), 0)
    ci = lax.broadcasted_iota(i32, (ts, ts), 1)
    return [jnp.where(ri - ci == d, 1.0, 0.0).astype(bf16) for d in (3, 2, 1)]


def _causal_conv_silu(src_ref, dst_ref, cz_ref, carry_ref, w, smats, ts, head_scale=None):
    xb = src_ref[...]
    x = xb.astype(f32)
    y = w[3:4, :] * x
    for j, sm in enumerate(smats):
        y = y + w[j:j + 1, :] * jnp.dot(sm, xb, preferred_element_type=f32)
    cz_ref[0:8, :] = carry_ref[...]
    cz_ref[8:16, :] = jnp.zeros((8, x.shape[1]), f32)
    corr = w[0:1, :] * cz_ref[5:13, :] + w[1:2, :] * cz_ref[6:14, :] + w[2:3, :] * cz_ref[7:15, :]
    carry_ref[...] = x[ts - 8:ts, :]

    def post(rows, yv):
        a = yv * _sigmoid(yv)
        if head_scale is None:
            dst_ref[rows, :] = a
            return
        for h in range(a.shape[1] // LANES):
            cs = slice(h * LANES, (h + 1) * LANES)
            ah = a[:, cs]
            dst_ref[rows, cs] = ah * (lax.rsqrt(jnp.sum(ah * ah, axis=-1, keepdims=True) + NORM_EPS) * head_scale)

    post(slice(0, 8), y[0:8] + corr)
    post(slice(8, ts), y[8:ts])


def _gdn_kernel(q_ref, k_ref, v_ref, z_ref, ab_ref, cw_ref, alog_ref, dtb_ref, nw_ref, o_ref,
                xpad_ref, qs_ref, ks_ref, vs_ref, carry_ref, state_ref, g_ref, beta_ref,
                u_s, lhs_s, intra_s, kdt_s, *, ts):
    C = GDN_CHUNK
    D = GDN_HEAD_DIM
    P = 2 * C

    @pl.when(pl.program_id(1) == 0)
    def _():
        carry_ref[...] = jnp.zeros_like(carry_ref)
        state_ref[...] = jnp.zeros_like(state_ref)

    smats = _shift_matrices(ts)
    for p, (src, dst, scale) in enumerate(((q_ref, qs_ref, D ** -0.5), (k_ref, ks_ref, 1.0), (v_ref, vs_ref, None))):
        _causal_conv_silu(src, dst, xpad_ref, carry_ref.at[p], cw_ref[:, p * 1024:(p + 1) * 1024], smats, ts, scale)

    ab = ab_ref[...].astype(f32)
    g = -jnp.exp(alog_ref[...]) * _softplus(ab + dtb_ref[...])
    rin = lax.broadcasted_iota(i32, (ts, LANES), 0) & (C - 1)
    gsum = g
    for d in (1, 2, 4, 8, 16, 32):
        gsum = gsum + jnp.where(rin >= d, pltpu.roll(gsum, d, 0), 0.0)
    g_ref[...] = gsum
    beta_ref[...] = _sigmoid(ab)

    ri = lax.broadcasted_iota(i32, (P, P), 0)
    ci = lax.broadcasted_iota(i32, (P, P), 1)
    same = (ri >= C) == (ci >= C)
    eye = ri == ci
    causal = same & (ri >= ci)
    strict = same & (ri > ci)
    eye_f = jnp.where(eye, 1.0, 0.0).astype(f32)
    first_cols = ci < C
    nw = nw_ref[...]

    def stack(a, b):
        return jnp.concatenate([a, b], axis=0)

    def mm(a, b):
        return jnp.dot(a, b, preferred_element_type=f32)

    npair = GDN_HEADS // 2
    nchunk = ts // C
    hcols = [slice(h * D, (h + 1) * D) for h in range(GDN_HEADS)]

    for cg in range(0, nchunk, CHUNKS_IN_FLIGHT):
        probs = [(c, hp) for c in range(cg, cg + CHUNKS_IN_FLIGHT) for hp in range(npair)]
        qn, kn, vb, gcol, eg, egl, kb = [], [], [], [], [], [], []
        for c, hp in probs:
            rows = slice(c * C, (c + 1) * C)
            c0, c1 = hcols[2 * hp], hcols[2 * hp + 1]
            gc = g_ref[rows, :]
            bc = beta_ref[rows, :]
            qn.append(stack(qs_ref[rows, c0], qs_ref[rows, c1]))
            kn.append(stack(ks_ref[rows, c0], ks_ref[rows, c1]))
            v2 = stack(vs_ref[rows, c0], vs_ref[rows, c1])
            h0, h1 = 2 * hp, 2 * hp + 1
            gcl = stack(gc[:, h0:h0 + 1], gc[:, h1:h1 + 1])
            bcl = stack(bc[:, 8 + h0:9 + h0], bc[:, 8 + h1:9 + h1])
            glast = stack(jnp.broadcast_to(gc[C - 1:C, h0:h0 + 1], (C, 1)),
                          jnp.broadcast_to(gc[C - 1:C, h1:h1 + 1], (C, 1)))
            gcol.append(gcl)
            eg.append(jnp.exp(gcl))
            egl.append(jnp.exp(glast - gcl))
            kb.append(kn[-1] * bcl)
            vb.append(v2 * bcl)
        n = len(probs)
        a2 = [lax.dot_general(stack(kb[i], qn[i]).astype(bf16), kn[i].astype(bf16), (((1,), (1,)), ((), ())),
                              preferred_element_type=f32) for i in range(n)]
        lmat, intra = [], []
        for i in range(n):
            gm = jnp.broadcast_to(gcol[i], (P, P))
            grow = jnp.sum(jnp.where(eye, gm, 0.0), axis=0, keepdims=True)
            decay = jnp.where(causal, jnp.exp(jnp.minimum(gm - grow, 0.0)), 0.0)
            lmat.append(jnp.where(strict, a2[i][:P] * decay, 0.0))
            intra.append(a2[i][P:] * decay)
        lb = [l.astype(bf16) for l in lmat]
        xm = [eye_f - l for l in lmat]
        pm = [mm(b, b) for b in lb]
        for it in range(5):
            pb = [p.astype(bf16) for p in pm]
            xm = [x + mm(x.astype(bf16), b) for x, b in zip(xm, pb)]
            if it < 4:
                pm = [mm(b, b) for b in pb]
        uw = [mm(xm[i].astype(bf16), jnp.concatenate([vb[i], kb[i] * eg[i]], axis=1).astype(bf16))
              for i in range(n)]
        for i, (c, hp) in enumerate(probs):
            j = c * npair + hp
            w2 = uw[i][:, D:]
            qd = qn[i] * eg[i]
            kdt = (kn[i] * egl[i]).T
            u_s[j] = uw[i][:, :D]
            lhs_s[j, 0] = stack(w2[:C], qd[:C]).astype(bf16)
            lhs_s[j, 1] = stack(w2[C:], qd[C:]).astype(bf16)
            intra_s[j] = intra[i].astype(bf16)
            kdt_s[j, 0] = jnp.where(first_cols, kdt, 0.0).astype(bf16)
            kdt_s[j, 1] = jnp.where(first_cols, 0.0, kdt).astype(bf16)

    for c in range(nchunk):
        rows = slice(c * C, (c + 1) * C)
        gl = g_ref[(c + 1) * C - 1:(c + 1) * C, :]
        st = [state_ref[h] for h in range(GDN_HEADS)]
        wq = [mm(lhs_s[c * npair + h // 2, h % 2], st[h].astype(bf16)) for h in range(GDN_HEADS)]
        vnb = [(u_s[c * npair + hp] - stack(wq[2 * hp][:C], wq[2 * hp + 1][:C])).astype(bf16)
               for hp in range(npair)]
        o2 = [stack(wq[2 * hp][C:], wq[2 * hp + 1][C:]) + mm(intra_s[c * npair + hp], vnb[hp])
              for hp in range(npair)]
        for h in range(GDN_HEADS):
            state_ref[h] = st[h] * jnp.exp(gl[:, h:h + 1]) + mm(kdt_s[c * npair + h // 2, h % 2], vnb[h // 2])
        for hp in range(npair):
            c0, c1 = hcols[2 * hp], hcols[2 * hp + 1]
            z2 = stack(z_ref[rows, c0], z_ref[rows, c1]).astype(f32)
            on = (o2[hp] * lax.rsqrt(jnp.mean(o2[hp] * o2[hp], axis=-1, keepdims=True) + NORM_EPS) * nw
                  * (z2 * _sigmoid(z2))).astype(o_ref.dtype)
            o_ref[rows, c0] = on[:C]
            o_ref[rows, c1] = on[C:]


def gdn_branch(fat, conv_w, a_log_row, dt_bias_row, norm_w_row, B, S, ts=256):
    T = B * S
    ns = S // ts
    nprob = (ts // GDN_CHUNK) * (GDN_HEADS // 2)
    row = lambda b, s: b * ns + s
    blk = lambda cb: pl.BlockSpec((ts, 1024), lambda b, s: (row(b, s), cb))
    full = lambda shp: pl.BlockSpec(shp, lambda b, s: (0,) * len(shp))
    return pl.pallas_call(
        functools.partial(_gdn_kernel, ts=ts),
        grid=(B, ns),
        in_specs=[blk(C_GQ // 1024), blk(C_GK // 1024), blk(C_GV // 1024), blk(C_GZ // 1024),
                  pl.BlockSpec((ts, LANES), lambda b, s: (row(b, s), C_AB // LANES)),
                  full((CONV_WIDTH, 3072)), full((1, LANES)), full((1, LANES)), full((1, LANES))],
        out_specs=pl.BlockSpec((ts, 1024), lambda b, s: (row(b, s), 0)),
        out_shape=jax.ShapeDtypeStruct((T, 1024), bf16),
        scratch_shapes=[pltpu.VMEM((16, 1024), f32),
                        pltpu.VMEM((ts, 1024), f32), pltpu.VMEM((ts, 1024), f32), pltpu.VMEM((ts, 1024), f32),
                        pltpu.VMEM((3, 8, 1024), f32),
                        pltpu.VMEM((GDN_HEADS, GDN_HEAD_DIM, GDN_HEAD_DIM), f32),
                        pltpu.VMEM((ts, LANES), f32), pltpu.VMEM((ts, LANES), f32),
                        pltpu.VMEM((nprob, 128, GDN_HEAD_DIM), f32),
                        pltpu.VMEM((nprob, 2, 128, GDN_HEAD_DIM), bf16),
                        pltpu.VMEM((nprob, 128, 128), bf16),
                        pltpu.VMEM((nprob, 2, GDN_HEAD_DIM, 128), bf16)],
        compiler_params=_params(("arbitrary", "arbitrary")),
        name="gdn",
    )(fat, fat, fat, fat, fat, conv_w, a_log_row, dt_bias_row, norm_w_row)


def _lru_kernel(x_ref, gate_ref, cw_ref, cb_ref, wax_ref, ba_ref, bx_ref, lam_ref, o_ref,
                xpad_ref, a_ref, u_ref, h_ref, carry_ref, hc_ref, *, ts):
    @pl.when(pl.program_id(1) == 0)
    def _():
        carry_ref[...] = jnp.zeros_like(carry_ref)
        hc_ref[...] = jnp.zeros_like(hc_ref)

    x = x_ref[...].astype(f32)
    xpad_ref[0:8, :] = carry_ref[...]
    xpad_ref[8:8 + ts, :] = x
    carry_ref[...] = x[ts - 8:ts, :]
    w = cw_ref[...]
    xpad_ref[8:8 + ts, :] = (w[0:1, :] * xpad_ref[5:5 + ts, :] + w[1:2, :] * xpad_ref[6:6 + ts, :]
                             + w[2:3, :] * xpad_ref[7:7 + ts, :] + w[3:4, :] * x + cb_ref[...])
    nsp = _softplus(-lam_ref[...])
    for blk in range(LRU_BLOCKS):
        cs = slice(blk * LRU_BLOCK_DIM, (blk + 1) * LRU_BLOCK_DIM)
        xc = xpad_ref[8:8 + ts, cs]
        ri = jnp.dot(xc.astype(bf16), wax_ref[blk], preferred_element_type=f32)
        r = _sigmoid(ri[:, :LRU_BLOCK_DIM] + ba_ref[:, cs])
        i = _sigmoid(ri[:, LRU_BLOCK_DIM:] + bx_ref[:, cs])
        log_a = -LRU_C * r * nsp[:, cs]
        a_ref[:, cs] = jnp.exp(log_a)
        u_ref[:, cs] = jnp.sqrt(1.0 - jnp.exp(2.0 * log_a)) * (i * xc)

    rowi = lax.broadcasted_iota(i32, (8, D_MODEL), 0)

    def group(gi, h):
        r0 = pl.multiple_of(gi * 8, 8)
        a = a_ref[pl.ds(r0, 8), :]
        b = u_ref[pl.ds(r0, 8), :]
        for d in (1, 2, 4):
            m = rowi >= d
            a_s = pltpu.roll(a, d, 0)
            b_s = pltpu.roll(b, d, 0)
            b = jnp.where(m, a * b_s + b, b)
            a = jnp.where(m, a * a_s, a)
        hh = a * h + b
        h_ref[pl.ds(r0, 8), :] = hh
        return hh[7:8, :]

    hc_ref[...] = lax.fori_loop(0, ts // 8, group, hc_ref[...])
    gt = gate_ref[...].astype(f32)
    o_ref[...] = (h_ref[...] * jax.nn.gelu(gt)).astype(o_ref.dtype)


def lru_branch(fat, conv_w, conv_b, wax, b_a, b_x, lam, B, S, ts=512):
    T = B * S
    ns = S // ts
    row = lambda b, s: b * ns + s
    full = lambda shp: pl.BlockSpec(shp, lambda b, s: (0,) * len(shp))
    return pl.pallas_call(
        functools.partial(_lru_kernel, ts=ts),
        grid=(B, ns),
        in_specs=[pl.BlockSpec((ts, 1024), lambda b, s: (row(b, s), C_LX // 1024)),
                  pl.BlockSpec((ts, 1024), lambda b, s: (row(b, s), C_LG // 1024)),
                  full((CONV_WIDTH, 1024)), full((1, 1024)), full((LRU_BLOCKS, LRU_BLOCK_DIM, 2 * LRU_BLOCK_DIM)),
                  full((1, 1024)), full((1, 1024)), full((1, 1024))],
        out_specs=pl.BlockSpec((ts, 1024), lambda b, s: (row(b, s), 0)),
        out_shape=jax.ShapeDtypeStruct((T, 1024), bf16),
        scratch_shapes=[pltpu.VMEM((ts + 8, 1024), f32), pltpu.VMEM((ts, 1024), f32),
                        pltpu.VMEM((ts, 1024), f32), pltpu.VMEM((ts, 1024), f32),
                        pltpu.VMEM((8, 1024), f32), pltpu.VMEM((1, 1024), f32)],
        compiler_params=_params(("arbitrary", "arbitrary")),
        name="lru",
    )(fat, fat, conv_w, conv_b, wax, b_a, b_x, lam)


def _swa_kernel(q_ref, kc_ref, kp_ref, vc_ref, vp_ref, bias_ref, sink_ref, o_ref, kb_ref, vb_ref, *, tq):
    W = WINDOW
    hd = SWA_HEAD_DIM
    kb_ref[0:W, :] = kp_ref[...]
    kb_ref[W:W + tq, :] = kc_ref[...]
    vb_ref[0:W, :] = vp_ref[...]
    vb_ref[W:W + tq, :] = vc_ref[...]
    first_tile = pl.program_id(1) == 0
    col = lax.broadcasted_iota(i32, (W, 2 * W), 1)
    scale = hd ** -0.5

    def qblock(n, carry):
        r0 = pl.multiple_of(n * W, W)
        pen = jnp.where(jnp.logical_and(first_tile, n == 0), NEG_BIG, 0.0).astype(f32)
        penm = jnp.where(col < W, pen, 0.0)
        heads = range(SWA_Q_HEADS)
        kk = [kb_ref[pl.ds(r0, 2 * W), hk * hd:(hk + 1) * hd] for hk in range(SWA_KV_HEADS)]
        vv = [vb_ref[pl.ds(r0, 2 * W), hk * hd:(hk + 1) * hd] for hk in range(SWA_KV_HEADS)]
        qs = q_ref[pl.ds(r0, W), :] * scale
        s = [lax.dot_general(qs[:, h * hd:(h + 1) * hd], kk[h // SWA_GROUP], (((1,), (1,)), ((), ())),
                             preferred_element_type=f32) + (bias_ref[h] + penm) for h in heads]
        sink = [sink_ref[h:h + 1, 0:1] for h in heads]
        m = [jnp.maximum(jnp.max(s[h], axis=-1, keepdims=True), sink[h]) for h in heads]
        pe = [jnp.exp(s[h] - m[h]) for h in heads]
        o = [jnp.dot(pe[h].astype(bf16), vv[h // SWA_GROUP], preferred_element_type=f32) for h in heads]
        den = [jnp.sum(pe[h], axis=-1, keepdims=True) + jnp.exp(sink[h] - m[h]) for h in heads]
        o_ref[pl.ds(r0, W), :] = jnp.concatenate([o[h] / den[h] for h in heads], axis=1).astype(o_ref.dtype)
        return carry

    lax.fori_loop(0, tq // W, qblock, 0)


def swa_branch(fat, bias_tab, sinks_tab, B, S, tq=512):
    T = B * S
    ns = S // tq
    nb = S // WINDOW
    per = tq // WINDOW
    row = lambda b, s: b * ns + s
    prev = lambda b, s: b * nb + jnp.maximum(s * per - 1, 0)
    kvw = SWA_KV_HEADS * SWA_HEAD_DIM
    full = lambda shp: pl.BlockSpec(shp, lambda b, s: (0,) * len(shp))
    return pl.pallas_call(
        functools.partial(_swa_kernel, tq=tq),
        grid=(B, ns),
        in_specs=[pl.BlockSpec((tq, 1024), lambda b, s: (row(b, s), C_SQ // 1024)),
                  pl.BlockSpec((tq, kvw), lambda b, s: (row(b, s), C_SK // kvw)),
                  pl.BlockSpec((WINDOW, kvw), lambda b, s: (prev(b, s), C_SK // kvw)),
                  pl.BlockSpec((tq, kvw), lambda b, s: (row(b, s), C_SV // kvw)),
                  pl.BlockSpec((WINDOW, kvw), lambda b, s: (prev(b, s), C_SV // kvw)),
                  full((SWA_Q_HEADS, WINDOW, 2 * WINDOW)), full((SWA_Q_HEADS, LANES))],
        out_specs=pl.BlockSpec((tq, 1024), lambda b, s: (row(b, s), 0)),
        out_shape=jax.ShapeDtypeStruct((T, 1024), bf16),
        scratch_shapes=[pltpu.VMEM((tq + WINDOW, kvw), bf16), pltpu.VMEM((tq + WINDOW, kvw), bf16)],
        compiler_params=_params(("arbitrary", "arbitrary")),
        name="swa",
    )(fat, fat, fat, fat, fat, bias_tab, sinks_tab)


ROW_TILE = D_MODEL // LANES


def _load_row_tiles(ref, n, lead=()):
    return jnp.concatenate([ref[lead + (pl.ds(s, n, stride=ROW_TILE), slice(None))] for s in range(ROW_TILE)], axis=1)


def _store_row_tiles(ref, val):
    n = val.shape[0]
    for s in range(ROW_TILE):
        ref[pl.ds(s, n, stride=ROW_TILE), :] = val[:, s * LANES:(s + 1) * LANES]


def _merge_kernel(oa_ref, ob_ref, oc_ref, ga_ref, gb_ref, gc_ref, x_ref, wa_ref, wb_ref, wc_ref, wo_ref,
                  g_ref, b_ref, o_ref, ot_ref, *, alpha):
    ya = jnp.dot(oa_ref[...], wa_ref[...], preferred_element_type=f32)
    yb = jnp.dot(ob_ref[...], wb_ref[...], preferred_element_type=f32)
    yc = jnp.dot(oc_ref[...], wc_ref[...], preferred_element_type=f32)
    mix = (_sigmoid(ga_ref[...].astype(f32)) * ya + _sigmoid(gb_ref[...].astype(f32)) * yb
           + _sigmoid(gc_ref[...].astype(f32)) * yc)
    y = jnp.dot(mix.astype(bf16), wo_ref[...], preferred_element_type=f32)
    x1 = _layer_norm(alpha * x_ref[...] + y, g_ref[...], b_ref[...])
    o_ref[...] = x1
    _store_row_tiles(ot_ref, x1)


def merge_ln(oa, ob, oc, fat, x2d, wa, wb, wc, wo, g, b, alpha, tm=512):
    T = x2d.shape[0]
    act = pl.BlockSpec((tm, 1024), lambda i: (i, 0))
    fatb = lambda cb: pl.BlockSpec((tm, 1024), lambda i: (i, cb))
    wsp = pl.BlockSpec((1024, 1024), lambda i: (0, 0))
    vec = pl.BlockSpec((1, 1024), lambda i: (0, 0))
    return pl.pallas_call(
        functools.partial(_merge_kernel, alpha=alpha),
        grid=(T // tm,),
        in_specs=[act, act, act, fatb(C_MA // 1024), fatb(C_MB // 1024), fatb(C_MC // 1024), act,
                  wsp, wsp, wsp, wsp, vec, vec],
        out_specs=[act, pl.BlockSpec((tm * ROW_TILE, LANES), lambda i: (i, 0))],
        out_shape=[jax.ShapeDtypeStruct((T, 1024), f32), jax.ShapeDtypeStruct((T * ROW_TILE, LANES), f32)],
        compiler_params=_params(("arbitrary",)),
        name="merge_ln",
    )(oa, ob, oc, fat, fat, fat, x2d, wa, wb, wc, wo, g, b)


def _route_kernel(x_ref, rw_ref, rb_ref, gates_ref, eidx_ref, rank_ref, cnt_ref, run_ref, *, tm):
    @pl.when(pl.program_id(0) == 0)
    def _():
        run_ref[...] = jnp.zeros_like(run_ref)

    logits = jnp.dot(x_ref[...], rw_ref[...], preferred_element_type=f32,
                     precision=lax.Precision.HIGHEST) + rb_ref[...]
    lane = lax.broadcasted_iota(i32, (tm, LANES), 1)
    lane_f = lane.astype(f32)
    work = logits
    vals, idxs, hots = [], [], []
    for _ in range(TOP_K):
        m = jnp.max(work, axis=-1, keepdims=True)
        idx = jnp.min(jnp.where(work == m, lane_f, float(LANES)), axis=-1, keepdims=True)
        hot = lane_f == idx
        vals.append(m)
        idxs.append(idx)
        hots.append(hot)
        work = jnp.where(hot, -jnp.inf, work)
    es = [jnp.exp(v - vals[0]) for v in vals]
    den = es[0] + es[1] + es[2] + es[3]
    sel = jnp.zeros((tm, LANES), f32)
    for hot in hots:
        sel = sel + jnp.where(hot, 1.0, 0.0)
    ri = lax.broadcasted_iota(i32, (tm, tm), 0)
    ci = lax.broadcasted_iota(i32, (tm, tm), 1)
    tril = jnp.where(ri > ci, 1.0, 0.0).astype(bf16)
    before = jnp.dot(tril, sel.astype(bf16), preferred_element_type=f32) + run_ref[...]
    run_ref[...] = run_ref[...] + jnp.sum(sel, axis=0, keepdims=True)
    cnt_ref[...] = run_ref[...]
    gates = jnp.zeros((tm, LANES), f32)
    eidx = jnp.zeros((tm, LANES), f32)
    rank = jnp.zeros((tm, LANES), f32)
    for k in range(TOP_K):
        rk = jnp.sum(jnp.where(hots[k], before, 0.0), axis=-1, keepdims=True)
        gates = jnp.where(lane == k, es[k] / den, gates)
        eidx = jnp.where(lane == k, idxs[k], eidx)
        rank = jnp.where(lane == k, rk, rank)
    gates_ref[...] = gates
    eidx_ref[...] = eidx.astype(i32)
    rank_ref[...] = rank.astype(i32)


def route(x2d, rw_pad, rb_pad, tm=512):
    T = x2d.shape[0]
    outb = pl.BlockSpec((tm, LANES), lambda i: (i, 0))
    return pl.pallas_call(
        functools.partial(_route_kernel, tm=tm),
        grid=(T // tm,),
        in_specs=[pl.BlockSpec((tm, 1024), lambda i: (i, 0)),
                  pl.BlockSpec((1024, LANES), lambda i: (0, 0)),
                  pl.BlockSpec((1, LANES), lambda i: (0, 0))],
        out_specs=[outb, outb, outb, pl.BlockSpec((1, LANES), lambda i: (0, 0))],
        out_shape=[jax.ShapeDtypeStruct((T, LANES), f32), jax.ShapeDtypeStruct((T, LANES), i32),
                   jax.ShapeDtypeStruct((T, LANES), i32), jax.ShapeDtypeStruct((1, LANES), f32)],
        scratch_shapes=[pltpu.VMEM((1, LANES), f32)],
        compiler_params=_params(("arbitrary",)),
        name="route",
    )(x2d, rw_pad, rb_pad)


ROW_UNROLL = 8


def _row_copy(src_ref, src_row, dst_ref, dst_row, sem):
    tile = lambda r: pl.ds(pl.multiple_of(r * ROW_TILE, ROW_TILE), ROW_TILE)
    return pltpu.make_async_copy(src_ref.at[tile(src_row)], dst_ref.at[tile(dst_row)], sem)


def _dispatch_kernel(dest_ref, x_ref, buf_in_ref, buf_ref, sem, *, tm):
    del buf_in_ref

    def issue(g, carry):
        for j in range(ROW_UNROLL):
            r = g * ROW_UNROLL + j
            for k in range(TOP_K):
                _row_copy(x_ref, r, buf_ref, dest_ref[r * TOP_K + k], sem).start(priority=k % 2)
        return carry

    lax.fori_loop(0, tm // ROW_UNROLL, issue, 0)

    def drain(g, carry):
        for j in range(ROW_UNROLL * TOP_K):
            _row_copy(x_ref, 0, buf_ref, 0, sem).wait()
        return carry

    lax.fori_loop(0, tm // ROW_UNROLL, drain, 0)


def dispatch(dest_flat, xt, buf0, tm=512):
    T = xt.shape[0] // ROW_TILE
    return pl.pallas_call(
        functools.partial(_dispatch_kernel, tm=tm),
        grid=(T // tm,),
        in_specs=[pl.BlockSpec((tm * TOP_K,), lambda i: (i,), memory_space=pltpu.SMEM),
                  pl.BlockSpec((tm * ROW_TILE, LANES), lambda i: (i, 0)),
                  pl.BlockSpec(memory_space=pl.ANY)],
        out_specs=pl.BlockSpec(memory_space=pl.ANY),
        out_shape=jax.ShapeDtypeStruct(buf0.shape, buf0.dtype),
        scratch_shapes=[pltpu.SemaphoreType.DMA(())],
        input_output_aliases={2: 0},
        compiler_params=_params(("arbitrary",)),
        name="dispatch",
    )(dest_flat, xt, buf0)


GU_GROUP = 2 * LANES


def _gu_prep_kernel(w_ref, o_ref):
    ri = lax.broadcasted_iota(i32, (GU_GROUP, GU_GROUP), 0)
    ci = lax.broadcasted_iota(i32, (GU_GROUP, GU_GROUP), 1)
    src = jnp.where(ci < LANES, 2 * ci, 2 * (ci - LANES) + 1)
    perm = jnp.where(ri == src, 1.0, 0.0).astype(bf16)
    for g in range(w_ref.shape[3] // GU_GROUP):
        cs = slice(g * GU_GROUP, (g + 1) * GU_GROUP)
        o_ref[0, :, cs] = jnp.dot(w_ref[0, 0, :, cs].astype(bf16), perm, preferred_element_type=f32).astype(bf16)


def gu_prep(w_gu_all, layer, tk=512):
    _, E, D, N = w_gu_all.shape
    return pl.pallas_call(
        _gu_prep_kernel,
        grid=(E, D // tk),
        in_specs=[pl.BlockSpec((1, 1, tk, N), lambda e, k: (layer, e, k, 0))],
        out_specs=pl.BlockSpec((1, tk, N), lambda e, k: (e, k, 0)),
        out_shape=jax.ShapeDtypeStruct((E, D, N), bf16),
        compiler_params=_params(("arbitrary", "arbitrary")),
        name="gu_prep",
    )(w_gu_all)


def _expert_kernel(blk_e_ref, nused_ref, x_ref, wgu_ref, bgu_ref, wd_ref, bd_ref, o_ref):
    i = pl.program_id(0)

    @pl.when(i < nused_ref[0])
    def _():
        de = wd_ref.shape[1]
        xb = _load_row_tiles(x_ref, EXPERT_BLK).astype(bf16)
        hgu = jnp.dot(xb, wgu_ref[0], preferred_element_type=f32) + bgu_ref[0]
        acts = []
        for g in range(2 * de // GU_GROUP):
            gate = jnp.minimum(hgu[:, g * GU_GROUP:g * GU_GROUP + LANES], SWIGLU_LIMIT)
            lin = jnp.clip(hgu[:, g * GU_GROUP + LANES:(g + 1) * GU_GROUP], -SWIGLU_LIMIT, SWIGLU_LIMIT)
            acts.append((gate * _sigmoid(SWIGLU_ALPHA * gate) * (lin + 1.0)).astype(bf16))
        act = jnp.concatenate(acts, axis=1)
        _store_row_tiles(o_ref, jnp.dot(act, wd_ref[0], preferred_element_type=f32) + bd_ref[0])

    @pl.when(i >= nused_ref[0])
    def _():
        o_ref[...] = jnp.zeros_like(o_ref)


def experts(blk_e, nused, buf, wgu, bgu, wd, bd):
    D = wd.shape[2]
    nblk = buf.shape[0] // (EXPERT_BLK * ROW_TILE)
    de = wd.shape[1]
    tile_blk = (EXPERT_BLK * ROW_TILE, LANES)
    live = lambda i, be, nu: jnp.minimum(i, nu[0] - 1)
    grid_spec = pltpu.PrefetchScalarGridSpec(
        num_scalar_prefetch=2,
        grid=(nblk,),
        in_specs=[pl.BlockSpec(tile_blk, lambda i, be, nu: (live(i, be, nu), 0)),
                  pl.BlockSpec((1, D, 2 * de), lambda i, be, nu: (be[i], 0, 0)),
                  pl.BlockSpec((1, 1, 2 * de), lambda i, be, nu: (be[i], 0, 0)),
                  pl.BlockSpec((1, de, D), lambda i, be, nu: (be[i], 0, 0)),
                  pl.BlockSpec((1, 1, D), lambda i, be, nu: (be[i], 0, 0))],
        out_specs=pl.BlockSpec(tile_blk, lambda i, be, nu: (i, 0)),
    )
    return pl.pallas_call(
        _expert_kernel,
        grid_spec=grid_spec,
        out_shape=jax.ShapeDtypeStruct(buf.shape, f32),
        compiler_params=_params(("arbitrary",)),
        name="experts",
    )(blk_e, nused, buf, wgu, bgu, wd, bd)


def _combine_kernel(dest_ref, dest_next_ref, gates_ref, x_ref, p_ref, obuf_ref, wg_ref, wp_ref,
                    g2_ref, b2_ref, g3_ref, b3_ref, o_ref, rows_ref, sems, *, tm, alpha):
    i = pl.program_id(0)
    n = pl.num_programs(0)
    slot = i % 2

    def gather(idx_ref, s):
        def issue(g, carry):
            for j in range(ROW_UNROLL):
                r = g * ROW_UNROLL + j
                for k in range(TOP_K):
                    _row_copy(obuf_ref, idx_ref[r * TOP_K + k], rows_ref.at[s, k], r, sems.at[s]).start(priority=k % 2)
            return carry

        lax.fori_loop(0, tm // ROW_UNROLL, issue, 0)

    @pl.when(i == 0)
    def _():
        gather(dest_ref, 0)

    @pl.when(i + 1 < n)
    def _():
        gather(dest_next_ref, 1 - slot)

    def drain(g, carry):
        for j in range(ROW_UNROLL * TOP_K):
            _row_copy(obuf_ref, 0, rows_ref.at[slot, 0], 0, sems.at[slot]).wait()
        return carry

    lax.fori_loop(0, tm // ROW_UNROLL, drain, 0)

    gates = gates_ref[...]
    y = gates[:, 0:1] * _load_row_tiles(rows_ref, tm, (slot, 0))
    for k in range(1, TOP_K):
        y = y + gates[:, k:k + 1] * _load_row_tiles(rows_ref, tm, (slot, k))
    x2 = _layer_norm(alpha * x_ref[...] + y, g2_ref[...], b2_ref[...])
    gate = _sigmoid(jnp.dot(x2.astype(bf16), wg_ref[...], preferred_element_type=f32))
    proj = jnp.dot(p_ref[...].astype(bf16), wp_ref[...], preferred_element_type=f32)
    o_ref[...] = _layer_norm(alpha * x2 + gate * proj, g3_ref[...], b3_ref[...])


def combine_ple(dest_flat, gates, x2d, p2d, layer, obuf, wg, wp, g2, b2, g3, b3, alpha, tm=256):
    T = x2d.shape[0]
    nt = T // tm
    act = pl.BlockSpec((tm, 1024), lambda i: (i, 0))
    vec = pl.BlockSpec((1, 1024), lambda i: (0, 0))
    return pl.pallas_call(
        functools.partial(_combine_kernel, tm=tm, alpha=alpha),
        grid=(nt,),
        in_specs=[pl.BlockSpec((tm * TOP_K,), lambda i: (i,), memory_space=pltpu.SMEM),
                  pl.BlockSpec((tm * TOP_K,), lambda i: (jnp.minimum(i + 1, nt - 1),), memory_space=pltpu.SMEM),
                  pl.BlockSpec((tm, LANES), lambda i: (i, 0)),
                  act,
                  pl.BlockSpec((tm, PLE_DIM), lambda i: (layer * nt + i, 0)),
                  pl.BlockSpec(memory_space=pl.ANY),
                  pl.BlockSpec((1024, 1024), lambda i: (0, 0)),
                  pl.BlockSpec((PLE_DIM, 1024), lambda i: (0, 0)),
                  vec, vec, vec, vec],
        out_specs=act,
        out_shape=jax.ShapeDtypeStruct((T, 1024), f32),
        scratch_shapes=[pltpu.VMEM((2, TOP_K, tm * ROW_TILE, LANES), f32), pltpu.SemaphoreType.DMA((2,))],
        compiler_params=_params(("arbitrary",)),
        name="combine_ple",
    )(dest_flat, dest_flat, gates, x2d, p2d, obuf, wg, wp, g2, b2, g3, b3)


def _t5_bucket_np(dist):
    max_exact = REL_BUCKETS // 2
    d = np.maximum(dist.astype(np.float32), np.float32(1.0))
    large = max_exact + (np.log(d / np.float32(max_exact)) / np.float32(math.log(REL_MAX_DISTANCE / max_exact))
                         * np.float32(REL_BUCKETS - max_exact)).astype(np.int32)
    large = np.minimum(large, REL_BUCKETS - 1)
    return np.where(dist < max_exact, dist, large)


def _swa_bias_table(rel_bias):
    kj = np.arange(2 * WINDOW)[None, :]
    dist = (np.arange(WINDOW)[:, None] + WINDOW) - kj
    in_window = (dist >= 0) & (dist < WINDOW)
    bucket = _t5_bucket_np(np.maximum(dist, 0))
    bias = jnp.transpose(rel_bias[bucket].astype(f32), (2, 0, 1))
    return jnp.where(jnp.asarray(in_window)[None], bias, NEG_BIG)


def _pad_row(v, width=LANES, fill=0.0):
    v = v.astype(f32).reshape(1, -1)
    return jnp.pad(v, ((0, 0), (0, width - v.shape[1])), constant_values=fill)


def _wcat(w_in):
    cols = [w_in[:, 0:4096], w_in[:, 4112:6160], w_in[:, 6160:7184], w_in[:, 7696:10768],
            w_in[:, 7184:7440], w_in[:, 7440:7696], w_in[:, 4096:4112],
            jnp.zeros((w_in.shape[0], FAT_W - 10768), w_in.dtype)]
    return jnp.concatenate(cols, axis=1).astype(bf16)


def kernel(x, p, w_in, conv_qkv_w, gdn_a_log, gdn_dt_bias, gdn_norm_w, rg_conv_w, rg_conv_b, rg_w_a, rg_b_a, rg_w_x, rg_b_x, rg_lambda, attn_sinks, rel_bias, w_o_gdn, w_o_lru, w_o_swa, w_out, ln1_g, ln1_b, router_w, router_b, w_gu, b_gu, w_down, b_down, ln2_g, ln2_b, ple_w_gate, ple_w_proj, ln3_g, ln3_b):
    B, S, D = x.shape
    depth = w_in.shape[0]
    T = B * S
    A = T * TOP_K
    alpha = (2.0 * depth) ** 0.25
    P = A + N_EXPERTS * EXPERT_BLK
    nblk = P // EXPERT_BLK
    row = lambda v: v.astype(f32).reshape(1, -1)

    bias_tab = _swa_bias_table(rel_bias)
    p2d = p.reshape(depth * T, PLE_DIM)
    xc = x.reshape(T, D)
    for i in range(depth):
        fat = inproj(xc, _wcat(w_in[i]))
        o_gdn = gdn_branch(fat, conv_qkv_w[i], _pad_row(gdn_a_log[i]), _pad_row(gdn_dt_bias[i]),
                           row(gdn_norm_w[i]), B, S)
        wax = jnp.concatenate([rg_w_a[i], rg_w_x[i]], axis=-1).astype(bf16)
        o_lru = lru_branch(fat, rg_conv_w[i], row(rg_conv_b[i]), wax, row(rg_b_a[i]), row(rg_b_x[i]),
                           row(rg_lambda[i]), B, S)
        sinks_tab = jnp.broadcast_to(attn_sinks[i].astype(f32)[:, None], (SWA_Q_HEADS, LANES))
        o_swa = swa_branch(fat, bias_tab, sinks_tab, B, S)
        x1, x1t = merge_ln(o_gdn, o_lru, o_swa, fat, xc, w_o_gdn[i].astype(bf16), w_o_lru[i].astype(bf16),
                      w_o_swa[i].astype(bf16), w_out[i].astype(bf16), row(ln1_g[i]), row(ln1_b[i]), alpha)

        rw_pad = jnp.pad(router_w[i], ((0, 0), (0, LANES - N_EXPERTS)))
        rb_pad = _pad_row(router_b[i], fill=NEG_BIG)
        gates, eidx, rank, cnt = route(x1, rw_pad, rb_pad)
        counts = cnt[0, :N_EXPERTS].astype(i32)
        padded = ((counts + EXPERT_BLK - 1) // EXPERT_BLK) * EXPERT_BLK
        pad_ends = jnp.cumsum(padded)
        pad_starts = pad_ends - padded
        dest = (pad_starts[eidx[:, :TOP_K]] + rank[:, :TOP_K]).reshape(A)
        blk_start = jnp.arange(nblk, dtype=i32) * EXPERT_BLK
        blk_e = jnp.minimum(jnp.sum((pad_ends[None, :] <= blk_start[:, None]).astype(i32), axis=1),
                            N_EXPERTS - 1).astype(i32)
        nused = (pad_ends[-1:] // EXPERT_BLK).astype(i32)

        buf = dispatch(dest, x1t, jnp.zeros((P * ROW_TILE, LANES), f32))
        bgu = jnp.transpose(b_gu[i].reshape(N_EXPERTS, -1, LANES, 2), (0, 1, 3, 2)).reshape(N_EXPERTS, 1, -1)
        obuf = experts(blk_e, nused, buf, gu_prep(w_gu, i), bgu, w_down[i].astype(bf16), b_down[i][:, None, :])
        xc = combine_ple(dest, gates, x1, p2d, i, obuf, ple_w_gate[i].astype(bf16),
                         ple_w_proj[i].astype(bf16), row(ln2_g[i]), row(ln2_b[i]), row(ln3_g[i]), row(ln3_b[i]),
                         alpha)
    return xc.reshape(B, S, D)
```

```python
import functools
import math

import numpy as np
import jax
import jax.numpy as jnp
from jax import lax
from jax.experimental import pallas as pl
from jax.experimental.pallas import tpu as pltpu

f32 = jnp.float32
bf16 = jnp.bfloat16
i32 = jnp.int32

D_MODEL = 1024
PLE_DIM = 256
GDN_HEADS = 8
GDN_HEAD_DIM = 128
GDN_CHUNK = 64
CONV_WIDTH = 4
LRU_BLOCKS = 8
LRU_BLOCK_DIM = 128
LRU_C = 8.0
SWA_Q_HEADS = 16
SWA_KV_HEADS = 4
SWA_HEAD_DIM = 64
SWA_GROUP = 4
WINDOW = 128
REL_BUCKETS = 32
REL_MAX_DISTANCE = 128
N_EXPERTS = 32
TOP_K = 4
SWIGLU_LIMIT = 7.0
SWIGLU_ALPHA = 1.702
LN_EPS = 1e-5
NORM_EPS = 1e-6
NEG_BIG = -1e30

LANES = 128
VMEM_LIMIT = 56 * 1024 * 1024

C_GQ, C_GK, C_GV, C_GZ = 0, 1024, 2048, 3072
C_LX, C_LG = 4096, 5120
C_SQ = 6144
C_MA, C_MB, C_MC = 7168, 8192, 9216
C_SK, C_SV = 10240, 10496
C_AB = 10752
FAT_W = 10880
FAT_TN = 2176

EXPERT_BLK = 512
CHUNKS_IN_FLIGHT = 2


def _params(sem):
    return pltpu.CompilerParams(dimension_semantics=sem, vmem_limit_bytes=VMEM_LIMIT)


def _sigmoid(x):
    return 1.0 / (1.0 + jnp.exp(-x))


def _softplus(x):
    return jnp.maximum(x, 0.0) + jnp.log(1.0 + jnp.exp(-jnp.abs(x)))


def _layer_norm(z, g, b):
    mu = jnp.mean(z, axis=-1, keepdims=True)
    zc = z - mu
    var = jnp.mean(zc * zc, axis=-1, keepdims=True)
    return zc * lax.rsqrt(var + LN_EPS) * g + b


def _inproj_kernel(x_ref, w_ref, o_ref, xb_ref):
    @pl.when(pl.program_id(1) == 0)
    def _():
        xb_ref[...] = x_ref[...].astype(bf16)

    o_ref[...] = jnp.dot(xb_ref[...], w_ref[...], preferred_element_type=f32).astype(o_ref.dtype)


def inproj(x2d, wcat, tm=2048):
    T, K = x2d.shape
    N = wcat.shape[1]
    tn = FAT_TN
    return pl.pallas_call(
        _inproj_kernel,
        grid=(T // tm, N // tn),
        in_specs=[pl.BlockSpec((tm, K), lambda i, j: (i, 0)),
                  pl.BlockSpec((K, tn), lambda i, j: (0, j))],
        out_specs=pl.BlockSpec((tm, tn), lambda i, j: (i, j)),
        out_shape=jax.ShapeDtypeStruct((T, N), bf16),
        scratch_shapes=[pltpu.VMEM((tm, K), bf16)],
        compiler_params=_params(("arbitrary", "arbitrary")),
        name="inproj",
    )(x2d, wcat)


def _shift_matrices(ts):
    ri = lax.broadcasted_iota(i32, (ts, ts), 0)
    ci = lax.broadcasted_iota(i32, (ts, ts), 1)
    return [jnp.where(ri - ci == d, 1.0, 0.0).astype(bf16) for d in (3, 2, 1)]


def _causal_conv_silu(src_ref, dst_ref, cz_ref, carry_ref, w, smats, ts, head_scale=None):
    xb = src_ref[...]
    x = xb.astype(f32)
    y = w[3:4, :] * x
    for j, sm in enumerate(smats):
        y = y + w[j:j + 1, :] * jnp.dot(sm, xb, preferred_element_type=f32)
    cz_ref[0:8, :] = carry_ref[...]
    cz_ref[8:16, :] = jnp.zeros((8, x.shape[1]), f32)
    corr = w[0:1, :] * cz_ref[5:13, :] + w[1:2, :] * cz_ref[6:14, :] + w[2:3, :] * cz_ref[7:15, :]
    carry_ref[...] = x[ts - 8:ts, :]

    def post(rows, yv):
        a = yv * _sigmoid(yv)
        if head_scale is None:
            dst_ref[rows, :] = a
            return
        for h in range(a.shape[1] // LANES):
            cs = slice(h * LANES, (h + 1) * LANES)
            ah = a[:, cs]
            dst_ref[rows, cs] = ah * (lax.rsqrt(jnp.sum(ah * ah, axis=-1, keepdims=True) + NORM_EPS) * head_scale)

    post(slice(0, 8), y[0:8] + corr)
    post(slice(8, ts), y[8:ts])


def _gdn_kernel(q_ref, k_ref, v_ref, z_ref, ab_ref, cw_ref, alog_ref, dtb_ref, nw_ref, o_ref,
                xpad_ref, qs_ref, ks_ref, vs_ref, carry_ref, state_ref, g_ref, beta_ref,
                u_s, lhs_s, intra_s, kdt_s, *, ts):
    C = GDN_CHUNK
    D = GDN_HEAD_DIM
    P = 2 * C

    @pl.when(pl.program_id(1) == 0)
    def _():
        carry_ref[...] = jnp.zeros_like(carry_ref)
        state_ref[...] = jnp.zeros_like(state_ref)

    smats = _shift_matrices(ts)
    for p, (src, dst, scale) in enumerate(((q_ref, qs_ref, D ** -0.5), (k_ref, ks_ref, 1.0), (v_ref, vs_ref, None))):
        _causal_conv_silu(src, dst, xpad_ref, carry_ref.at[p], cw_ref[:, p * 1024:(p + 1) * 1024], smats, ts, scale)

    ab = ab_ref[...].astype(f32)
    g = -jnp.exp(alog_ref[...]) * _softplus(ab + dtb_ref[...])
    rin = lax.broadcasted_iota(i32, (ts, LANES), 0) & (C - 1)
    gsum = g
    for d in (1, 2, 4, 8, 16, 32):
        gsum = gsum + jnp.where(rin >= d, pltpu.roll(gsum, d, 0), 0.0)
    g_ref[...] = gsum
    beta_ref[...] = _sigmoid(ab)

    ri = lax.broadcasted_iota(i32, (P, P), 0)
    ci = lax.broadcasted_iota(i32, (P, P), 1)
    same = (ri >= C) == (ci >= C)
    eye = ri == ci
    causal = same & (ri >= ci)
    strict = same & (ri > ci)
    eye_f = jnp.where(eye, 1.0, 0.0).astype(f32)
    first_cols = ci < C
    nw = nw_ref[...]

    def stack(a, b):
        return jnp.concatenate([a, b], axis=0)

    def mm(a, b):
        return jnp.dot(a, b, preferred_element_type=f32)

    npair = GDN_HEADS // 2
    nchunk = ts // C
    hcols = [slice(h * D, (h + 1) * D) for h in range(GDN_HEADS)]

    for cg in range(0, nchunk, CHUNKS_IN_FLIGHT):
        probs = [(c, hp) for c in range(cg, cg + CHUNKS_IN_FLIGHT) for hp in range(npair)]
        qn, kn, vb, gcol, eg, egl, kb = [], [], [], [], [], [], []
        for c, hp in probs:
            rows = slice(c * C, (c + 1) * C)
            c0, c1 = hcols[2 * hp], hcols[2 * hp + 1]
            gc = g_ref[rows, :]
            bc = beta_ref[rows, :]
            qn.append(stack(qs_ref[rows, c0], qs_ref[rows, c1]))
            kn.append(stack(ks_ref[rows, c0], ks_ref[rows, c1]))
            v2 = stack(vs_ref[rows, c0], vs_ref[rows, c1])
            h0, h1 = 2 * hp, 2 * hp + 1
            gcl = stack(gc[:, h0:h0 + 1], gc[:, h1:h1 + 1])
            bcl = stack(bc[:, 8 + h0:9 + h0], bc[:, 8 + h1:9 + h1])
            glast = stack(jnp.broadcast_to(gc[C - 1:C, h0:h0 + 1], (C, 1)),
                          jnp.broadcast_to(gc[C - 1:C, h1:h1 + 1], (C, 1)))
            gcol.append(gcl)
            eg.append(jnp.exp(gcl))
            egl.append(jnp.exp(glast - gcl))
            kb.append(kn[-1] * bcl)
            vb.append(v2 * bcl)
        n = len(probs)
        a2 = [lax.dot_general(stack(kb[i], qn[i]).astype(bf16), kn[i].astype(bf16), (((1,), (1,)), ((), ())),
                              preferred_element_type=f32) for i in range(n)]
        lmat, intra = [], []
        for i in range(n):
            gm = jnp.broadcast_to(gcol[i], (P, P))
            grow = jnp.sum(jnp.where(eye, gm, 0.0), axis=0, keepdims=True)
            decay = jnp.where(causal, jnp.exp(jnp.minimum(gm - grow, 0.0)), 0.0)
            lmat.append(jnp.where(strict, a2[i][:P] * decay, 0.0))
            intra.append(a2[i][P:] * decay)
        lb = [l.astype(bf16) for l in lmat]
        xm = [eye_f - l for l in lmat]
        pm = [mm(b, b) for b in lb]
        for it in range(5):
            pb = [p.astype(bf16) for p in pm]
            xm = [x + mm(x.astype(bf16), b) for x, b in zip(xm, pb)]
            if it < 4:
                pm = [mm(b, b) for b in pb]
        uw = [mm(xm[i].astype(bf16), jnp.concatenate([vb[i], kb[i] * eg[i]], axis=1).astype(bf16))
              for i in range(n)]
        for i, (c, hp) in enumerate(probs):
            j = c * npair + hp
            w2 = uw[i][:, D:]
            qd = qn[i] * eg[i]
            kdt = (kn[i] * egl[i]).T
            u_s[j] = uw[i][:, :D]
            lhs_s[j, 0] = stack(w2[:C], qd[:C]).astype(bf16)
            lhs_s[j, 1] = stack(w2[C:], qd[C:]).astype(bf16)
            intra_s[j] = intra[i].astype(bf16)
            kdt_s[j, 0] = jnp.where(first_cols, kdt, 0.0).astype(bf16)
            kdt_s[j, 1] = jnp.where(first_cols, 0.0, kdt).astype(bf16)

    for c in range(nchunk):
        rows = slice(c * C, (c + 1) * C)
        gl = g_ref[(c + 1) * C - 1:(c + 1) * C, :]
        st = [state_ref[h] for h in range(GDN_HEADS)]
        wq = [mm(lhs_s[c * npair + h // 2, h % 2], st[h].astype(bf16)) for h in range(GDN_HEADS)]
        vnb = [(u_s[c * npair + hp] - stack(wq[2 * hp][:C], wq[2 * hp + 1][:C])).astype(bf16)
               for hp in range(npair)]
        o2 = [stack(wq[2 * hp][C:], wq[2 * hp + 1][C:]) + mm(intra_s[c * npair + hp], vnb[hp])
              for hp in range(npair)]
        for h in range(GDN_HEADS):
            state_ref[h] = st[h] * jnp.exp(gl[:, h:h + 1]) + mm(kdt_s[c * npair + h // 2, h % 2], vnb[h // 2])
        for hp in range(npair):
            c0, c1 = hcols[2 * hp], hcols[2 * hp + 1]
            z2 = stack(z_ref[rows, c0], z_ref[rows, c1]).astype(f32)
            on = (o2[hp] * lax.rsqrt(jnp.mean(o2[hp] * o2[hp], axis=-1, keepdims=True) + NORM_EPS) * nw
                  * (z2 * _sigmoid(z2))).astype(o_ref.dtype)
            o_ref[rows, c0] = on[:C]
            o_ref[rows, c1] = on[C:]


def gdn_branch(fat, conv_w, a_log_row, dt_bias_row, norm_w_row, B, S, ts=256):
    T = B * S
    ns = S // ts
    nprob = (ts // GDN_CHUNK) * (GDN_HEADS // 2)
    row = lambda b, s: b * ns + s
    blk = lambda cb: pl.BlockSpec((ts, 1024), lambda b, s: (row(b, s), cb))
    full = lambda shp: pl.BlockSpec(shp, lambda b, s: (0,) * len(shp))
    return pl.pallas_call(
        functools.partial(_gdn_kernel, ts=ts),
        grid=(B, ns),
        in_specs=[blk(C_GQ // 1024), blk(C_GK // 1024), blk(C_GV // 1024), blk(C_GZ // 1024),
                  pl.BlockSpec((ts, LANES), lambda b, s: (row(b, s), C_AB // LANES)),
                  full((CONV_WIDTH, 3072)), full((1, LANES)), full((1, LANES)), full((1, LANES))],
        out_specs=pl.BlockSpec((ts, 1024), lambda b, s: (row(b, s), 0)),
        out_shape=jax.ShapeDtypeStruct((T, 1024), bf16),
        scratch_shapes=[pltpu.VMEM((16, 1024), f32),
                        pltpu.VMEM((ts, 1024), f32), pltpu.VMEM((ts, 1024), f32), pltpu.VMEM((ts, 1024), f32),
                        pltpu.VMEM((3, 8, 1024), f32),
                        pltpu.VMEM((GDN_HEADS, GDN_HEAD_DIM, GDN_HEAD_DIM), f32),
                        pltpu.VMEM((ts, LANES), f32), pltpu.VMEM((ts, LANES), f32),
                        pltpu.VMEM((nprob, 128, GDN_HEAD_DIM), f32),
                        pltpu.VMEM((nprob, 2, 128, GDN_HEAD_DIM), bf16),
                        pltpu.VMEM((nprob, 128, 128), bf16),
                        pltpu.VMEM((nprob, 2, GDN_HEAD_DIM, 128), bf16)],
        compiler_params=_params(("arbitrary", "arbitrary")),
        name="gdn",
    )(fat, fat, fat, fat, fat, conv_w, a_log_row, dt_bias_row, norm_w_row)


def _lru_kernel(x_ref, gate_ref, cw_ref, cb_ref, wax_ref, ba_ref, bx_ref, lam_ref, o_ref,
                cz_ref, xc_ref, a_ref, u_ref, h_ref, carry_ref, hc_ref, *, ts):
    half = ts // 2

    @pl.when(pl.program_id(1) == 0)
    def _():
        carry_ref[...] = jnp.zeros_like(carry_ref)
        hc_ref[...] = jnp.zeros_like(hc_ref)

    w = cw_ref[...]
    smats = _shift_matrices(half)
    cz_ref[8:16, :] = jnp.zeros((8, D_MODEL), f32)
    for hf in range(2):
        rows = slice(hf * half, (hf + 1) * half)
        xb = x_ref[rows, :]
        x = xb.astype(f32)
        y = w[3:4, :] * x + cb_ref[...]
        for j, sm in enumerate(smats):
            y = y + w[j:j + 1, :] * jnp.dot(sm, xb, preferred_element_type=f32)
        cz_ref[0:8, :] = carry_ref[...]
        corr = w[0:1, :] * cz_ref[5:13, :] + w[1:2, :] * cz_ref[6:14, :] + w[2:3, :] * cz_ref[7:15, :]
        carry_ref[...] = x[half - 8:half, :]
        xc_ref[hf * half:hf * half + 8, :] = y[0:8] + corr
        xc_ref[hf * half + 8:(hf + 1) * half, :] = y[8:half]

    nsp = _softplus(-lam_ref[...])
    for blk in range(LRU_BLOCKS):
        cs = slice(blk * LRU_BLOCK_DIM, (blk + 1) * LRU_BLOCK_DIM)
        xc = xc_ref[:, cs]
        ri = jnp.dot(xc.astype(bf16), wax_ref[blk], preferred_element_type=f32)
        r = _sigmoid(ri[:, :LRU_BLOCK_DIM] + ba_ref[:, cs])
        i = _sigmoid(ri[:, LRU_BLOCK_DIM:] + bx_ref[:, cs])
        a = jnp.exp(-LRU_C * r * nsp[:, cs])
        a_ref[:, cs] = a
        u_ref[:, cs] = jnp.sqrt(1.0 - a * a) * (i * xc)

    rowi = lax.broadcasted_iota(i32, (8, D_MODEL), 0)

    def group(gi, h):
        r0 = pl.multiple_of(gi * 8, 8)
        a = a_ref[pl.ds(r0, 8), :]
        b = u_ref[pl.ds(r0, 8), :]
        for d in (1, 2, 4):
            m = rowi >= d
            a_s = pltpu.roll(a, d, 0)
            b_s = pltpu.roll(b, d, 0)
            b = jnp.where(m, a * b_s + b, b)
            a = jnp.where(m, a * a_s, a)
        hh = a * h + b
        h_ref[pl.ds(r0, 8), :] = hh
        return hh[7:8, :]

    hc_ref[...] = lax.fori_loop(0, ts // 8, group, hc_ref[...])
    gt = gate_ref[...].astype(f32)
    o_ref[...] = (h_ref[...] * jax.nn.gelu(gt)).astype(o_ref.dtype)


def lru_branch(fat, conv_w, conv_b, wax, b_a, b_x, lam, B, S, ts=512):
    T = B * S
    ns = S // ts
    row = lambda b, s: b * ns + s
    full = lambda shp: pl.BlockSpec(shp, lambda b, s: (0,) * len(shp))
    return pl.pallas_call(
        functools.partial(_lru_kernel, ts=ts),
        grid=(B, ns),
        in_specs=[pl.BlockSpec((ts, 1024), lambda b, s: (row(b, s), C_LX // 1024)),
                  pl.BlockSpec((ts, 1024), lambda b, s: (row(b, s), C_LG // 1024)),
                  full((CONV_WIDTH, 1024)), full((1, 1024)), full((LRU_BLOCKS, LRU_BLOCK_DIM, 2 * LRU_BLOCK_DIM)),
                  full((1, 1024)), full((1, 1024)), full((1, 1024))],
        out_specs=pl.BlockSpec((ts, 1024), lambda b, s: (row(b, s), 0)),
        out_shape=jax.ShapeDtypeStruct((T, 1024), bf16),
        scratch_shapes=[pltpu.VMEM((16, 1024), f32), pltpu.VMEM((ts, 1024), f32),
                        pltpu.VMEM((ts, 1024), f32), pltpu.VMEM((ts, 1024), f32), pltpu.VMEM((ts, 1024), f32),
                        pltpu.VMEM((8, 1024), f32), pltpu.VMEM((1, 1024), f32)],
        compiler_params=_params(("arbitrary", "arbitrary")),
        name="lru",
    )(fat, fat, conv_w, conv_b, wax, b_a, b_x, lam)


def _swa_kernel(q_ref, kc_ref, kp_ref, vc_ref, vp_ref, bias_ref, sink_ref, o_ref, kb_ref, vb_ref, *, tq):
    W = WINDOW
    hd = SWA_HEAD_DIM
    kb_ref[0:W, :] = kp_ref[...]
    kb_ref[W:W + tq, :] = kc_ref[...]
    vb_ref[0:W, :] = vp_ref[...]
    vb_ref[W:W + tq, :] = vc_ref[...]
    first_tile = pl.program_id(1) == 0
    col = lax.broadcasted_iota(i32, (W, 2 * W), 1)
    scale = hd ** -0.5

    def qblock(n, carry):
        r0 = pl.multiple_of(n * W, W)
        pen = jnp.where(jnp.logical_and(first_tile, n == 0), NEG_BIG, 0.0).astype(f32)
        penm = jnp.where(col < W, pen, 0.0)
        heads = range(SWA_Q_HEADS)
        kk = [kb_ref[pl.ds(r0, 2 * W), hk * hd:(hk + 1) * hd] for hk in range(SWA_KV_HEADS)]
        vv = [vb_ref[pl.ds(r0, 2 * W), hk * hd:(hk + 1) * hd] for hk in range(SWA_KV_HEADS)]
        qs = q_ref[pl.ds(r0, W), :] * scale
        s = [lax.dot_general(qs[:, h * hd:(h + 1) * hd], kk[h // SWA_GROUP], (((1,), (1,)), ((), ())),
                             preferred_element_type=f32) + (bias_ref[h] + penm) for h in heads]
        sink = [sink_ref[h:h + 1, 0:1] for h in heads]
        m = [jnp.maximum(jnp.max(s[h], axis=-1, keepdims=True), sink[h]) for h in heads]
        pe = [jnp.exp(s[h] - m[h]) for h in heads]
        o = [jnp.dot(pe[h].astype(bf16), vv[h // SWA_GROUP], preferred_element_type=f32) for h in heads]
        den = [jnp.sum(pe[h], axis=-1, keepdims=True) + jnp.exp(sink[h] - m[h]) for h in heads]
        o_ref[pl.ds(r0, W), :] = jnp.concatenate([o[h] / den[h] for h in heads], axis=1).astype(o_ref.dtype)
        return carry

    lax.fori_loop(0, tq // W, qblock, 0)


def swa_branch(fat, bias_tab, sinks_tab, B, S, tq=512):
    T = B * S
    ns = S // tq
    nb = S // WINDOW
    per = tq // WINDOW
    row = lambda b, s: b * ns + s
    prev = lambda b, s: b * nb + jnp.maximum(s * per - 1, 0)
    kvw = SWA_KV_HEADS * SWA_HEAD_DIM
    full = lambda shp: pl.BlockSpec(shp, lambda b, s: (0,) * len(shp))
    return pl.pallas_call(
        functools.partial(_swa_kernel, tq=tq),
        grid=(B, ns),
        in_specs=[pl.BlockSpec((tq, 1024), lambda b, s: (row(b, s), C_SQ // 1024)),
                  pl.BlockSpec((tq, kvw), lambda b, s: (row(b, s), C_SK // kvw)),
                  pl.BlockSpec((WINDOW, kvw), lambda b, s: (prev(b, s), C_SK // kvw)),
                  pl.BlockSpec((tq, kvw), lambda b, s: (row(b, s), C_SV // kvw)),
                  pl.BlockSpec((WINDOW, kvw), lambda b, s: (prev(b, s), C_SV // kvw)),
                  full((SWA_Q_HEADS, WINDOW, 2 * WINDOW)), full((SWA_Q_HEADS, LANES))],
        out_specs=pl.BlockSpec((tq, 1024), lambda b, s: (row(b, s), 0)),
        out_shape=jax.ShapeDtypeStruct((T, 1024), bf16),
        scratch_shapes=[pltpu.VMEM((tq + WINDOW, kvw), bf16), pltpu.VMEM((tq + WINDOW, kvw), bf16)],
        compiler_params=_params(("arbitrary", "arbitrary")),
        name="swa",
    )(fat, fat, fat, fat, fat, bias_tab, sinks_tab)


ROW_TILE = D_MODEL // LANES


def _load_row_tiles(ref, n, lead=()):
    return jnp.concatenate([ref[lead + (pl.ds(s, n, stride=ROW_TILE), slice(None))] for s in range(ROW_TILE)], axis=1)


def _store_row_tiles(ref, val):
    n = val.shape[0]
    for s in range(ROW_TILE):
        ref[pl.ds(s, n, stride=ROW_TILE), :] = val[:, s * LANES:(s + 1) * LANES]


def _merge_kernel(oa_ref, ob_ref, oc_ref, ga_ref, gb_ref, gc_ref, x_ref, wa_ref, wb_ref, wc_ref, wo_ref,
                  g_ref, b_ref, o_ref, ot_ref, zbuf_ref, zero_ref, zsem, *, alpha, zero_rows):
    i = pl.program_id(0)

    @pl.when(i == 0)
    def _():
        zero_ref[...] = jnp.zeros_like(zero_ref)

    zr = zero_ref.shape[0]
    fills = [pltpu.make_async_copy(zero_ref, zbuf_ref.at[pl.ds(pl.multiple_of(i * zero_rows + c * zr, zr), zr)], zsem)
             for c in range(zero_rows // zr)]
    for f in fills:
        f.start()
    ya = jnp.dot(oa_ref[...], wa_ref[...], preferred_element_type=f32)
    yb = jnp.dot(ob_ref[...], wb_ref[...], preferred_element_type=f32)
    yc = jnp.dot(oc_ref[...], wc_ref[...], preferred_element_type=f32)
    mix = (_sigmoid(ga_ref[...].astype(f32)) * ya + _sigmoid(gb_ref[...].astype(f32)) * yb
           + _sigmoid(gc_ref[...].astype(f32)) * yc)
    y = jnp.dot(mix.astype(bf16), wo_ref[...], preferred_element_type=f32)
    x1 = _layer_norm(alpha * x_ref[...] + y, g_ref[...], b_ref[...])
    o_ref[...] = x1
    _store_row_tiles(ot_ref, x1)
    for f in fills:
        f.wait()


ZERO_FILL_COPIES = 4


def merge_ln(oa, ob, oc, fat, x2d, wa, wb, wc, wo, g, b, alpha, buf_rows, tm=512):
    T = x2d.shape[0]
    zero_rows = buf_rows // (T // tm)
    assert zero_rows * (T // tm) == buf_rows and zero_rows % (ZERO_FILL_COPIES * ROW_TILE) == 0
    act = pl.BlockSpec((tm, 1024), lambda i: (i, 0))
    fatb = lambda cb: pl.BlockSpec((tm, 1024), lambda i: (i, cb))
    wsp = pl.BlockSpec((1024, 1024), lambda i: (0, 0))
    vec = pl.BlockSpec((1, 1024), lambda i: (0, 0))
    return pl.pallas_call(
        functools.partial(_merge_kernel, alpha=alpha, zero_rows=zero_rows),
        grid=(T // tm,),
        in_specs=[act, act, act, fatb(C_MA // 1024), fatb(C_MB // 1024), fatb(C_MC // 1024), act,
                  wsp, wsp, wsp, wsp, vec, vec],
        out_specs=[act, pl.BlockSpec((tm * ROW_TILE, LANES), lambda i: (i, 0)), pl.BlockSpec(memory_space=pl.ANY)],
        out_shape=[jax.ShapeDtypeStruct((T, 1024), f32), jax.ShapeDtypeStruct((T * ROW_TILE, LANES), f32),
                   jax.ShapeDtypeStruct((buf_rows, LANES), f32)],
        scratch_shapes=[pltpu.VMEM((zero_rows // ZERO_FILL_COPIES, LANES), f32), pltpu.SemaphoreType.DMA(())],
        compiler_params=_params(("arbitrary",)),
        name="merge_ln",
    )(oa, ob, oc, fat, fat, fat, x2d, wa, wb, wc, wo, g, b)


def _route_kernel(x_ref, rw_ref, rb_ref, gates_ref, eidx_ref, rank_ref, cnt_ref, run_ref, *, tm):
    @pl.when(pl.program_id(0) == 0)
    def _():
        run_ref[...] = jnp.zeros_like(run_ref)

    logits = jnp.dot(x_ref[...], rw_ref[...], preferred_element_type=f32,
                     precision=lax.Precision.HIGHEST) + rb_ref[...]
    lane = lax.broadcasted_iota(i32, (tm, LANES), 1)
    lane_f = lane.astype(f32)
    work = logits
    vals, idxs, hots = [], [], []
    for _ in range(TOP_K):
        m = jnp.max(work, axis=-1, keepdims=True)
        idx = jnp.min(jnp.where(work == m, lane_f, float(LANES)), axis=-1, keepdims=True)
        hot = lane_f == idx
        vals.append(m)
        idxs.append(idx)
        hots.append(hot)
        work = jnp.where(hot, -jnp.inf, work)
    es = [jnp.exp(v - vals[0]) for v in vals]
    den = es[0] + es[1] + es[2] + es[3]
    sel = jnp.zeros((tm, LANES), f32)
    for hot in hots:
        sel = sel + jnp.where(hot, 1.0, 0.0)
    ri = lax.broadcasted_iota(i32, (tm, tm), 0)
    ci = lax.broadcasted_iota(i32, (tm, tm), 1)
    tril = jnp.where(ri > ci, 1.0, 0.0).astype(bf16)
    before = jnp.dot(tril, sel.astype(bf16), preferred_element_type=f32) + run_ref[...]
    run_ref[...] = run_ref[...] + jnp.sum(sel, axis=0, keepdims=True)
    cnt_ref[...] = run_ref[...]
    gates = jnp.zeros((tm, LANES), f32)
    eidx = jnp.zeros((tm, LANES), f32)
    rank = jnp.zeros((tm, LANES), f32)
    for k in range(TOP_K):
        rk = jnp.sum(jnp.where(hots[k], before, 0.0), axis=-1, keepdims=True)
        gates = jnp.where(lane == k, es[k] / den, gates)
        eidx = jnp.where(lane == k, idxs[k], eidx)
        rank = jnp.where(lane == k, rk, rank)
    gates_ref[...] = gates
    eidx_ref[...] = eidx.astype(i32)
    rank_ref[...] = rank.astype(i32)


def route(x2d, rw_pad, rb_pad, tm=512):
    T = x2d.shape[0]
    outb = pl.BlockSpec((tm, LANES), lambda i: (i, 0))
    return pl.pallas_call(
        functools.partial(_route_kernel, tm=tm),
        grid=(T // tm,),
        in_specs=[pl.BlockSpec((tm, 1024), lambda i: (i, 0)),
                  pl.BlockSpec((1024, LANES), lambda i: (0, 0)),
                  pl.BlockSpec((1, LANES), lambda i: (0, 0))],
        out_specs=[outb, outb, outb, pl.BlockSpec((1, LANES), lambda i: (0, 0))],
        out_shape=[jax.ShapeDtypeStruct((T, LANES), f32), jax.ShapeDtypeStruct((T, LANES), i32),
                   jax.ShapeDtypeStruct((T, LANES), i32), jax.ShapeDtypeStruct((1, LANES), f32)],
        scratch_shapes=[pltpu.VMEM((1, LANES), f32)],
        compiler_params=_params(("arbitrary",)),
        name="route",
    )(x2d, rw_pad, rb_pad)


ROW_UNROLL = 8


def _row_copy(src_ref, src_row, dst_ref, dst_row, sem):
    tile = lambda r: pl.ds(pl.multiple_of(r * ROW_TILE, ROW_TILE), ROW_TILE)
    return pltpu.make_async_copy(src_ref.at[tile(src_row)], dst_ref.at[tile(dst_row)], sem)


def _dispatch_kernel(dest_ref, x_ref, buf_in_ref, buf_ref, sem, *, tm):
    del buf_in_ref

    def issue(g, carry):
        for j in range(ROW_UNROLL):
            r = g * ROW_UNROLL + j
            for k in range(TOP_K):
                _row_copy(x_ref, r, buf_ref, dest_ref[r * TOP_K + k], sem).start(priority=k % 2)
        return carry

    lax.fori_loop(0, tm // ROW_UNROLL, issue, 0)

    def drain(g, carry):
        for j in range(ROW_UNROLL * TOP_K):
            _row_copy(x_ref, 0, buf_ref, 0, sem).wait()
        return carry

    lax.fori_loop(0, tm // ROW_UNROLL, drain, 0)


def dispatch(dest_flat, xt, buf0, tm=512):
    T = xt.shape[0] // ROW_TILE
    return pl.pallas_call(
        functools.partial(_dispatch_kernel, tm=tm),
        grid=(T // tm,),
        in_specs=[pl.BlockSpec((tm * TOP_K,), lambda i: (i,), memory_space=pltpu.SMEM),
                  pl.BlockSpec((tm * ROW_TILE, LANES), lambda i: (i, 0)),
                  pl.BlockSpec(memory_space=pl.ANY)],
        out_specs=pl.BlockSpec(memory_space=pl.ANY),
        out_shape=jax.ShapeDtypeStruct(buf0.shape, buf0.dtype),
        scratch_shapes=[pltpu.SemaphoreType.DMA(())],
        input_output_aliases={2: 0},
        compiler_params=_params(("arbitrary",)),
        name="dispatch",
    )(dest_flat, xt, buf0)


GU_GROUP = 2 * LANES


def _gu_prep_kernel(w_ref, o_ref):
    ri = lax.broadcasted_iota(i32, (GU_GROUP, GU_GROUP), 0)
    ci = lax.broadcasted_iota(i32, (GU_GROUP, GU_GROUP), 1)
    src = jnp.where(ci < LANES, 2 * ci, 2 * (ci - LANES) + 1)
    perm = jnp.where(ri == src, 1.0, 0.0).astype(bf16)
    for g in range(w_ref.shape[3] // GU_GROUP):
        cs = slice(g * GU_GROUP, (g + 1) * GU_GROUP)
        o_ref[0, :, cs] = jnp.dot(w_ref[0, 0, :, cs].astype(bf16), perm, preferred_element_type=f32).astype(bf16)


def gu_prep(w_gu_all, layer, tk=512):
    _, E, D, N = w_gu_all.shape
    return pl.pallas_call(
        _gu_prep_kernel,
        grid=(E, D // tk),
        in_specs=[pl.BlockSpec((1, 1, tk, N), lambda e, k: (layer, e, k, 0))],
        out_specs=pl.BlockSpec((1, tk, N), lambda e, k: (e, k, 0)),
        out_shape=jax.ShapeDtypeStruct((E, D, N), bf16),
        compiler_params=_params(("arbitrary", "arbitrary")),
        name="gu_prep",
    )(w_gu_all)


def _expert_kernel(blk_e_ref, nused_ref, x_ref, wgu_ref, bgu_ref, wd_ref, bd_ref, o_ref):
    i = pl.program_id(0)

    @pl.when(i < nused_ref[0])
    def _():
        de = wd_ref.shape[1]
        xb = _load_row_tiles(x_ref, EXPERT_BLK).astype(bf16)
        hgu = jnp.dot(xb, wgu_ref[0], preferred_element_type=f32) + bgu_ref[0]
        acts = []
        for g in range(2 * de // GU_GROUP):
            gate = jnp.minimum(hgu[:, g * GU_GROUP:g * GU_GROUP + LANES], SWIGLU_LIMIT)
            lin = jnp.clip(hgu[:, g * GU_GROUP + LANES:(g + 1) * GU_GROUP], -SWIGLU_LIMIT, SWIGLU_LIMIT)
            acts.append((gate * _sigmoid(SWIGLU_ALPHA * gate) * (lin + 1.0)).astype(bf16))
        act = jnp.concatenate(acts, axis=1)
        _store_row_tiles(o_ref, jnp.dot(act, wd_ref[0], preferred_element_type=f32) + bd_ref[0])

    @pl.when(i >= nused_ref[0])
    def _():
        o_ref[...] = jnp.zeros_like(o_ref)


def experts(blk_e, nused, buf, wgu, bgu, wd, bd):
    D = wd.shape[2]
    nblk = buf.shape[0] // (EXPERT_BLK * ROW_TILE)
    de = wd.shape[1]
    tile_blk = (EXPERT_BLK * ROW_TILE, LANES)
    live = lambda i, be, nu: jnp.minimum(i, nu[0] - 1)
    grid_spec = pltpu.PrefetchScalarGridSpec(
        num_scalar_prefetch=2,
        grid=(nblk,),
        in_specs=[pl.BlockSpec(tile_blk, lambda i, be, nu: (live(i, be, nu), 0)),
                  pl.BlockSpec((1, D, 2 * de), lambda i, be, nu: (be[i], 0, 0)),
                  pl.BlockSpec((1, 1, 2 * de), lambda i, be, nu: (be[i], 0, 0)),
                  pl.BlockSpec((1, de, D), lambda i, be, nu: (be[i], 0, 0)),
                  pl.BlockSpec((1, 1, D), lambda i, be, nu: (be[i], 0, 0))],
        out_specs=pl.BlockSpec(tile_blk, lambda i, be, nu: (i, 0)),
    )
    return pl.pallas_call(
        _expert_kernel,
        grid_spec=grid_spec,
        out_shape=jax.ShapeDtypeStruct(buf.shape, f32),
        compiler_params=_params(("arbitrary",)),
        name="experts",
    )(blk_e, nused, buf, wgu, bgu, wd, bd)


def _combine_kernel(dest_ref, dest_next_ref, gates_ref, x_ref, p_ref, obuf_ref, wg_ref, wp_ref,
                    g2_ref, b2_ref, g3_ref, b3_ref, o_ref, rows_ref, sems, *, tm, alpha):
    i = pl.program_id(0)
    n = pl.num_programs(0)
    slot = i % 2

    def gather(idx_ref, s):
        def issue(g, carry):
            for j in range(ROW_UNROLL):
                r = g * ROW_UNROLL + j
                for k in range(TOP_K):
                    _row_copy(obuf_ref, idx_ref[r * TOP_K + k], rows_ref.at[s, k], r, sems.at[s]).start(priority=k % 2)
            return carry

        lax.fori_loop(0, tm // ROW_UNROLL, issue, 0)

    @pl.when(i == 0)
    def _():
        gather(dest_ref, 0)

    @pl.when(i + 1 < n)
    def _():
        gather(dest_next_ref, 1 - slot)

    def drain(g, carry):
        for j in range(ROW_UNROLL * TOP_K):
            _row_copy(obuf_ref, 0, rows_ref.at[slot, 0], 0, sems.at[slot]).wait()
        return carry

    lax.fori_loop(0, tm // ROW_UNROLL, drain, 0)

    gates = gates_ref[...]
    y = gates[:, 0:1] * _load_row_tiles(rows_ref, tm, (slot, 0))
    for k in range(1, TOP_K):
        y = y + gates[:, k:k + 1] * _load_row_tiles(rows_ref, tm, (slot, k))
    x2 = _layer_norm(alpha * x_ref[...] + y, g2_ref[...], b2_ref[...])
    gate = _sigmoid(jnp.dot(x2.astype(bf16), wg_ref[...], preferred_element_type=f32))
    proj = jnp.dot(p_ref[...].astype(bf16), wp_ref[...], preferred_element_type=f32)
    o_ref[...] = _layer_norm(alpha * x2 + gate * proj, g3_ref[...], b3_ref[...])


def combine_ple(dest_flat, gates, x2d, p2d, layer, obuf, wg, wp, g2, b2, g3, b3, alpha, tm=256):
    T = x2d.shape[0]
    nt = T // tm
    act = pl.BlockSpec((tm, 1024), lambda i: (i, 0))
    vec = pl.BlockSpec((1, 1024), lambda i: (0, 0))
    return pl.pallas_call(
        functools.partial(_combine_kernel, tm=tm, alpha=alpha),
        grid=(nt,),
        in_specs=[pl.BlockSpec((tm * TOP_K,), lambda i: (i,), memory_space=pltpu.SMEM),
                  pl.BlockSpec((tm * TOP_K,), lambda i: (jnp.minimum(i + 1, nt - 1),), memory_space=pltpu.SMEM),
                  pl.BlockSpec((tm, LANES), lambda i: (i, 0)),
                  act,
                  pl.BlockSpec((tm, PLE_DIM), lambda i: (layer * nt + i, 0)),
                  pl.BlockSpec(memory_space=pl.ANY),
                  pl.BlockSpec((1024, 1024), lambda i: (0, 0)),
                  pl.BlockSpec((PLE_DIM, 1024), lambda i: (0, 0)),
                  vec, vec, vec, vec],
        out_specs=act,
        out_shape=jax.ShapeDtypeStruct((T, 1024), f32),
        scratch_shapes=[pltpu.VMEM((2, TOP_K, tm * ROW_TILE, LANES), f32), pltpu.SemaphoreType.DMA((2,))],
        compiler_params=_params(("arbitrary",)),
        name="combine_ple",
    )(dest_flat, dest_flat, gates, x2d, p2d, obuf, wg, wp, g2, b2, g3, b3)


def _t5_bucket_np(dist):
    max_exact = REL_BUCKETS // 2
    d = np.maximum(dist.astype(np.float32), np.float32(1.0))
    large = max_exact + (np.log(d / np.float32(max_exact)) / np.float32(math.log(REL_MAX_DISTANCE / max_exact))
                         * np.float32(REL_BUCKETS - max_exact)).astype(np.int32)
    large = np.minimum(large, REL_BUCKETS - 1)
    return np.where(dist < max_exact, dist, large)


def _swa_bias_table(rel_bias):
    kj = np.arange(2 * WINDOW)[None, :]
    dist = (np.arange(WINDOW)[:, None] + WINDOW) - kj
    in_window = (dist >= 0) & (dist < WINDOW)
    bucket = _t5_bucket_np(np.maximum(dist, 0))
    bias = jnp.transpose(rel_bias[bucket].astype(f32), (2, 0, 1))
    return jnp.where(jnp.asarray(in_window)[None], bias, NEG_BIG)


def _pad_row(v, width=LANES, fill=0.0):
    v = v.astype(f32).reshape(1, -1)
    return jnp.pad(v, ((0, 0), (0, width - v.shape[1])), constant_values=fill)


def _wcat(w_in):
    cols = [w_in[:, 0:4096], w_in[:, 4112:6160], w_in[:, 6160:7184], w_in[:, 7696:10768],
            w_in[:, 7184:7440], w_in[:, 7440:7696], w_in[:, 4096:4112],
            jnp.zeros((w_in.shape[0], FAT_W - 10768), w_in.dtype)]
    return jnp.concatenate(cols, axis=1).astype(bf16)


def kernel(x, p, w_in, conv_qkv_w, gdn_a_log, gdn_dt_bias, gdn_norm_w, rg_conv_w, rg_conv_b, rg_w_a, rg_b_a, rg_w_x, rg_b_x, rg_lambda, attn_sinks, rel_bias, w_o_gdn, w_o_lru, w_o_swa, w_out, ln1_g, ln1_b, router_w, router_b, w_gu, b_gu, w_down, b_down, ln2_g, ln2_b, ple_w_gate, ple_w_proj, ln3_g, ln3_b):
    B, S, D = x.shape
    depth = w_in.shape[0]
    T = B * S
    A = T * TOP_K
    alpha = (2.0 * depth) ** 0.25
    P = A + N_EXPERTS * EXPERT_BLK
    nblk = P // EXPERT_BLK
    row = lambda v: v.astype(f32).reshape(1, -1)

    bias_tab = _swa_bias_table(rel_bias)
    p2d = p.reshape(depth * T, PLE_DIM)
    xc = x.reshape(T, D)
    for i in range(depth):
        fat = inproj(xc, _wcat(w_in[i]))
        o_gdn = gdn_branch(fat, conv_qkv_w[i], _pad_row(gdn_a_log[i]), _pad_row(gdn_dt_bias[i]),
                           row(gdn_norm_w[i]), B, S)
        wax = jnp.concatenate([rg_w_a[i], rg_w_x[i]], axis=-1).astype(bf16)
        o_lru = lru_branch(fat, rg_conv_w[i], row(rg_conv_b[i]), wax, row(rg_b_a[i]), row(rg_b_x[i]),
                           row(rg_lambda[i]), B, S)
        sinks_tab = jnp.broadcast_to(attn_sinks[i].astype(f32)[:, None], (SWA_Q_HEADS, LANES))
        o_swa = swa_branch(fat, bias_tab, sinks_tab, B, S)
        x1, x1t, buf0 = merge_ln(o_gdn, o_lru, o_swa, fat, xc, w_o_gdn[i].astype(bf16), w_o_lru[i].astype(bf16),
                                 w_o_swa[i].astype(bf16), w_out[i].astype(bf16), row(ln1_g[i]), row(ln1_b[i]), alpha,
                                 P * ROW_TILE)

        rw_pad = jnp.pad(router_w[i], ((0, 0), (0, LANES - N_EXPERTS)))
        rb_pad = _pad_row(router_b[i], fill=NEG_BIG)
        gates, eidx, rank, cnt = route(x1, rw_pad, rb_pad)
        counts = cnt[0, :N_EXPERTS].astype(i32)
        padded = ((counts + EXPERT_BLK - 1) // EXPERT_BLK) * EXPERT_BLK
        pad_ends = jnp.cumsum(padded)
        pad_starts = pad_ends - padded
        hit = eidx[:, :TOP_K, None] == jnp.arange(N_EXPERTS, dtype=i32)
        dest = (jnp.sum(jnp.where(hit, pad_starts, 0), axis=-1) + rank[:, :TOP_K]).reshape(A)
        blk_start = jnp.arange(nblk, dtype=i32) * EXPERT_BLK
        blk_e = jnp.minimum(jnp.sum((pad_ends[None, :] <= blk_start[:, None]).astype(i32), axis=1),
                            N_EXPERTS - 1).astype(i32)
        nused = (pad_ends[-1:] // EXPERT_BLK).astype(i32)

        buf = dispatch(dest, x1t, buf0)
        bgu = jnp.transpose(b_gu[i].reshape(N_EXPERTS, -1, LANES, 2), (0, 1, 3, 2)).reshape(N_EXPERTS, 1, -1)
        obuf = experts(blk_e, nused, buf, gu_prep(w_gu, i), bgu, w_down[i].astype(bf16), b_down[i][:, None, :])
        xc = combine_ple(dest, gates, x1, p2d, i, obuf, ple_w_gate[i].astype(bf16),
                         ple_w_proj[i].astype(bf16), row(ln2_g[i]), row(ln2_b[i]), row(ln3_g[i]), row(ln3_b[i]),
                         alpha)
    return xc.reshape(B, S, D)
```

```python
import functools
import math

import numpy as np
import jax
import jax.numpy as jnp
from jax import lax
from jax.experimental import pallas as pl
from jax.experimental.pallas import tpu as pltpu

f32 = jnp.float32
bf16 = jnp.bfloat16
i32 = jnp.int32

D_MODEL = 1024
PLE_DIM = 256
GDN_HEADS = 8
GDN_HEAD_DIM = 128
GDN_CHUNK = 64
CONV_WIDTH = 4
LRU_BLOCKS = 8
LRU_BLOCK_DIM = 128
LRU_C = 8.0
SWA_Q_HEADS = 16
SWA_KV_HEADS = 4
SWA_HEAD_DIM = 64
SWA_GROUP = 4
WINDOW = 128
REL_BUCKETS = 32
REL_MAX_DISTANCE = 128
N_EXPERTS = 32
TOP_K = 4
SWIGLU_LIMIT = 7.0
SWIGLU_ALPHA = 1.702
LN_EPS = 1e-5
NORM_EPS = 1e-6
NEG_BIG = -1e30

LANES = 128
VMEM_LIMIT = 56 * 1024 * 1024

C_GQ, C_GK, C_GV, C_GZ = 0, 1024, 2048, 3072
C_LX, C_LG = 4096, 5120
C_SQ = 6144
C_MA, C_MB, C_MC = 7168, 8192, 9216
C_SK, C_SV = 10240, 10496
C_AB = 10752
FAT_W = 10880
FAT_TN = 2176

ROW_TILE = D_MODEL // LANES

EXPERT_BLK = 512
CHUNKS_IN_FLIGHT = 2


def _params(sem):
    return pltpu.CompilerParams(dimension_semantics=sem, vmem_limit_bytes=VMEM_LIMIT)


def _sigmoid(x):
    return 1.0 / (1.0 + jnp.exp(-x))


def _softplus(x):
    return jnp.maximum(x, 0.0) + jnp.log(1.0 + jnp.exp(-jnp.abs(x)))


def _layer_norm(z, g, b):
    mu = jnp.mean(z, axis=-1, keepdims=True)
    zc = z - mu
    var = jnp.mean(zc * zc, axis=-1, keepdims=True)
    return zc * lax.rsqrt(var + LN_EPS) * g + b


def _inproj_kernel(x_ref, w_ref, o_ref, xb_ref):
    @pl.when(pl.program_id(1) == 0)
    def _():
        xb_ref[...] = x_ref[...].astype(bf16)

    o_ref[...] = jnp.dot(xb_ref[...], w_ref[...], preferred_element_type=f32).astype(o_ref.dtype)


def inproj(x2d, wcat, tm=2048):
    T, K = x2d.shape
    N = wcat.shape[1]
    tn = FAT_TN
    return pl.pallas_call(
        _inproj_kernel,
        grid=(T // tm, N // tn),
        in_specs=[pl.BlockSpec((tm, K), lambda i, j: (i, 0)),
                  pl.BlockSpec((K, tn), lambda i, j: (0, j))],
        out_specs=pl.BlockSpec((tm, tn), lambda i, j: (i, j)),
        out_shape=jax.ShapeDtypeStruct((T, N), bf16),
        scratch_shapes=[pltpu.VMEM((tm, K), bf16)],
        compiler_params=_params(("arbitrary", "arbitrary")),
        name="inproj",
    )(x2d, wcat)


ZERO_FILL_COPIES = 4


def _zero_fill(step, zero_ref, zbuf_ref, zsem, zero_rows):
    zr = zero_ref.shape[0]
    return [pltpu.make_async_copy(zero_ref, zbuf_ref.at[pl.ds(pl.multiple_of(step * zero_rows + c * zr, zr), zr)], zsem)
            for c in range(zero_rows // zr)]


def _shift_matrices(ts):
    ri = lax.broadcasted_iota(i32, (ts, ts), 0)
    ci = lax.broadcasted_iota(i32, (ts, ts), 1)
    return [jnp.where(ri - ci == d, 1.0, 0.0).astype(bf16) for d in (3, 2, 1)]


def _causal_conv_silu(src_ref, dst_ref, cz_ref, carry_ref, w, smats, ts, head_scale=None):
    xb = src_ref[...]
    x = xb.astype(f32)
    y = w[3:4, :] * x
    for j, sm in enumerate(smats):
        y = y + w[j:j + 1, :] * jnp.dot(sm, xb, preferred_element_type=f32)
    cz_ref[0:8, :] = carry_ref[...]
    cz_ref[8:16, :] = jnp.zeros((8, x.shape[1]), f32)
    corr = w[0:1, :] * cz_ref[5:13, :] + w[1:2, :] * cz_ref[6:14, :] + w[2:3, :] * cz_ref[7:15, :]
    carry_ref[...] = x[ts - 8:ts, :]

    def post(rows, yv):
        a = yv * _sigmoid(yv)
        if head_scale is None:
            dst_ref[rows, :] = a
            return
        for h in range(a.shape[1] // LANES):
            cs = slice(h * LANES, (h + 1) * LANES)
            ah = a[:, cs]
            dst_ref[rows, cs] = ah * (lax.rsqrt(jnp.sum(ah * ah, axis=-1, keepdims=True) + NORM_EPS) * head_scale)

    post(slice(0, 8), y[0:8] + corr)
    post(slice(8, ts), y[8:ts])


def _gdn_kernel(q_ref, k_ref, v_ref, z_ref, ab_ref, cw_ref, alog_ref, dtb_ref, nw_ref, o_ref, zbuf_ref,
                xpad_ref, qs_ref, ks_ref, vs_ref, carry_ref, state_ref, g_ref, beta_ref,
                u_s, lhs_s, intra_s, kdt_s, zero_ref, zsem, *, ts, zero_rows):
    C = GDN_CHUNK
    D = GDN_HEAD_DIM
    P = 2 * C

    step = pl.program_id(0) * pl.num_programs(1) + pl.program_id(1)

    @pl.when(step == 0)
    def _():
        zero_ref[...] = jnp.zeros_like(zero_ref)

    fills = _zero_fill(step, zero_ref, zbuf_ref, zsem, zero_rows)
    for f in fills:
        f.start()

    @pl.when(pl.program_id(1) == 0)
    def _():
        carry_ref[...] = jnp.zeros_like(carry_ref)
        state_ref[...] = jnp.zeros_like(state_ref)

    smats = _shift_matrices(ts)
    for p, (src, dst, scale) in enumerate(((q_ref, qs_ref, D ** -0.5), (k_ref, ks_ref, 1.0), (v_ref, vs_ref, None))):
        _causal_conv_silu(src, dst, xpad_ref, carry_ref.at[p], cw_ref[:, p * 1024:(p + 1) * 1024], smats, ts, scale)

    ab = ab_ref[...].astype(f32)
    g = -jnp.exp(alog_ref[...]) * _softplus(ab + dtb_ref[...])
    rin = lax.broadcasted_iota(i32, (ts, LANES), 0) & (C - 1)
    gsum = g
    for d in (1, 2, 4, 8, 16, 32):
        gsum = gsum + jnp.where(rin >= d, pltpu.roll(gsum, d, 0), 0.0)
    g_ref[...] = gsum
    beta_ref[...] = _sigmoid(ab)

    ri = lax.broadcasted_iota(i32, (P, P), 0)
    ci = lax.broadcasted_iota(i32, (P, P), 1)
    same = (ri >= C) == (ci >= C)
    eye = ri == ci
    causal = same & (ri >= ci)
    strict = same & (ri > ci)
    eye_f = jnp.where(eye, 1.0, 0.0).astype(f32)
    first_cols = ci < C
    nw = nw_ref[...]

    def stack(a, b):
        return jnp.concatenate([a, b], axis=0)

    def mm(a, b):
        return jnp.dot(a, b, preferred_element_type=f32)

    npair = GDN_HEADS // 2
    nchunk = ts // C
    hcols = [slice(h * D, (h + 1) * D) for h in range(GDN_HEADS)]

    for cg in range(0, nchunk, CHUNKS_IN_FLIGHT):
        probs = [(c, hp) for c in range(cg, cg + CHUNKS_IN_FLIGHT) for hp in range(npair)]
        qn, kn, vb, gcol, eg, egl, kb = [], [], [], [], [], [], []
        for c, hp in probs:
            rows = slice(c * C, (c + 1) * C)
            c0, c1 = hcols[2 * hp], hcols[2 * hp + 1]
            gc = g_ref[rows, :]
            bc = beta_ref[rows, :]
            qn.append(stack(qs_ref[rows, c0], qs_ref[rows, c1]))
            kn.append(stack(ks_ref[rows, c0], ks_ref[rows, c1]))
            v2 = stack(vs_ref[rows, c0], vs_ref[rows, c1])
            h0, h1 = 2 * hp, 2 * hp + 1
            gcl = stack(gc[:, h0:h0 + 1], gc[:, h1:h1 + 1])
            bcl = stack(bc[:, 8 + h0:9 + h0], bc[:, 8 + h1:9 + h1])
            glast = stack(jnp.broadcast_to(gc[C - 1:C, h0:h0 + 1], (C, 1)),
                          jnp.broadcast_to(gc[C - 1:C, h1:h1 + 1], (C, 1)))
            gcol.append(gcl)
            eg.append(jnp.exp(gcl))
            egl.append(jnp.exp(glast - gcl))
            kb.append(kn[-1] * bcl)
            vb.append(v2 * bcl)
        n = len(probs)
        a2 = [lax.dot_general(stack(kb[i], qn[i]).astype(bf16), kn[i].astype(bf16), (((1,), (1,)), ((), ())),
                              preferred_element_type=f32) for i in range(n)]
        lmat, intra = [], []
        for i in range(n):
            gm = jnp.broadcast_to(gcol[i], (P, P))
            grow = jnp.sum(jnp.where(eye, gm, 0.0), axis=0, keepdims=True)
            decay = jnp.where(causal, jnp.exp(jnp.minimum(gm - grow, 0.0)), 0.0)
            lmat.append(jnp.where(strict, a2[i][:P] * decay, 0.0))
            intra.append(a2[i][P:] * decay)
        lb = [l.astype(bf16) for l in lmat]
        xm = [eye_f - l for l in lmat]
        pm = [mm(b, b) for b in lb]
        for it in range(5):
            pb = [p.astype(bf16) for p in pm]
            xm = [x + mm(x.astype(bf16), b) for x, b in zip(xm, pb)]
            if it < 4:
                pm = [mm(b, b) for b in pb]
        uw = [mm(xm[i].astype(bf16), jnp.concatenate([vb[i], kb[i] * eg[i]], axis=1).astype(bf16))
              for i in range(n)]
        for i, (c, hp) in enumerate(probs):
            j = c * npair + hp
            w2 = uw[i][:, D:]
            qd = qn[i] * eg[i]
            kdt = (kn[i] * egl[i]).T
            u_s[j] = uw[i][:, :D]
            lhs_s[j, 0] = stack(w2[:C], qd[:C]).astype(bf16)
            lhs_s[j, 1] = stack(w2[C:], qd[C:]).astype(bf16)
            intra_s[j] = intra[i].astype(bf16)
            kdt_s[j, 0] = jnp.where(first_cols, kdt, 0.0).astype(bf16)
            kdt_s[j, 1] = jnp.where(first_cols, 0.0, kdt).astype(bf16)

    for c in range(nchunk):
        rows = slice(c * C, (c + 1) * C)
        gl = g_ref[(c + 1) * C - 1:(c + 1) * C, :]
        st = [state_ref[h] for h in range(GDN_HEADS)]
        wq = [mm(lhs_s[c * npair + h // 2, h % 2], st[h].astype(bf16)) for h in range(GDN_HEADS)]
        vnb = [(u_s[c * npair + hp] - stack(wq[2 * hp][:C], wq[2 * hp + 1][:C])).astype(bf16)
               for hp in range(npair)]
        o2 = [stack(wq[2 * hp][C:], wq[2 * hp + 1][C:]) + mm(intra_s[c * npair + hp], vnb[hp])
              for hp in range(npair)]
        for h in range(GDN_HEADS):
            state_ref[h] = st[h] * jnp.exp(gl[:, h:h + 1]) + mm(kdt_s[c * npair + h // 2, h % 2], vnb[h // 2])
        for hp in range(npair):
            c0, c1 = hcols[2 * hp], hcols[2 * hp + 1]
            z2 = stack(z_ref[rows, c0], z_ref[rows, c1]).astype(f32)
            on = (o2[hp] * lax.rsqrt(jnp.mean(o2[hp] * o2[hp], axis=-1, keepdims=True) + NORM_EPS) * nw
                  * (z2 * _sigmoid(z2))).astype(o_ref.dtype)
            o_ref[rows, c0] = on[:C]
            o_ref[rows, c1] = on[C:]
    for f in fills:
        f.wait()


def gdn_branch(fat, conv_w, a_log_row, dt_bias_row, norm_w_row, B, S, buf_rows, ts=256):
    T = B * S
    ns = S // ts
    nprob = (ts // GDN_CHUNK) * (GDN_HEADS // 2)
    zero_rows = buf_rows // (B * ns)
    assert zero_rows * B * ns == buf_rows and zero_rows % (ZERO_FILL_COPIES * ROW_TILE) == 0
    row = lambda b, s: b * ns + s
    blk = lambda cb: pl.BlockSpec((ts, 1024), lambda b, s: (row(b, s), cb))
    full = lambda shp: pl.BlockSpec(shp, lambda b, s: (0,) * len(shp))
    return pl.pallas_call(
        functools.partial(_gdn_kernel, ts=ts, zero_rows=zero_rows),
        grid=(B, ns),
        in_specs=[blk(C_GQ // 1024), blk(C_GK // 1024), blk(C_GV // 1024), blk(C_GZ // 1024),
                  pl.BlockSpec((ts, LANES), lambda b, s: (row(b, s), C_AB // LANES)),
                  full((CONV_WIDTH, 3072)), full((1, LANES)), full((1, LANES)), full((1, LANES))],
        out_specs=[pl.BlockSpec((ts, 1024), lambda b, s: (row(b, s), 0)), pl.BlockSpec(memory_space=pl.ANY)],
        out_shape=[jax.ShapeDtypeStruct((T, 1024), bf16), jax.ShapeDtypeStruct((buf_rows, LANES), f32)],
        scratch_shapes=[pltpu.VMEM((16, 1024), f32),
                        pltpu.VMEM((ts, 1024), f32), pltpu.VMEM((ts, 1024), f32), pltpu.VMEM((ts, 1024), f32),
                        pltpu.VMEM((3, 8, 1024), f32),
                        pltpu.VMEM((GDN_HEADS, GDN_HEAD_DIM, GDN_HEAD_DIM), f32),
                        pltpu.VMEM((ts, LANES), f32), pltpu.VMEM((ts, LANES), f32),
                        pltpu.VMEM((nprob, 128, GDN_HEAD_DIM), f32),
                        pltpu.VMEM((nprob, 2, 128, GDN_HEAD_DIM), bf16),
                        pltpu.VMEM((nprob, 128, 128), bf16),
                        pltpu.VMEM((nprob, 2, GDN_HEAD_DIM, 128), bf16),
                        pltpu.VMEM((zero_rows // ZERO_FILL_COPIES, LANES), f32), pltpu.SemaphoreType.DMA(())],
        compiler_params=_params(("arbitrary", "arbitrary")),
        name="gdn",
    )(fat, fat, fat, fat, fat, conv_w, a_log_row, dt_bias_row, norm_w_row)


def _lru_kernel(x_ref, gate_ref, cw_ref, cb_ref, wax_ref, ba_ref, bx_ref, lam_ref, o_ref,
                cz_ref, xc_ref, a_ref, u_ref, h_ref, carry_ref, hc_ref, *, ts):
    half = ts // 2

    @pl.when(pl.program_id(1) == 0)
    def _():
        carry_ref[...] = jnp.zeros_like(carry_ref)
        hc_ref[...] = jnp.zeros_like(hc_ref)

    w = cw_ref[...]
    smats = _shift_matrices(half)
    cz_ref[8:16, :] = jnp.zeros((8, D_MODEL), f32)
    for hf in range(2):
        rows = slice(hf * half, (hf + 1) * half)
        xb = x_ref[rows, :]
        x = xb.astype(f32)
        y = w[3:4, :] * x + cb_ref[...]
        for j, sm in enumerate(smats):
            y = y + w[j:j + 1, :] * jnp.dot(sm, xb, preferred_element_type=f32)
        cz_ref[0:8, :] = carry_ref[...]
        corr = w[0:1, :] * cz_ref[5:13, :] + w[1:2, :] * cz_ref[6:14, :] + w[2:3, :] * cz_ref[7:15, :]
        carry_ref[...] = x[half - 8:half, :]
        xc_ref[hf * half:hf * half + 8, :] = y[0:8] + corr
        xc_ref[hf * half + 8:(hf + 1) * half, :] = y[8:half]

    nsp = _softplus(-lam_ref[...])
    for blk in range(LRU_BLOCKS):
        cs = slice(blk * LRU_BLOCK_DIM, (blk + 1) * LRU_BLOCK_DIM)
        xc = xc_ref[:, cs]
        ri = jnp.dot(xc.astype(bf16), wax_ref[blk], preferred_element_type=f32)
        r = _sigmoid(ri[:, :LRU_BLOCK_DIM] + ba_ref[:, cs])
        i = _sigmoid(ri[:, LRU_BLOCK_DIM:] + bx_ref[:, cs])
        a = jnp.exp(-LRU_C * r * nsp[:, cs])
        a_ref[:, cs] = a
        u_ref[:, cs] = jnp.sqrt(1.0 - a * a) * (i * xc)

    rowi = lax.broadcasted_iota(i32, (8, D_MODEL), 0)

    def group(gi, h):
        r0 = pl.multiple_of(gi * 8, 8)
        a = a_ref[pl.ds(r0, 8), :]
        b = u_ref[pl.ds(r0, 8), :]
        for d in (1, 2, 4):
            m = rowi >= d
            a_s = pltpu.roll(a, d, 0)
            b_s = pltpu.roll(b, d, 0)
            b = jnp.where(m, a * b_s + b, b)
            a = jnp.where(m, a * a_s, a)
        hh = a * h + b
        h_ref[pl.ds(r0, 8), :] = hh
        return hh[7:8, :]

    hc_ref[...] = lax.fori_loop(0, ts // 8, group, hc_ref[...])
    gt = gate_ref[...].astype(f32)
    o_ref[...] = (h_ref[...] * jax.nn.gelu(gt)).astype(o_ref.dtype)


def lru_branch(fat, conv_w, conv_b, wax, b_a, b_x, lam, B, S, ts=512):
    T = B * S
    ns = S // ts
    row = lambda b, s: b * ns + s
    full = lambda shp: pl.BlockSpec(shp, lambda b, s: (0,) * len(shp))
    return pl.pallas_call(
        functools.partial(_lru_kernel, ts=ts),
        grid=(B, ns),
        in_specs=[pl.BlockSpec((ts, 1024), lambda b, s: (row(b, s), C_LX // 1024)),
                  pl.BlockSpec((ts, 1024), lambda b, s: (row(b, s), C_LG // 1024)),
                  full((CONV_WIDTH, 1024)), full((1, 1024)), full((LRU_BLOCKS, LRU_BLOCK_DIM, 2 * LRU_BLOCK_DIM)),
                  full((1, 1024)), full((1, 1024)), full((1, 1024))],
        out_specs=pl.BlockSpec((ts, 1024), lambda b, s: (row(b, s), 0)),
        out_shape=jax.ShapeDtypeStruct((T, 1024), bf16),
        scratch_shapes=[pltpu.VMEM((16, 1024), f32), pltpu.VMEM((ts, 1024), f32),
                        pltpu.VMEM((ts, 1024), f32), pltpu.VMEM((ts, 1024), f32), pltpu.VMEM((ts, 1024), f32),
                        pltpu.VMEM((8, 1024), f32), pltpu.VMEM((1, 1024), f32)],
        compiler_params=_params(("arbitrary", "arbitrary")),
        name="lru",
    )(fat, fat, conv_w, conv_b, wax, b_a, b_x, lam)


def _swa_kernel(q_ref, kc_ref, kp_ref, vc_ref, vp_ref, bias_ref, sink_ref, o_ref, kb_ref, vb_ref, *, tq):
    W = WINDOW
    hd = SWA_HEAD_DIM
    kb_ref[0:W, :] = kp_ref[...]
    kb_ref[W:W + tq, :] = kc_ref[...]
    vb_ref[0:W, :] = vp_ref[...]
    vb_ref[W:W + tq, :] = vc_ref[...]
    first_tile = pl.program_id(1) == 0
    col = lax.broadcasted_iota(i32, (W, 2 * W), 1)
    scale = hd ** -0.5

    def qblock(n, carry):
        r0 = pl.multiple_of(n * W, W)
        pen = jnp.where(jnp.logical_and(first_tile, n == 0), NEG_BIG, 0.0).astype(f32)
        penm = jnp.where(col < W, pen, 0.0)
        heads = range(SWA_Q_HEADS)
        kk = [kb_ref[pl.ds(r0, 2 * W), hk * hd:(hk + 1) * hd] for hk in range(SWA_KV_HEADS)]
        vv = [vb_ref[pl.ds(r0, 2 * W), hk * hd:(hk + 1) * hd] for hk in range(SWA_KV_HEADS)]
        qs = q_ref[pl.ds(r0, W), :] * scale
        s = [lax.dot_general(qs[:, h * hd:(h + 1) * hd], kk[h // SWA_GROUP], (((1,), (1,)), ((), ())),
                             preferred_element_type=f32) + (bias_ref[h] + penm) for h in heads]
        sink = [sink_ref[h:h + 1, 0:1] for h in heads]
        m = [jnp.maximum(jnp.max(s[h], axis=-1, keepdims=True), sink[h]) for h in heads]
        pe = [jnp.exp(s[h] - m[h]) for h in heads]
        o = [jnp.dot(pe[h].astype(bf16), vv[h // SWA_GROUP], preferred_element_type=f32) for h in heads]
        den = [jnp.sum(pe[h], axis=-1, keepdims=True) + jnp.exp(sink[h] - m[h]) for h in heads]
        o_ref[pl.ds(r0, W), :] = jnp.concatenate([o[h] / den[h] for h in heads], axis=1).astype(o_ref.dtype)
        return carry

    lax.fori_loop(0, tq // W, qblock, 0)


def swa_branch(fat, bias_tab, sinks_tab, B, S, tq=512):
    T = B * S
    ns = S // tq
    nb = S // WINDOW
    per = tq // WINDOW
    row = lambda b, s: b * ns + s
    prev = lambda b, s: b * nb + jnp.maximum(s * per - 1, 0)
    kvw = SWA_KV_HEADS * SWA_HEAD_DIM
    full = lambda shp: pl.BlockSpec(shp, lambda b, s: (0,) * len(shp))
    return pl.pallas_call(
        functools.partial(_swa_kernel, tq=tq),
        grid=(B, ns),
        in_specs=[pl.BlockSpec((tq, 1024), lambda b, s: (row(b, s), C_SQ // 1024)),
                  pl.BlockSpec((tq, kvw), lambda b, s: (row(b, s), C_SK // kvw)),
                  pl.BlockSpec((WINDOW, kvw), lambda b, s: (prev(b, s), C_SK // kvw)),
                  pl.BlockSpec((tq, kvw), lambda b, s: (row(b, s), C_SV // kvw)),
                  pl.BlockSpec((WINDOW, kvw), lambda b, s: (prev(b, s), C_SV // kvw)),
                  full((SWA_Q_HEADS, WINDOW, 2 * WINDOW)), full((SWA_Q_HEADS, LANES))],
        out_specs=pl.BlockSpec((tq, 1024), lambda b, s: (row(b, s), 0)),
        out_shape=jax.ShapeDtypeStruct((T, 1024), bf16),
        scratch_shapes=[pltpu.VMEM((tq + WINDOW, kvw), bf16), pltpu.VMEM((tq + WINDOW, kvw), bf16)],
        compiler_params=_params(("arbitrary", "arbitrary")),
        name="swa",
    )(fat, fat, fat, fat, fat, bias_tab, sinks_tab)


def _load_row_tiles(ref, n, lead=()):
    return jnp.concatenate([ref[lead + (pl.ds(s, n, stride=ROW_TILE), slice(None))] for s in range(ROW_TILE)], axis=1)


def _store_row_tiles(ref, val):
    n = val.shape[0]
    for s in range(ROW_TILE):
        ref[pl.ds(s, n, stride=ROW_TILE), :] = val[:, s * LANES:(s + 1) * LANES]


def _merge_kernel(oa_ref, ob_ref, oc_ref, ga_ref, gb_ref, gc_ref, x_ref, wa_ref, wb_ref, wc_ref, wo_ref,
                  g_ref, b_ref, o_ref, ot_ref, *, alpha):
    ya = jnp.dot(oa_ref[...], wa_ref[...], preferred_element_type=f32)
    yb = jnp.dot(ob_ref[...], wb_ref[...], preferred_element_type=f32)
    yc = jnp.dot(oc_ref[...], wc_ref[...], preferred_element_type=f32)
    mix = (_sigmoid(ga_ref[...].astype(f32)) * ya + _sigmoid(gb_ref[...].astype(f32)) * yb
           + _sigmoid(gc_ref[...].astype(f32)) * yc)
    y = jnp.dot(mix.astype(bf16), wo_ref[...], preferred_element_type=f32)
    x1 = _layer_norm(alpha * x_ref[...] + y, g_ref[...], b_ref[...])
    o_ref[...] = x1
    _store_row_tiles(ot_ref, x1)


def merge_ln(oa, ob, oc, fat, x2d, wa, wb, wc, wo, g, b, alpha, tm=512):
    T = x2d.shape[0]
    act = pl.BlockSpec((tm, 1024), lambda i: (i, 0))
    fatb = lambda cb: pl.BlockSpec((tm, 1024), lambda i: (i, cb))
    wsp = pl.BlockSpec((1024, 1024), lambda i: (0, 0))
    vec = pl.BlockSpec((1, 1024), lambda i: (0, 0))
    return pl.pallas_call(
        functools.partial(_merge_kernel, alpha=alpha),
        grid=(T // tm,),
        in_specs=[act, act, act, fatb(C_MA // 1024), fatb(C_MB // 1024), fatb(C_MC // 1024), act,
                  wsp, wsp, wsp, wsp, vec, vec],
        out_specs=[act, pl.BlockSpec((tm * ROW_TILE, LANES), lambda i: (i, 0))],
        out_shape=[jax.ShapeDtypeStruct((T, 1024), f32), jax.ShapeDtypeStruct((T * ROW_TILE, LANES), f32)],
        compiler_params=_params(("arbitrary",)),
        name="merge_ln",
    )(oa, ob, oc, fat, fat, fat, x2d, wa, wb, wc, wo, g, b)


def _route_kernel(x_ref, rw_ref, rb_ref, gates_ref, eidx_ref, rank_ref, cnt_ref, run_ref, *, tm):
    @pl.when(pl.program_id(0) == 0)
    def _():
        run_ref[...] = jnp.zeros_like(run_ref)

    x = x_ref[...]
    xh = x.astype(bf16)
    xl = (x - xh.astype(f32)).astype(bf16)
    hw = jnp.dot(xh, rw_ref[...], preferred_element_type=f32)
    logits = (hw[:, :LANES] + (hw[:, LANES:] + jnp.dot(xl, rw_ref[:, :LANES], preferred_element_type=f32))
              + rb_ref[...])
    lane = lax.broadcasted_iota(i32, (tm, LANES), 1)
    lane_f = lane.astype(f32)
    work = logits
    vals, idxs, hots = [], [], []
    for _ in range(TOP_K):
        m = jnp.max(work, axis=-1, keepdims=True)
        idx = jnp.min(jnp.where(work == m, lane_f, float(LANES)), axis=-1, keepdims=True)
        hot = lane_f == idx
        vals.append(m)
        idxs.append(idx)
        hots.append(hot)
        work = jnp.where(hot, -jnp.inf, work)
    es = [jnp.exp(v - vals[0]) for v in vals]
    den = es[0] + es[1] + es[2] + es[3]
    sel = jnp.zeros((tm, LANES), f32)
    for hot in hots:
        sel = sel + jnp.where(hot, 1.0, 0.0)
    ri = lax.broadcasted_iota(i32, (tm, tm), 0)
    ci = lax.broadcasted_iota(i32, (tm, tm), 1)
    tril = jnp.where(ri > ci, 1.0, 0.0).astype(bf16)
    before = jnp.dot(tril, sel.astype(bf16), preferred_element_type=f32) + run_ref[...]
    run_ref[...] = run_ref[...] + jnp.sum(sel, axis=0, keepdims=True)
    cnt_ref[...] = run_ref[...]
    gates = jnp.zeros((tm, LANES), f32)
    eidx = jnp.zeros((tm, LANES), f32)
    rank = jnp.zeros((tm, LANES), f32)
    for k in range(TOP_K):
        rk = jnp.sum(jnp.where(hots[k], before, 0.0), axis=-1, keepdims=True)
        gates = jnp.where(lane == k, es[k] / den, gates)
        eidx = jnp.where(lane == k, idxs[k], eidx)
        rank = jnp.where(lane == k, rk, rank)
    gates_ref[...] = gates
    eidx_ref[...] = eidx.astype(i32)
    rank_ref[...] = rank.astype(i32)


def route(x2d, rw_pad, rb_pad, tm=512):
    T = x2d.shape[0]
    outb = pl.BlockSpec((tm, LANES), lambda i: (i, 0))
    return pl.pallas_call(
        functools.partial(_route_kernel, tm=tm),
        grid=(T // tm,),
        in_specs=[pl.BlockSpec((tm, 1024), lambda i: (i, 0)),
                  pl.BlockSpec((1024, 2 * LANES), lambda i: (0, 0)),
                  pl.BlockSpec((1, LANES), lambda i: (0, 0))],
        out_specs=[outb, outb, outb, pl.BlockSpec((1, LANES), lambda i: (0, 0))],
        out_shape=[jax.ShapeDtypeStruct((T, LANES), f32), jax.ShapeDtypeStruct((T, LANES), i32),
                   jax.ShapeDtypeStruct((T, LANES), i32), jax.ShapeDtypeStruct((1, LANES), f32)],
        scratch_shapes=[pltpu.VMEM((1, LANES), f32)],
        compiler_params=_params(("arbitrary",)),
        name="route",
    )(x2d, rw_pad, rb_pad)


ROW_UNROLL = 8


def _row_copy(src_ref, src_row, dst_ref, dst_row, sem):
    tile = lambda r: pl.ds(pl.multiple_of(r * ROW_TILE, ROW_TILE), ROW_TILE)
    return pltpu.make_async_copy(src_ref.at[tile(src_row)], dst_ref.at[tile(dst_row)], sem)


def _dispatch_kernel(dest_ref, x_ref, buf_in_ref, buf_ref, sem, *, tm):
    del buf_in_ref

    def issue(g, carry):
        for j in range(ROW_UNROLL):
            r = g * ROW_UNROLL + j
            for k in range(TOP_K):
                _row_copy(x_ref, r, buf_ref, dest_ref[r * TOP_K + k], sem).start(priority=k % 2)
        return carry

    lax.fori_loop(0, tm // ROW_UNROLL, issue, 0)

    def drain(g, carry):
        for j in range(ROW_UNROLL * TOP_K):
            _row_copy(x_ref, 0, buf_ref, 0, sem).wait()
        return carry

    lax.fori_loop(0, tm // ROW_UNROLL, drain, 0)


def dispatch(dest_flat, xt, buf0, tm=512):
    T = xt.shape[0] // ROW_TILE
    return pl.pallas_call(
        functools.partial(_dispatch_kernel, tm=tm),
        grid=(T // tm,),
        in_specs=[pl.BlockSpec((tm * TOP_K,), lambda i: (i,), memory_space=pltpu.SMEM),
                  pl.BlockSpec((tm * ROW_TILE, LANES), lambda i: (i, 0)),
                  pl.BlockSpec(memory_space=pl.ANY)],
        out_specs=pl.BlockSpec(memory_space=pl.ANY),
        out_shape=jax.ShapeDtypeStruct(buf0.shape, buf0.dtype),
        scratch_shapes=[pltpu.SemaphoreType.DMA(())],
        input_output_aliases={2: 0},
        compiler_params=_params(("arbitrary",)),
        name="dispatch",
    )(dest_flat, xt, buf0)


GU_GROUP = 2 * LANES


def _gu_prep_kernel(w_ref, o_ref):
    ri = lax.broadcasted_iota(i32, (GU_GROUP, GU_GROUP), 0)
    ci = lax.broadcasted_iota(i32, (GU_GROUP, GU_GROUP), 1)
    src = jnp.where(ci < LANES, 2 * ci, 2 * (ci - LANES) + 1)
    perm = jnp.where(ri == src, 1.0, 0.0).astype(bf16)
    for g in range(w_ref.shape[3] // GU_GROUP):
        cs = slice(g * GU_GROUP, (g + 1) * GU_GROUP)
        o_ref[0, :, cs] = jnp.dot(w_ref[0, 0, :, cs].astype(bf16), perm, preferred_element_type=f32).astype(bf16)


def gu_prep(w_gu_all, layer, tk=512):
    _, E, D, N = w_gu_all.shape
    return pl.pallas_call(
        _gu_prep_kernel,
        grid=(E, D // tk),
        in_specs=[pl.BlockSpec((1, 1, tk, N), lambda e, k: (layer, e, k, 0))],
        out_specs=pl.BlockSpec((1, tk, N), lambda e, k: (e, k, 0)),
        out_shape=jax.ShapeDtypeStruct((E, D, N), bf16),
        compiler_params=_params(("arbitrary", "arbitrary")),
        name="gu_prep",
    )(w_gu_all)


def _expert_kernel(blk_e_ref, nused_ref, x_ref, wgu_ref, bgu_ref, wd_ref, bd_ref, o_ref):
    i = pl.program_id(0)

    @pl.when(i < nused_ref[0])
    def _():
        de = wd_ref.shape[1]
        xb = _load_row_tiles(x_ref, EXPERT_BLK).astype(bf16)
        hgu = jnp.dot(xb, wgu_ref[0], preferred_element_type=f32) + bgu_ref[0]
        acts = []
        for g in range(2 * de // GU_GROUP):
            gate = jnp.minimum(hgu[:, g * GU_GROUP:g * GU_GROUP + LANES], SWIGLU_LIMIT)
            lin = jnp.clip(hgu[:, g * GU_GROUP + LANES:(g + 1) * GU_GROUP], -SWIGLU_LIMIT, SWIGLU_LIMIT)
            acts.append((gate * _sigmoid(SWIGLU_ALPHA * gate) * (lin + 1.0)).astype(bf16))
        act = jnp.concatenate(acts, axis=1)
        _store_row_tiles(o_ref, jnp.dot(act, wd_ref[0], preferred_element_type=f32) + bd_ref[0])

    @pl.when(i >= nused_ref[0])
    def _():
        o_ref[...] = jnp.zeros_like(o_ref)


def experts(blk_e, nused, buf, wgu, bgu, wd, bd):
    D = wd.shape[2]
    nblk = buf.shape[0] // (EXPERT_BLK * ROW_TILE)
    de = wd.shape[1]
    tile_blk = (EXPERT_BLK * ROW_TILE, LANES)
    live = lambda i, be, nu: jnp.minimum(i, nu[0] - 1)
    grid_spec = pltpu.PrefetchScalarGridSpec(
        num_scalar_prefetch=2,
        grid=(nblk,),
        in_specs=[pl.BlockSpec(tile_blk, lambda i, be, nu: (live(i, be, nu), 0)),
                  pl.BlockSpec((1, D, 2 * de), lambda i, be, nu: (be[i], 0, 0)),
                  pl.BlockSpec((1, 1, 2 * de), lambda i, be, nu: (be[i], 0, 0)),
                  pl.BlockSpec((1, de, D), lambda i, be, nu: (be[i], 0, 0)),
                  pl.BlockSpec((1, 1, D), lambda i, be, nu: (be[i], 0, 0))],
        out_specs=pl.BlockSpec(tile_blk, lambda i, be, nu: (i, 0)),
    )
    return pl.pallas_call(
        _expert_kernel,
        grid_spec=grid_spec,
        out_shape=jax.ShapeDtypeStruct(buf.shape, f32),
        compiler_params=_params(("arbitrary",)),
        name="experts",
    )(blk_e, nused, buf, wgu, bgu, wd, bd)


def _combine_kernel(dest_ref, dest_next_ref, gates_ref, x_ref, p_ref, obuf_ref, wg_ref, wp_ref,
                    g2_ref, b2_ref, g3_ref, b3_ref, o_ref, rows_ref, sems, *, tm, alpha):
    i = pl.program_id(0)
    n = pl.num_programs(0)
    slot = i % 2

    def gather(idx_ref, s):
        def issue(g, carry):
            for j in range(ROW_UNROLL):
                r = g * ROW_UNROLL + j
                for k in range(TOP_K):
                    _row_copy(obuf_ref, idx_ref[r * TOP_K + k], rows_ref.at[s, k], r, sems.at[s]).start(priority=k % 2)
            return carry

        lax.fori_loop(0, tm // ROW_UNROLL, issue, 0)

    @pl.when(i == 0)
    def _():
        gather(dest_ref, 0)

    @pl.when(i + 1 < n)
    def _():
        gather(dest_next_ref, 1 - slot)

    def drain(g, carry):
        for j in range(ROW_UNROLL * TOP_K):
            _row_copy(obuf_ref, 0, rows_ref.at[slot, 0], 0, sems.at[slot]).wait()
        return carry

    lax.fori_loop(0, tm // ROW_UNROLL, drain, 0)

    gates = gates_ref[...]
    y = gates[:, 0:1] * _load_row_tiles(rows_ref, tm, (slot, 0))
    for k in range(1, TOP_K):
        y = y + gates[:, k:k + 1] * _load_row_tiles(rows_ref, tm, (slot, k))
    x2 = _layer_norm(alpha * x_ref[...] + y, g2_ref[...], b2_ref[...])
    gate = _sigmoid(jnp.dot(x2.astype(bf16), wg_ref[...], preferred_element_type=f32))
    proj = jnp.dot(p_ref[...].astype(bf16), wp_ref[...], preferred_element_type=f32)
    o_ref[...] = _layer_norm(alpha * x2 + gate * proj, g3_ref[...], b3_ref[...])


def combine_ple(dest_flat, gates, x2d, p2d, layer, obuf, wg, wp, g2, b2, g3, b3, alpha, tm=256):
    T = x2d.shape[0]
    nt = T // tm
    act = pl.BlockSpec((tm, 1024), lambda i: (i, 0))
    vec = pl.BlockSpec((1, 1024), lambda i: (0, 0))
    return pl.pallas_call(
        functools.partial(_combine_kernel, tm=tm, alpha=alpha),
        grid=(nt,),
        in_specs=[pl.BlockSpec((tm * TOP_K,), lambda i: (i,), memory_space=pltpu.SMEM),
                  pl.BlockSpec((tm * TOP_K,), lambda i: (jnp.minimum(i + 1, nt - 1),), memory_space=pltpu.SMEM),
                  pl.BlockSpec((tm, LANES), lambda i: (i, 0)),
                  act,
                  pl.BlockSpec((tm, PLE_DIM), lambda i: (layer * nt + i, 0)),
                  pl.BlockSpec(memory_space=pl.ANY),
                  pl.BlockSpec((1024, 1024), lambda i: (0, 0)),
                  pl.BlockSpec((PLE_DIM, 1024), lambda i: (0, 0)),
                  vec, vec, vec, vec],
        out_specs=act,
        out_shape=jax.ShapeDtypeStruct((T, 1024), f32),
        scratch_shapes=[pltpu.VMEM((2, TOP_K, tm * ROW_TILE, LANES), f32), pltpu.SemaphoreType.DMA((2,))],
        compiler_params=_params(("arbitrary",)),
        name="combine_ple",
    )(dest_flat, dest_flat, gates, x2d, p2d, obuf, wg, wp, g2, b2, g3, b3)


def _t5_bucket_np(dist):
    max_exact = REL_BUCKETS // 2
    d = np.maximum(dist.astype(np.float32), np.float32(1.0))
    large = max_exact + (np.log(d / np.float32(max_exact)) / np.float32(math.log(REL_MAX_DISTANCE / max_exact))
                         * np.float32(REL_BUCKETS - max_exact)).astype(np.int32)
    large = np.minimum(large, REL_BUCKETS - 1)
    return np.where(dist < max_exact, dist, large)


def _swa_bias_table(rel_bias):
    dist = np.arange(2 * WINDOW - 1, -WINDOW, -1)
    in_window = (dist >= 0) & (dist < WINDOW)
    per_dist = rel_bias[_t5_bucket_np(np.maximum(dist, 0))].astype(f32).T
    per_dist = jnp.where(jnp.asarray(in_window)[None], per_dist, NEG_BIG)
    return jnp.stack([per_dist[:, WINDOW - 1 - q:3 * WINDOW - 1 - q] for q in range(WINDOW)], axis=1)


def _pad_row(v, width=LANES, fill=0.0):
    v = v.astype(f32).reshape(1, -1)
    return jnp.pad(v, ((0, 0), (0, width - v.shape[1])), constant_values=fill)


def _wcat(w_in):
    cols = [w_in[:, 0:4096], w_in[:, 4112:6160], w_in[:, 6160:7184], w_in[:, 7696:10768],
            w_in[:, 7184:7440], w_in[:, 7440:7696], w_in[:, 4096:4112],
            jnp.zeros((w_in.shape[0], FAT_W - 10768), w_in.dtype)]
    return jnp.concatenate(cols, axis=1).astype(bf16)


def kernel(x, p, w_in, conv_qkv_w, gdn_a_log, gdn_dt_bias, gdn_norm_w, rg_conv_w, rg_conv_b, rg_w_a, rg_b_a, rg_w_x, rg_b_x, rg_lambda, attn_sinks, rel_bias, w_o_gdn, w_o_lru, w_o_swa, w_out, ln1_g, ln1_b, router_w, router_b, w_gu, b_gu, w_down, b_down, ln2_g, ln2_b, ple_w_gate, ple_w_proj, ln3_g, ln3_b):
    B, S, D = x.shape
    depth = w_in.shape[0]
    T = B * S
    A = T * TOP_K
    alpha = (2.0 * depth) ** 0.25
    P = A + N_EXPERTS * EXPERT_BLK
    nblk = P // EXPERT_BLK
    row = lambda v: v.astype(f32).reshape(1, -1)

    bias_tab = _swa_bias_table(rel_bias)
    p2d = p.reshape(depth * T, PLE_DIM)
    xc = x.reshape(T, D)
    for i in range(depth):
        fat = inproj(xc, _wcat(w_in[i]))
        o_gdn, buf0 = gdn_branch(fat, conv_qkv_w[i], _pad_row(gdn_a_log[i]), _pad_row(gdn_dt_bias[i]),
                                 row(gdn_norm_w[i]), B, S, P * ROW_TILE)
        wax = jnp.concatenate([rg_w_a[i], rg_w_x[i]], axis=-1).astype(bf16)
        o_lru = lru_branch(fat, rg_conv_w[i], row(rg_conv_b[i]), wax, row(rg_b_a[i]), row(rg_b_x[i]),
                           row(rg_lambda[i]), B, S)
        sinks_tab = jnp.broadcast_to(attn_sinks[i].astype(f32)[:, None], (SWA_Q_HEADS, LANES))
        o_swa = swa_branch(fat, bias_tab, sinks_tab, B, S)
        x1, x1t = merge_ln(o_gdn, o_lru, o_swa, fat, xc, w_o_gdn[i].astype(bf16), w_o_lru[i].astype(bf16),
                           w_o_swa[i].astype(bf16), w_out[i].astype(bf16), row(ln1_g[i]), row(ln1_b[i]), alpha)

        rw_pad = jnp.pad(router_w[i].astype(f32), ((0, 0), (0, LANES - N_EXPERTS)))
        rw_hi = rw_pad.astype(bf16)
        rw_pad = jnp.concatenate([rw_hi, (rw_pad - rw_hi.astype(f32)).astype(bf16)], axis=1)
        rb_pad = _pad_row(router_b[i], fill=NEG_BIG)
        gates, eidx, rank, cnt = route(x1, rw_pad, rb_pad)
        counts = cnt[0, :N_EXPERTS].astype(i32)
        padded = ((counts + EXPERT_BLK - 1) // EXPERT_BLK) * EXPERT_BLK
        pad_ends = jnp.cumsum(padded)
        pad_starts = pad_ends - padded
        hit = eidx[:, :TOP_K, None] == jnp.arange(N_EXPERTS, dtype=i32)
        dest = (jnp.sum(jnp.where(hit, pad_starts, 0), axis=-1) + rank[:, :TOP_K]).reshape(A)
        blk_start = jnp.arange(nblk, dtype=i32) * EXPERT_BLK
        blk_e = jnp.minimum(jnp.sum((pad_ends[None, :] <= blk_start[:, None]).astype(i32), axis=1),
                            N_EXPERTS - 1).astype(i32)
        nused = (pad_ends[-1:] // EXPERT_BLK).astype(i32)

        buf = dispatch(dest, x1t, buf0)
        bgu = jnp.transpose(b_gu[i].reshape(N_EXPERTS, -1, LANES, 2), (0, 1, 3, 2)).reshape(N_EXPERTS, 1, -1)
        obuf = experts(blk_e, nused, buf, gu_prep(w_gu, i), bgu, w_down[i].astype(bf16), b_down[i][:, None, :])
        xc = combine_ple(dest, gates, x1, p2d, i, obuf, ple_w_gate[i].astype(bf16),
                         ple_w_proj[i].astype(bf16), row(ln2_g[i]), row(ln2_b[i]), row(ln3_g[i]), row(ln3_b[i]),
                         alpha)
    return xc.reshape(B, S, D)
```

```python
import functools
import math

import numpy as np
import jax
import jax.numpy as jnp
from jax import lax
from jax.experimental import pallas as pl
from jax.experimental.pallas import tpu as pltpu

f32 = jnp.float32
bf16 = jnp.bfloat16
i32 = jnp.int32

D_MODEL = 1024
PLE_DIM = 256
GDN_HEADS = 8
GDN_HEAD_DIM = 128
GDN_CHUNK = 64
CONV_WIDTH = 4
LRU_BLOCKS = 8
LRU_BLOCK_DIM = 128
LRU_C = 8.0
SWA_Q_HEADS = 16
SWA_KV_HEADS = 4
SWA_HEAD_DIM = 64
SWA_GROUP = 4
WINDOW = 128
REL_BUCKETS = 32
REL_MAX_DISTANCE = 128
N_EXPERTS = 32
TOP_K = 4
SWIGLU_LIMIT = 7.0
SWIGLU_ALPHA = 1.702
LN_EPS = 1e-5
NORM_EPS = 1e-6
NEG_BIG = -1e30

LANES = 128
VMEM_LIMIT = 56 * 1024 * 1024

C_GQ, C_GK, C_GV, C_GZ = 0, 1024, 2048, 3072
C_LX, C_LG = 4096, 5120
C_SQ = 6144
C_MA, C_MB, C_MC = 7168, 8192, 9216
C_SK, C_SV = 10240, 10496
C_AB = 10752
FAT_W = 10880
FAT_TN = 2176

ROW_TILE = D_MODEL // LANES

EXPERT_BLK = 512
CHUNKS_IN_FLIGHT = 2
SWA_KV_IN_FLIGHT = 4


def _params(sem):
    return pltpu.CompilerParams(dimension_semantics=sem, vmem_limit_bytes=VMEM_LIMIT)


def _sigmoid(x):
    return 1.0 / (1.0 + jnp.exp(-x))


def _softplus(x):
    return jnp.maximum(x, 0.0) + jnp.log(1.0 + jnp.exp(-jnp.abs(x)))


def _layer_norm(z, g, b):
    mu = jnp.mean(z, axis=-1, keepdims=True)
    zc = z - mu
    var = jnp.mean(zc * zc, axis=-1, keepdims=True)
    return zc * lax.rsqrt(var + LN_EPS) * g + b


def _inproj_kernel(x_ref, w_ref, o_ref, xb_ref):
    @pl.when(pl.program_id(1) == 0)
    def _():
        xb_ref[...] = x_ref[...].astype(bf16)

    o_ref[...] = jnp.dot(xb_ref[...], w_ref[...], preferred_element_type=f32).astype(o_ref.dtype)


def inproj(x2d, wcat, tm=2048):
    T, K = x2d.shape
    N = wcat.shape[1]
    tn = FAT_TN
    return pl.pallas_call(
        _inproj_kernel,
        grid=(T // tm, N // tn),
        in_specs=[pl.BlockSpec((tm, K), lambda i, j: (i, 0)),
                  pl.BlockSpec((K, tn), lambda i, j: (0, j))],
        out_specs=pl.BlockSpec((tm, tn), lambda i, j: (i, j)),
        out_shape=jax.ShapeDtypeStruct((T, N), bf16),
        scratch_shapes=[pltpu.VMEM((tm, K), bf16)],
        compiler_params=_params(("arbitrary", "arbitrary")),
        name="inproj",
    )(x2d, wcat)


ZERO_FILL_COPIES = 4


def _zero_fill(step, zero_ref, zbuf_ref, zsem, zero_rows):
    zr = zero_ref.shape[0]
    return [pltpu.make_async_copy(zero_ref, zbuf_ref.at[pl.ds(pl.multiple_of(step * zero_rows + c * zr, zr), zr)], zsem)
            for c in range(zero_rows // zr)]


def _shift_matrices(ts):
    ri = lax.broadcasted_iota(i32, (ts, ts), 0)
    ci = lax.broadcasted_iota(i32, (ts, ts), 1)
    return [jnp.where(ri - ci == d, 1.0, 0.0).astype(bf16) for d in (3, 2, 1)]


def _causal_conv_silu(src_ref, dst_ref, cz_ref, carry_ref, w, smats, ts, head_scale=None):
    xb = src_ref[...]
    x = xb.astype(f32)
    y = w[3:4, :] * x
    for j, sm in enumerate(smats):
        y = y + w[j:j + 1, :] * jnp.dot(sm, xb, preferred_element_type=f32)
    cz_ref[0:8, :] = carry_ref[...]
    cz_ref[8:16, :] = jnp.zeros((8, x.shape[1]), f32)
    corr = w[0:1, :] * cz_ref[5:13, :] + w[1:2, :] * cz_ref[6:14, :] + w[2:3, :] * cz_ref[7:15, :]
    carry_ref[...] = x[ts - 8:ts, :]

    def post(rows, yv):
        a = yv * _sigmoid(yv)
        if head_scale is None:
            dst_ref[rows, :] = a
            return
        for h in range(a.shape[1] // LANES):
            cs = slice(h * LANES, (h + 1) * LANES)
            ah = a[:, cs]
            dst_ref[rows, cs] = ah * (lax.rsqrt(jnp.sum(ah * ah, axis=-1, keepdims=True) + NORM_EPS) * head_scale)

    post(slice(0, 8), y[0:8] + corr)
    post(slice(8, ts), y[8:ts])


def _gdn_kernel(q_ref, k_ref, v_ref, z_ref, ab_ref, cw_ref, alog_ref, dtb_ref, nw_ref, o_ref, zbuf_ref,
                xpad_ref, qs_ref, ks_ref, vs_ref, carry_ref, state_ref, g_ref, beta_ref,
                u_s, lhs_s, intra_s, kdt_s, zero_ref, zsem, *, ts, zero_rows):
    C = GDN_CHUNK
    D = GDN_HEAD_DIM
    P = 2 * C

    step = pl.program_id(0) * pl.num_programs(1) + pl.program_id(1)

    @pl.when(step == 0)
    def _():
        zero_ref[...] = jnp.zeros_like(zero_ref)

    fills = _zero_fill(step, zero_ref, zbuf_ref, zsem, zero_rows)
    for f in fills:
        f.start()

    @pl.when(pl.program_id(1) == 0)
    def _():
        carry_ref[...] = jnp.zeros_like(carry_ref)
        state_ref[...] = jnp.zeros_like(state_ref)

    smats = _shift_matrices(ts)
    for p, (src, dst, scale) in enumerate(((q_ref, qs_ref, D ** -0.5), (k_ref, ks_ref, 1.0), (v_ref, vs_ref, None))):
        _causal_conv_silu(src, dst, xpad_ref, carry_ref.at[p], cw_ref[:, p * 1024:(p + 1) * 1024], smats, ts, scale)

    ab = ab_ref[...].astype(f32)
    g = -jnp.exp(alog_ref[...]) * _softplus(ab + dtb_ref[...])
    rin = lax.broadcasted_iota(i32, (ts, LANES), 0) & (C - 1)
    gsum = g
    for d in (1, 2, 4, 8, 16, 32):
        gsum = gsum + jnp.where(rin >= d, pltpu.roll(gsum, d, 0), 0.0)
    g_ref[...] = gsum
    beta_ref[...] = _sigmoid(ab)

    ri = lax.broadcasted_iota(i32, (P, P), 0)
    ci = lax.broadcasted_iota(i32, (P, P), 1)
    same = (ri >= C) == (ci >= C)
    eye = ri == ci
    causal = same & (ri >= ci)
    strict = same & (ri > ci)
    eye_f = jnp.where(eye, 1.0, 0.0).astype(f32)
    first_cols = ci < C
    nw = nw_ref[...]

    def stack(a, b):
        return jnp.concatenate([a, b], axis=0)

    def mm(a, b):
        return jnp.dot(a, b, preferred_element_type=f32)

    npair = GDN_HEADS // 2
    nchunk = ts // C
    hcols = [slice(h * D, (h + 1) * D) for h in range(GDN_HEADS)]

    for cg in range(0, nchunk, CHUNKS_IN_FLIGHT):
        probs = [(c, hp) for c in range(cg, cg + CHUNKS_IN_FLIGHT) for hp in range(npair)]
        qn, kn, vb, gcol, eg, egl, kb = [], [], [], [], [], [], []
        for c, hp in probs:
            rows = slice(c * C, (c + 1) * C)
            c0, c1 = hcols[2 * hp], hcols[2 * hp + 1]
            gc = g_ref[rows, :]
            bc = beta_ref[rows, :]
            qn.append(stack(qs_ref[rows, c0], qs_ref[rows, c1]))
            kn.append(stack(ks_ref[rows, c0], ks_ref[rows, c1]))
            v2 = stack(vs_ref[rows, c0], vs_ref[rows, c1])
            h0, h1 = 2 * hp, 2 * hp + 1
            gcl = stack(gc[:, h0:h0 + 1], gc[:, h1:h1 + 1])
            bcl = stack(bc[:, 8 + h0:9 + h0], bc[:, 8 + h1:9 + h1])
            glast = stack(jnp.broadcast_to(gc[C - 1:C, h0:h0 + 1], (C, 1)),
                          jnp.broadcast_to(gc[C - 1:C, h1:h1 + 1], (C, 1)))
            gcol.append(gcl)
            eg.append(jnp.exp(gcl))
            egl.append(jnp.exp(glast - gcl))
            kb.append(kn[-1] * bcl)
            vb.append(v2 * bcl)
        n = len(probs)
        a2 = [lax.dot_general(stack(kb[i], qn[i]).astype(bf16), kn[i].astype(bf16), (((1,), (1,)), ((), ())),
                              preferred_element_type=f32) for i in range(n)]
        lmat, intra = [], []
        for i in range(n):
            gm = jnp.broadcast_to(gcol[i], (P, P))
            grow = jnp.sum(jnp.where(eye, gm, 0.0), axis=0, keepdims=True)
            decay = jnp.where(causal, jnp.exp(jnp.minimum(gm - grow, 0.0)), 0.0)
            lmat.append(jnp.where(strict, a2[i][:P] * decay, 0.0))
            intra.append(a2[i][P:] * decay)
        lb = [l.astype(bf16) for l in lmat]
        xm = [eye_f - l for l in lmat]
        pm = [mm(b, b) for b in lb]
        for it in range(5):
            pb = [p.astype(bf16) for p in pm]
            xm = [x + mm(x.astype(bf16), b) for x, b in zip(xm, pb)]
            if it < 4:
                pm = [mm(b, b) for b in pb]
        uw = [mm(xm[i].astype(bf16), jnp.concatenate([vb[i], kb[i] * eg[i]], axis=1).astype(bf16))
              for i in range(n)]
        for i, (c, hp) in enumerate(probs):
            j = c * npair + hp
            w2 = uw[i][:, D:]
            qd = qn[i] * eg[i]
            kdt = (kn[i] * egl[i]).T
            u_s[j] = uw[i][:, :D]
            lhs_s[j, 0] = stack(w2[:C], qd[:C]).astype(bf16)
            lhs_s[j, 1] = stack(w2[C:], qd[C:]).astype(bf16)
            intra_s[j] = intra[i].astype(bf16)
            kdt_s[j, 0] = jnp.where(first_cols, kdt, 0.0).astype(bf16)
            kdt_s[j, 1] = jnp.where(first_cols, 0.0, kdt).astype(bf16)

    for c in range(nchunk):
        rows = slice(c * C, (c + 1) * C)
        gl = g_ref[(c + 1) * C - 1:(c + 1) * C, :]
        st = [state_ref[h] for h in range(GDN_HEADS)]
        wq = [mm(lhs_s[c * npair + h // 2, h % 2], st[h].astype(bf16)) for h in range(GDN_HEADS)]
        vnb = [(u_s[c * npair + hp] - stack(wq[2 * hp][:C], wq[2 * hp + 1][:C])).astype(bf16)
               for hp in range(npair)]
        o2 = [stack(wq[2 * hp][C:], wq[2 * hp + 1][C:]) + mm(intra_s[c * npair + hp], vnb[hp])
              for hp in range(npair)]
        for h in range(GDN_HEADS):
            state_ref[h] = st[h] * jnp.exp(gl[:, h:h + 1]) + mm(kdt_s[c * npair + h // 2, h % 2], vnb[h // 2])
        for hp in range(npair):
            c0, c1 = hcols[2 * hp], hcols[2 * hp + 1]
            z2 = stack(z_ref[rows, c0], z_ref[rows, c1]).astype(f32)
            on = (o2[hp] * lax.rsqrt(jnp.mean(o2[hp] * o2[hp], axis=-1, keepdims=True) + NORM_EPS) * nw
                  * (z2 * _sigmoid(z2))).astype(o_ref.dtype)
            o_ref[rows, c0] = on[:C]
            o_ref[rows, c1] = on[C:]
    for f in fills:
        f.wait()


def gdn_branch(fat, conv_w, a_log_row, dt_bias_row, norm_w_row, B, S, buf_rows, ts=256):
    T = B * S
    ns = S // ts
    nprob = (ts // GDN_CHUNK) * (GDN_HEADS // 2)
    zero_rows = buf_rows // (B * ns)
    assert zero_rows * B * ns == buf_rows and zero_rows % (ZERO_FILL_COPIES * ROW_TILE) == 0
    row = lambda b, s: b * ns + s
    blk = lambda cb: pl.BlockSpec((ts, 1024), lambda b, s: (row(b, s), cb))
    full = lambda shp: pl.BlockSpec(shp, lambda b, s: (0,) * len(shp))
    return pl.pallas_call(
        functools.partial(_gdn_kernel, ts=ts, zero_rows=zero_rows),
        grid=(B, ns),
        in_specs=[blk(C_GQ // 1024), blk(C_GK // 1024), blk(C_GV // 1024), blk(C_GZ // 1024),
                  pl.BlockSpec((ts, LANES), lambda b, s: (row(b, s), C_AB // LANES)),
                  full((CONV_WIDTH, 3072)), full((1, LANES)), full((1, LANES)), full((1, LANES))],
        out_specs=[pl.BlockSpec((ts, 1024), lambda b, s: (row(b, s), 0)), pl.BlockSpec(memory_space=pl.ANY)],
        out_shape=[jax.ShapeDtypeStruct((T, 1024), bf16), jax.ShapeDtypeStruct((buf_rows, LANES), f32)],
        scratch_shapes=[pltpu.VMEM((16, 1024), f32),
                        pltpu.VMEM((ts, 1024), f32), pltpu.VMEM((ts, 1024), f32), pltpu.VMEM((ts, 1024), f32),
                        pltpu.VMEM((3, 8, 1024), f32),
                        pltpu.VMEM((GDN_HEADS, GDN_HEAD_DIM, GDN_HEAD_DIM), f32),
                        pltpu.VMEM((ts, LANES), f32), pltpu.VMEM((ts, LANES), f32),
                        pltpu.VMEM((nprob, 128, GDN_HEAD_DIM), f32),
                        pltpu.VMEM((nprob, 2, 128, GDN_HEAD_DIM), bf16),
                        pltpu.VMEM((nprob, 128, 128), bf16),
                        pltpu.VMEM((nprob, 2, GDN_HEAD_DIM, 128), bf16),
                        pltpu.VMEM((zero_rows // ZERO_FILL_COPIES, LANES), f32), pltpu.SemaphoreType.DMA(())],
        compiler_params=_params(("arbitrary", "arbitrary")),
        name="gdn",
    )(fat, fat, fat, fat, fat, conv_w, a_log_row, dt_bias_row, norm_w_row)


def _lru_kernel(x_ref, gate_ref, cw_ref, cb_ref, wax_ref, ba_ref, bx_ref, lam_ref, o_ref,
                cz_ref, xc_ref, a_ref, u_ref, h_ref, carry_ref, hc_ref, *, ts):
    half = ts // 2

    @pl.when(pl.program_id(1) == 0)
    def _():
        carry_ref[...] = jnp.zeros_like(carry_ref)
        hc_ref[...] = jnp.zeros_like(hc_ref)

    w = cw_ref[...]
    smats = _shift_matrices(half)
    cz_ref[8:16, :] = jnp.zeros((8, D_MODEL), f32)
    for hf in range(2):
        rows = slice(hf * half, (hf + 1) * half)
        xb = x_ref[rows, :]
        x = xb.astype(f32)
        y = w[3:4, :] * x + cb_ref[...]
        for j, sm in enumerate(smats):
            y = y + w[j:j + 1, :] * jnp.dot(sm, xb, preferred_element_type=f32)
        cz_ref[0:8, :] = carry_ref[...]
        corr = w[0:1, :] * cz_ref[5:13, :] + w[1:2, :] * cz_ref[6:14, :] + w[2:3, :] * cz_ref[7:15, :]
        carry_ref[...] = x[half - 8:half, :]
        xc_ref[hf * half:hf * half + 8, :] = y[0:8] + corr
        xc_ref[hf * half + 8:(hf + 1) * half, :] = y[8:half]

    nsp = _softplus(-lam_ref[...])
    for blk in range(LRU_BLOCKS):
        cs = slice(blk * LRU_BLOCK_DIM, (blk + 1) * LRU_BLOCK_DIM)
        xc = xc_ref[:, cs]
        ri = jnp.dot(xc.astype(bf16), wax_ref[blk], preferred_element_type=f32)
        r = _sigmoid(ri[:, :LRU_BLOCK_DIM] + ba_ref[:, cs])
        i = _sigmoid(ri[:, LRU_BLOCK_DIM:] + bx_ref[:, cs])
        a = jnp.exp(-LRU_C * r * nsp[:, cs])
        a_ref[:, cs] = a
        u_ref[:, cs] = jnp.sqrt(1.0 - a * a) * (i * xc)

    rowi = lax.broadcasted_iota(i32, (8, D_MODEL), 0)

    def group(gi, h):
        r0 = pl.multiple_of(gi * 8, 8)
        a = a_ref[pl.ds(r0, 8), :]
        b = u_ref[pl.ds(r0, 8), :]
        for d in (1, 2, 4):
            m = rowi >= d
            a_s = pltpu.roll(a, d, 0)
            b_s = pltpu.roll(b, d, 0)
            b = jnp.where(m, a * b_s + b, b)
            a = jnp.where(m, a * a_s, a)
        hh = a * h + b
        h_ref[pl.ds(r0, 8), :] = hh
        return hh[7:8, :]

    hc_ref[...] = lax.fori_loop(0, ts // 8, group, hc_ref[...])
    gt = gate_ref[...].astype(f32)
    o_ref[...] = (h_ref[...] * jax.nn.gelu(gt)).astype(o_ref.dtype)


def lru_branch(fat, conv_w, conv_b, wax, b_a, b_x, lam, B, S, ts=512):
    T = B * S
    ns = S // ts
    row = lambda b, s: b * ns + s
    full = lambda shp: pl.BlockSpec(shp, lambda b, s: (0,) * len(shp))
    return pl.pallas_call(
        functools.partial(_lru_kernel, ts=ts),
        grid=(B, ns),
        in_specs=[pl.BlockSpec((ts, 1024), lambda b, s: (row(b, s), C_LX // 1024)),
                  pl.BlockSpec((ts, 1024), lambda b, s: (row(b, s), C_LG // 1024)),
                  full((CONV_WIDTH, 1024)), full((1, 1024)), full((LRU_BLOCKS, LRU_BLOCK_DIM, 2 * LRU_BLOCK_DIM)),
                  full((1, 1024)), full((1, 1024)), full((1, 1024))],
        out_specs=pl.BlockSpec((ts, 1024), lambda b, s: (row(b, s), 0)),
        out_shape=jax.ShapeDtypeStruct((T, 1024), bf16),
        scratch_shapes=[pltpu.VMEM((16, 1024), f32), pltpu.VMEM((ts, 1024), f32),
                        pltpu.VMEM((ts, 1024), f32), pltpu.VMEM((ts, 1024), f32), pltpu.VMEM((ts, 1024), f32),
                        pltpu.VMEM((8, 1024), f32), pltpu.VMEM((1, 1024), f32)],
        compiler_params=_params(("arbitrary", "arbitrary")),
        name="lru",
    )(fat, fat, conv_w, conv_b, wax, b_a, b_x, lam)


def _swa_kernel(q_ref, kc_ref, kp_ref, vc_ref, vp_ref, biast_ref, sink_ref, o_ref, kb_ref, vt_ref, *, tq):
    W = WINDOW
    hd = SWA_HEAD_DIM
    kb_ref[0:W, :] = kp_ref[...]
    kb_ref[W:W + tq, :] = kc_ref[...]
    vt_ref[0] = vp_ref[...].astype(f32).T.astype(bf16)
    for j in range(tq // W):
        vt_ref[j + 1] = vc_ref[j * W:(j + 1) * W, :].astype(f32).T.astype(bf16)
    first_tile = pl.program_id(1) == 0
    ones_rows = jnp.ones((8, 2 * W), bf16)
    scale = hd ** -0.5

    def qblock(n, carry):
        r0 = pl.multiple_of(n * W, W)
        tab = jnp.where(jnp.logical_and(first_tile, n == 0), 1, 0)
        qs = q_ref[pl.ds(r0, W), :] * scale
        vt_band = jnp.concatenate([vt_ref[n], vt_ref[n + 1]], axis=1)
        outs = []
        for hk0 in range(0, SWA_KV_HEADS, SWA_KV_IN_FLIGHT):
            hks = range(hk0, hk0 + SWA_KV_IN_FLIGHT)
            heads = range(hk0 * SWA_GROUP, (hk0 + SWA_KV_IN_FLIGHT) * SWA_GROUP)
            kk = {hk: kb_ref[pl.ds(r0, 2 * W), hk * hd:(hk + 1) * hd] for hk in hks}
            lhs_v = {hk: jnp.concatenate([vt_band[hk * hd:(hk + 1) * hd, :], ones_rows], axis=0) for hk in hks}
            st = [lax.dot_general(kk[h // SWA_GROUP], qs[:, h * hd:(h + 1) * hd], (((1,), (1,)), ((), ())),
                                  preferred_element_type=f32) + biast_ref[tab, h] for h in heads]
            m = [jnp.maximum(jnp.max(t, axis=0, keepdims=True), sink_ref[h]) for t, h in zip(st, heads)]
            pt = [jnp.exp(t - mm).astype(bf16) for t, mm in zip(st, m)]
            ov = [jnp.dot(lhs_v[h // SWA_GROUP], t, preferred_element_type=f32) for t, h in zip(pt, heads)]
            outs += [o[:hd] / (o[hd:hd + 1] + jnp.exp(sink_ref[h] - mm)) for o, mm, h in zip(ov, m, heads)]
        o_ref[pl.ds(r0, W), :] = jnp.concatenate(outs, axis=0).T.astype(o_ref.dtype)
        return carry

    lax.fori_loop(0, tq // W, qblock, 0)


def swa_branch(fat, bias_tab_t, sinks, B, S, tq=512):
    T = B * S
    ns = S // tq
    nb = S // WINDOW
    per = tq // WINDOW
    row = lambda b, s: b * ns + s
    prev = lambda b, s: b * nb + jnp.maximum(s * per - 1, 0)
    kvw = SWA_KV_HEADS * SWA_HEAD_DIM
    full = lambda shp: pl.BlockSpec(shp, lambda b, s: (0,) * len(shp))
    return pl.pallas_call(
        functools.partial(_swa_kernel, tq=tq),
        grid=(B, ns),
        in_specs=[pl.BlockSpec((tq, 1024), lambda b, s: (row(b, s), C_SQ // 1024)),
                  pl.BlockSpec((tq, kvw), lambda b, s: (row(b, s), C_SK // kvw)),
                  pl.BlockSpec((WINDOW, kvw), lambda b, s: (prev(b, s), C_SK // kvw)),
                  pl.BlockSpec((tq, kvw), lambda b, s: (row(b, s), C_SV // kvw)),
                  pl.BlockSpec((WINDOW, kvw), lambda b, s: (prev(b, s), C_SV // kvw)),
                  full((2, SWA_Q_HEADS, 2 * WINDOW, WINDOW)), pl.BlockSpec(memory_space=pltpu.SMEM)],
        out_specs=pl.BlockSpec((tq, 1024), lambda b, s: (row(b, s), 0)),
        out_shape=jax.ShapeDtypeStruct((T, 1024), bf16),
        scratch_shapes=[pltpu.VMEM((tq + WINDOW, kvw), bf16), pltpu.VMEM((per + 1, kvw, WINDOW), bf16)],
        compiler_params=_params(("arbitrary", "arbitrary")),
        name="swa",
    )(fat, fat, fat, fat, fat, bias_tab_t, sinks)


def _load_row_tiles(ref, n, lead=()):
    return jnp.concatenate([ref[lead + (pl.ds(s, n, stride=ROW_TILE), slice(None))] for s in range(ROW_TILE)], axis=1)


def _store_row_tiles(ref, val):
    n = val.shape[0]
    for s in range(ROW_TILE):
        ref[pl.ds(s, n, stride=ROW_TILE), :] = val[:, s * LANES:(s + 1) * LANES]


def _merge_kernel(oa_ref, ob_ref, oc_ref, ga_ref, gb_ref, gc_ref, x_ref, wa_ref, wb_ref, wc_ref, wo_ref,
                  g_ref, b_ref, o_ref, ot_ref, *, alpha):
    ya = jnp.dot(oa_ref[...], wa_ref[...], preferred_element_type=f32)
    yb = jnp.dot(ob_ref[...], wb_ref[...], preferred_element_type=f32)
    yc = jnp.dot(oc_ref[...], wc_ref[...], preferred_element_type=f32)
    mix = (_sigmoid(ga_ref[...].astype(f32)) * ya + _sigmoid(gb_ref[...].astype(f32)) * yb
           + _sigmoid(gc_ref[...].astype(f32)) * yc)
    y = jnp.dot(mix.astype(bf16), wo_ref[...], preferred_element_type=f32)
    x1 = _layer_norm(alpha * x_ref[...] + y, g_ref[...], b_ref[...])
    o_ref[...] = x1
    _store_row_tiles(ot_ref, x1)


def merge_ln(oa, ob, oc, fat, x2d, wa, wb, wc, wo, g, b, alpha, tm=512):
    T = x2d.shape[0]
    act = pl.BlockSpec((tm, 1024), lambda i: (i, 0))
    fatb = lambda cb: pl.BlockSpec((tm, 1024), lambda i: (i, cb))
    wsp = pl.BlockSpec((1024, 1024), lambda i: (0, 0))
    vec = pl.BlockSpec((1, 1024), lambda i: (0, 0))
    return pl.pallas_call(
        functools.partial(_merge_kernel, alpha=alpha),
        grid=(T // tm,),
        in_specs=[act, act, act, fatb(C_MA // 1024), fatb(C_MB // 1024), fatb(C_MC // 1024), act,
                  wsp, wsp, wsp, wsp, vec, vec],
        out_specs=[act, pl.BlockSpec((tm * ROW_TILE, LANES), lambda i: (i, 0))],
        out_shape=[jax.ShapeDtypeStruct((T, 1024), f32), jax.ShapeDtypeStruct((T * ROW_TILE, LANES), f32)],
        compiler_params=_params(("arbitrary",)),
        name="merge_ln",
    )(oa, ob, oc, fat, fat, fat, x2d, wa, wb, wc, wo, g, b)


def _route_kernel(x_ref, rw_ref, rb_ref, gates_ref, eidx_ref, rank_ref, cnt_ref, run_ref, *, tm):
    @pl.when(pl.program_id(0) == 0)
    def _():
        run_ref[...] = jnp.zeros_like(run_ref)

    x = x_ref[...]
    xh = x.astype(bf16)
    xl = (x - xh.astype(f32)).astype(bf16)
    hw = jnp.dot(xh, rw_ref[...], preferred_element_type=f32)
    logits = (hw[:, :LANES] + (hw[:, LANES:] + jnp.dot(xl, rw_ref[:, :LANES], preferred_element_type=f32))
              + rb_ref[...])
    lane = lax.broadcasted_iota(i32, (tm, LANES), 1)
    lane_f = lane.astype(f32)
    work = logits
    vals, idxs, hots = [], [], []
    for _ in range(TOP_K):
        m = jnp.max(work, axis=-1, keepdims=True)
        idx = jnp.min(jnp.where(work == m, lane_f, float(LANES)), axis=-1, keepdims=True)
        hot = lane_f == idx
        vals.append(m)
        idxs.append(idx)
        hots.append(hot)
        work = jnp.where(hot, -jnp.inf, work)
    es = [jnp.exp(v - vals[0]) for v in vals]
    den = es[0] + es[1] + es[2] + es[3]
    sel = jnp.zeros((tm, LANES), f32)
    for hot in hots:
        sel = sel + jnp.where(hot, 1.0, 0.0)
    ri = lax.broadcasted_iota(i32, (tm, tm), 0)
    ci = lax.broadcasted_iota(i32, (tm, tm), 1)
    tril = jnp.where(ri > ci, 1.0, 0.0).astype(bf16)
    before = jnp.dot(tril, sel.astype(bf16), preferred_element_type=f32) + run_ref[...]
    run_ref[...] = run_ref[...] + jnp.sum(sel, axis=0, keepdims=True)
    cnt_ref[...] = run_ref[...]
    gates = jnp.zeros((tm, LANES), f32)
    eidx = jnp.zeros((tm, LANES), f32)
    rank = jnp.zeros((tm, LANES), f32)
    for k in range(TOP_K):
        rk = jnp.sum(jnp.where(hots[k], before, 0.0), axis=-1, keepdims=True)
        gates = jnp.where(lane == k, es[k] / den, gates)
        eidx = jnp.where(lane == k, idxs[k], eidx)
        rank = jnp.where(lane == k, rk, rank)
    gates_ref[...] = gates
    eidx_ref[...] = eidx.astype(i32)
    rank_ref[...] = rank.astype(i32)


def route(x2d, rw_pad, rb_pad, tm=512):
    T = x2d.shape[0]
    outb = pl.BlockSpec((tm, LANES), lambda i: (i, 0))
    return pl.pallas_call(
        functools.partial(_route_kernel, tm=tm),
        grid=(T // tm,),
        in_specs=[pl.BlockSpec((tm, 1024), lambda i: (i, 0)),
                  pl.BlockSpec((1024, 2 * LANES), lambda i: (0, 0)),
                  pl.BlockSpec((1, LANES), lambda i: (0, 0))],
        out_specs=[outb, outb, outb, pl.BlockSpec((1, LANES), lambda i: (0, 0))],
        out_shape=[jax.ShapeDtypeStruct((T, LANES), f32), jax.ShapeDtypeStruct((T, LANES), i32),
                   jax.ShapeDtypeStruct((T, LANES), i32), jax.ShapeDtypeStruct((1, LANES), f32)],
        scratch_shapes=[pltpu.VMEM((1, LANES), f32)],
        compiler_params=_params(("arbitrary",)),
        name="route",
    )(x2d, rw_pad, rb_pad)


ROW_UNROLL = 8


def _row_copy(src_ref, src_row, dst_ref, dst_row, sem):
    tile = lambda r: pl.ds(pl.multiple_of(r * ROW_TILE, ROW_TILE), ROW_TILE)
    return pltpu.make_async_copy(src_ref.at[tile(src_row)], dst_ref.at[tile(dst_row)], sem)


def _dispatch_kernel(dest_ref, x_ref, buf_in_ref, buf_ref, sem, *, tm):
    del buf_in_ref

    def issue(g, carry):
        for j in range(ROW_UNROLL):
            r = g * ROW_UNROLL + j
            for k in range(TOP_K):
                _row_copy(x_ref, r, buf_ref, dest_ref[r * TOP_K + k], sem).start(priority=k % 2)
        return carry

    lax.fori_loop(0, tm // ROW_UNROLL, issue, 0)

    def drain(g, carry):
        for j in range(ROW_UNROLL * TOP_K):
            _row_copy(x_ref, 0, buf_ref, 0, sem).wait()
        return carry

    lax.fori_loop(0, tm // ROW_UNROLL, drain, 0)


def dispatch(dest_flat, xt, buf0, tm=512):
    T = xt.shape[0] // ROW_TILE
    return pl.pallas_call(
        functools.partial(_dispatch_kernel, tm=tm),
        grid=(T // tm,),
        in_specs=[pl.BlockSpec((tm * TOP_K,), lambda i: (i,), memory_space=pltpu.SMEM),
                  pl.BlockSpec((tm * ROW_TILE, LANES), lambda i: (i, 0)),
                  pl.BlockSpec(memory_space=pl.ANY)],
        out_specs=pl.BlockSpec(memory_space=pl.ANY),
        out_shape=jax.ShapeDtypeStruct(buf0.shape, buf0.dtype),
        scratch_shapes=[pltpu.SemaphoreType.DMA(())],
        input_output_aliases={2: 0},
        compiler_params=_params(("arbitrary",)),
        name="dispatch",
    )(dest_flat, xt, buf0)


GU_GROUP = 2 * LANES


def _gu_prep_kernel(w_ref, o_ref):
    ri = lax.broadcasted_iota(i32, (GU_GROUP, GU_GROUP), 0)
    ci = lax.broadcasted_iota(i32, (GU_GROUP, GU_GROUP), 1)
    src = jnp.where(ci < LANES, 2 * ci, 2 * (ci - LANES) + 1)
    perm = jnp.where(ri == src, 1.0, 0.0).astype(bf16)
    for g in range(w_ref.shape[3] // GU_GROUP):
        cs = slice(g * GU_GROUP, (g + 1) * GU_GROUP)
        o_ref[0, :, cs] = jnp.dot(w_ref[0, 0, :, cs].astype(bf16), perm, preferred_element_type=f32).astype(bf16)


def gu_prep(w_gu_all, layer, tk=512):
    _, E, D, N = w_gu_all.shape
    return pl.pallas_call(
        _gu_prep_kernel,
        grid=(E, D // tk),
        in_specs=[pl.BlockSpec((1, 1, tk, N), lambda e, k: (layer, e, k, 0))],
        out_specs=pl.BlockSpec((1, tk, N), lambda e, k: (e, k, 0)),
        out_shape=jax.ShapeDtypeStruct((E, D, N), bf16),
        compiler_params=_params(("arbitrary", "arbitrary")),
        name="gu_prep",
    )(w_gu_all)


def _expert_kernel(blk_e_ref, nused_ref, x_ref, wgu_ref, bgu_ref, wd_ref, bd_ref, o_ref):
    i = pl.program_id(0)

    @pl.when(i < nused_ref[0])
    def _():
        de = wd_ref.shape[1]
        xb = _load_row_tiles(x_ref, EXPERT_BLK).astype(bf16)
        hgu = jnp.dot(xb, wgu_ref[0], preferred_element_type=f32) + bgu_ref[0]
        acts = []
        for g in range(2 * de // GU_GROUP):
            gate = jnp.minimum(hgu[:, g * GU_GROUP:g * GU_GROUP + LANES], SWIGLU_LIMIT)
            lin = jnp.clip(hgu[:, g * GU_GROUP + LANES:(g + 1) * GU_GROUP], -SWIGLU_LIMIT, SWIGLU_LIMIT)
            acts.append((gate * _sigmoid(SWIGLU_ALPHA * gate) * (lin + 1.0)).astype(bf16))
        act = jnp.concatenate(acts, axis=1)
        _store_row_tiles(o_ref, jnp.dot(act, wd_ref[0], preferred_element_type=f32) + bd_ref[0])

    @pl.when(i >= nused_ref[0])
    def _():
        o_ref[...] = jnp.zeros_like(o_ref)


def experts(blk_e, nused, buf, wgu, bgu, wd, bd):
    D = wd.shape[2]
    nblk = buf.shape[0] // (EXPERT_BLK * ROW_TILE)
    de = wd.shape[1]
    tile_blk = (EXPERT_BLK * ROW_TILE, LANES)
    live = lambda i, be, nu: jnp.minimum(i, nu[0] - 1)
    grid_spec = pltpu.PrefetchScalarGridSpec(
        num_scalar_prefetch=2,
        grid=(nblk,),
        in_specs=[pl.BlockSpec(tile_blk, lambda i, be, nu: (live(i, be, nu), 0)),
                  pl.BlockSpec((1, D, 2 * de), lambda i, be, nu: (be[i], 0, 0)),
                  pl.BlockSpec((1, 1, 2 * de), lambda i, be, nu: (be[i], 0, 0)),
                  pl.BlockSpec((1, de, D), lambda i, be, nu: (be[i], 0, 0)),
                  pl.BlockSpec((1, 1, D), lambda i, be, nu: (be[i], 0, 0))],
        out_specs=pl.BlockSpec(tile_blk, lambda i, be, nu: (i, 0)),
    )
    return pl.pallas_call(
        _expert_kernel,
        grid_spec=grid_spec,
        out_shape=jax.ShapeDtypeStruct(buf.shape, f32),
        compiler_params=_params(("arbitrary",)),
        name="experts",
    )(blk_e, nused, buf, wgu, bgu, wd, bd)


def _combine_kernel(dest_ref, dest_next_ref, gates_ref, x_ref, p_ref, obuf_ref, wg_ref, wp_ref,
                    g2_ref, b2_ref, g3_ref, b3_ref, o_ref, rows_ref, sems, *, tm, alpha):
    i = pl.program_id(0)
    n = pl.num_programs(0)
    slot = i % 2

    def gather(idx_ref, s):
        def issue(g, carry):
            for j in range(ROW_UNROLL):
                r = g * ROW_UNROLL + j
                for k in range(TOP_K):
                    _row_copy(obuf_ref, idx_ref[r * TOP_K + k], rows_ref.at[s, k], r, sems.at[s]).start(priority=k % 2)
            return carry

        lax.fori_loop(0, tm // ROW_UNROLL, issue, 0)

    @pl.when(i == 0)
    def _():
        gather(dest_ref, 0)

    @pl.when(i + 1 < n)
    def _():
        gather(dest_next_ref, 1 - slot)

    def drain(g, carry):
        for j in range(ROW_UNROLL * TOP_K):
            _row_copy(obuf_ref, 0, rows_ref.at[slot, 0], 0, sems.at[slot]).wait()
        return carry

    lax.fori_loop(0, tm // ROW_UNROLL, drain, 0)

    gates = gates_ref[...]
    y = gates[:, 0:1] * _load_row_tiles(rows_ref, tm, (slot, 0))
    for k in range(1, TOP_K):
        y = y + gates[:, k:k + 1] * _load_row_tiles(rows_ref, tm, (slot, k))
    x2 = _layer_norm(alpha * x_ref[...] + y, g2_ref[...], b2_ref[...])
    gate = _sigmoid(jnp.dot(x2.astype(bf16), wg_ref[...], preferred_element_type=f32))
    proj = jnp.dot(p_ref[...].astype(bf16), wp_ref[...], preferred_element_type=f32)
    o_ref[...] = _layer_norm(alpha * x2 + gate * proj, g3_ref[...], b3_ref[...])


def combine_ple(dest_flat, gates, x2d, p2d, layer, obuf, wg, wp, g2, b2, g3, b3, alpha, tm=256):
    T = x2d.shape[0]
    nt = T // tm
    act = pl.BlockSpec((tm, 1024), lambda i: (i, 0))
    vec = pl.BlockSpec((1, 1024), lambda i: (0, 0))
    return pl.pallas_call(
        functools.partial(_combine_kernel, tm=tm, alpha=alpha),
        grid=(nt,),
        in_specs=[pl.BlockSpec((tm * TOP_K,), lambda i: (i,), memory_space=pltpu.SMEM),
                  pl.BlockSpec((tm * TOP_K,), lambda i: (jnp.minimum(i + 1, nt - 1),), memory_space=pltpu.SMEM),
                  pl.BlockSpec((tm, LANES), lambda i: (i, 0)),
                  act,
                  pl.BlockSpec((tm, PLE_DIM), lambda i: (layer * nt + i, 0)),
                  pl.BlockSpec(memory_space=pl.ANY),
                  pl.BlockSpec((1024, 1024), lambda i: (0, 0)),
                  pl.BlockSpec((PLE_DIM, 1024), lambda i: (0, 0)),
                  vec, vec, vec, vec],
        out_specs=act,
        out_shape=jax.ShapeDtypeStruct((T, 1024), f32),
        scratch_shapes=[pltpu.VMEM((2, TOP_K, tm * ROW_TILE, LANES), f32), pltpu.SemaphoreType.DMA((2,))],
        compiler_params=_params(("arbitrary",)),
        name="combine_ple",
    )(dest_flat, dest_flat, gates, x2d, p2d, obuf, wg, wp, g2, b2, g3, b3)


def _t5_bucket_np(dist):
    max_exact = REL_BUCKETS // 2
    d = np.maximum(dist.astype(np.float32), np.float32(1.0))
    large = max_exact + (np.log(d / np.float32(max_exact)) / np.float32(math.log(REL_MAX_DISTANCE / max_exact))
                         * np.float32(REL_BUCKETS - max_exact)).astype(np.int32)
    large = np.minimum(large, REL_BUCKETS - 1)
    return np.where(dist < max_exact, dist, large)


def _swa_bias_table(rel_bias):
    dist = np.arange(2 * WINDOW - 1, -WINDOW, -1)
    in_window = (dist >= 0) & (dist < WINDOW)
    per_dist = rel_bias[_t5_bucket_np(np.maximum(dist, 0))].astype(f32).T
    per_dist = jnp.where(jnp.asarray(in_window)[None], per_dist, NEG_BIG)
    bias = jnp.stack([per_dist[:, WINDOW - 1 - q:3 * WINDOW - 1 - q] for q in range(WINDOW)], axis=2)
    first = jnp.where((jnp.arange(2 * WINDOW) >= WINDOW)[None, :, None], bias, NEG_BIG)
    return jnp.stack([bias, first])


def _pad_row(v, width=LANES, fill=0.0):
    v = v.astype(f32).reshape(1, -1)
    return jnp.pad(v, ((0, 0), (0, width - v.shape[1])), constant_values=fill)


def _wcat(w_in):
    cols = [w_in[:, 0:4096], w_in[:, 4112:6160], w_in[:, 6160:7184], w_in[:, 7696:10768],
            w_in[:, 7184:7440], w_in[:, 7440:7696], w_in[:, 4096:4112],
            jnp.zeros((w_in.shape[0], FAT_W - 10768), w_in.dtype)]
    return jnp.concatenate(cols, axis=1).astype(bf16)


def kernel(x, p, w_in, conv_qkv_w, gdn_a_log, gdn_dt_bias, gdn_norm_w, rg_conv_w, rg_conv_b, rg_w_a, rg_b_a, rg_w_x, rg_b_x, rg_lambda, attn_sinks, rel_bias, w_o_gdn, w_o_lru, w_o_swa, w_out, ln1_g, ln1_b, router_w, router_b, w_gu, b_gu, w_down, b_down, ln2_g, ln2_b, ple_w_gate, ple_w_proj, ln3_g, ln3_b):
    B, S, D = x.shape
    depth = w_in.shape[0]
    T = B * S
    A = T * TOP_K
    alpha = (2.0 * depth) ** 0.25
    P = A + N_EXPERTS * EXPERT_BLK
    nblk = P // EXPERT_BLK
    row = lambda v: v.astype(f32).reshape(1, -1)

    bias_tab = _swa_bias_table(rel_bias)
    p2d = p.reshape(depth * T, PLE_DIM)
    xc = x.reshape(T, D)
    for i in range(depth):
        fat = inproj(xc, _wcat(w_in[i]))
        o_gdn, buf0 = gdn_branch(fat, conv_qkv_w[i], _pad_row(gdn_a_log[i]), _pad_row(gdn_dt_bias[i]),
                                 row(gdn_norm_w[i]), B, S, P * ROW_TILE)
        wax = jnp.concatenate([rg_w_a[i], rg_w_x[i]], axis=-1).astype(bf16)
        o_lru = lru_branch(fat, rg_conv_w[i], row(rg_conv_b[i]), wax, row(rg_b_a[i]), row(rg_b_x[i]),
                           row(rg_lambda[i]), B, S)
        o_swa = swa_branch(fat, bias_tab, attn_sinks[i].astype(f32), B, S)
        x1, x1t = merge_ln(o_gdn, o_lru, o_swa, fat, xc, w_o_gdn[i].astype(bf16), w_o_lru[i].astype(bf16),
                           w_o_swa[i].astype(bf16), w_out[i].astype(bf16), row(ln1_g[i]), row(ln1_b[i]), alpha)

        rw_pad = jnp.pad(router_w[i].astype(f32), ((0, 0), (0, LANES - N_EXPERTS)))
        rw_hi = rw_pad.astype(bf16)
        rw_pad = jnp.concatenate([rw_hi, (rw_pad - rw_hi.astype(f32)).astype(bf16)], axis=1)
        rb_pad = _pad_row(router_b[i], fill=NEG_BIG)
        gates, eidx, rank, cnt = route(x1, rw_pad, rb_pad)
        counts = cnt[0, :N_EXPERTS].astype(i32)
        padded = ((counts + EXPERT_BLK - 1) // EXPERT_BLK) * EXPERT_BLK
        pad_ends = jnp.cumsum(padded)
        pad_starts = pad_ends - padded
        hit = eidx[:, :TOP_K, None] == jnp.arange(N_EXPERTS, dtype=i32)
        dest = (jnp.sum(jnp.where(hit, pad_starts, 0), axis=-1) + rank[:, :TOP_K]).reshape(A)
        blk_start = jnp.arange(nblk, dtype=i32) * EXPERT_BLK
        blk_e = jnp.minimum(jnp.sum((pad_ends[None, :] <= blk_start[:, None]).astype(i32), axis=1),
                            N_EXPERTS - 1).astype(i32)
        nused = (pad_ends[-1:] // EXPERT_BLK).astype(i32)

        buf = dispatch(dest, x1t, buf0)
        bgu = jnp.transpose(b_gu[i].reshape(N_EXPERTS, -1, LANES, 2), (0, 1, 3, 2)).reshape(N_EXPERTS, 1, -1)
        obuf = experts(blk_e, nused, buf, gu_prep(w_gu, i), bgu, w_down[i].astype(bf16), b_down[i][:, None, :])
        xc = combine_ple(dest, gates, x1, p2d, i, obuf, ple_w_gate[i].astype(bf16),
                         ple_w_proj[i].astype(bf16), row(ln2_g[i]), row(ln2_b[i]), row(ln3_g[i]), row(ln3_b[i]),
                         alpha)
    return xc.reshape(B, S, D)
```

```python
import functools
import math

import numpy as np
import jax
import jax.numpy as jnp
from jax import lax
from jax.experimental import pallas as pl
from jax.experimental.pallas import tpu as pltpu

f32 = jnp.float32
bf16 = jnp.bfloat16
i32 = jnp.int32

D_MODEL = 1024
PLE_DIM = 256
GDN_HEADS = 8
GDN_HEAD_DIM = 128
GDN_CHUNK = 64
CONV_WIDTH = 4
LRU_BLOCKS = 8
LRU_BLOCK_DIM = 128
LRU_C = 8.0
SWA_Q_HEADS = 16
SWA_KV_HEADS = 4
SWA_HEAD_DIM = 64
SWA_GROUP = 4
WINDOW = 128
REL_BUCKETS = 32
REL_MAX_DISTANCE = 128
N_EXPERTS = 32
TOP_K = 4
SWIGLU_LIMIT = 7.0
SWIGLU_ALPHA = 1.702
LN_EPS = 1e-5
NORM_EPS = 1e-6
NEG_BIG = -1e30

LANES = 128
VMEM_LIMIT = 56 * 1024 * 1024

C_GQ, C_GK, C_GV, C_GZ = 0, 1024, 2048, 3072
C_LX, C_LG = 4096, 5120
C_SQ = 6144
C_MA, C_MB, C_MC = 7168, 8192, 9216
C_SK, C_SV = 10240, 10496
C_AB = 10752
FAT_W = 10880
FAT_TN = 2176

ROW_TILE = D_MODEL // LANES

EXPERT_BLK = 512
CHUNKS_IN_FLIGHT = 2
SWA_KV_IN_FLIGHT = 4


def _params(sem):
    return pltpu.CompilerParams(dimension_semantics=sem, vmem_limit_bytes=VMEM_LIMIT)


def _sigmoid(x):
    return 1.0 / (1.0 + jnp.exp(-x))


def _softplus(x):
    return jnp.maximum(x, 0.0) + jnp.log(1.0 + jnp.exp(-jnp.abs(x)))


def _layer_norm(z, g, b):
    mu = jnp.mean(z, axis=-1, keepdims=True)
    zc = z - mu
    var = jnp.mean(zc * zc, axis=-1, keepdims=True)
    return zc * lax.rsqrt(var + LN_EPS) * g + b


def _inproj_kernel(x_ref, w_ref, o_ref, xb_ref):
    @pl.when(pl.program_id(1) == 0)
    def _():
        xb_ref[...] = x_ref[...].astype(bf16)

    o_ref[...] = jnp.dot(xb_ref[...], w_ref[...], preferred_element_type=f32).astype(o_ref.dtype)


def inproj(x2d, wcat, tm=2048):
    T, K = x2d.shape
    N = wcat.shape[1]
    tn = FAT_TN
    return pl.pallas_call(
        _inproj_kernel,
        grid=(T // tm, N // tn),
        in_specs=[pl.BlockSpec((tm, K), lambda i, j: (i, 0)),
                  pl.BlockSpec((K, tn), lambda i, j: (0, j))],
        out_specs=pl.BlockSpec((tm, tn), lambda i, j: (i, j)),
        out_shape=jax.ShapeDtypeStruct((T, N), bf16),
        scratch_shapes=[pltpu.VMEM((tm, K), bf16)],
        compiler_params=_params(("arbitrary", "arbitrary")),
        name="inproj",
    )(x2d, wcat)


ZERO_FILL_COPIES = 4


def _zero_fill(step, zero_ref, zbuf_ref, zsem, zero_rows):
    zr = zero_ref.shape[0]
    return [pltpu.make_async_copy(zero_ref, zbuf_ref.at[pl.ds(pl.multiple_of(step * zero_rows + c * zr, zr), zr)], zsem)
            for c in range(zero_rows // zr)]


def _shift_matrices(ts):
    ri = lax.broadcasted_iota(i32, (ts, ts), 0)
    ci = lax.broadcasted_iota(i32, (ts, ts), 1)
    return [jnp.where(ri - ci == d, 1.0, 0.0).astype(bf16) for d in (3, 2, 1)]


def _causal_conv_silu(src_ref, dst_ref, cz_ref, carry_ref, w, smats, ts, head_scale=None):
    xb = src_ref[...]
    x = xb.astype(f32)
    y = w[3:4, :] * x
    for j, sm in enumerate(smats):
        y = y + w[j:j + 1, :] * jnp.dot(sm, xb, preferred_element_type=f32)
    cz_ref[0:8, :] = carry_ref[...]
    cz_ref[8:16, :] = jnp.zeros((8, x.shape[1]), f32)
    corr = w[0:1, :] * cz_ref[5:13, :] + w[1:2, :] * cz_ref[6:14, :] + w[2:3, :] * cz_ref[7:15, :]
    carry_ref[...] = x[ts - 8:ts, :]

    def post(rows, yv):
        a = yv * _sigmoid(yv)
        if head_scale is None:
            dst_ref[rows, :] = a
            return
        for h in range(a.shape[1] // LANES):
            cs = slice(h * LANES, (h + 1) * LANES)
            ah = a[:, cs]
            dst_ref[rows, cs] = ah * (lax.rsqrt(jnp.sum(ah * ah, axis=-1, keepdims=True) + NORM_EPS) * head_scale)

    post(slice(0, 8), y[0:8] + corr)
    post(slice(8, ts), y[8:ts])


def _gdn_kernel(q_ref, k_ref, v_ref, z_ref, ab_ref, cw_ref, alog_ref, dtb_ref, nw_ref, o_ref, zbuf_ref,
                xpad_ref, qs_ref, ks_ref, vs_ref, carry_ref, state_ref, g_ref, beta_ref,
                u_s, lhs_s, intra_s, kdt_s, zero_ref, zsem, *, ts, zero_rows):
    C = GDN_CHUNK
    D = GDN_HEAD_DIM
    P = 2 * C

    step = pl.program_id(0) * pl.num_programs(1) + pl.program_id(1)

    @pl.when(step == 0)
    def _():
        zero_ref[...] = jnp.zeros_like(zero_ref)

    fills = _zero_fill(step, zero_ref, zbuf_ref, zsem, zero_rows)
    for f in fills:
        f.start()

    @pl.when(pl.program_id(1) == 0)
    def _():
        carry_ref[...] = jnp.zeros_like(carry_ref)
        state_ref[...] = jnp.zeros_like(state_ref)

    smats = _shift_matrices(ts)
    for p, (src, dst, scale) in enumerate(((q_ref, qs_ref, D ** -0.5), (k_ref, ks_ref, 1.0), (v_ref, vs_ref, None))):
        _causal_conv_silu(src, dst, xpad_ref, carry_ref.at[p], cw_ref[:, p * 1024:(p + 1) * 1024], smats, ts, scale)

    ab = ab_ref[...].astype(f32)
    g = -jnp.exp(alog_ref[...]) * _softplus(ab + dtb_ref[...])
    rin = lax.broadcasted_iota(i32, (ts, LANES), 0) & (C - 1)
    gsum = g
    for d in (1, 2, 4, 8, 16, 32):
        gsum = gsum + jnp.where(rin >= d, pltpu.roll(gsum, d, 0), 0.0)
    g_ref[...] = gsum
    beta_ref[...] = _sigmoid(ab)

    ri = lax.broadcasted_iota(i32, (P, P), 0)
    ci = lax.broadcasted_iota(i32, (P, P), 1)
    same = (ri >= C) == (ci >= C)
    eye = ri == ci
    causal = same & (ri >= ci)
    strict = same & (ri > ci)
    eye_f = jnp.where(eye, 1.0, 0.0).astype(f32)
    first_cols = ci < C
    nw = nw_ref[...]

    def stack(a, b):
        return jnp.concatenate([a, b], axis=0)

    def mm(a, b):
        return jnp.dot(a, b, preferred_element_type=f32)

    npair = GDN_HEADS // 2
    nchunk = ts // C
    hcols = [slice(h * D, (h + 1) * D) for h in range(GDN_HEADS)]

    for cg in range(0, nchunk, CHUNKS_IN_FLIGHT):
        probs = [(c, hp) for c in range(cg, cg + CHUNKS_IN_FLIGHT) for hp in range(npair)]
        qn, kn, vb, gcol, eg, egl, kb = [], [], [], [], [], [], []
        for c, hp in probs:
            rows = slice(c * C, (c + 1) * C)
            c0, c1 = hcols[2 * hp], hcols[2 * hp + 1]
            gc = g_ref[rows, :]
            bc = beta_ref[rows, :]
            qn.append(stack(qs_ref[rows, c0], qs_ref[rows, c1]))
            kn.append(stack(ks_ref[rows, c0], ks_ref[rows, c1]))
            v2 = stack(vs_ref[rows, c0], vs_ref[rows, c1])
            h0, h1 = 2 * hp, 2 * hp + 1
            gcl = stack(gc[:, h0:h0 + 1], gc[:, h1:h1 + 1])
            bcl = stack(bc[:, 8 + h0:9 + h0], bc[:, 8 + h1:9 + h1])
            glast = stack(jnp.broadcast_to(gc[C - 1:C, h0:h0 + 1], (C, 1)),
                          jnp.broadcast_to(gc[C - 1:C, h1:h1 + 1], (C, 1)))
            gcol.append(gcl)
            eg.append(jnp.exp(gcl))
            egl.append(jnp.exp(glast - gcl))
            kb.append(kn[-1] * bcl)
            vb.append(v2 * bcl)
        n = len(probs)
        a2 = [lax.dot_general(stack(kb[i], qn[i]).astype(bf16), kn[i].astype(bf16), (((1,), (1,)), ((), ())),
                              preferred_element_type=f32) for i in range(n)]
        lmat, intra = [], []
        for i in range(n):
            gm = jnp.broadcast_to(gcol[i], (P, P))
            grow = jnp.sum(jnp.where(eye, gm, 0.0), axis=0, keepdims=True)
            decay = jnp.where(causal, jnp.exp(jnp.minimum(gm - grow, 0.0)), 0.0)
            lmat.append(jnp.where(strict, a2[i][:P] * decay, 0.0))
            intra.append(a2[i][P:] * decay)
        lb = [l.astype(bf16) for l in lmat]
        xm = [eye_f - l for l in lmat]
        pm = [mm(b, b) for b in lb]
        for it in range(5):
            pb = [p.astype(bf16) for p in pm]
            xm = [x + mm(x.astype(bf16), b) for x, b in zip(xm, pb)]
            if it < 4:
                pm = [mm(b, b) for b in pb]
        uw = [mm(xm[i].astype(bf16), jnp.concatenate([vb[i], kb[i] * eg[i]], axis=1).astype(bf16))
              for i in range(n)]
        for i, (c, hp) in enumerate(probs):
            j = c * npair + hp
            w2 = uw[i][:, D:]
            qd = qn[i] * eg[i]
            kdt = (kn[i] * egl[i]).T
            u_s[j] = uw[i][:, :D]
            lhs_s[j, 0] = stack(w2[:C], qd[:C]).astype(bf16)
            lhs_s[j, 1] = stack(w2[C:], qd[C:]).astype(bf16)
            intra_s[j] = intra[i].astype(bf16)
            kdt_s[j, 0] = jnp.where(first_cols, kdt, 0.0).astype(bf16)
            kdt_s[j, 1] = jnp.where(first_cols, 0.0, kdt).astype(bf16)

    for c in range(nchunk):
        rows = slice(c * C, (c + 1) * C)
        gl = g_ref[(c + 1) * C - 1:(c + 1) * C, :]
        st = [state_ref[h] for h in range(GDN_HEADS)]
        wq = [mm(lhs_s[c * npair + h // 2, h % 2], st[h].astype(bf16)) for h in range(GDN_HEADS)]
        vnb = [(u_s[c * npair + hp] - stack(wq[2 * hp][:C], wq[2 * hp + 1][:C])).astype(bf16)
               for hp in range(npair)]
        o2 = [stack(wq[2 * hp][C:], wq[2 * hp + 1][C:]) + mm(intra_s[c * npair + hp], vnb[hp])
              for hp in range(npair)]
        for h in range(GDN_HEADS):
            state_ref[h] = st[h] * jnp.exp(gl[:, h:h + 1]) + mm(kdt_s[c * npair + h // 2, h % 2], vnb[h // 2])
        for hp in range(npair):
            c0, c1 = hcols[2 * hp], hcols[2 * hp + 1]
            z2 = stack(z_ref[rows, c0], z_ref[rows, c1]).astype(f32)
            on = (o2[hp] * lax.rsqrt(jnp.mean(o2[hp] * o2[hp], axis=-1, keepdims=True) + NORM_EPS) * nw
                  * (z2 * _sigmoid(z2))).astype(o_ref.dtype)
            o_ref[rows, c0] = on[:C]
            o_ref[rows, c1] = on[C:]
    for f in fills:
        f.wait()


def gdn_branch(fat, conv_w, a_log_row, dt_bias_row, norm_w_row, B, S, buf_rows, ts=256):
    T = B * S
    ns = S // ts
    nprob = (ts // GDN_CHUNK) * (GDN_HEADS // 2)
    zero_rows = buf_rows // (B * ns)
    assert zero_rows * B * ns == buf_rows and zero_rows % (ZERO_FILL_COPIES * ROW_TILE) == 0
    row = lambda b, s: b * ns + s
    blk = lambda cb: pl.BlockSpec((ts, 1024), lambda b, s: (row(b, s), cb))
    full = lambda shp: pl.BlockSpec(shp, lambda b, s: (0,) * len(shp))
    return pl.pallas_call(
        functools.partial(_gdn_kernel, ts=ts, zero_rows=zero_rows),
        grid=(B, ns),
        in_specs=[blk(C_GQ // 1024), blk(C_GK // 1024), blk(C_GV // 1024), blk(C_GZ // 1024),
                  pl.BlockSpec((ts, LANES), lambda b, s: (row(b, s), C_AB // LANES)),
                  full((CONV_WIDTH, 3072)), full((1, LANES)), full((1, LANES)), full((1, LANES))],
        out_specs=[pl.BlockSpec((ts, 1024), lambda b, s: (row(b, s), 0)), pl.BlockSpec(memory_space=pl.ANY)],
        out_shape=[jax.ShapeDtypeStruct((T, 1024), bf16), jax.ShapeDtypeStruct((buf_rows, LANES), f32)],
        scratch_shapes=[pltpu.VMEM((16, 1024), f32),
                        pltpu.VMEM((ts, 1024), f32), pltpu.VMEM((ts, 1024), f32), pltpu.VMEM((ts, 1024), f32),
                        pltpu.VMEM((3, 8, 1024), f32),
                        pltpu.VMEM((GDN_HEADS, GDN_HEAD_DIM, GDN_HEAD_DIM), f32),
                        pltpu.VMEM((ts, LANES), f32), pltpu.VMEM((ts, LANES), f32),
                        pltpu.VMEM((nprob, 128, GDN_HEAD_DIM), f32),
                        pltpu.VMEM((nprob, 2, 128, GDN_HEAD_DIM), bf16),
                        pltpu.VMEM((nprob, 128, 128), bf16),
                        pltpu.VMEM((nprob, 2, GDN_HEAD_DIM, 128), bf16),
                        pltpu.VMEM((zero_rows // ZERO_FILL_COPIES, LANES), f32), pltpu.SemaphoreType.DMA(())],
        compiler_params=_params(("arbitrary", "arbitrary")),
        name="gdn",
    )(fat, fat, fat, fat, fat, conv_w, a_log_row, dt_bias_row, norm_w_row)


def _lru_kernel(x_ref, gate_ref, cw_ref, cb_ref, wax_ref, ba_ref, bx_ref, lam_ref, o_ref,
                cz_ref, xc_ref, a_ref, u_ref, h_ref, carry_ref, hc_ref, *, ts):
    half = ts // 2

    @pl.when(pl.program_id(1) == 0)
    def _():
        carry_ref[...] = jnp.zeros_like(carry_ref)
        hc_ref[...] = jnp.zeros_like(hc_ref)

    w = cw_ref[...]
    smats = _shift_matrices(half)
    cz_ref[8:16, :] = jnp.zeros((8, D_MODEL), f32)
    for hf in range(2):
        rows = slice(hf * half, (hf + 1) * half)
        xb = x_ref[rows, :]
        x = xb.astype(f32)
        y = w[3:4, :] * x + cb_ref[...]
        for j, sm in enumerate(smats):
            y = y + w[j:j + 1, :] * jnp.dot(sm, xb, preferred_element_type=f32)
        cz_ref[0:8, :] = carry_ref[...]
        corr = w[0:1, :] * cz_ref[5:13, :] + w[1:2, :] * cz_ref[6:14, :] + w[2:3, :] * cz_ref[7:15, :]
        carry_ref[...] = x[half - 8:half, :]
        xc_ref[hf * half:hf * half + 8, :] = y[0:8] + corr
        xc_ref[hf * half + 8:(hf + 1) * half, :] = y[8:half]

    nsp = _softplus(-lam_ref[...])
    for blk in range(LRU_BLOCKS):
        cs = slice(blk * LRU_BLOCK_DIM, (blk + 1) * LRU_BLOCK_DIM)
        xc = xc_ref[:, cs]
        ri = jnp.dot(xc.astype(bf16), wax_ref[blk], preferred_element_type=f32)
        r = _sigmoid(ri[:, :LRU_BLOCK_DIM] + ba_ref[:, cs])
        i = _sigmoid(ri[:, LRU_BLOCK_DIM:] + bx_ref[:, cs])
        a = jnp.exp(-LRU_C * r * nsp[:, cs])
        a_ref[:, cs] = a
        u_ref[:, cs] = jnp.sqrt(1.0 - a * a) * (i * xc)

    rowi = lax.broadcasted_iota(i32, (8, D_MODEL), 0)

    def group(gi, h):
        r0 = pl.multiple_of(gi * 8, 8)
        a = a_ref[pl.ds(r0, 8), :]
        b = u_ref[pl.ds(r0, 8), :]
        for d in (1, 2, 4):
            m = rowi >= d
            a_s = pltpu.roll(a, d, 0)
            b_s = pltpu.roll(b, d, 0)
            b = jnp.where(m, a * b_s + b, b)
            a = jnp.where(m, a * a_s, a)
        hh = a * h + b
        h_ref[pl.ds(r0, 8), :] = hh
        return hh[7:8, :]

    hc_ref[...] = lax.fori_loop(0, ts // 8, group, hc_ref[...])
    gt = gate_ref[...].astype(f32)
    o_ref[...] = (h_ref[...] * jax.nn.gelu(gt)).astype(o_ref.dtype)


def lru_branch(fat, conv_w, conv_b, wax, b_a, b_x, lam, B, S, ts=512):
    T = B * S
    ns = S // ts
    row = lambda b, s: b * ns + s
    full = lambda shp: pl.BlockSpec(shp, lambda b, s: (0,) * len(shp))
    return pl.pallas_call(
        functools.partial(_lru_kernel, ts=ts),
        grid=(B, ns),
        in_specs=[pl.BlockSpec((ts, 1024), lambda b, s: (row(b, s), C_LX // 1024)),
                  pl.BlockSpec((ts, 1024), lambda b, s: (row(b, s), C_LG // 1024)),
                  full((CONV_WIDTH, 1024)), full((1, 1024)), full((LRU_BLOCKS, LRU_BLOCK_DIM, 2 * LRU_BLOCK_DIM)),
                  full((1, 1024)), full((1, 1024)), full((1, 1024))],
        out_specs=pl.BlockSpec((ts, 1024), lambda b, s: (row(b, s), 0)),
        out_shape=jax.ShapeDtypeStruct((T, 1024), bf16),
        scratch_shapes=[pltpu.VMEM((16, 1024), f32), pltpu.VMEM((ts, 1024), f32),
                        pltpu.VMEM((ts, 1024), f32), pltpu.VMEM((ts, 1024), f32), pltpu.VMEM((ts, 1024), f32),
                        pltpu.VMEM((8, 1024), f32), pltpu.VMEM((1, 1024), f32)],
        compiler_params=_params(("arbitrary", "arbitrary")),
        name="lru",
    )(fat, fat, conv_w, conv_b, wax, b_a, b_x, lam)


def _swa_kernel(q_ref, kc_ref, kp_ref, vc_ref, vp_ref, biast_ref, sink_ref, o_ref, kb_ref, vt_ref, *, tq):
    W = WINDOW
    hd = SWA_HEAD_DIM
    kb_ref[0:W, :] = kp_ref[...]
    kb_ref[W:W + tq, :] = kc_ref[...]
    vt_ref[0] = vp_ref[...].astype(f32).T.astype(bf16)
    for j in range(tq // W):
        vt_ref[j + 1] = vc_ref[j * W:(j + 1) * W, :].astype(f32).T.astype(bf16)
    first_tile = pl.program_id(1) == 0
    ones_rows = jnp.ones((8, 2 * W), bf16)
    scale = hd ** -0.5

    def qblock(n, carry):
        r0 = pl.multiple_of(n * W, W)
        tab = jnp.where(jnp.logical_and(first_tile, n == 0), 1, 0)
        qs = q_ref[pl.ds(r0, W), :] * scale
        vt_band = jnp.concatenate([vt_ref[n], vt_ref[n + 1]], axis=1)
        outs = []
        for hk0 in range(0, SWA_KV_HEADS, SWA_KV_IN_FLIGHT):
            hks = range(hk0, hk0 + SWA_KV_IN_FLIGHT)
            heads = range(hk0 * SWA_GROUP, (hk0 + SWA_KV_IN_FLIGHT) * SWA_GROUP)
            kk = {hk: kb_ref[pl.ds(r0, 2 * W), hk * hd:(hk + 1) * hd] for hk in hks}
            lhs_v = {hk: jnp.concatenate([vt_band[hk * hd:(hk + 1) * hd, :], ones_rows], axis=0) for hk in hks}
            st = [lax.dot_general(kk[h // SWA_GROUP], qs[:, h * hd:(h + 1) * hd], (((1,), (1,)), ((), ())),
                                  preferred_element_type=f32) + biast_ref[tab, h] for h in heads]
            m = [jnp.maximum(jnp.max(t, axis=0, keepdims=True), sink_ref[h]) for t, h in zip(st, heads)]
            pt = [jnp.exp(t - mm).astype(bf16) for t, mm in zip(st, m)]
            ov = [jnp.dot(lhs_v[h // SWA_GROUP], t, preferred_element_type=f32) for t, h in zip(pt, heads)]
            outs += [o[:hd] / (o[hd:hd + 1] + jnp.exp(sink_ref[h] - mm)) for o, mm, h in zip(ov, m, heads)]
        o_ref[pl.ds(r0, W), :] = jnp.concatenate(outs, axis=0).T.astype(o_ref.dtype)
        return carry

    lax.fori_loop(0, tq // W, qblock, 0)


def swa_branch(fat, bias_tab_t, sinks, B, S, tq=512):
    T = B * S
    ns = S // tq
    nb = S // WINDOW
    per = tq // WINDOW
    row = lambda b, s: b * ns + s
    prev = lambda b, s: b * nb + jnp.maximum(s * per - 1, 0)
    kvw = SWA_KV_HEADS * SWA_HEAD_DIM
    full = lambda shp: pl.BlockSpec(shp, lambda b, s: (0,) * len(shp))
    return pl.pallas_call(
        functools.partial(_swa_kernel, tq=tq),
        grid=(B, ns),
        in_specs=[pl.BlockSpec((tq, 1024), lambda b, s: (row(b, s), C_SQ // 1024)),
                  pl.BlockSpec((tq, kvw), lambda b, s: (row(b, s), C_SK // kvw)),
                  pl.BlockSpec((WINDOW, kvw), lambda b, s: (prev(b, s), C_SK // kvw)),
                  pl.BlockSpec((tq, kvw), lambda b, s: (row(b, s), C_SV // kvw)),
                  pl.BlockSpec((WINDOW, kvw), lambda b, s: (prev(b, s), C_SV // kvw)),
                  full((2, SWA_Q_HEADS, 2 * WINDOW, WINDOW)), pl.BlockSpec(memory_space=pltpu.SMEM)],
        out_specs=pl.BlockSpec((tq, 1024), lambda b, s: (row(b, s), 0)),
        out_shape=jax.ShapeDtypeStruct((T, 1024), bf16),
        scratch_shapes=[pltpu.VMEM((tq + WINDOW, kvw), bf16), pltpu.VMEM((per + 1, kvw, WINDOW), bf16)],
        compiler_params=_params(("arbitrary", "arbitrary")),
        name="swa",
    )(fat, fat, fat, fat, fat, bias_tab_t, sinks)


def _load_row_tiles(ref, n, lead=()):
    return jnp.concatenate([ref[lead + (pl.ds(s, n, stride=ROW_TILE), slice(None))] for s in range(ROW_TILE)], axis=1)


def _store_row_tiles(ref, val):
    n = val.shape[0]
    for s in range(ROW_TILE):
        ref[pl.ds(s, n, stride=ROW_TILE), :] = val[:, s * LANES:(s + 1) * LANES]


def _merge_kernel(oa_ref, ob_ref, oc_ref, ga_ref, gb_ref, gc_ref, x_ref, wa_ref, wb_ref, wc_ref, wo_ref,
                  g_ref, b_ref, o_ref, ot_ref, *, alpha):
    ya = jnp.dot(oa_ref[...], wa_ref[...], preferred_element_type=f32)
    yb = jnp.dot(ob_ref[...], wb_ref[...], preferred_element_type=f32)
    yc = jnp.dot(oc_ref[...], wc_ref[...], preferred_element_type=f32)
    mix = (_sigmoid(ga_ref[...].astype(f32)) * ya + _sigmoid(gb_ref[...].astype(f32)) * yb
           + _sigmoid(gc_ref[...].astype(f32)) * yc)
    y = jnp.dot(mix.astype(bf16), wo_ref[...], preferred_element_type=f32)
    x1 = _layer_norm(alpha * x_ref[...] + y, g_ref[...], b_ref[...])
    o_ref[...] = x1
    _store_row_tiles(ot_ref, x1)


def merge_ln(oa, ob, oc, fat, x2d, wa, wb, wc, wo, g, b, alpha, tm=512):
    T = x2d.shape[0]
    act = pl.BlockSpec((tm, 1024), lambda i: (i, 0))
    fatb = lambda cb: pl.BlockSpec((tm, 1024), lambda i: (i, cb))
    wsp = pl.BlockSpec((1024, 1024), lambda i: (0, 0))
    vec = pl.BlockSpec((1, 1024), lambda i: (0, 0))
    return pl.pallas_call(
        functools.partial(_merge_kernel, alpha=alpha),
        grid=(T // tm,),
        in_specs=[act, act, act, fatb(C_MA // 1024), fatb(C_MB // 1024), fatb(C_MC // 1024), act,
                  wsp, wsp, wsp, wsp, vec, vec],
        out_specs=[act, pl.BlockSpec((tm * ROW_TILE, LANES), lambda i: (i, 0))],
        out_shape=[jax.ShapeDtypeStruct((T, 1024), f32), jax.ShapeDtypeStruct((T * ROW_TILE, LANES), f32)],
        compiler_params=_params(("arbitrary",)),
        name="merge_ln",
    )(oa, ob, oc, fat, fat, fat, x2d, wa, wb, wc, wo, g, b)


def _route_kernel(x_ref, rw_ref, rb_ref, gates_ref, eidx_ref, rank_ref, cnt_ref, run_ref, *, tm):
    @pl.when(pl.program_id(0) == 0)
    def _():
        run_ref[...] = jnp.zeros_like(run_ref)

    x = x_ref[...]
    xh = x.astype(bf16)
    xl = (x - xh.astype(f32)).astype(bf16)
    hw = jnp.dot(xh, rw_ref[...], preferred_element_type=f32)
    logits = (hw[:, :LANES] + (hw[:, LANES:] + jnp.dot(xl, rw_ref[:, :LANES], preferred_element_type=f32))
              + rb_ref[...])
    lane = lax.broadcasted_iota(i32, (tm, LANES), 1)
    lane_f = lane.astype(f32)
    work = logits
    vals, idxs, hots = [], [], []
    for _ in range(TOP_K):
        m = jnp.max(work, axis=-1, keepdims=True)
        idx = jnp.min(jnp.where(work == m, lane_f, float(LANES)), axis=-1, keepdims=True)
        hot = lane_f == idx
        vals.append(m)
        idxs.append(idx)
        hots.append(hot)
        work = jnp.where(hot, -jnp.inf, work)
    es = [jnp.exp(v - vals[0]) for v in vals]
    den = es[0] + es[1] + es[2] + es[3]
    sel = jnp.zeros((tm, LANES), f32)
    for hot in hots:
        sel = sel + jnp.where(hot, 1.0, 0.0)
    ri = lax.broadcasted_iota(i32, (tm, tm), 0)
    ci = lax.broadcasted_iota(i32, (tm, tm), 1)
    tril = jnp.where(ri > ci, 1.0, 0.0).astype(bf16)
    before = jnp.dot(tril, sel.astype(bf16), preferred_element_type=f32) + run_ref[...]
    run_ref[...] = run_ref[...] + jnp.sum(sel, axis=0, keepdims=True)
    cnt_ref[...] = run_ref[...]
    gates = jnp.zeros((tm, LANES), f32)
    eidx = jnp.zeros((tm, LANES), f32)
    rank = jnp.zeros((tm, LANES), f32)
    for k in range(TOP_K):
        rk = jnp.sum(jnp.where(hots[k], before, 0.0), axis=-1, keepdims=True)
        gates = jnp.where(lane == k, es[k] / den, gates)
        eidx = jnp.where(lane == k, idxs[k], eidx)
        rank = jnp.where(lane == k, rk, rank)
    gates_ref[...] = gates
    eidx_ref[...] = eidx.astype(i32)
    rank_ref[...] = rank.astype(i32)


def route(x2d, rw_pad, rb_pad, tm=512):
    T = x2d.shape[0]
    outb = pl.BlockSpec((tm, LANES), lambda i: (i, 0))
    return pl.pallas_call(
        functools.partial(_route_kernel, tm=tm),
        grid=(T // tm,),
        in_specs=[pl.BlockSpec((tm, 1024), lambda i: (i, 0)),
                  pl.BlockSpec((1024, 2 * LANES), lambda i: (0, 0)),
                  pl.BlockSpec((1, LANES), lambda i: (0, 0))],
        out_specs=[outb, outb, outb, pl.BlockSpec((1, LANES), lambda i: (0, 0))],
        out_shape=[jax.ShapeDtypeStruct((T, LANES), f32), jax.ShapeDtypeStruct((T, LANES), i32),
                   jax.ShapeDtypeStruct((T, LANES), i32), jax.ShapeDtypeStruct((1, LANES), f32)],
        scratch_shapes=[pltpu.VMEM((1, LANES), f32)],
        compiler_params=_params(("arbitrary",)),
        name="route",
    )(x2d, rw_pad, rb_pad)


GU_GROUP = 2 * LANES


def _gu_prep_tile(w_ref, o_ref):
    ri = lax.broadcasted_iota(i32, (GU_GROUP, GU_GROUP), 0)
    ci = lax.broadcasted_iota(i32, (GU_GROUP, GU_GROUP), 1)
    src = jnp.where(ci < LANES, 2 * ci, 2 * (ci - LANES) + 1)
    perm = jnp.where(ri == src, 1.0, 0.0).astype(bf16)
    for g in range(w_ref.shape[3] // GU_GROUP):
        cs = slice(g * GU_GROUP, (g + 1) * GU_GROUP)
        o_ref[0, :, cs] = jnp.dot(w_ref[0, 0, :, cs].astype(bf16), perm, preferred_element_type=f32).astype(bf16)


ROW_UNROLL = 8


def _row_copy(src_ref, src_row, dst_ref, dst_row, sem):
    tile = lambda r: pl.ds(pl.multiple_of(r * ROW_TILE, ROW_TILE), ROW_TILE)
    return pltpu.make_async_copy(src_ref.at[tile(src_row)], dst_ref.at[tile(dst_row)], sem)


def _dispatch_kernel(dest_ref, x_ref, buf_in_ref, w_ref, buf_ref, wo_ref, sem, *, tm, nt, ng):
    del buf_in_ref
    i = pl.program_id(0)

    @pl.when(i < nt)
    def _():
        def issue(g, carry):
            for j in range(ROW_UNROLL):
                r = g * ROW_UNROLL + j
                for k in range(TOP_K):
                    _row_copy(x_ref, r, buf_ref, dest_ref[r * TOP_K + k], sem).start(priority=k % 2)
            return carry

        lax.fori_loop(0, tm // ROW_UNROLL, issue, 0)

    @pl.when(i < ng)
    def _():
        _gu_prep_tile(w_ref, wo_ref)

    @pl.when(i < nt)
    def _():
        def drain(g, carry):
            for j in range(ROW_UNROLL * TOP_K):
                _row_copy(x_ref, 0, buf_ref, 0, sem).wait()
            return carry

        lax.fori_loop(0, tm // ROW_UNROLL, drain, 0)


def dispatch_prep(dest_flat, xt, buf0, w_gu_all, layer, tm=512, tk=512):
    T = xt.shape[0] // ROW_TILE
    _, E, D, N = w_gu_all.shape
    nt = T // tm
    kt = D // tk
    ng = E * kt
    tok = lambda i: jnp.minimum(i, nt - 1)
    gu = lambda i: jnp.minimum(i, ng - 1)
    return pl.pallas_call(
        functools.partial(_dispatch_kernel, tm=tm, nt=nt, ng=ng),
        grid=(max(nt, ng),),
        in_specs=[pl.BlockSpec((tm * TOP_K,), lambda i: (tok(i),), memory_space=pltpu.SMEM),
                  pl.BlockSpec((tm * ROW_TILE, LANES), lambda i: (tok(i), 0)),
                  pl.BlockSpec(memory_space=pl.ANY),
                  pl.BlockSpec((1, 1, tk, N), lambda i: (layer, gu(i) // kt, gu(i) % kt, 0))],
        out_specs=[pl.BlockSpec(memory_space=pl.ANY),
                   pl.BlockSpec((1, tk, N), lambda i: (gu(i) // kt, gu(i) % kt, 0))],
        out_shape=[jax.ShapeDtypeStruct(buf0.shape, buf0.dtype), jax.ShapeDtypeStruct((E, D, N), bf16)],
        scratch_shapes=[pltpu.SemaphoreType.DMA(())],
        input_output_aliases={2: 0},
        compiler_params=_params(("arbitrary",)),
        name="dispatch_prep",
    )(dest_flat, xt, buf0, w_gu_all)


def _expert_kernel(blk_e_ref, nused_ref, x_ref, wgu_ref, bgu_ref, wd_ref, bd_ref, o_ref, wdb_ref):
    i = pl.program_id(0)

    @pl.when(jnp.logical_or(i == 0, blk_e_ref[i] != blk_e_ref[jnp.maximum(i - 1, 0)]))
    def _():
        wdb_ref[...] = wd_ref[0].astype(bf16)

    @pl.when(i < nused_ref[0])
    def _():
        de = wd_ref.shape[1]
        xb = _load_row_tiles(x_ref, EXPERT_BLK).astype(bf16)
        hgu = jnp.dot(xb, wgu_ref[0], preferred_element_type=f32) + bgu_ref[0]
        acts = []
        for g in range(2 * de // GU_GROUP):
            gate = jnp.minimum(hgu[:, g * GU_GROUP:g * GU_GROUP + LANES], SWIGLU_LIMIT)
            lin = jnp.clip(hgu[:, g * GU_GROUP + LANES:(g + 1) * GU_GROUP], -SWIGLU_LIMIT, SWIGLU_LIMIT)
            acts.append((gate * _sigmoid(SWIGLU_ALPHA * gate) * (lin + 1.0)).astype(bf16))
        act = jnp.concatenate(acts, axis=1)
        _store_row_tiles(o_ref, jnp.dot(act, wdb_ref[...], preferred_element_type=f32) + bd_ref[0])

    @pl.when(i >= nused_ref[0])
    def _():
        o_ref[...] = jnp.zeros_like(o_ref)


def experts(blk_e, nused, buf, wgu, bgu, wd, bd):
    D = wd.shape[2]
    nblk = buf.shape[0] // (EXPERT_BLK * ROW_TILE)
    de = wd.shape[1]
    tile_blk = (EXPERT_BLK * ROW_TILE, LANES)
    live = lambda i, be, nu: jnp.minimum(i, nu[0] - 1)
    grid_spec = pltpu.PrefetchScalarGridSpec(
        num_scalar_prefetch=2,
        grid=(nblk,),
        in_specs=[pl.BlockSpec(tile_blk, lambda i, be, nu: (live(i, be, nu), 0)),
                  pl.BlockSpec((1, D, 2 * de), lambda i, be, nu: (be[i], 0, 0)),
                  pl.BlockSpec((1, 1, 2 * de), lambda i, be, nu: (be[i], 0, 0)),
                  pl.BlockSpec((1, de, D), lambda i, be, nu: (be[i], 0, 0)),
                  pl.BlockSpec((1, 1, D), lambda i, be, nu: (be[i], 0, 0))],
        out_specs=pl.BlockSpec(tile_blk, lambda i, be, nu: (i, 0)),
        scratch_shapes=[pltpu.VMEM((de, D), bf16)],
    )
    return pl.pallas_call(
        _expert_kernel,
        grid_spec=grid_spec,
        out_shape=jax.ShapeDtypeStruct(buf.shape, f32),
        compiler_params=_params(("arbitrary",)),
        name="experts",
    )(blk_e, nused, buf, wgu, bgu, wd, bd)


def _combine_kernel(dest_ref, dest_next_ref, gates_ref, x_ref, p_ref, obuf_ref, wg_ref, wp_ref,
                    g2_ref, b2_ref, g3_ref, b3_ref, o_ref, rows_ref, sems, *, tm, alpha):
    i = pl.program_id(0)
    n = pl.num_programs(0)
    slot = i % 2

    def gather(idx_ref, s):
        def issue(g, carry):
            for j in range(ROW_UNROLL):
                r = g * ROW_UNROLL + j
                for k in range(TOP_K):
                    _row_copy(obuf_ref, idx_ref[r * TOP_K + k], rows_ref.at[s, k], r, sems.at[s]).start(priority=k % 2)
            return carry

        lax.fori_loop(0, tm // ROW_UNROLL, issue, 0)

    @pl.when(i == 0)
    def _():
        gather(dest_ref, 0)

    @pl.when(i + 1 < n)
    def _():
        gather(dest_next_ref, 1 - slot)

    def drain(g, carry):
        for j in range(ROW_UNROLL * TOP_K):
            _row_copy(obuf_ref, 0, rows_ref.at[slot, 0], 0, sems.at[slot]).wait()
        return carry

    lax.fori_loop(0, tm // ROW_UNROLL, drain, 0)

    gates = gates_ref[...]
    y = gates[:, 0:1] * _load_row_tiles(rows_ref, tm, (slot, 0))
    for k in range(1, TOP_K):
        y = y + gates[:, k:k + 1] * _load_row_tiles(rows_ref, tm, (slot, k))
    x2 = _layer_norm(alpha * x_ref[...] + y, g2_ref[...], b2_ref[...])
    gate = _sigmoid(jnp.dot(x2.astype(bf16), wg_ref[...], preferred_element_type=f32))
    proj = jnp.dot(p_ref[...].astype(bf16), wp_ref[...], preferred_element_type=f32)
    o_ref[...] = _layer_norm(alpha * x2 + gate * proj, g3_ref[...], b3_ref[...])


def combine_ple(dest_flat, gates, x2d, p2d, layer, obuf, wg, wp, g2, b2, g3, b3, alpha, tm=256):
    T = x2d.shape[0]
    nt = T // tm
    act = pl.BlockSpec((tm, 1024), lambda i: (i, 0))
    vec = pl.BlockSpec((1, 1024), lambda i: (0, 0))
    return pl.pallas_call(
        functools.partial(_combine_kernel, tm=tm, alpha=alpha),
        grid=(nt,),
        in_specs=[pl.BlockSpec((tm * TOP_K,), lambda i: (i,), memory_space=pltpu.SMEM),
                  pl.BlockSpec((tm * TOP_K,), lambda i: (jnp.minimum(i + 1, nt - 1),), memory_space=pltpu.SMEM),
                  pl.BlockSpec((tm, LANES), lambda i: (i, 0)),
                  act,
                  pl.BlockSpec((tm, PLE_DIM), lambda i: (layer * nt + i, 0)),
                  pl.BlockSpec(memory_space=pl.ANY),
                  pl.BlockSpec((1024, 1024), lambda i: (0, 0)),
                  pl.BlockSpec((PLE_DIM, 1024), lambda i: (0, 0)),
                  vec, vec, vec, vec],
        out_specs=act,
        out_shape=jax.ShapeDtypeStruct((T, 1024), f32),
        scratch_shapes=[pltpu.VMEM((2, TOP_K, tm * ROW_TILE, LANES), f32), pltpu.SemaphoreType.DMA((2,))],
        compiler_params=_params(("arbitrary",)),
        name="combine_ple",
    )(dest_flat, dest_flat, gates, x2d, p2d, obuf, wg, wp, g2, b2, g3, b3)


def _t5_bucket_np(dist):
    max_exact = REL_BUCKETS // 2
    d = np.maximum(dist.astype(np.float32), np.float32(1.0))
    large = max_exact + (np.log(d / np.float32(max_exact)) / np.float32(math.log(REL_MAX_DISTANCE / max_exact))
                         * np.float32(REL_BUCKETS - max_exact)).astype(np.int32)
    large = np.minimum(large, REL_BUCKETS - 1)
    return np.where(dist < max_exact, dist, large)


def _swa_bias_table(rel_bias):
    dist = np.arange(2 * WINDOW - 1, -WINDOW, -1)
    in_window = (dist >= 0) & (dist < WINDOW)
    per_dist = rel_bias[_t5_bucket_np(np.maximum(dist, 0))].astype(f32).T
    per_dist = jnp.where(jnp.asarray(in_window)[None], per_dist, NEG_BIG)
    bias = jnp.stack([per_dist[:, WINDOW - 1 - q:3 * WINDOW - 1 - q] for q in range(WINDOW)], axis=1)
    bias = jnp.transpose(bias, (0, 2, 1))
    first = jnp.where((jnp.arange(2 * WINDOW) >= WINDOW)[None, :, None], bias, NEG_BIG)
    return jnp.stack([bias, first])


def _pad_row(v, width=LANES, fill=0.0):
    v = v.astype(f32).reshape(1, -1)
    return jnp.pad(v, ((0, 0), (0, width - v.shape[1])), constant_values=fill)


def _wcat(w_in):
    cols = [w_in[:, 0:4096], w_in[:, 4112:6160], w_in[:, 6160:7184], w_in[:, 7696:10768],
            w_in[:, 7184:7440], w_in[:, 7440:7696], w_in[:, 4096:4112],
            jnp.zeros((w_in.shape[0], FAT_W - 10768), w_in.dtype)]
    return jnp.concatenate(cols, axis=1).astype(bf16)


def kernel(x, p, w_in, conv_qkv_w, gdn_a_log, gdn_dt_bias, gdn_norm_w, rg_conv_w, rg_conv_b, rg_w_a, rg_b_a, rg_w_x, rg_b_x, rg_lambda, attn_sinks, rel_bias, w_o_gdn, w_o_lru, w_o_swa, w_out, ln1_g, ln1_b, router_w, router_b, w_gu, b_gu, w_down, b_down, ln2_g, ln2_b, ple_w_gate, ple_w_proj, ln3_g, ln3_b):
    B, S, D = x.shape
    depth = w_in.shape[0]
    T = B * S
    A = T * TOP_K
    alpha = (2.0 * depth) ** 0.25
    P = A + N_EXPERTS * EXPERT_BLK
    nblk = P // EXPERT_BLK
    row = lambda v: v.astype(f32).reshape(1, -1)

    bias_tab = _swa_bias_table(rel_bias)
    p2d = p.reshape(depth * T, PLE_DIM)
    xc = x.reshape(T, D)
    for i in range(depth):
        fat = inproj(xc, _wcat(w_in[i]))
        o_gdn, buf0 = gdn_branch(fat, conv_qkv_w[i], _pad_row(gdn_a_log[i]), _pad_row(gdn_dt_bias[i]),
                                 row(gdn_norm_w[i]), B, S, P * ROW_TILE)
        wax = jnp.concatenate([rg_w_a[i], rg_w_x[i]], axis=-1).astype(bf16)
        o_lru = lru_branch(fat, rg_conv_w[i], row(rg_conv_b[i]), wax, row(rg_b_a[i]), row(rg_b_x[i]),
                           row(rg_lambda[i]), B, S)
        o_swa = swa_branch(fat, bias_tab, attn_sinks[i].astype(f32), B, S)
        x1, x1t = merge_ln(o_gdn, o_lru, o_swa, fat, xc, w_o_gdn[i].astype(bf16), w_o_lru[i].astype(bf16),
                           w_o_swa[i].astype(bf16), w_out[i].astype(bf16), row(ln1_g[i]), row(ln1_b[i]), alpha)

        rw_pad = jnp.pad(router_w[i].astype(f32), ((0, 0), (0, LANES - N_EXPERTS)))
        rw_hi = rw_pad.astype(bf16)
        rw_pad = jnp.concatenate([rw_hi, (rw_pad - rw_hi.astype(f32)).astype(bf16)], axis=1)
        rb_pad = _pad_row(router_b[i], fill=NEG_BIG)
        gates, eidx, rank, cnt = route(x1, rw_pad, rb_pad)
        counts = cnt[0, :N_EXPERTS].astype(i32)
        padded = ((counts + EXPERT_BLK - 1) // EXPERT_BLK) * EXPERT_BLK
        pad_ends = jnp.cumsum(padded)
        pad_starts = pad_ends - padded
        hit = eidx[:, :TOP_K, None] == jnp.arange(N_EXPERTS, dtype=i32)
        dest = (jnp.sum(jnp.where(hit, pad_starts, 0), axis=-1) + rank[:, :TOP_K]).reshape(A)
        blk_start = jnp.arange(nblk, dtype=i32) * EXPERT_BLK
        blk_e = jnp.minimum(jnp.sum((pad_ends[None, :] <= blk_start[:, None]).astype(i32), axis=1),
                            N_EXPERTS - 1).astype(i32)
        nused = (pad_ends[-1:] // EXPERT_BLK).astype(i32)

        buf, wgu = dispatch_prep(dest, x1t, buf0, w_gu, i)
        bgu = jnp.transpose(b_gu[i].reshape(N_EXPERTS, -1, LANES, 2), (0, 1, 3, 2)).reshape(N_EXPERTS, 1, -1)
        obuf = experts(blk_e, nused, buf, wgu, bgu, w_down[i], b_down[i][:, None, :])
        xc = combine_ple(dest, gates, x1, p2d, i, obuf, ple_w_gate[i].astype(bf16),
                         ple_w_proj[i].astype(bf16), row(ln2_g[i]), row(ln2_b[i]), row(ln3_g[i]), row(ln3_b[i]),
                         alpha)
    return xc.reshape(B, S, D)
```

```python
import functools
import math

import numpy as np
import jax
import jax.numpy as jnp
from jax import lax
from jax.experimental import pallas as pl
from jax.experimental.pallas import tpu as pltpu

f32 = jnp.float32
bf16 = jnp.bfloat16
i32 = jnp.int32

D_MODEL = 1024
PLE_DIM = 256
GDN_HEADS = 8
GDN_HEAD_DIM = 128
GDN_CHUNK = 64
CONV_WIDTH = 4
LRU_BLOCKS = 8
LRU_BLOCK_DIM = 128
LRU_C = 8.0
SWA_Q_HEADS = 16
SWA_KV_HEADS = 4
SWA_HEAD_DIM = 64
SWA_GROUP = 4
WINDOW = 128
REL_BUCKETS = 32
REL_MAX_DISTANCE = 128
N_EXPERTS = 32
TOP_K = 4
SWIGLU_LIMIT = 7.0
SWIGLU_ALPHA = 1.702
LN_EPS = 1e-5
NORM_EPS = 1e-6
NEG_BIG = -1e30

LANES = 128
VMEM_LIMIT = 56 * 1024 * 1024

C_GQ, C_GK, C_GV, C_GZ = 0, 1024, 2048, 3072
C_LX, C_LG = 4096, 5120
C_SQ = 6144
C_MA, C_MB, C_MC = 7168, 8192, 9216
C_SK, C_SV = 10240, 10496
C_AB = 10752
FAT_W = 10880
FAT_TN = 2176

ROW_TILE = D_MODEL // LANES

EXPERT_BLK = 512
CHUNKS_IN_FLIGHT = 4
SWA_KV_IN_FLIGHT = 4


def _params(sem):
    return pltpu.CompilerParams(dimension_semantics=sem, vmem_limit_bytes=VMEM_LIMIT)


def _sigmoid(x):
    return 1.0 / (1.0 + jnp.exp(-x))


def _softplus(x):
    return jnp.maximum(x, 0.0) + jnp.log(1.0 + jnp.exp(-jnp.abs(x)))


def _layer_norm(z, g, b):
    mu = jnp.mean(z, axis=-1, keepdims=True)
    zc = z - mu
    var = jnp.mean(zc * zc, axis=-1, keepdims=True)
    return zc * lax.rsqrt(var + LN_EPS) * g + b


def _inproj_kernel(x_ref, w_ref, o_ref, xb_ref):
    @pl.when(pl.program_id(1) == 0)
    def _():
        xb_ref[...] = x_ref[...].astype(bf16)

    o_ref[...] = jnp.dot(xb_ref[...], w_ref[...], preferred_element_type=f32).astype(o_ref.dtype)


def inproj(x2d, wcat, tm=2048):
    T, K = x2d.shape
    N = wcat.shape[1]
    tn = FAT_TN
    return pl.pallas_call(
        _inproj_kernel,
        grid=(T // tm, N // tn),
        in_specs=[pl.BlockSpec((tm, K), lambda i, j: (i, 0)),
                  pl.BlockSpec((K, tn), lambda i, j: (0, j))],
        out_specs=pl.BlockSpec((tm, tn), lambda i, j: (i, j)),
        out_shape=jax.ShapeDtypeStruct((T, N), bf16),
        scratch_shapes=[pltpu.VMEM((tm, K), bf16)],
        compiler_params=_params(("arbitrary", "arbitrary")),
        name="inproj",
    )(x2d, wcat)


ZERO_FILL_COPIES = 4


def _zero_fill(step, zero_ref, zbuf_ref, zsem, zero_rows):
    zr = zero_ref.shape[0]
    return [pltpu.make_async_copy(zero_ref, zbuf_ref.at[pl.ds(pl.multiple_of(step * zero_rows + c * zr, zr), zr)], zsem)
            for c in range(zero_rows // zr)]


def _shift_matrices(ts):
    ri = lax.broadcasted_iota(i32, (ts, ts), 0)
    ci = lax.broadcasted_iota(i32, (ts, ts), 1)
    return [jnp.where(ri - ci == d, 1.0, 0.0).astype(bf16) for d in (3, 2, 1)]


def _causal_conv_silu(src_ref, dst_ref, cz_ref, carry_ref, w, smats, ts, head_scale=None):
    xb = src_ref[...]
    x = xb.astype(f32)
    y = w[3:4, :] * x
    for j, sm in enumerate(smats):
        y = y + w[j:j + 1, :] * jnp.dot(sm, xb, preferred_element_type=f32)
    cz_ref[0:8, :] = carry_ref[...]
    cz_ref[8:16, :] = jnp.zeros((8, x.shape[1]), f32)
    corr = w[0:1, :] * cz_ref[5:13, :] + w[1:2, :] * cz_ref[6:14, :] + w[2:3, :] * cz_ref[7:15, :]
    carry_ref[...] = x[ts - 8:ts, :]

    def post(rows, yv):
        a = yv * _sigmoid(yv)
        if head_scale is None:
            dst_ref[rows, :] = a
            return
        for h in range(a.shape[1] // LANES):
            cs = slice(h * LANES, (h + 1) * LANES)
            ah = a[:, cs]
            dst_ref[rows, cs] = ah * (lax.rsqrt(jnp.sum(ah * ah, axis=-1, keepdims=True) + NORM_EPS) * head_scale)

    post(slice(0, 8), y[0:8] + corr)
    post(slice(8, ts), y[8:ts])


def _gdn_kernel(q_ref, k_ref, v_ref, z_ref, ab_ref, cw_ref, alog_ref, dtb_ref, nw_ref, o_ref, zbuf_ref,
                xpad_ref, qs_ref, ks_ref, vs_ref, carry_ref, state_ref, g_ref, beta_ref,
                u_s, lhs_s, intra_s, kdt_s, zero_ref, zsem, *, ts, zero_rows):
    C = GDN_CHUNK
    D = GDN_HEAD_DIM
    P = 2 * C

    step = pl.program_id(0) * pl.num_programs(1) + pl.program_id(1)

    @pl.when(step == 0)
    def _():
        zero_ref[...] = jnp.zeros_like(zero_ref)

    fills = _zero_fill(step, zero_ref, zbuf_ref, zsem, zero_rows)
    for f in fills:
        f.start()

    @pl.when(pl.program_id(1) == 0)
    def _():
        carry_ref[...] = jnp.zeros_like(carry_ref)
        state_ref[...] = jnp.zeros_like(state_ref)

    smats = _shift_matrices(ts)
    for p, (src, dst, scale) in enumerate(((q_ref, qs_ref, D ** -0.5), (k_ref, ks_ref, 1.0), (v_ref, vs_ref, None))):
        _causal_conv_silu(src, dst, xpad_ref, carry_ref.at[p], cw_ref[:, p * 1024:(p + 1) * 1024], smats, ts, scale)

    ab = ab_ref[...].astype(f32)
    g = -jnp.exp(alog_ref[...]) * _softplus(ab + dtb_ref[...])
    rin = lax.broadcasted_iota(i32, (ts, LANES), 0) & (C - 1)
    gsum = g
    for d in (1, 2, 4, 8, 16, 32):
        gsum = gsum + jnp.where(rin >= d, pltpu.roll(gsum, d, 0), 0.0)
    g_ref[...] = gsum
    beta_ref[...] = _sigmoid(ab)

    ri = lax.broadcasted_iota(i32, (P, P), 0)
    ci = lax.broadcasted_iota(i32, (P, P), 1)
    same = (ri >= C) == (ci >= C)
    eye = ri == ci
    causal = same & (ri >= ci)
    strict = same & (ri > ci)
    eye_f = jnp.where(eye, 1.0, 0.0).astype(f32)
    first_cols = ci < C
    nw = nw_ref[...]

    def stack(a, b):
        return jnp.concatenate([a, b], axis=0)

    def mm(a, b):
        return jnp.dot(a, b, preferred_element_type=f32)

    npair = GDN_HEADS // 2
    nchunk = ts // C
    hcols = [slice(h * D, (h + 1) * D) for h in range(GDN_HEADS)]

    for cg in range(0, nchunk, CHUNKS_IN_FLIGHT):
        probs = [(c, hp) for c in range(cg, cg + CHUNKS_IN_FLIGHT) for hp in range(npair)]
        qn, kn, vb, gcol, eg, egl, kb = [], [], [], [], [], [], []
        for c, hp in probs:
            rows = slice(c * C, (c + 1) * C)
            c0, c1 = hcols[2 * hp], hcols[2 * hp + 1]
            gc = g_ref[rows, :]
            bc = beta_ref[rows, :]
            qn.append(stack(qs_ref[rows, c0], qs_ref[rows, c1]))
            kn.append(stack(ks_ref[rows, c0], ks_ref[rows, c1]))
            v2 = stack(vs_ref[rows, c0], vs_ref[rows, c1])
            h0, h1 = 2 * hp, 2 * hp + 1
            gcl = stack(gc[:, h0:h0 + 1], gc[:, h1:h1 + 1])
            bcl = stack(bc[:, 8 + h0:9 + h0], bc[:, 8 + h1:9 + h1])
            glast = stack(jnp.broadcast_to(gc[C - 1:C, h0:h0 + 1], (C, 1)),
                          jnp.broadcast_to(gc[C - 1:C, h1:h1 + 1], (C, 1)))
            gcol.append(gcl)
            eg.append(jnp.exp(gcl))
            egl.append(jnp.exp(glast - gcl))
            kb.append(kn[-1] * bcl)
            vb.append(v2 * bcl)
        n = len(probs)
        a2 = [lax.dot_general(stack(kb[i], qn[i]).astype(bf16), kn[i].astype(bf16), (((1,), (1,)), ((), ())),
                              preferred_element_type=f32) for i in range(n)]
        lmat, intra = [], []
        for i in range(n):
            gm = jnp.broadcast_to(gcol[i], (P, P))
            grow = jnp.sum(jnp.where(eye, gm, 0.0), axis=0, keepdims=True)
            decay = jnp.where(causal, jnp.exp(jnp.minimum(gm - grow, 0.0)), 0.0)
            lmat.append(jnp.where(strict, a2[i][:P] * decay, 0.0))
            intra.append(a2[i][P:] * decay)
        lb = [l.astype(bf16) for l in lmat]
        xm = [eye_f - l for l in lmat]
        pm = [mm(b, b) for b in lb]
        for it in range(5):
            pb = [p.astype(bf16) for p in pm]
            xm = [x + mm(x.astype(bf16), b) for x, b in zip(xm, pb)]
            if it < 4:
                pm = [mm(b, b) for b in pb]
        uw = [mm(xm[i].astype(bf16), jnp.concatenate([vb[i], kb[i] * eg[i]], axis=1).astype(bf16))
              for i in range(n)]
        for i, (c, hp) in enumerate(probs):
            j = c * npair + hp
            w2 = uw[i][:, D:]
            qd = qn[i] * eg[i]
            kdt = (kn[i] * egl[i]).T
            u_s[j] = uw[i][:, :D]
            lhs_s[j, 0] = stack(w2[:C], qd[:C]).astype(bf16)
            lhs_s[j, 1] = stack(w2[C:], qd[C:]).astype(bf16)
            intra_s[j] = intra[i].astype(bf16)
            kdt_s[j, 0] = jnp.where(first_cols, kdt, 0.0).astype(bf16)
            kdt_s[j, 1] = jnp.where(first_cols, 0.0, kdt).astype(bf16)

    for c in range(nchunk):
        rows = slice(c * C, (c + 1) * C)
        gl = g_ref[(c + 1) * C - 1:(c + 1) * C, :]
        st = [state_ref[h] for h in range(GDN_HEADS)]
        wq = [mm(lhs_s[c * npair + h // 2, h % 2], st[h].astype(bf16)) for h in range(GDN_HEADS)]
        vnb = [(u_s[c * npair + hp] - stack(wq[2 * hp][:C], wq[2 * hp + 1][:C])).astype(bf16)
               for hp in range(npair)]
        o2 = [stack(wq[2 * hp][C:], wq[2 * hp + 1][C:]) + mm(intra_s[c * npair + hp], vnb[hp])
              for hp in range(npair)]
        for h in range(GDN_HEADS):
            state_ref[h] = st[h] * jnp.exp(gl[:, h:h + 1]) + mm(kdt_s[c * npair + h // 2, h % 2], vnb[h // 2])
        for hp in range(npair):
            c0, c1 = hcols[2 * hp], hcols[2 * hp + 1]
            z2 = stack(z_ref[rows, c0], z_ref[rows, c1]).astype(f32)
            on = (o2[hp] * lax.rsqrt(jnp.mean(o2[hp] * o2[hp], axis=-1, keepdims=True) + NORM_EPS) * nw
                  * (z2 * _sigmoid(z2))).astype(o_ref.dtype)
            o_ref[rows, c0] = on[:C]
            o_ref[rows, c1] = on[C:]
    for f in fills:
        f.wait()


def gdn_branch(fat, conv_w, a_log_row, dt_bias_row, norm_w_row, B, S, buf_rows, ts=256):
    T = B * S
    ns = S // ts
    nprob = (ts // GDN_CHUNK) * (GDN_HEADS // 2)
    zero_rows = buf_rows // (B * ns)
    assert zero_rows * B * ns == buf_rows and zero_rows % (ZERO_FILL_COPIES * ROW_TILE) == 0
    row = lambda b, s: b * ns + s
    blk = lambda cb: pl.BlockSpec((ts, 1024), lambda b, s: (row(b, s), cb))
    full = lambda shp: pl.BlockSpec(shp, lambda b, s: (0,) * len(shp))
    return pl.pallas_call(
        functools.partial(_gdn_kernel, ts=ts, zero_rows=zero_rows),
        grid=(B, ns),
        in_specs=[blk(C_GQ // 1024), blk(C_GK // 1024), blk(C_GV // 1024), blk(C_GZ // 1024),
                  pl.BlockSpec((ts, LANES), lambda b, s: (row(b, s), C_AB // LANES)),
                  full((CONV_WIDTH, 3072)), full((1, LANES)), full((1, LANES)), full((1, LANES))],
        out_specs=[pl.BlockSpec((ts, 1024), lambda b, s: (row(b, s), 0)), pl.BlockSpec(memory_space=pl.ANY)],
        out_shape=[jax.ShapeDtypeStruct((T, 1024), bf16), jax.ShapeDtypeStruct((buf_rows, LANES), f32)],
        scratch_shapes=[pltpu.VMEM((16, 1024), f32),
                        pltpu.VMEM((ts, 1024), f32), pltpu.VMEM((ts, 1024), f32), pltpu.VMEM((ts, 1024), f32),
                        pltpu.VMEM((3, 8, 1024), f32),
                        pltpu.VMEM((GDN_HEADS, GDN_HEAD_DIM, GDN_HEAD_DIM), f32),
                        pltpu.VMEM((ts, LANES), f32), pltpu.VMEM((ts, LANES), f32),
                        pltpu.VMEM((nprob, 128, GDN_HEAD_DIM), f32),
                        pltpu.VMEM((nprob, 2, 128, GDN_HEAD_DIM), bf16),
                        pltpu.VMEM((nprob, 128, 128), bf16),
                        pltpu.VMEM((nprob, 2, GDN_HEAD_DIM, 128), bf16),
                        pltpu.VMEM((zero_rows // ZERO_FILL_COPIES, LANES), f32), pltpu.SemaphoreType.DMA(())],
        compiler_params=_params(("arbitrary", "arbitrary")),
        name="gdn",
    )(fat, fat, fat, fat, fat, conv_w, a_log_row, dt_bias_row, norm_w_row)


def _lru_kernel(x_ref, gate_ref, cw_ref, cb_ref, wax_ref, ba_ref, bx_ref, lam_ref, o_ref,
                cz_ref, xc_ref, a_ref, u_ref, h_ref, carry_ref, hc_ref, *, ts):
    half = ts // 2

    @pl.when(pl.program_id(1) == 0)
    def _():
        carry_ref[...] = jnp.zeros_like(carry_ref)
        hc_ref[...] = jnp.zeros_like(hc_ref)

    w = cw_ref[...]
    smats = _shift_matrices(half)
    cz_ref[8:16, :] = jnp.zeros((8, D_MODEL), f32)
    for hf in range(2):
        rows = slice(hf * half, (hf + 1) * half)
        xb = x_ref[rows, :]
        x = xb.astype(f32)
        y = w[3:4, :] * x + cb_ref[...]
        for j, sm in enumerate(smats):
            y = y + w[j:j + 1, :] * jnp.dot(sm, xb, preferred_element_type=f32)
        cz_ref[0:8, :] = carry_ref[...]
        corr = w[0:1, :] * cz_ref[5:13, :] + w[1:2, :] * cz_ref[6:14, :] + w[2:3, :] * cz_ref[7:15, :]
        carry_ref[...] = x[half - 8:half, :]
        xc_ref[hf * half:hf * half + 8, :] = y[0:8] + corr
        xc_ref[hf * half + 8:(hf + 1) * half, :] = y[8:half]

    nsp = _softplus(-lam_ref[...])
    for blk in range(LRU_BLOCKS):
        cs = slice(blk * LRU_BLOCK_DIM, (blk + 1) * LRU_BLOCK_DIM)
        xc = xc_ref[:, cs]
        ri = jnp.dot(xc.astype(bf16), wax_ref[blk], preferred_element_type=f32)
        r = _sigmoid(ri[:, :LRU_BLOCK_DIM] + ba_ref[:, cs])
        i = _sigmoid(ri[:, LRU_BLOCK_DIM:] + bx_ref[:, cs])
        a = jnp.exp(-LRU_C * r * nsp[:, cs])
        a_ref[:, cs] = a
        u_ref[:, cs] = jnp.sqrt(1.0 - a * a) * (i * xc)

    rowi = lax.broadcasted_iota(i32, (8, D_MODEL), 0)

    def group(gi, h):
        r0 = pl.multiple_of(gi * 8, 8)
        a = a_ref[pl.ds(r0, 8), :]
        b = u_ref[pl.ds(r0, 8), :]
        for d in (1, 2, 4):
            m = rowi >= d
            a_s = pltpu.roll(a, d, 0)
            b_s = pltpu.roll(b, d, 0)
            b = jnp.where(m, a * b_s + b, b)
            a = jnp.where(m, a * a_s, a)
        hh = a * h + b
        h_ref[pl.ds(r0, 8), :] = hh
        return hh[7:8, :]

    hc_ref[...] = lax.fori_loop(0, ts // 8, group, hc_ref[...])
    gt = gate_ref[...].astype(f32)
    o_ref[...] = (h_ref[...] * jax.nn.gelu(gt)).astype(o_ref.dtype)


def lru_branch(fat, conv_w, conv_b, wax, b_a, b_x, lam, B, S, ts=512):
    T = B * S
    ns = S // ts
    row = lambda b, s: b * ns + s
    full = lambda shp: pl.BlockSpec(shp, lambda b, s: (0,) * len(shp))
    return pl.pallas_call(
        functools.partial(_lru_kernel, ts=ts),
        grid=(B, ns),
        in_specs=[pl.BlockSpec((ts, 1024), lambda b, s: (row(b, s), C_LX // 1024)),
                  pl.BlockSpec((ts, 1024), lambda b, s: (row(b, s), C_LG // 1024)),
                  full((CONV_WIDTH, 1024)), full((1, 1024)), full((LRU_BLOCKS, LRU_BLOCK_DIM, 2 * LRU_BLOCK_DIM)),
                  full((1, 1024)), full((1, 1024)), full((1, 1024))],
        out_specs=pl.BlockSpec((ts, 1024), lambda b, s: (row(b, s), 0)),
        out_shape=jax.ShapeDtypeStruct((T, 1024), bf16),
        scratch_shapes=[pltpu.VMEM((16, 1024), f32), pltpu.VMEM((ts, 1024), f32),
                        pltpu.VMEM((ts, 1024), f32), pltpu.VMEM((ts, 1024), f32), pltpu.VMEM((ts, 1024), f32),
                        pltpu.VMEM((8, 1024), f32), pltpu.VMEM((1, 1024), f32)],
        compiler_params=_params(("arbitrary", "arbitrary")),
        name="lru",
    )(fat, fat, conv_w, conv_b, wax, b_a, b_x, lam)


def _swa_kernel(q_ref, kc_ref, kp_ref, vc_ref, vp_ref, biast_ref, sink_ref, o_ref, kb_ref, vt_ref, *, tq):
    W = WINDOW
    hd = SWA_HEAD_DIM
    kb_ref[0:W, :] = kp_ref[...]
    kb_ref[W:W + tq, :] = kc_ref[...]
    vt_ref[0] = vp_ref[...].astype(f32).T.astype(bf16)
    for j in range(tq // W):
        vt_ref[j + 1] = vc_ref[j * W:(j + 1) * W, :].astype(f32).T.astype(bf16)
    first_tile = pl.program_id(1) == 0
    ones_rows = jnp.ones((8, 2 * W), bf16)
    scale = hd ** -0.5

    def qblock(n, carry):
        r0 = pl.multiple_of(n * W, W)
        tab = jnp.where(jnp.logical_and(first_tile, n == 0), 1, 0)
        qs = q_ref[pl.ds(r0, W), :] * scale
        vt_band = jnp.concatenate([vt_ref[n], vt_ref[n + 1]], axis=1)
        outs = []
        for hk0 in range(0, SWA_KV_HEADS, SWA_KV_IN_FLIGHT):
            hks = range(hk0, hk0 + SWA_KV_IN_FLIGHT)
            heads = range(hk0 * SWA_GROUP, (hk0 + SWA_KV_IN_FLIGHT) * SWA_GROUP)
            kk = {hk: kb_ref[pl.ds(r0, 2 * W), hk * hd:(hk + 1) * hd] for hk in hks}
            lhs_v = {hk: jnp.concatenate([vt_band[hk * hd:(hk + 1) * hd, :], ones_rows], axis=0) for hk in hks}
            st = [lax.dot_general(kk[h // SWA_GROUP], qs[:, h * hd:(h + 1) * hd], (((1,), (1,)), ((), ())),
                                  preferred_element_type=f32) + biast_ref[tab, h] for h in heads]
            m = [jnp.maximum(jnp.max(t, axis=0, keepdims=True), sink_ref[h]) for t, h in zip(st, heads)]
            pt = [jnp.exp(t - mm).astype(bf16) for t, mm in zip(st, m)]
            ov = [jnp.dot(lhs_v[h // SWA_GROUP], t, preferred_element_type=f32) for t, h in zip(pt, heads)]
            outs += [o[:hd] / (o[hd:hd + 1] + jnp.exp(sink_ref[h] - mm)) for o, mm, h in zip(ov, m, heads)]
        o_ref[pl.ds(r0, W), :] = jnp.concatenate(outs, axis=0).T.astype(o_ref.dtype)
        return carry

    lax.fori_loop(0, tq // W, qblock, 0)


def swa_branch(fat, bias_tab_t, sinks, B, S, tq=512):
    T = B * S
    ns = S // tq
    nb = S // WINDOW
    per = tq // WINDOW
    row = lambda b, s: b * ns + s
    prev = lambda b, s: b * nb + jnp.maximum(s * per - 1, 0)
    kvw = SWA_KV_HEADS * SWA_HEAD_DIM
    full = lambda shp: pl.BlockSpec(shp, lambda b, s: (0,) * len(shp))
    return pl.pallas_call(
        functools.partial(_swa_kernel, tq=tq),
        grid=(B, ns),
        in_specs=[pl.BlockSpec((tq, 1024), lambda b, s: (row(b, s), C_SQ // 1024)),
                  pl.BlockSpec((tq, kvw), lambda b, s: (row(b, s), C_SK // kvw)),
                  pl.BlockSpec((WINDOW, kvw), lambda b, s: (prev(b, s), C_SK // kvw)),
                  pl.BlockSpec((tq, kvw), lambda b, s: (row(b, s), C_SV // kvw)),
                  pl.BlockSpec((WINDOW, kvw), lambda b, s: (prev(b, s), C_SV // kvw)),
                  full((2, SWA_Q_HEADS, 2 * WINDOW, WINDOW)), pl.BlockSpec(memory_space=pltpu.SMEM)],
        out_specs=pl.BlockSpec((tq, 1024), lambda b, s: (row(b, s), 0)),
        out_shape=jax.ShapeDtypeStruct((T, 1024), bf16),
        scratch_shapes=[pltpu.VMEM((tq + WINDOW, kvw), bf16), pltpu.VMEM((per + 1, kvw, WINDOW), bf16)],
        compiler_params=_params(("arbitrary", "arbitrary")),
        name="swa",
    )(fat, fat, fat, fat, fat, bias_tab_t, sinks)


def _load_row_tiles(ref, n, lead=()):
    return jnp.concatenate([ref[lead + (pl.ds(s, n, stride=ROW_TILE), slice(None))] for s in range(ROW_TILE)], axis=1)


def _store_row_tiles(ref, val):
    n = val.shape[0]
    for s in range(ROW_TILE):
        ref[pl.ds(s, n, stride=ROW_TILE), :] = val[:, s * LANES:(s + 1) * LANES]


def _merge_kernel(oa_ref, ob_ref, oc_ref, ga_ref, gb_ref, gc_ref, x_ref, wa_ref, wb_ref, wc_ref, wo_ref,
                  g_ref, b_ref, o_ref, ot_ref, *, alpha):
    ya = jnp.dot(oa_ref[...], wa_ref[...], preferred_element_type=f32)
    yb = jnp.dot(ob_ref[...], wb_ref[...], preferred_element_type=f32)
    yc = jnp.dot(oc_ref[...], wc_ref[...], preferred_element_type=f32)
    mix = (_sigmoid(ga_ref[...].astype(f32)) * ya + _sigmoid(gb_ref[...].astype(f32)) * yb
           + _sigmoid(gc_ref[...].astype(f32)) * yc)
    y = jnp.dot(mix.astype(bf16), wo_ref[...], preferred_element_type=f32)
    x1 = _layer_norm(alpha * x_ref[...] + y, g_ref[...], b_ref[...])
    o_ref[...] = x1
    _store_row_tiles(ot_ref, x1)


def merge_ln(oa, ob, oc, fat, x2d, wa, wb, wc, wo, g, b, alpha, tm=512):
    T = x2d.shape[0]
    act = pl.BlockSpec((tm, 1024), lambda i: (i, 0))
    fatb = lambda cb: pl.BlockSpec((tm, 1024), lambda i: (i, cb))
    wsp = pl.BlockSpec((1024, 1024), lambda i: (0, 0))
    vec = pl.BlockSpec((1, 1024), lambda i: (0, 0))
    return pl.pallas_call(
        functools.partial(_merge_kernel, alpha=alpha),
        grid=(T // tm,),
        in_specs=[act, act, act, fatb(C_MA // 1024), fatb(C_MB // 1024), fatb(C_MC // 1024), act,
                  wsp, wsp, wsp, wsp, vec, vec],
        out_specs=[act, pl.BlockSpec((tm * ROW_TILE, LANES), lambda i: (i, 0))],
        out_shape=[jax.ShapeDtypeStruct((T, 1024), f32), jax.ShapeDtypeStruct((T * ROW_TILE, LANES), f32)],
        compiler_params=_params(("arbitrary",)),
        name="merge_ln",
    )(oa, ob, oc, fat, fat, fat, x2d, wa, wb, wc, wo, g, b)


def _route_kernel(x_ref, rw_ref, rb_ref, gates_ref, eidx_ref, rank_ref, cnt_ref, run_ref, *, tm):
    @pl.when(pl.program_id(0) == 0)
    def _():
        run_ref[...] = jnp.zeros_like(run_ref)

    x = x_ref[...]
    xh = x.astype(bf16)
    xl = (x - xh.astype(f32)).astype(bf16)
    hw = jnp.dot(xh, rw_ref[...], preferred_element_type=f32)
    logits = (hw[:, :LANES] + (hw[:, LANES:] + jnp.dot(xl, rw_ref[:, :LANES], preferred_element_type=f32))
              + rb_ref[...])
    lane = lax.broadcasted_iota(i32, (tm, LANES), 1)
    lane_f = lane.astype(f32)
    work = logits
    vals, idxs, hots = [], [], []
    for _ in range(TOP_K):
        m = jnp.max(work, axis=-1, keepdims=True)
        idx = jnp.min(jnp.where(work == m, lane_f, float(LANES)), axis=-1, keepdims=True)
        hot = lane_f == idx
        vals.append(m)
        idxs.append(idx)
        hots.append(hot)
        work = jnp.where(hot, -jnp.inf, work)
    es = [jnp.exp(v - vals[0]) for v in vals]
    den = es[0] + es[1] + es[2] + es[3]
    sel = jnp.zeros((tm, LANES), f32)
    for hot in hots:
        sel = sel + jnp.where(hot, 1.0, 0.0)
    ri = lax.broadcasted_iota(i32, (tm, tm), 0)
    ci = lax.broadcasted_iota(i32, (tm, tm), 1)
    tril = jnp.where(ri > ci, 1.0, 0.0).astype(bf16)
    before = jnp.dot(tril, sel.astype(bf16), preferred_element_type=f32) + run_ref[...]
    run_ref[...] = run_ref[...] + jnp.sum(sel, axis=0, keepdims=True)
    cnt_ref[...] = run_ref[...]
    gates = jnp.zeros((tm, LANES), f32)
    eidx = jnp.zeros((tm, LANES), f32)
    rank = jnp.zeros((tm, LANES), f32)
    for k in range(TOP_K):
        rk = jnp.sum(jnp.where(hots[k], before, 0.0), axis=-1, keepdims=True)
        gates = jnp.where(lane == k, es[k] / den, gates)
        eidx = jnp.where(lane == k, idxs[k], eidx)
        rank = jnp.where(lane == k, rk, rank)
    gates_ref[...] = gates
    eidx_ref[...] = eidx.astype(i32)
    rank_ref[...] = rank.astype(i32)


def route(x2d, rw_pad, rb_pad, tm=512):
    T = x2d.shape[0]
    outb = pl.BlockSpec((tm, LANES), lambda i: (i, 0))
    return pl.pallas_call(
        functools.partial(_route_kernel, tm=tm),
        grid=(T // tm,),
        in_specs=[pl.BlockSpec((tm, 1024), lambda i: (i, 0)),
                  pl.BlockSpec((1024, 2 * LANES), lambda i: (0, 0)),
                  pl.BlockSpec((1, LANES), lambda i: (0, 0))],
        out_specs=[outb, outb, outb, pl.BlockSpec((1, LANES), lambda i: (0, 0))],
        out_shape=[jax.ShapeDtypeStruct((T, LANES), f32), jax.ShapeDtypeStruct((T, LANES), i32),
                   jax.ShapeDtypeStruct((T, LANES), i32), jax.ShapeDtypeStruct((1, LANES), f32)],
        scratch_shapes=[pltpu.VMEM((1, LANES), f32)],
        compiler_params=_params(("arbitrary",)),
        name="route",
    )(x2d, rw_pad, rb_pad)


GU_GROUP = 2 * LANES


def _gu_prep_tile(w_ref, o_ref):
    ri = lax.broadcasted_iota(i32, (GU_GROUP, GU_GROUP), 0)
    ci = lax.broadcasted_iota(i32, (GU_GROUP, GU_GROUP), 1)
    src = jnp.where(ci < LANES, 2 * ci, 2 * (ci - LANES) + 1)
    perm = jnp.where(ri == src, 1.0, 0.0).astype(bf16)
    for g in range(w_ref.shape[3] // GU_GROUP):
        cs = slice(g * GU_GROUP, (g + 1) * GU_GROUP)
        o_ref[0, :, cs] = jnp.dot(w_ref[0, 0, :, cs].astype(bf16), perm, preferred_element_type=f32).astype(bf16)


ROW_UNROLL = 8


def _row_copy(src_ref, src_row, dst_ref, dst_row, sem):
    tile = lambda r: pl.ds(pl.multiple_of(r * ROW_TILE, ROW_TILE), ROW_TILE)
    return pltpu.make_async_copy(src_ref.at[tile(src_row)], dst_ref.at[tile(dst_row)], sem)


def _dispatch_kernel(dest_ref, x_ref, buf_in_ref, w_ref, buf_ref, wo_ref, sem, *, tm, nt, ng):
    del buf_in_ref
    i = pl.program_id(0)

    @pl.when(i < nt)
    def _():
        def issue(g, carry):
            for j in range(ROW_UNROLL):
                r = g * ROW_UNROLL + j
                for k in range(TOP_K):
                    _row_copy(x_ref, r, buf_ref, dest_ref[r * TOP_K + k], sem).start(priority=k % 2)
            return carry

        lax.fori_loop(0, tm // ROW_UNROLL, issue, 0)

    @pl.when(i < ng)
    def _():
        _gu_prep_tile(w_ref, wo_ref)

    @pl.when(i < nt)
    def _():
        def drain(g, carry):
            for j in range(ROW_UNROLL * TOP_K):
                _row_copy(x_ref, 0, buf_ref, 0, sem).wait()
            return carry

        lax.fori_loop(0, tm // ROW_UNROLL, drain, 0)


def dispatch_prep(dest_flat, xt, buf0, w_gu_all, layer, tm=512, tk=512):
    T = xt.shape[0] // ROW_TILE
    _, E, D, N = w_gu_all.shape
    nt = T // tm
    kt = D // tk
    ng = E * kt
    tok = lambda i: jnp.minimum(i, nt - 1)
    gu = lambda i: jnp.minimum(i, ng - 1)
    return pl.pallas_call(
        functools.partial(_dispatch_kernel, tm=tm, nt=nt, ng=ng),
        grid=(max(nt, ng),),
        in_specs=[pl.BlockSpec((tm * TOP_K,), lambda i: (tok(i),), memory_space=pltpu.SMEM),
                  pl.BlockSpec((tm * ROW_TILE, LANES), lambda i: (tok(i), 0)),
                  pl.BlockSpec(memory_space=pl.ANY),
                  pl.BlockSpec((1, 1, tk, N), lambda i: (layer, gu(i) // kt, gu(i) % kt, 0))],
        out_specs=[pl.BlockSpec(memory_space=pl.ANY),
                   pl.BlockSpec((1, tk, N), lambda i: (gu(i) // kt, gu(i) % kt, 0))],
        out_shape=[jax.ShapeDtypeStruct(buf0.shape, buf0.dtype), jax.ShapeDtypeStruct((E, D, N), bf16)],
        scratch_shapes=[pltpu.SemaphoreType.DMA(())],
        input_output_aliases={2: 0},
        compiler_params=_params(("arbitrary",)),
        name="dispatch_prep",
    )(dest_flat, xt, buf0, w_gu_all)


def _expert_kernel(blk_e_ref, nused_ref, x_ref, wgu_ref, bgu_ref, wd_ref, bd_ref, o_ref, wdb_ref):
    i = pl.program_id(0)

    @pl.when(jnp.logical_or(i == 0, blk_e_ref[i] != blk_e_ref[jnp.maximum(i - 1, 0)]))
    def _():
        wdb_ref[...] = wd_ref[0, 0].astype(bf16)

    @pl.when(i < nused_ref[0])
    def _():
        de = wdb_ref.shape[0]
        xb = _load_row_tiles(x_ref, EXPERT_BLK).astype(bf16)
        hgu = jnp.dot(xb, wgu_ref[0], preferred_element_type=f32) + bgu_ref[0]
        acts = []
        for g in range(2 * de // GU_GROUP):
            gate = jnp.minimum(hgu[:, g * GU_GROUP:g * GU_GROUP + LANES], SWIGLU_LIMIT)
            lin = jnp.clip(hgu[:, g * GU_GROUP + LANES:(g + 1) * GU_GROUP], -SWIGLU_LIMIT, SWIGLU_LIMIT)
            acts.append((gate * _sigmoid(SWIGLU_ALPHA * gate) * (lin + 1.0)).astype(bf16))
        act = jnp.concatenate(acts, axis=1)
        _store_row_tiles(o_ref, jnp.dot(act, wdb_ref[...], preferred_element_type=f32) + bd_ref[0])

    @pl.when(i >= nused_ref[0])
    def _():
        o_ref[...] = jnp.zeros_like(o_ref)


def experts(blk_e, nused, buf, wgu, bgu, wd_all, layer, bd):
    D = wd_all.shape[3]
    nblk = buf.shape[0] // (EXPERT_BLK * ROW_TILE)
    de = wd_all.shape[2]
    tile_blk = (EXPERT_BLK * ROW_TILE, LANES)
    live = lambda i, be, nu: jnp.minimum(i, nu[0] - 1)
    grid_spec = pltpu.PrefetchScalarGridSpec(
        num_scalar_prefetch=2,
        grid=(nblk,),
        in_specs=[pl.BlockSpec(tile_blk, lambda i, be, nu: (live(i, be, nu), 0)),
                  pl.BlockSpec((1, D, 2 * de), lambda i, be, nu: (be[i], 0, 0)),
                  pl.BlockSpec((1, 1, 2 * de), lambda i, be, nu: (be[i], 0, 0)),
                  pl.BlockSpec((1, 1, de, D), lambda i, be, nu: (layer, be[i], 0, 0)),
                  pl.BlockSpec((1, 1, D), lambda i, be, nu: (be[i], 0, 0))],
        out_specs=pl.BlockSpec(tile_blk, lambda i, be, nu: (i, 0)),
        scratch_shapes=[pltpu.VMEM((de, D), bf16)],
    )
    return pl.pallas_call(
        _expert_kernel,
        grid_spec=grid_spec,
        out_shape=jax.ShapeDtypeStruct(buf.shape, f32),
        compiler_params=_params(("arbitrary",)),
        name="experts",
    )(blk_e, nused, buf, wgu, bgu, wd_all, bd)


def _combine_kernel(dest_ref, dest_next_ref, gates_ref, x_ref, p_ref, obuf_ref, wg_ref, wp_ref,
                    g2_ref, b2_ref, g3_ref, b3_ref, o_ref, rows_ref, sems, *, tm, alpha):
    i = pl.program_id(0)
    n = pl.num_programs(0)
    slot = i % 2

    def gather(idx_ref, s):
        def issue(g, carry):
            for j in range(ROW_UNROLL):
                r = g * ROW_UNROLL + j
                for k in range(TOP_K):
                    _row_copy(obuf_ref, idx_ref[r * TOP_K + k], rows_ref.at[s, k], r, sems.at[s]).start(priority=k % 2)
            return carry

        lax.fori_loop(0, tm // ROW_UNROLL, issue, 0)

    @pl.when(i == 0)
    def _():
        gather(dest_ref, 0)

    @pl.when(i + 1 < n)
    def _():
        gather(dest_next_ref, 1 - slot)

    def drain(g, carry):
        for j in range(ROW_UNROLL * TOP_K):
            _row_copy(obuf_ref, 0, rows_ref.at[slot, 0], 0, sems.at[slot]).wait()
        return carry

    lax.fori_loop(0, tm // ROW_UNROLL, drain, 0)

    gates = gates_ref[...]
    y = gates[:, 0:1] * _load_row_tiles(rows_ref, tm, (slot, 0))
    for k in range(1, TOP_K):
        y = y + gates[:, k:k + 1] * _load_row_tiles(rows_ref, tm, (slot, k))
    x2 = _layer_norm(alpha * x_ref[...] + y, g2_ref[...], b2_ref[...])
    gate = _sigmoid(jnp.dot(x2.astype(bf16), wg_ref[...], preferred_element_type=f32))
    proj = jnp.dot(p_ref[...].astype(bf16), wp_ref[...], preferred_element_type=f32)
    o_ref[...] = _layer_norm(alpha * x2 + gate * proj, g3_ref[...], b3_ref[...])


def combine_ple(dest_flat, gates, x2d, p2d, layer, obuf, wg, wp, g2, b2, g3, b3, alpha, tm=256):
    T = x2d.shape[0]
    nt = T // tm
    act = pl.BlockSpec((tm, 1024), lambda i: (i, 0))
    vec = pl.BlockSpec((1, 1024), lambda i: (0, 0))
    return pl.pallas_call(
        functools.partial(_combine_kernel, tm=tm, alpha=alpha),
        grid=(nt,),
        in_specs=[pl.BlockSpec((tm * TOP_K,), lambda i: (i,), memory_space=pltpu.SMEM),
                  pl.BlockSpec((tm * TOP_K,), lambda i: (jnp.minimum(i + 1, nt - 1),), memory_space=pltpu.SMEM),
                  pl.BlockSpec((tm, LANES), lambda i: (i, 0)),
                  act,
                  pl.BlockSpec((tm, PLE_DIM), lambda i: (layer * nt + i, 0)),
                  pl.BlockSpec(memory_space=pl.ANY),
                  pl.BlockSpec((1024, 1024), lambda i: (0, 0)),
                  pl.BlockSpec((PLE_DIM, 1024), lambda i: (0, 0)),
                  vec, vec, vec, vec],
        out_specs=act,
        out_shape=jax.ShapeDtypeStruct((T, 1024), f32),
        scratch_shapes=[pltpu.VMEM((2, TOP_K, tm * ROW_TILE, LANES), f32), pltpu.SemaphoreType.DMA((2,))],
        compiler_params=_params(("arbitrary",)),
        name="combine_ple",
    )(dest_flat, dest_flat, gates, x2d, p2d, obuf, wg, wp, g2, b2, g3, b3)


def _t5_bucket_np(dist):
    max_exact = REL_BUCKETS // 2
    d = np.maximum(dist.astype(np.float32), np.float32(1.0))
    large = max_exact + (np.log(d / np.float32(max_exact)) / np.float32(math.log(REL_MAX_DISTANCE / max_exact))
                         * np.float32(REL_BUCKETS - max_exact)).astype(np.int32)
    large = np.minimum(large, REL_BUCKETS - 1)
    return np.where(dist < max_exact, dist, large)


def _swa_bias_table(rel_bias):
    dist = np.arange(2 * WINDOW - 1, -WINDOW, -1)
    in_window = (dist >= 0) & (dist < WINDOW)
    per_dist = rel_bias[_t5_bucket_np(np.maximum(dist, 0))].astype(f32).T
    per_dist = jnp.where(jnp.asarray(in_window)[None], per_dist, NEG_BIG)
    bias = jnp.stack([per_dist[:, WINDOW - 1 - q:3 * WINDOW - 1 - q] for q in range(WINDOW)], axis=1)
    bias = jnp.transpose(bias, (0, 2, 1))
    first = jnp.where((jnp.arange(2 * WINDOW) >= WINDOW)[None, :, None], bias, NEG_BIG)
    return jnp.stack([bias, first])


def _pad_row(v, width=LANES, fill=0.0):
    v = v.astype(f32).reshape(1, -1)
    return jnp.pad(v, ((0, 0), (0, width - v.shape[1])), constant_values=fill)


def _wcat(w_in):
    cols = [w_in[:, 0:4096], w_in[:, 4112:6160], w_in[:, 6160:7184], w_in[:, 7696:10768],
            w_in[:, 7184:7440], w_in[:, 7440:7696], w_in[:, 4096:4112],
            jnp.zeros((w_in.shape[0], FAT_W - 10768), w_in.dtype)]
    return jnp.concatenate(cols, axis=1).astype(bf16)


def kernel(x, p, w_in, conv_qkv_w, gdn_a_log, gdn_dt_bias, gdn_norm_w, rg_conv_w, rg_conv_b, rg_w_a, rg_b_a, rg_w_x, rg_b_x, rg_lambda, attn_sinks, rel_bias, w_o_gdn, w_o_lru, w_o_swa, w_out, ln1_g, ln1_b, router_w, router_b, w_gu, b_gu, w_down, b_down, ln2_g, ln2_b, ple_w_gate, ple_w_proj, ln3_g, ln3_b):
    B, S, D = x.shape
    depth = w_in.shape[0]
    T = B * S
    A = T * TOP_K
    alpha = (2.0 * depth) ** 0.25
    P = A + N_EXPERTS * EXPERT_BLK
    nblk = P // EXPERT_BLK
    row = lambda v: v.astype(f32).reshape(1, -1)

    bias_tab = _swa_bias_table(rel_bias)
    p2d = p.reshape(depth * T, PLE_DIM)
    xc = x.reshape(T, D)
    for i in range(depth):
        fat = inproj(xc, _wcat(w_in[i]))
        o_gdn, buf0 = gdn_branch(fat, conv_qkv_w[i], _pad_row(gdn_a_log[i]), _pad_row(gdn_dt_bias[i]),
                                 row(gdn_norm_w[i]), B, S, P * ROW_TILE)
        wax = jnp.concatenate([rg_w_a[i], rg_w_x[i]], axis=-1).astype(bf16)
        o_lru = lru_branch(fat, rg_conv_w[i], row(rg_conv_b[i]), wax, row(rg_b_a[i]), row(rg_b_x[i]),
                           row(rg_lambda[i]), B, S)
        o_swa = swa_branch(fat, bias_tab, attn_sinks[i].astype(f32), B, S)
        x1, x1t = merge_ln(o_gdn, o_lru, o_swa, fat, xc, w_o_gdn[i].astype(bf16), w_o_lru[i].astype(bf16),
                           w_o_swa[i].astype(bf16), w_out[i].astype(bf16), row(ln1_g[i]), row(ln1_b[i]), alpha)

        rw_pad = jnp.pad(router_w[i].astype(f32), ((0, 0), (0, LANES - N_EXPERTS)))
        rw_hi = rw_pad.astype(bf16)
        rw_pad = jnp.concatenate([rw_hi, (rw_pad - rw_hi.astype(f32)).astype(bf16)], axis=1)
        rb_pad = _pad_row(router_b[i], fill=NEG_BIG)
        gates, eidx, rank, cnt = route(x1, rw_pad, rb_pad)
        counts = cnt[0, :N_EXPERTS].astype(i32)
        padded = ((counts + EXPERT_BLK - 1) // EXPERT_BLK) * EXPERT_BLK
        pad_ends = jnp.cumsum(padded)
        pad_starts = pad_ends - padded
        hit = eidx[:, :TOP_K, None] == jnp.arange(N_EXPERTS, dtype=i32)
        dest = (jnp.sum(jnp.where(hit, pad_starts, 0), axis=-1) + rank[:, :TOP_K]).reshape(A)
        blk_start = jnp.arange(nblk, dtype=i32) * EXPERT_BLK
        blk_e = jnp.minimum(jnp.sum((pad_ends[None, :] <= blk_start[:, None]).astype(i32), axis=1),
                            N_EXPERTS - 1).astype(i32)
        nused = (pad_ends[-1:] // EXPERT_BLK).astype(i32)

        buf, wgu = dispatch_prep(dest, x1t, buf0, w_gu, i)
        bgu = jnp.transpose(b_gu[i].reshape(N_EXPERTS, -1, LANES, 2), (0, 1, 3, 2)).reshape(N_EXPERTS, 1, -1)
        obuf = experts(blk_e, nused, buf, wgu, bgu, w_down, i, b_down[i][:, None, :])
        xc = combine_ple(dest, gates, x1, p2d, i, obuf, ple_w_gate[i].astype(bf16),
                         ple_w_proj[i].astype(bf16), row(ln2_g[i]), row(ln2_b[i]), row(ln3_g[i]), row(ln3_b[i]),
                         alpha)
    return xc.reshape(B, S, D)
```

```python
import functools
import math

import numpy as np
import jax
import jax.numpy as jnp
from jax import lax
from jax.experimental import pallas as pl
from jax.experimental.pallas import tpu as pltpu

f32 = jnp.float32
bf16 = jnp.bfloat16
i32 = jnp.int32

D_MODEL = 1024
PLE_DIM = 256
GDN_HEADS = 8
GDN_HEAD_DIM = 128
GDN_CHUNK = 64
CONV_WIDTH = 4
LRU_BLOCKS = 8
LRU_BLOCK_DIM = 128
LRU_C = 8.0
SWA_Q_HEADS = 16
SWA_KV_HEADS = 4
SWA_HEAD_DIM = 64
SWA_GROUP = 4
WINDOW = 128
REL_BUCKETS = 32
REL_MAX_DISTANCE = 128
N_EXPERTS = 32
TOP_K = 4
SWIGLU_LIMIT = 7.0
SWIGLU_ALPHA = 1.702
LN_EPS = 1e-5
NORM_EPS = 1e-6
NEG_BIG = -1e30

LANES = 128
VMEM_LIMIT = 56 * 1024 * 1024

C_GQ, C_GK, C_GV, C_GZ = 0, 1024, 2048, 3072
C_LX, C_LG = 4096, 5120
C_SQ = 6144
C_MA, C_MB, C_MC = 7168, 8192, 9216
C_SK, C_SV = 10240, 10496
C_AB = 10752
FAT_W = 10880
FAT_TN = 2176

ROW_TILE = D_MODEL // LANES

EXPERT_BLK = 512
CHUNKS_IN_FLIGHT = 4
SWA_KV_IN_FLIGHT = 4


def _params(sem):
    return pltpu.CompilerParams(dimension_semantics=sem, vmem_limit_bytes=VMEM_LIMIT)


def _sigmoid(x):
    return 1.0 / (1.0 + jnp.exp(-x))


def _softplus(x):
    return jnp.maximum(x, 0.0) + jnp.log(1.0 + jnp.exp(-jnp.abs(x)))


def _layer_norm(z, g, b):
    mu = jnp.mean(z, axis=-1, keepdims=True)
    zc = z - mu
    var = jnp.mean(zc * zc, axis=-1, keepdims=True)
    return zc * lax.rsqrt(var + LN_EPS) * g + b


def _inproj_kernel(x_ref, w_ref, o_ref, xb_ref):
    @pl.when(pl.program_id(1) == 0)
    def _():
        xb_ref[...] = x_ref[...].astype(bf16)

    o_ref[...] = jnp.dot(xb_ref[...], w_ref[...], preferred_element_type=f32).astype(o_ref.dtype)


def inproj(x2d, wcat, tm=2048):
    T, K = x2d.shape
    N = wcat.shape[1]
    tn = FAT_TN
    return pl.pallas_call(
        _inproj_kernel,
        grid=(T // tm, N // tn),
        in_specs=[pl.BlockSpec((tm, K), lambda i, j: (i, 0)),
                  pl.BlockSpec((K, tn), lambda i, j: (0, j))],
        out_specs=pl.BlockSpec((tm, tn), lambda i, j: (i, j)),
        out_shape=jax.ShapeDtypeStruct((T, N), bf16),
        scratch_shapes=[pltpu.VMEM((tm, K), bf16)],
        compiler_params=_params(("arbitrary", "arbitrary")),
        name="inproj",
    )(x2d, wcat)


ZERO_FILL_COPIES = 4


def _zero_fill(step, zero_ref, zbuf_ref, zsem, zero_rows):
    zr = zero_ref.shape[0]
    return [pltpu.make_async_copy(zero_ref, zbuf_ref.at[pl.ds(pl.multiple_of(step * zero_rows + c * zr, zr), zr)], zsem)
            for c in range(zero_rows // zr)]


def _shift_matrices(ts):
    ri = lax.broadcasted_iota(i32, (ts, ts), 0)
    ci = lax.broadcasted_iota(i32, (ts, ts), 1)
    return [jnp.where(ri - ci == d, 1.0, 0.0).astype(bf16) for d in (3, 2, 1)]


def _causal_conv_silu(src_ref, dst_ref, cz_ref, carry_ref, w, smats, ts, head_scale=None):
    xb = src_ref[...]
    x = xb.astype(f32)
    y = w[3:4, :] * x
    for j, sm in enumerate(smats):
        y = y + w[j:j + 1, :] * jnp.dot(sm, xb, preferred_element_type=f32)
    cz_ref[0:8, :] = carry_ref[...]
    cz_ref[8:16, :] = jnp.zeros((8, x.shape[1]), f32)
    corr = w[0:1, :] * cz_ref[5:13, :] + w[1:2, :] * cz_ref[6:14, :] + w[2:3, :] * cz_ref[7:15, :]
    carry_ref[...] = x[ts - 8:ts, :]

    def post(rows, yv):
        a = yv * _sigmoid(yv)
        if head_scale is None:
            dst_ref[rows, :] = a
            return
        for h in range(a.shape[1] // LANES):
            cs = slice(h * LANES, (h + 1) * LANES)
            ah = a[:, cs]
            dst_ref[rows, cs] = ah * (lax.rsqrt(jnp.sum(ah * ah, axis=-1, keepdims=True) + NORM_EPS) * head_scale)

    post(slice(0, 8), y[0:8] + corr)
    post(slice(8, ts), y[8:ts])


def _gdn_kernel(q_ref, k_ref, v_ref, z_ref, ab_ref, cw_ref, alog_ref, dtb_ref, nw_ref, o_ref, zbuf_ref,
                xpad_ref, qs_ref, ks_ref, vs_ref, carry_ref, state_ref, g_ref, beta_ref,
                u_s, lhs_s, intra_s, kdt_s, zero_ref, zsem, *, ts, zero_rows):
    C = GDN_CHUNK
    D = GDN_HEAD_DIM
    P = 2 * C

    step = pl.program_id(0) * pl.num_programs(1) + pl.program_id(1)

    @pl.when(step == 0)
    def _():
        zero_ref[...] = jnp.zeros_like(zero_ref)

    fills = _zero_fill(step, zero_ref, zbuf_ref, zsem, zero_rows)
    for f in fills:
        f.start()

    @pl.when(pl.program_id(1) == 0)
    def _():
        carry_ref[...] = jnp.zeros_like(carry_ref)
        state_ref[...] = jnp.zeros_like(state_ref)

    smats = _shift_matrices(ts)
    for p, (src, dst, scale) in enumerate(((q_ref, qs_ref, D ** -0.5), (k_ref, ks_ref, 1.0), (v_ref, vs_ref, None))):
        _causal_conv_silu(src, dst, xpad_ref, carry_ref.at[p], cw_ref[:, p * 1024:(p + 1) * 1024], smats, ts, scale)

    ab = ab_ref[...].astype(f32)
    g = -jnp.exp(alog_ref[...]) * _softplus(ab + dtb_ref[...])
    rin = lax.broadcasted_iota(i32, (ts, LANES), 0) & (C - 1)
    gsum = g
    for d in (1, 2, 4, 8, 16, 32):
        gsum = gsum + jnp.where(rin >= d, pltpu.roll(gsum, d, 0), 0.0)
    g_ref[...] = gsum
    beta_ref[...] = _sigmoid(ab)

    ri = lax.broadcasted_iota(i32, (P, P), 0)
    ci = lax.broadcasted_iota(i32, (P, P), 1)
    same = (ri >= C) == (ci >= C)
    eye = ri == ci
    causal = same & (ri >= ci)
    strict = same & (ri > ci)
    eye_f = jnp.where(eye, 1.0, 0.0).astype(f32)
    first_cols = ci < C
    nw = nw_ref[...]

    def stack(a, b):
        return jnp.concatenate([a, b], axis=0)

    def mm(a, b):
        return jnp.dot(a, b, preferred_element_type=f32)

    npair = GDN_HEADS // 2
    nchunk = ts // C
    hcols = [slice(h * D, (h + 1) * D) for h in range(GDN_HEADS)]

    for cg in range(0, nchunk, CHUNKS_IN_FLIGHT):
        probs = [(c, hp) for c in range(cg, cg + CHUNKS_IN_FLIGHT) for hp in range(npair)]
        qn, kn, vb, gcol, eg, egl, kb = [], [], [], [], [], [], []
        for c, hp in probs:
            rows = slice(c * C, (c + 1) * C)
            c0, c1 = hcols[2 * hp], hcols[2 * hp + 1]
            gc = g_ref[rows, :]
            bc = beta_ref[rows, :]
            qn.append(stack(qs_ref[rows, c0], qs_ref[rows, c1]))
            kn.append(stack(ks_ref[rows, c0], ks_ref[rows, c1]))
            v2 = stack(vs_ref[rows, c0], vs_ref[rows, c1])
            h0, h1 = 2 * hp, 2 * hp + 1
            gcl = stack(gc[:, h0:h0 + 1], gc[:, h1:h1 + 1])
            bcl = stack(bc[:, 8 + h0:9 + h0], bc[:, 8 + h1:9 + h1])
            glast = stack(jnp.broadcast_to(gc[C - 1:C, h0:h0 + 1], (C, 1)),
                          jnp.broadcast_to(gc[C - 1:C, h1:h1 + 1], (C, 1)))
            gcol.append(gcl)
            eg.append(jnp.exp(gcl))
            egl.append(jnp.exp(glast - gcl))
            kb.append(kn[-1] * bcl)
            vb.append(v2 * bcl)
        n = len(probs)
        a2 = [lax.dot_general(stack(kb[i], qn[i]).astype(bf16), kn[i].astype(bf16), (((1,), (1,)), ((), ())),
                              preferred_element_type=f32) for i in range(n)]
        lmat, intra = [], []
        for i in range(n):
            gm = jnp.broadcast_to(gcol[i], (P, P))
            grow = jnp.sum(jnp.where(eye, gm, 0.0), axis=0, keepdims=True)
            decay = jnp.where(causal, jnp.exp(jnp.minimum(gm - grow, 0.0)), 0.0)
            lmat.append(jnp.where(strict, a2[i][:P] * decay, 0.0))
            intra.append(a2[i][P:] * decay)
        lb = [l.astype(bf16) for l in lmat]
        xm = [eye_f - l for l in lmat]
        pm = [mm(b, b) for b in lb]
        for it in range(5):
            pb = [p.astype(bf16) for p in pm]
            xm = [x + mm(x.astype(bf16), b) for x, b in zip(xm, pb)]
            if it < 4:
                pm = [mm(b, b) for b in pb]
        uw = [mm(xm[i].astype(bf16), jnp.concatenate([vb[i], kb[i] * eg[i]], axis=1).astype(bf16))
              for i in range(n)]
        for i, (c, hp) in enumerate(probs):
            j = c * npair + hp
            w2 = uw[i][:, D:]
            qd = qn[i] * eg[i]
            kdt = (kn[i] * egl[i]).T
            u_s[j] = uw[i][:, :D]
            lhs_s[j, 0] = stack(w2[:C], qd[:C]).astype(bf16)
            lhs_s[j, 1] = stack(w2[C:], qd[C:]).astype(bf16)
            intra_s[j] = intra[i].astype(bf16)
            kdt_s[j, 0] = jnp.where(first_cols, kdt, 0.0).astype(bf16)
            kdt_s[j, 1] = jnp.where(first_cols, 0.0, kdt).astype(bf16)

    for c in range(nchunk):
        rows = slice(c * C, (c + 1) * C)
        gl = g_ref[(c + 1) * C - 1:(c + 1) * C, :]
        st = [state_ref[h] for h in range(GDN_HEADS)]
        wq = [mm(lhs_s[c * npair + h // 2, h % 2], st[h].astype(bf16)) for h in range(GDN_HEADS)]
        vnb = [(u_s[c * npair + hp] - stack(wq[2 * hp][:C], wq[2 * hp + 1][:C])).astype(bf16)
               for hp in range(npair)]
        o2 = [stack(wq[2 * hp][C:], wq[2 * hp + 1][C:]) + mm(intra_s[c * npair + hp], vnb[hp])
              for hp in range(npair)]
        for h in range(GDN_HEADS):
            state_ref[h] = st[h] * jnp.exp(gl[:, h:h + 1]) + mm(kdt_s[c * npair + h // 2, h % 2], vnb[h // 2])
        for hp in range(npair):
            c0, c1 = hcols[2 * hp], hcols[2 * hp + 1]
            z2 = stack(z_ref[rows, c0], z_ref[rows, c1]).astype(f32)
            on = (o2[hp] * lax.rsqrt(jnp.mean(o2[hp] * o2[hp], axis=-1, keepdims=True) + NORM_EPS) * nw
                  * (z2 * _sigmoid(z2))).astype(o_ref.dtype)
            o_ref[rows, c0] = on[:C]
            o_ref[rows, c1] = on[C:]
    for f in fills:
        f.wait()


def gdn_branch(fat, conv_w, a_log_row, dt_bias_row, norm_w_row, B, S, buf_rows, ts=256):
    T = B * S
    ns = S // ts
    nprob = (ts // GDN_CHUNK) * (GDN_HEADS // 2)
    zero_rows = buf_rows // (B * ns)
    assert zero_rows * B * ns == buf_rows and zero_rows % (ZERO_FILL_COPIES * ROW_TILE) == 0
    row = lambda b, s: b * ns + s
    blk = lambda cb: pl.BlockSpec((ts, 1024), lambda b, s: (row(b, s), cb))
    full = lambda shp: pl.BlockSpec(shp, lambda b, s: (0,) * len(shp))
    return pl.pallas_call(
        functools.partial(_gdn_kernel, ts=ts, zero_rows=zero_rows),
        grid=(B, ns),
        in_specs=[blk(C_GQ // 1024), blk(C_GK // 1024), blk(C_GV // 1024), blk(C_GZ // 1024),
                  pl.BlockSpec((ts, LANES), lambda b, s: (row(b, s), C_AB // LANES)),
                  full((CONV_WIDTH, 3072)), full((1, LANES)), full((1, LANES)), full((1, LANES))],
        out_specs=[pl.BlockSpec((ts, 1024), lambda b, s: (row(b, s), 0)), pl.BlockSpec(memory_space=pl.ANY)],
        out_shape=[jax.ShapeDtypeStruct((T, 1024), bf16), jax.ShapeDtypeStruct((buf_rows, LANES), f32)],
        scratch_shapes=[pltpu.VMEM((16, 1024), f32),
                        pltpu.VMEM((ts, 1024), f32), pltpu.VMEM((ts, 1024), f32), pltpu.VMEM((ts, 1024), f32),
                        pltpu.VMEM((3, 8, 1024), f32),
                        pltpu.VMEM((GDN_HEADS, GDN_HEAD_DIM, GDN_HEAD_DIM), f32),
                        pltpu.VMEM((ts, LANES), f32), pltpu.VMEM((ts, LANES), f32),
                        pltpu.VMEM((nprob, 128, GDN_HEAD_DIM), f32),
                        pltpu.VMEM((nprob, 2, 128, GDN_HEAD_DIM), bf16),
                        pltpu.VMEM((nprob, 128, 128), bf16),
                        pltpu.VMEM((nprob, 2, GDN_HEAD_DIM, 128), bf16),
                        pltpu.VMEM((zero_rows // ZERO_FILL_COPIES, LANES), f32), pltpu.SemaphoreType.DMA(())],
        compiler_params=_params(("arbitrary", "arbitrary")),
        name="gdn",
    )(fat, fat, fat, fat, fat, conv_w, a_log_row, dt_bias_row, norm_w_row)


def _lru_kernel(x_ref, gate_ref, cw_ref, cb_ref, wax_ref, ba_ref, bx_ref, lam_ref, o_ref,
                cz_ref, xc_ref, a_ref, u_ref, h_ref, carry_ref, hc_ref, *, ts):
    half = ts // 2

    @pl.when(pl.program_id(1) == 0)
    def _():
        carry_ref[...] = jnp.zeros_like(carry_ref)
        hc_ref[...] = jnp.zeros_like(hc_ref)

    w = cw_ref[...]
    smats = _shift_matrices(half)
    cz_ref[8:16, :] = jnp.zeros((8, D_MODEL), f32)
    for hf in range(2):
        rows = slice(hf * half, (hf + 1) * half)
        xb = x_ref[rows, :]
        x = xb.astype(f32)
        y = w[3:4, :] * x + cb_ref[...]
        for j, sm in enumerate(smats):
            y = y + w[j:j + 1, :] * jnp.dot(sm, xb, preferred_element_type=f32)
        cz_ref[0:8, :] = carry_ref[...]
        corr = w[0:1, :] * cz_ref[5:13, :] + w[1:2, :] * cz_ref[6:14, :] + w[2:3, :] * cz_ref[7:15, :]
        carry_ref[...] = x[half - 8:half, :]
        xc_ref[hf * half:hf * half + 8, :] = y[0:8] + corr
        xc_ref[hf * half + 8:(hf + 1) * half, :] = y[8:half]

    nsp = _softplus(-lam_ref[...])
    for blk in range(LRU_BLOCKS):
        cs = slice(blk * LRU_BLOCK_DIM, (blk + 1) * LRU_BLOCK_DIM)
        xc = xc_ref[:, cs]
        ri = jnp.dot(xc.astype(bf16), wax_ref[blk], preferred_element_type=f32)
        r = _sigmoid(ri[:, :LRU_BLOCK_DIM] + ba_ref[:, cs])
        i = _sigmoid(ri[:, LRU_BLOCK_DIM:] + bx_ref[:, cs])
        a = jnp.exp(-LRU_C * r * nsp[:, cs])
        a_ref[:, cs] = a
        u_ref[:, cs] = jnp.sqrt(1.0 - a * a) * (i * xc)

    rowi = lax.broadcasted_iota(i32, (8, D_MODEL), 0)

    def group(gi, h):
        r0 = pl.multiple_of(gi * 8, 8)
        a = a_ref[pl.ds(r0, 8), :]
        b = u_ref[pl.ds(r0, 8), :]
        for d in (1, 2, 4):
            m = rowi >= d
            a_s = pltpu.roll(a, d, 0)
            b_s = pltpu.roll(b, d, 0)
            b = jnp.where(m, a * b_s + b, b)
            a = jnp.where(m, a * a_s, a)
        hh = a * h + b
        h_ref[pl.ds(r0, 8), :] = hh
        return hh[7:8, :]

    hc_ref[...] = lax.fori_loop(0, ts // 8, group, hc_ref[...])
    gt = gate_ref[...].astype(f32)
    o_ref[...] = (h_ref[...] * jax.nn.gelu(gt)).astype(o_ref.dtype)


def lru_branch(fat, conv_w, conv_b, wax, b_a, b_x, lam, B, S, ts=512):
    T = B * S
    ns = S // ts
    row = lambda b, s: b * ns + s
    full = lambda shp: pl.BlockSpec(shp, lambda b, s: (0,) * len(shp))
    return pl.pallas_call(
        functools.partial(_lru_kernel, ts=ts),
        grid=(B, ns),
        in_specs=[pl.BlockSpec((ts, 1024), lambda b, s: (row(b, s), C_LX // 1024)),
                  pl.BlockSpec((ts, 1024), lambda b, s: (row(b, s), C_LG // 1024)),
                  full((CONV_WIDTH, 1024)), full((1, 1024)), full((LRU_BLOCKS, LRU_BLOCK_DIM, 2 * LRU_BLOCK_DIM)),
                  full((1, 1024)), full((1, 1024)), full((1, 1024))],
        out_specs=pl.BlockSpec((ts, 1024), lambda b, s: (row(b, s), 0)),
        out_shape=jax.ShapeDtypeStruct((T, 1024), bf16),
        scratch_shapes=[pltpu.VMEM((16, 1024), f32), pltpu.VMEM((ts, 1024), f32),
                        pltpu.VMEM((ts, 1024), f32), pltpu.VMEM((ts, 1024), f32), pltpu.VMEM((ts, 1024), f32),
                        pltpu.VMEM((8, 1024), f32), pltpu.VMEM((1, 1024), f32)],
        compiler_params=_params(("arbitrary", "arbitrary")),
        name="lru",
    )(fat, fat, conv_w, conv_b, wax, b_a, b_x, lam)


def _swa_kernel(q_ref, kc_ref, kp_ref, vc_ref, vp_ref, biast_ref, sink_ref, o_ref, kb_ref, vt_ref, *, tq):
    W = WINDOW
    hd = SWA_HEAD_DIM
    kb_ref[0:W, :] = kp_ref[...]
    kb_ref[W:W + tq, :] = kc_ref[...]
    vt_ref[0] = vp_ref[...].astype(f32).T.astype(bf16)
    for j in range(tq // W):
        vt_ref[j + 1] = vc_ref[j * W:(j + 1) * W, :].astype(f32).T.astype(bf16)
    first_tile = pl.program_id(1) == 0
    ones_rows = jnp.ones((8, 2 * W), bf16)
    scale = hd ** -0.5

    def qblock(n, carry):
        r0 = pl.multiple_of(n * W, W)
        tab = jnp.where(jnp.logical_and(first_tile, n == 0), 1, 0)
        qs = q_ref[pl.ds(r0, W), :] * scale
        vt_band = jnp.concatenate([vt_ref[n], vt_ref[n + 1]], axis=1)
        outs = []
        for hk0 in range(0, SWA_KV_HEADS, SWA_KV_IN_FLIGHT):
            hks = range(hk0, hk0 + SWA_KV_IN_FLIGHT)
            heads = range(hk0 * SWA_GROUP, (hk0 + SWA_KV_IN_FLIGHT) * SWA_GROUP)
            kk = {hk: kb_ref[pl.ds(r0, 2 * W), hk * hd:(hk + 1) * hd] for hk in hks}
            lhs_v = {hk: jnp.concatenate([vt_band[hk * hd:(hk + 1) * hd, :], ones_rows], axis=0) for hk in hks}
            st = [lax.dot_general(kk[h // SWA_GROUP], qs[:, h * hd:(h + 1) * hd], (((1,), (1,)), ((), ())),
                                  preferred_element_type=f32) + biast_ref[tab, h] for h in heads]
            m = [jnp.maximum(jnp.max(t, axis=0, keepdims=True), sink_ref[h]) for t, h in zip(st, heads)]
            pt = [jnp.exp(t - mm).astype(bf16) for t, mm in zip(st, m)]
            ov = [jnp.dot(lhs_v[h // SWA_GROUP], t, preferred_element_type=f32) for t, h in zip(pt, heads)]
            outs += [o[:hd] / (o[hd:hd + 1] + jnp.exp(sink_ref[h] - mm)) for o, mm, h in zip(ov, m, heads)]
        o_ref[pl.ds(r0, W), :] = jnp.concatenate(outs, axis=0).T.astype(o_ref.dtype)
        return carry

    lax.fori_loop(0, tq // W, qblock, 0)


def swa_branch(fat, bias_tab_t, sinks, B, S, tq=512):
    T = B * S
    ns = S // tq
    nb = S // WINDOW
    per = tq // WINDOW
    row = lambda b, s: b * ns + s
    prev = lambda b, s: b * nb + jnp.maximum(s * per - 1, 0)
    kvw = SWA_KV_HEADS * SWA_HEAD_DIM
    full = lambda shp: pl.BlockSpec(shp, lambda b, s: (0,) * len(shp))
    return pl.pallas_call(
        functools.partial(_swa_kernel, tq=tq),
        grid=(B, ns),
        in_specs=[pl.BlockSpec((tq, 1024), lambda b, s: (row(b, s), C_SQ // 1024)),
                  pl.BlockSpec((tq, kvw), lambda b, s: (row(b, s), C_SK // kvw)),
                  pl.BlockSpec((WINDOW, kvw), lambda b, s: (prev(b, s), C_SK // kvw)),
                  pl.BlockSpec((tq, kvw), lambda b, s: (row(b, s), C_SV // kvw)),
                  pl.BlockSpec((WINDOW, kvw), lambda b, s: (prev(b, s), C_SV // kvw)),
                  full((2, SWA_Q_HEADS, 2 * WINDOW, WINDOW)), pl.BlockSpec(memory_space=pltpu.SMEM)],
        out_specs=pl.BlockSpec((tq, 1024), lambda b, s: (row(b, s), 0)),
        out_shape=jax.ShapeDtypeStruct((T, 1024), bf16),
        scratch_shapes=[pltpu.VMEM((tq + WINDOW, kvw), bf16), pltpu.VMEM((per + 1, kvw, WINDOW), bf16)],
        compiler_params=_params(("arbitrary", "arbitrary")),
        name="swa",
    )(fat, fat, fat, fat, fat, bias_tab_t, sinks)


def _load_row_tiles(ref, n, lead=()):
    return jnp.concatenate([ref[lead + (pl.ds(s, n, stride=ROW_TILE), slice(None))] for s in range(ROW_TILE)], axis=1)


def _store_row_tiles(ref, val):
    n = val.shape[0]
    for s in range(ROW_TILE):
        ref[pl.ds(s, n, stride=ROW_TILE), :] = val[:, s * LANES:(s + 1) * LANES]


def _merge_kernel(oa_ref, ob_ref, oc_ref, ga_ref, gb_ref, gc_ref, x_ref, wa_ref, wb_ref, wc_ref, wo_ref,
                  g_ref, b_ref, o_ref, ot_ref, *, alpha):
    ya = jnp.dot(oa_ref[...], wa_ref[...], preferred_element_type=f32)
    yb = jnp.dot(ob_ref[...], wb_ref[...], preferred_element_type=f32)
    yc = jnp.dot(oc_ref[...], wc_ref[...], preferred_element_type=f32)
    mix = (_sigmoid(ga_ref[...].astype(f32)) * ya + _sigmoid(gb_ref[...].astype(f32)) * yb
           + _sigmoid(gc_ref[...].astype(f32)) * yc)
    y = jnp.dot(mix.astype(bf16), wo_ref[...], preferred_element_type=f32)
    x1 = _layer_norm(alpha * x_ref[...] + y, g_ref[...], b_ref[...])
    o_ref[...] = x1
    _store_row_tiles(ot_ref, x1)


def merge_ln(oa, ob, oc, fat, x2d, wa, wb, wc, wo, g, b, alpha, tm=512):
    T = x2d.shape[0]
    act = pl.BlockSpec((tm, 1024), lambda i: (i, 0))
    fatb = lambda cb: pl.BlockSpec((tm, 1024), lambda i: (i, cb))
    wsp = pl.BlockSpec((1024, 1024), lambda i: (0, 0))
    vec = pl.BlockSpec((1, 1024), lambda i: (0, 0))
    return pl.pallas_call(
        functools.partial(_merge_kernel, alpha=alpha),
        grid=(T // tm,),
        in_specs=[act, act, act, fatb(C_MA // 1024), fatb(C_MB // 1024), fatb(C_MC // 1024), act,
                  wsp, wsp, wsp, wsp, vec, vec],
        out_specs=[act, pl.BlockSpec((tm * ROW_TILE, LANES), lambda i: (i, 0))],
        out_shape=[jax.ShapeDtypeStruct((T, 1024), f32), jax.ShapeDtypeStruct((T * ROW_TILE, LANES), f32)],
        compiler_params=_params(("arbitrary",)),
        name="merge_ln",
    )(oa, ob, oc, fat, fat, fat, x2d, wa, wb, wc, wo, g, b)


def _route_kernel(x_ref, rw_ref, rb_ref, gates_ref, eidx_ref, rank_ref, cnt_ref, run_ref, *, tm):
    @pl.when(pl.program_id(0) == 0)
    def _():
        run_ref[...] = jnp.zeros_like(run_ref)

    x = x_ref[...]
    xh = x.astype(bf16)
    xl = (x - xh.astype(f32)).astype(bf16)
    hw = jnp.dot(xh, rw_ref[...], preferred_element_type=f32)
    logits = (hw[:, :LANES] + (hw[:, LANES:] + jnp.dot(xl, rw_ref[:, :LANES], preferred_element_type=f32))
              + rb_ref[...])
    lane = lax.broadcasted_iota(i32, (tm, LANES), 1)
    lane_f = lane.astype(f32)
    work = logits
    vals, idxs, hots = [], [], []
    for _ in range(TOP_K):
        m = jnp.max(work, axis=-1, keepdims=True)
        idx = jnp.min(jnp.where(work == m, lane_f, float(LANES)), axis=-1, keepdims=True)
        hot = lane_f == idx
        vals.append(m)
        idxs.append(idx)
        hots.append(hot)
        work = jnp.where(hot, -jnp.inf, work)
    es = [jnp.exp(v - vals[0]) for v in vals]
    den = es[0] + es[1] + es[2] + es[3]
    sel = jnp.zeros((tm, LANES), f32)
    for hot in hots:
        sel = sel + jnp.where(hot, 1.0, 0.0)
    ri = lax.broadcasted_iota(i32, (tm, tm), 0)
    ci = lax.broadcasted_iota(i32, (tm, tm), 1)
    tril = jnp.where(ri > ci, 1.0, 0.0).astype(bf16)
    before = jnp.dot(tril, sel.astype(bf16), preferred_element_type=f32) + run_ref[...]
    run_ref[...] = run_ref[...] + jnp.sum(sel, axis=0, keepdims=True)
    cnt_ref[...] = run_ref[...]
    gates = jnp.zeros((tm, LANES), f32)
    eidx = jnp.zeros((tm, LANES), f32)
    rank = jnp.zeros((tm, LANES), f32)
    for k in range(TOP_K):
        rk = jnp.sum(jnp.where(hots[k], before, 0.0), axis=-1, keepdims=True)
        gates = jnp.where(lane == k, es[k] / den, gates)
        eidx = jnp.where(lane == k, idxs[k], eidx)
        rank = jnp.where(lane == k, rk, rank)
    gates_ref[...] = gates
    eidx_ref[...] = eidx.astype(i32)
    rank_ref[...] = rank.astype(i32)


def route(x2d, rw_pad, rb_pad, tm=512):
    T = x2d.shape[0]
    outb = pl.BlockSpec((tm, LANES), lambda i: (i, 0))
    return pl.pallas_call(
        functools.partial(_route_kernel, tm=tm),
        grid=(T // tm,),
        in_specs=[pl.BlockSpec((tm, 1024), lambda i: (i, 0)),
                  pl.BlockSpec((1024, 2 * LANES), lambda i: (0, 0)),
                  pl.BlockSpec((1, LANES), lambda i: (0, 0))],
        out_specs=[outb, outb, outb, pl.BlockSpec((1, LANES), lambda i: (0, 0))],
        out_shape=[jax.ShapeDtypeStruct((T, LANES), f32), jax.ShapeDtypeStruct((T, LANES), i32),
                   jax.ShapeDtypeStruct((T, LANES), i32), jax.ShapeDtypeStruct((1, LANES), f32)],
        scratch_shapes=[pltpu.VMEM((1, LANES), f32)],
        compiler_params=_params(("arbitrary",)),
        name="route",
    )(x2d, rw_pad, rb_pad)


GU_GROUP = 2 * LANES


def _gu_prep_tile(w_ref, o_ref):
    ri = lax.broadcasted_iota(i32, (GU_GROUP, GU_GROUP), 0)
    ci = lax.broadcasted_iota(i32, (GU_GROUP, GU_GROUP), 1)
    src = jnp.where(ci < LANES, 2 * ci, 2 * (ci - LANES) + 1)
    perm = jnp.where(ri == src, 1.0, 0.0).astype(bf16)
    for g in range(w_ref.shape[3] // GU_GROUP):
        cs = slice(g * GU_GROUP, (g + 1) * GU_GROUP)
        o_ref[0, :, cs] = jnp.dot(w_ref[0, 0, :, cs].astype(bf16), perm, preferred_element_type=f32).astype(bf16)


ROW_UNROLL = 8


def _row_copy(src_ref, src_row, dst_ref, dst_row, sem):
    tile = lambda r: pl.ds(pl.multiple_of(r * ROW_TILE, ROW_TILE), ROW_TILE)
    return pltpu.make_async_copy(src_ref.at[tile(src_row)], dst_ref.at[tile(dst_row)], sem)


def _dispatch_kernel(dest_ref, x_ref, buf_in_ref, w_ref, buf_ref, wo_ref, sem, *, tm, nt, ng):
    del buf_in_ref
    i = pl.program_id(0)

    @pl.when(i < nt)
    def _():
        def issue(g, carry):
            for j in range(ROW_UNROLL):
                r = g * ROW_UNROLL + j
                for k in range(TOP_K):
                    _row_copy(x_ref, r, buf_ref, dest_ref[r * TOP_K + k], sem).start(priority=k % 2)
            return carry

        lax.fori_loop(0, tm // ROW_UNROLL, issue, 0)

    @pl.when(i < ng)
    def _():
        _gu_prep_tile(w_ref, wo_ref)

    @pl.when(i < nt)
    def _():
        def drain(g, carry):
            for j in range(ROW_UNROLL * TOP_K):
                _row_copy(x_ref, 0, buf_ref, 0, sem).wait()
            return carry

        lax.fori_loop(0, tm // ROW_UNROLL, drain, 0)


def dispatch_prep(dest_flat, xt, buf0, w_gu_all, layer, tm=512, tk=512):
    T = xt.shape[0] // ROW_TILE
    _, E, D, N = w_gu_all.shape
    nt = T // tm
    kt = D // tk
    ng = E * kt
    tok = lambda i: jnp.minimum(i, nt - 1)
    gu = lambda i: jnp.minimum(i, ng - 1)
    return pl.pallas_call(
        functools.partial(_dispatch_kernel, tm=tm, nt=nt, ng=ng),
        grid=(max(nt, ng),),
        in_specs=[pl.BlockSpec((tm * TOP_K,), lambda i: (tok(i),), memory_space=pltpu.SMEM),
                  pl.BlockSpec((tm * ROW_TILE, LANES), lambda i: (tok(i), 0)),
                  pl.BlockSpec(memory_space=pl.ANY),
                  pl.BlockSpec((1, 1, tk, N), lambda i: (layer, gu(i) // kt, gu(i) % kt, 0))],
        out_specs=[pl.BlockSpec(memory_space=pl.ANY),
                   pl.BlockSpec((1, tk, N), lambda i: (gu(i) // kt, gu(i) % kt, 0))],
        out_shape=[jax.ShapeDtypeStruct(buf0.shape, buf0.dtype), jax.ShapeDtypeStruct((E, D, N), bf16)],
        scratch_shapes=[pltpu.SemaphoreType.DMA(())],
        input_output_aliases={2: 0},
        compiler_params=_params(("arbitrary",)),
        name="dispatch_prep",
    )(dest_flat, xt, buf0, w_gu_all)


def _expert_kernel(blk_e_ref, nused_ref, x_ref, wgu_ref, bgu_ref, wd_ref, bd_ref, o_ref, wdb_ref):
    i = pl.program_id(0)

    @pl.when(jnp.logical_or(i == 0, blk_e_ref[i] != blk_e_ref[jnp.maximum(i - 1, 0)]))
    def _():
        wdb_ref[...] = wd_ref[0, 0].astype(bf16)

    @pl.when(i < nused_ref[0])
    def _():
        de = wdb_ref.shape[0]
        xb = _load_row_tiles(x_ref, EXPERT_BLK).astype(bf16)
        hgu = jnp.dot(xb, wgu_ref[0], preferred_element_type=f32) + bgu_ref[0]
        acts = []
        for g in range(2 * de // GU_GROUP):
            gate = jnp.minimum(hgu[:, g * GU_GROUP:g * GU_GROUP + LANES], SWIGLU_LIMIT)
            lin = jnp.clip(hgu[:, g * GU_GROUP + LANES:(g + 1) * GU_GROUP], -SWIGLU_LIMIT, SWIGLU_LIMIT)
            acts.append((gate * _sigmoid(SWIGLU_ALPHA * gate) * (lin + 1.0)).astype(bf16))
        act = jnp.concatenate(acts, axis=1)
        _store_row_tiles(o_ref, jnp.dot(act, wdb_ref[...], preferred_element_type=f32) + bd_ref[0])

    @pl.when(i >= nused_ref[0])
    def _():
        o_ref[...] = jnp.zeros_like(o_ref)


def experts(blk_e, nused, buf, wgu, bgu, wd_all, layer, bd):
    D = wd_all.shape[3]
    nblk = buf.shape[0] // (EXPERT_BLK * ROW_TILE)
    de = wd_all.shape[2]
    tile_blk = (EXPERT_BLK * ROW_TILE, LANES)
    live = lambda i, be, nu: jnp.minimum(i, nu[0] - 1)
    grid_spec = pltpu.PrefetchScalarGridSpec(
        num_scalar_prefetch=2,
        grid=(nblk,),
        in_specs=[pl.BlockSpec(tile_blk, lambda i, be, nu: (live(i, be, nu), 0)),
                  pl.BlockSpec((1, D, 2 * de), lambda i, be, nu: (be[i], 0, 0)),
                  pl.BlockSpec((1, 1, 2 * de), lambda i, be, nu: (be[i], 0, 0)),
                  pl.BlockSpec((1, 1, de, D), lambda i, be, nu: (layer, be[i], 0, 0)),
                  pl.BlockSpec((1, 1, D), lambda i, be, nu: (be[i], 0, 0))],
        out_specs=pl.BlockSpec(tile_blk, lambda i, be, nu: (i, 0)),
        scratch_shapes=[pltpu.VMEM((de, D), bf16)],
    )
    return pl.pallas_call(
        _expert_kernel,
        grid_spec=grid_spec,
        out_shape=jax.ShapeDtypeStruct(buf.shape, f32),
        compiler_params=_params(("arbitrary",)),
        name="experts",
    )(blk_e, nused, buf, wgu, bgu, wd_all, bd)


def _combine_kernel(dest_ref, dest_next_ref, gates_ref, x_ref, p_ref, obuf_ref, wg_ref, wp_ref,
                    g2_ref, b2_ref, g3_ref, b3_ref, o_ref, rows_ref, sems, *, tm, alpha):
    i = pl.program_id(0)
    n = pl.num_programs(0)
    slot = i % 2

    def gather(idx_ref, s):
        def issue(g, carry):
            for j in range(ROW_UNROLL):
                r = g * ROW_UNROLL + j
                for k in range(TOP_K):
                    _row_copy(obuf_ref, idx_ref[r * TOP_K + k], rows_ref.at[s, k], r, sems.at[s]).start(priority=k % 2)
            return carry

        lax.fori_loop(0, tm // ROW_UNROLL, issue, 0)

    @pl.when(i == 0)
    def _():
        gather(dest_ref, 0)

    @pl.when(i + 1 < n)
    def _():
        gather(dest_next_ref, 1 - slot)

    def drain(g, carry):
        for j in range(ROW_UNROLL * TOP_K):
            _row_copy(obuf_ref, 0, rows_ref.at[slot, 0], 0, sems.at[slot]).wait()
        return carry

    lax.fori_loop(0, tm // ROW_UNROLL, drain, 0)

    gates = gates_ref[...]
    y = gates[:, 0:1] * _load_row_tiles(rows_ref, tm, (slot, 0))
    for k in range(1, TOP_K):
        y = y + gates[:, k:k + 1] * _load_row_tiles(rows_ref, tm, (slot, k))
    x2 = _layer_norm(alpha * x_ref[...] + y, g2_ref[...], b2_ref[...])
    gate = _sigmoid(jnp.dot(x2.astype(bf16), wg_ref[...], preferred_element_type=f32))
    proj = jnp.dot(p_ref[...].astype(bf16), wp_ref[...], preferred_element_type=f32)
    o_ref[...] = _layer_norm(alpha * x2 + gate * proj, g3_ref[...], b3_ref[...])


def combine_ple(dest_flat, gates, x2d, p2d, layer, obuf, wg, wp, g2, b2, g3, b3, alpha, tm=256):
    T = x2d.shape[0]
    nt = T // tm
    act = pl.BlockSpec((tm, 1024), lambda i: (i, 0))
    vec = pl.BlockSpec((1, 1024), lambda i: (0, 0))
    return pl.pallas_call(
        functools.partial(_combine_kernel, tm=tm, alpha=alpha),
        grid=(nt,),
        in_specs=[pl.BlockSpec((tm * TOP_K,), lambda i: (i,), memory_space=pltpu.SMEM),
                  pl.BlockSpec((tm * TOP_K,), lambda i: (jnp.minimum(i + 1, nt - 1),), memory_space=pltpu.SMEM),
                  pl.BlockSpec((tm, LANES), lambda i: (i, 0)),
                  act,
                  pl.BlockSpec((tm, PLE_DIM), lambda i: (layer * nt + i, 0)),
                  pl.BlockSpec(memory_space=pl.ANY),
                  pl.BlockSpec((1024, 1024), lambda i: (0, 0)),
                  pl.BlockSpec((PLE_DIM, 1024), lambda i: (0, 0)),
                  vec, vec, vec, vec],
        out_specs=act,
        out_shape=jax.ShapeDtypeStruct((T, 1024), f32),
        scratch_shapes=[pltpu.VMEM((2, TOP_K, tm * ROW_TILE, LANES), f32), pltpu.SemaphoreType.DMA((2,))],
        compiler_params=_params(("arbitrary",)),
        name="combine_ple",
    )(dest_flat, dest_flat, gates, x2d, p2d, obuf, wg, wp, g2, b2, g3, b3)


def _t5_bucket_np(dist):
    max_exact = REL_BUCKETS // 2
    d = np.maximum(dist.astype(np.float32), np.float32(1.0))
    large = max_exact + (np.log(d / np.float32(max_exact)) / np.float32(math.log(REL_MAX_DISTANCE / max_exact))
                         * np.float32(REL_BUCKETS - max_exact)).astype(np.int32)
    large = np.minimum(large, REL_BUCKETS - 1)
    return np.where(dist < max_exact, dist, large)


def _swa_bias_table(rel_bias):
    dist = np.arange(2 * WINDOW - 1, -WINDOW, -1)
    in_window = (dist >= 0) & (dist < WINDOW)
    per_dist = rel_bias[_t5_bucket_np(np.maximum(dist, 0))].astype(f32).T
    per_dist = jnp.where(jnp.asarray(in_window)[None], per_dist, NEG_BIG)
    n = 3 * WINDOW - 1
    skew = jnp.tile(jnp.pad(per_dist, ((0, 0), (0, 1))), (1, WINDOW))[:, :WINDOW * n].reshape(-1, WINDOW, n)
    bias = jnp.transpose(skew[:, :, WINDOW - 1:], (0, 2, 1))
    first = jnp.where((jnp.arange(2 * WINDOW) >= WINDOW)[None, :, None], bias, NEG_BIG)
    return jnp.stack([bias, first])


def _pad_row(v, width=LANES, fill=0.0):
    v = v.astype(f32).reshape(1, -1)
    return jnp.pad(v, ((0, 0), (0, width - v.shape[1])), constant_values=fill)


def _wcat(w_in):
    cols = [w_in[:, 0:4096], w_in[:, 4112:6160], w_in[:, 6160:7184], w_in[:, 7696:10768],
            w_in[:, 7184:7440], w_in[:, 7440:7696], w_in[:, 4096:4112],
            jnp.zeros((w_in.shape[0], FAT_W - 10768), w_in.dtype)]
    return jnp.concatenate(cols, axis=1).astype(bf16)


def kernel(x, p, w_in, conv_qkv_w, gdn_a_log, gdn_dt_bias, gdn_norm_w, rg_conv_w, rg_conv_b, rg_w_a, rg_b_a, rg_w_x, rg_b_x, rg_lambda, attn_sinks, rel_bias, w_o_gdn, w_o_lru, w_o_swa, w_out, ln1_g, ln1_b, router_w, router_b, w_gu, b_gu, w_down, b_down, ln2_g, ln2_b, ple_w_gate, ple_w_proj, ln3_g, ln3_b):
    B, S, D = x.shape
    depth = w_in.shape[0]
    T = B * S
    A = T * TOP_K
    alpha = (2.0 * depth) ** 0.25
    P = A + N_EXPERTS * EXPERT_BLK
    nblk = P // EXPERT_BLK
    row = lambda v: v.astype(f32).reshape(1, -1)

    bias_tab = _swa_bias_table(rel_bias)
    p2d = p.reshape(depth * T, PLE_DIM)
    xc = x.reshape(T, D)
    for i in range(depth):
        fat = inproj(xc, _wcat(w_in[i]))
        o_gdn, buf0 = gdn_branch(fat, conv_qkv_w[i], _pad_row(gdn_a_log[i]), _pad_row(gdn_dt_bias[i]),
                                 row(gdn_norm_w[i]), B, S, P * ROW_TILE)
        wax = jnp.concatenate([rg_w_a[i], rg_w_x[i]], axis=-1).astype(bf16)
        o_lru = lru_branch(fat, rg_conv_w[i], row(rg_conv_b[i]), wax, row(rg_b_a[i]), row(rg_b_x[i]),
                           row(rg_lambda[i]), B, S)
        o_swa = swa_branch(fat, bias_tab, attn_sinks[i].astype(f32), B, S)
        x1, x1t = merge_ln(o_gdn, o_lru, o_swa, fat, xc, w_o_gdn[i].astype(bf16), w_o_lru[i].astype(bf16),
                           w_o_swa[i].astype(bf16), w_out[i].astype(bf16), row(ln1_g[i]), row(ln1_b[i]), alpha)

        rw_pad = jnp.pad(router_w[i].astype(f32), ((0, 0), (0, LANES - N_EXPERTS)))
        rw_hi = rw_pad.astype(bf16)
        rw_pad = jnp.concatenate([rw_hi, (rw_pad - rw_hi.astype(f32)).astype(bf16)], axis=1)
        rb_pad = _pad_row(router_b[i], fill=NEG_BIG)
        gates, eidx, rank, cnt = route(x1, rw_pad, rb_pad)
        counts = cnt[0, :N_EXPERTS].astype(i32)
        padded = ((counts + EXPERT_BLK - 1) // EXPERT_BLK) * EXPERT_BLK
        pad_ends = jnp.cumsum(padded)
        pad_starts = pad_ends - padded
        hit = eidx[:, :TOP_K, None] == jnp.arange(N_EXPERTS, dtype=i32)
        dest = (jnp.sum(jnp.where(hit, pad_starts, 0), axis=-1) + rank[:, :TOP_K]).reshape(A)
        blk_start = jnp.arange(nblk, dtype=i32) * EXPERT_BLK
        blk_e = jnp.minimum(jnp.sum((pad_ends[None, :] <= blk_start[:, None]).astype(i32), axis=1),
                            N_EXPERTS - 1).astype(i32)
        nused = (pad_ends[-1:] // EXPERT_BLK).astype(i32)

        buf, wgu = dispatch_prep(dest, x1t, buf0, w_gu, i)
        bgu = jnp.transpose(b_gu[i].reshape(N_EXPERTS, -1, LANES, 2), (0, 1, 3, 2)).reshape(N_EXPERTS, 1, -1)
        obuf = experts(blk_e, nused, buf, wgu, bgu, w_down, i, b_down[i][:, None, :])
        xc = combine_ple(dest, gates, x1, p2d, i, obuf, ple_w_gate[i].astype(bf16),
                         ple_w_proj[i].astype(bf16), row(ln2_g[i]), row(ln2_b[i]), row(ln3_g[i]), row(ln3_b[i]),
                         alpha)
    return xc.reshape(B, S, D)
```

```python
import functools
import math

import numpy as np
import jax
import jax.numpy as jnp
from jax import lax
from jax.experimental import pallas as pl
from jax.experimental.pallas import tpu as pltpu

f32 = jnp.float32
bf16 = jnp.bfloat16
i32 = jnp.int32

D_MODEL = 1024
PLE_DIM = 256
GDN_HEADS = 8
GDN_HEAD_DIM = 128
GDN_CHUNK = 64
CONV_WIDTH = 4
LRU_BLOCKS = 8
LRU_BLOCK_DIM = 128
LRU_C = 8.0
SWA_Q_HEADS = 16
SWA_KV_HEADS = 4
SWA_HEAD_DIM = 64
SWA_GROUP = 4
WINDOW = 128
REL_BUCKETS = 32
REL_MAX_DISTANCE = 128
N_EXPERTS = 32
TOP_K = 4
SWIGLU_LIMIT = 7.0
SWIGLU_ALPHA = 1.702
LN_EPS = 1e-5
NORM_EPS = 1e-6
NEG_BIG = -1e30

LANES = 128
VMEM_LIMIT = 56 * 1024 * 1024

C_GQ, C_GK, C_GV, C_GZ = 0, 1024, 2048, 3072
C_LX, C_LG = 4096, 5120
C_SQ = 6144
C_MA, C_MB, C_MC = 7168, 8192, 9216
C_SK, C_SV = 10240, 10496
C_AB = 10752
FAT_W = 10880
FAT_TN = 2176

ROW_TILE = D_MODEL // LANES

EXPERT_BLK = 512
ROUTE_GROUPS = 4
COMBINE_GROUPS = 4
CHUNKS_IN_FLIGHT = 4
SWA_KV_IN_FLIGHT = 4


def _params(sem):
    return pltpu.CompilerParams(dimension_semantics=sem, vmem_limit_bytes=VMEM_LIMIT)


def _sigmoid(x):
    return 1.0 / (1.0 + jnp.exp(-x))


def _softplus(x):
    return jnp.maximum(x, 0.0) + jnp.log(1.0 + jnp.exp(-jnp.abs(x)))


def _layer_norm(z, g, b):
    mu = jnp.mean(z, axis=-1, keepdims=True)
    zc = z - mu
    var = jnp.mean(zc * zc, axis=-1, keepdims=True)
    return zc * lax.rsqrt(var + LN_EPS) * g + b


def _inproj_kernel(x_ref, w_ref, o_ref, xb_ref):
    @pl.when(pl.program_id(1) == 0)
    def _():
        xb_ref[...] = x_ref[...].astype(bf16)

    o_ref[...] = jnp.dot(xb_ref[...], w_ref[...], preferred_element_type=f32).astype(o_ref.dtype)


def inproj(x2d, wcat, tm=2048):
    T, K = x2d.shape
    N = wcat.shape[1]
    tn = FAT_TN
    return pl.pallas_call(
        _inproj_kernel,
        grid=(T // tm, N // tn),
        in_specs=[pl.BlockSpec((tm, K), lambda i, j: (i, 0)),
                  pl.BlockSpec((K, tn), lambda i, j: (0, j))],
        out_specs=pl.BlockSpec((tm, tn), lambda i, j: (i, j)),
        out_shape=jax.ShapeDtypeStruct((T, N), bf16),
        scratch_shapes=[pltpu.VMEM((tm, K), bf16)],
        compiler_params=_params(("arbitrary", "arbitrary")),
        name="inproj",
    )(x2d, wcat)


ZERO_FILL_COPIES = 4


def _zero_fill(step, zero_ref, zbuf_ref, zsem, zero_rows):
    zr = zero_ref.shape[0]
    return [pltpu.make_async_copy(zero_ref, zbuf_ref.at[pl.ds(pl.multiple_of(step * zero_rows + c * zr, zr), zr)], zsem)
            for c in range(zero_rows // zr)]


def _shift_matrices(ts):
    ri = lax.broadcasted_iota(i32, (ts, ts), 0)
    ci = lax.broadcasted_iota(i32, (ts, ts), 1)
    return [jnp.where(ri - ci == d, 1.0, 0.0).astype(bf16) for d in (3, 2, 1)]


def _causal_conv_silu(src_ref, dst_ref, cz_ref, carry_ref, w, smats, ts, head_scale=None):
    xb = src_ref[...]
    x = xb.astype(f32)
    y = w[3:4, :] * x
    for j, sm in enumerate(smats):
        y = y + w[j:j + 1, :] * jnp.dot(sm, xb, preferred_element_type=f32)
    cz_ref[0:8, :] = carry_ref[...]
    cz_ref[8:16, :] = jnp.zeros((8, x.shape[1]), f32)
    corr = w[0:1, :] * cz_ref[5:13, :] + w[1:2, :] * cz_ref[6:14, :] + w[2:3, :] * cz_ref[7:15, :]
    carry_ref[...] = x[ts - 8:ts, :]

    def post(rows, yv):
        a = yv * _sigmoid(yv)
        if head_scale is None:
            dst_ref[rows, :] = a
            return
        for h in range(a.shape[1] // LANES):
            cs = slice(h * LANES, (h + 1) * LANES)
            ah = a[:, cs]
            dst_ref[rows, cs] = ah * (lax.rsqrt(jnp.sum(ah * ah, axis=-1, keepdims=True) + NORM_EPS) * head_scale)

    post(slice(0, 8), y[0:8] + corr)
    post(slice(8, ts), y[8:ts])


def _gdn_kernel(q_ref, k_ref, v_ref, z_ref, ab_ref, cw_ref, alog_ref, dtb_ref, nw_ref, o_ref, zbuf_ref,
                xpad_ref, qs_ref, ks_ref, vs_ref, carry_ref, state_ref, g_ref, beta_ref,
                u_s, lhs_s, intra_s, kdt_s, zero_ref, zsem, *, ts, zero_rows):
    C = GDN_CHUNK
    D = GDN_HEAD_DIM
    P = 2 * C

    step = pl.program_id(0) * pl.num_programs(1) + pl.program_id(1)

    @pl.when(step == 0)
    def _():
        zero_ref[...] = jnp.zeros_like(zero_ref)

    fills = _zero_fill(step, zero_ref, zbuf_ref, zsem, zero_rows)
    for f in fills:
        f.start()

    @pl.when(pl.program_id(1) == 0)
    def _():
        carry_ref[...] = jnp.zeros_like(carry_ref)
        state_ref[...] = jnp.zeros_like(state_ref)

    smats = _shift_matrices(ts)
    for p, (src, dst, scale) in enumerate(((q_ref, qs_ref, D ** -0.5), (k_ref, ks_ref, 1.0), (v_ref, vs_ref, None))):
        _causal_conv_silu(src, dst, xpad_ref, carry_ref.at[p], cw_ref[:, p * 1024:(p + 1) * 1024], smats, ts, scale)

    ab = ab_ref[...].astype(f32)
    g = -jnp.exp(alog_ref[...]) * _softplus(ab + dtb_ref[...])
    rin = lax.broadcasted_iota(i32, (ts, LANES), 0) & (C - 1)
    gsum = g
    for d in (1, 2, 4, 8, 16, 32):
        gsum = gsum + jnp.where(rin >= d, pltpu.roll(gsum, d, 0), 0.0)
    g_ref[...] = gsum
    beta_ref[...] = _sigmoid(ab)

    ri = lax.broadcasted_iota(i32, (P, P), 0)
    ci = lax.broadcasted_iota(i32, (P, P), 1)
    same = (ri >= C) == (ci >= C)
    eye = ri == ci
    causal = same & (ri >= ci)
    strict = same & (ri > ci)
    eye_f = jnp.where(eye, 1.0, 0.0).astype(f32)
    first_cols = ci < C
    nw = nw_ref[...]

    def stack(a, b):
        return jnp.concatenate([a, b], axis=0)

    def mm(a, b):
        return jnp.dot(a, b, preferred_element_type=f32)

    npair = GDN_HEADS // 2
    nchunk = ts // C
    hcols = [slice(h * D, (h + 1) * D) for h in range(GDN_HEADS)]

    for cg in range(0, nchunk, CHUNKS_IN_FLIGHT):
        probs = [(c, hp) for c in range(cg, cg + CHUNKS_IN_FLIGHT) for hp in range(npair)]
        qn, kn, vb, gcol, eg, egl, kb = [], [], [], [], [], [], []
        for c, hp in probs:
            rows = slice(c * C, (c + 1) * C)
            c0, c1 = hcols[2 * hp], hcols[2 * hp + 1]
            gc = g_ref[rows, :]
            bc = beta_ref[rows, :]
            qn.append(stack(qs_ref[rows, c0], qs_ref[rows, c1]))
            kn.append(stack(ks_ref[rows, c0], ks_ref[rows, c1]))
            v2 = stack(vs_ref[rows, c0], vs_ref[rows, c1])
            h0, h1 = 2 * hp, 2 * hp + 1
            gcl = stack(gc[:, h0:h0 + 1], gc[:, h1:h1 + 1])
            bcl = stack(bc[:, 8 + h0:9 + h0], bc[:, 8 + h1:9 + h1])
            glast = stack(jnp.broadcast_to(gc[C - 1:C, h0:h0 + 1], (C, 1)),
                          jnp.broadcast_to(gc[C - 1:C, h1:h1 + 1], (C, 1)))
            gcol.append(gcl)
            eg.append(jnp.exp(gcl))
            egl.append(jnp.exp(glast - gcl))
            kb.append(kn[-1] * bcl)
            vb.append(v2 * bcl)
        n = len(probs)
        a2 = [lax.dot_general(stack(kb[i], qn[i]).astype(bf16), kn[i].astype(bf16), (((1,), (1,)), ((), ())),
                              preferred_element_type=f32) for i in range(n)]
        lmat, intra = [], []
        for i in range(n):
            gm = jnp.broadcast_to(gcol[i], (P, P))
            grow = jnp.sum(jnp.where(eye, gm, 0.0), axis=0, keepdims=True)
            decay = jnp.where(causal, jnp.exp(jnp.minimum(gm - grow, 0.0)), 0.0)
            lmat.append(jnp.where(strict, a2[i][:P] * decay, 0.0))
            intra.append(a2[i][P:] * decay)
        lb = [l.astype(bf16) for l in lmat]
        xm = [eye_f - l for l in lmat]
        pm = [mm(b, b) for b in lb]
        for it in range(5):
            pb = [p.astype(bf16) for p in pm]
            xm = [x + mm(x.astype(bf16), b) for x, b in zip(xm, pb)]
            if it < 4:
                pm = [mm(b, b) for b in pb]
        uw = [mm(xm[i].astype(bf16), jnp.concatenate([vb[i], kb[i] * eg[i]], axis=1).astype(bf16))
              for i in range(n)]
        for i, (c, hp) in enumerate(probs):
            j = c * npair + hp
            w2 = uw[i][:, D:]
            qd = qn[i] * eg[i]
            kdt = (kn[i] * egl[i]).T
            u_s[j] = uw[i][:, :D]
            lhs_s[j, 0] = stack(w2[:C], qd[:C]).astype(bf16)
            lhs_s[j, 1] = stack(w2[C:], qd[C:]).astype(bf16)
            intra_s[j] = intra[i].astype(bf16)
            kdt_s[j, 0] = jnp.where(first_cols, kdt, 0.0).astype(bf16)
            kdt_s[j, 1] = jnp.where(first_cols, 0.0, kdt).astype(bf16)

    for c in range(nchunk):
        rows = slice(c * C, (c + 1) * C)
        gl = g_ref[(c + 1) * C - 1:(c + 1) * C, :]
        st = [state_ref[h] for h in range(GDN_HEADS)]
        wq = [mm(lhs_s[c * npair + h // 2, h % 2], st[h].astype(bf16)) for h in range(GDN_HEADS)]
        vnb = [(u_s[c * npair + hp] - stack(wq[2 * hp][:C], wq[2 * hp + 1][:C])).astype(bf16)
               for hp in range(npair)]
        o2 = [stack(wq[2 * hp][C:], wq[2 * hp + 1][C:]) + mm(intra_s[c * npair + hp], vnb[hp])
              for hp in range(npair)]
        for h in range(GDN_HEADS):
            state_ref[h] = st[h] * jnp.exp(gl[:, h:h + 1]) + mm(kdt_s[c * npair + h // 2, h % 2], vnb[h // 2])
        for hp in range(npair):
            c0, c1 = hcols[2 * hp], hcols[2 * hp + 1]
            z2 = stack(z_ref[rows, c0], z_ref[rows, c1]).astype(f32)
            on = (o2[hp] * lax.rsqrt(jnp.mean(o2[hp] * o2[hp], axis=-1, keepdims=True) + NORM_EPS) * nw
                  * (z2 * _sigmoid(z2))).astype(o_ref.dtype)
            o_ref[rows, c0] = on[:C]
            o_ref[rows, c1] = on[C:]
    for f in fills:
        f.wait()


def gdn_branch(fat, conv_w, a_log_row, dt_bias_row, norm_w_row, B, S, buf_rows, ts=256):
    T = B * S
    ns = S // ts
    nprob = (ts // GDN_CHUNK) * (GDN_HEADS // 2)
    zero_rows = buf_rows // (B * ns)
    assert zero_rows * B * ns == buf_rows and zero_rows % (ZERO_FILL_COPIES * ROW_TILE) == 0
    row = lambda b, s: b * ns + s
    blk = lambda cb: pl.BlockSpec((ts, 1024), lambda b, s: (row(b, s), cb))
    full = lambda shp: pl.BlockSpec(shp, lambda b, s: (0,) * len(shp))
    return pl.pallas_call(
        functools.partial(_gdn_kernel, ts=ts, zero_rows=zero_rows),
        grid=(B, ns),
        in_specs=[blk(C_GQ // 1024), blk(C_GK // 1024), blk(C_GV // 1024), blk(C_GZ // 1024),
                  pl.BlockSpec((ts, LANES), lambda b, s: (row(b, s), C_AB // LANES)),
                  full((CONV_WIDTH, 3072)), full((1, LANES)), full((1, LANES)), full((1, LANES))],
        out_specs=[pl.BlockSpec((ts, 1024), lambda b, s: (row(b, s), 0)), pl.BlockSpec(memory_space=pl.ANY)],
        out_shape=[jax.ShapeDtypeStruct((T, 1024), bf16), jax.ShapeDtypeStruct((buf_rows, LANES), f32)],
        scratch_shapes=[pltpu.VMEM((16, 1024), f32),
                        pltpu.VMEM((ts, 1024), f32), pltpu.VMEM((ts, 1024), f32), pltpu.VMEM((ts, 1024), f32),
                        pltpu.VMEM((3, 8, 1024), f32),
                        pltpu.VMEM((GDN_HEADS, GDN_HEAD_DIM, GDN_HEAD_DIM), f32),
                        pltpu.VMEM((ts, LANES), f32), pltpu.VMEM((ts, LANES), f32),
                        pltpu.VMEM((nprob, 128, GDN_HEAD_DIM), f32),
                        pltpu.VMEM((nprob, 2, 128, GDN_HEAD_DIM), bf16),
                        pltpu.VMEM((nprob, 128, 128), bf16),
                        pltpu.VMEM((nprob, 2, GDN_HEAD_DIM, 128), bf16),
                        pltpu.VMEM((zero_rows // ZERO_FILL_COPIES, LANES), f32), pltpu.SemaphoreType.DMA(())],
        compiler_params=_params(("arbitrary", "arbitrary")),
        name="gdn",
    )(fat, fat, fat, fat, fat, conv_w, a_log_row, dt_bias_row, norm_w_row)


def _lru_kernel(x_ref, gate_ref, cw_ref, cb_ref, wax_ref, ba_ref, bx_ref, lam_ref, o_ref,
                cz_ref, xc_ref, a_ref, u_ref, h_ref, carry_ref, hc_ref, *, ts):
    half = ts // 2

    @pl.when(pl.program_id(1) == 0)
    def _():
        carry_ref[...] = jnp.zeros_like(carry_ref)
        hc_ref[...] = jnp.zeros_like(hc_ref)

    w = cw_ref[...]
    smats = _shift_matrices(half)
    cz_ref[8:16, :] = jnp.zeros((8, D_MODEL), f32)
    for hf in range(2):
        rows = slice(hf * half, (hf + 1) * half)
        xb = x_ref[rows, :]
        x = xb.astype(f32)
        y = w[3:4, :] * x + cb_ref[...]
        for j, sm in enumerate(smats):
            y = y + w[j:j + 1, :] * jnp.dot(sm, xb, preferred_element_type=f32)
        cz_ref[0:8, :] = carry_ref[...]
        corr = w[0:1, :] * cz_ref[5:13, :] + w[1:2, :] * cz_ref[6:14, :] + w[2:3, :] * cz_ref[7:15, :]
        carry_ref[...] = x[half - 8:half, :]
        xc_ref[hf * half:hf * half + 8, :] = y[0:8] + corr
        xc_ref[hf * half + 8:(hf + 1) * half, :] = y[8:half]

    nsp = _softplus(-lam_ref[...])
    for blk in range(LRU_BLOCKS):
        cs = slice(blk * LRU_BLOCK_DIM, (blk + 1) * LRU_BLOCK_DIM)
        xc = xc_ref[:, cs]
        ri = jnp.dot(xc.astype(bf16), wax_ref[blk], preferred_element_type=f32)
        r = _sigmoid(ri[:, :LRU_BLOCK_DIM] + ba_ref[:, cs])
        i = _sigmoid(ri[:, LRU_BLOCK_DIM:] + bx_ref[:, cs])
        a = jnp.exp(-LRU_C * r * nsp[:, cs])
        a_ref[:, cs] = a
        u_ref[:, cs] = jnp.sqrt(1.0 - a * a) * (i * xc)

    rowi = lax.broadcasted_iota(i32, (8, D_MODEL), 0)

    def group(gi, h):
        r0 = pl.multiple_of(gi * 8, 8)
        a = a_ref[pl.ds(r0, 8), :]
        b = u_ref[pl.ds(r0, 8), :]
        for d in (1, 2, 4):
            m = rowi >= d
            a_s = pltpu.roll(a, d, 0)
            b_s = pltpu.roll(b, d, 0)
            b = jnp.where(m, a * b_s + b, b)
            a = jnp.where(m, a * a_s, a)
        hh = a * h + b
        h_ref[pl.ds(r0, 8), :] = hh
        return hh[7:8, :]

    hc_ref[...] = lax.fori_loop(0, ts // 8, group, hc_ref[...])
    gt = gate_ref[...].astype(f32)
    o_ref[...] = (h_ref[...] * jax.nn.gelu(gt)).astype(o_ref.dtype)


def lru_branch(fat, conv_w, conv_b, wax, b_a, b_x, lam, B, S, ts=512):
    T = B * S
    ns = S // ts
    row = lambda b, s: b * ns + s
    full = lambda shp: pl.BlockSpec(shp, lambda b, s: (0,) * len(shp))
    return pl.pallas_call(
        functools.partial(_lru_kernel, ts=ts),
        grid=(B, ns),
        in_specs=[pl.BlockSpec((ts, 1024), lambda b, s: (row(b, s), C_LX // 1024)),
                  pl.BlockSpec((ts, 1024), lambda b, s: (row(b, s), C_LG // 1024)),
                  full((CONV_WIDTH, 1024)), full((1, 1024)), full((LRU_BLOCKS, LRU_BLOCK_DIM, 2 * LRU_BLOCK_DIM)),
                  full((1, 1024)), full((1, 1024)), full((1, 1024))],
        out_specs=pl.BlockSpec((ts, 1024), lambda b, s: (row(b, s), 0)),
        out_shape=jax.ShapeDtypeStruct((T, 1024), bf16),
        scratch_shapes=[pltpu.VMEM((16, 1024), f32), pltpu.VMEM((ts, 1024), f32),
                        pltpu.VMEM((ts, 1024), f32), pltpu.VMEM((ts, 1024), f32), pltpu.VMEM((ts, 1024), f32),
                        pltpu.VMEM((8, 1024), f32), pltpu.VMEM((1, 1024), f32)],
        compiler_params=_params(("arbitrary", "arbitrary")),
        name="lru",
    )(fat, fat, conv_w, conv_b, wax, b_a, b_x, lam)


def _swa_kernel(q_ref, kc_ref, kp_ref, vc_ref, vp_ref, biast_ref, sink_ref, o_ref, kb_ref, vt_ref, *, tq):
    W = WINDOW
    hd = SWA_HEAD_DIM
    kb_ref[0:W, :] = kp_ref[...]
    kb_ref[W:W + tq, :] = kc_ref[...]
    vt_ref[0] = vp_ref[...].astype(f32).T.astype(bf16)
    for j in range(tq // W):
        vt_ref[j + 1] = vc_ref[j * W:(j + 1) * W, :].astype(f32).T.astype(bf16)
    first_tile = pl.program_id(1) == 0
    ones_rows = jnp.ones((8, 2 * W), bf16)
    scale = hd ** -0.5

    def qblock(n, carry):
        r0 = pl.multiple_of(n * W, W)
        tab = jnp.where(jnp.logical_and(first_tile, n == 0), 1, 0)
        qs = q_ref[pl.ds(r0, W), :] * scale
        vt_band = jnp.concatenate([vt_ref[n], vt_ref[n + 1]], axis=1)
        outs = []
        for hk0 in range(0, SWA_KV_HEADS, SWA_KV_IN_FLIGHT):
            hks = range(hk0, hk0 + SWA_KV_IN_FLIGHT)
            heads = range(hk0 * SWA_GROUP, (hk0 + SWA_KV_IN_FLIGHT) * SWA_GROUP)
            kk = {hk: kb_ref[pl.ds(r0, 2 * W), hk * hd:(hk + 1) * hd] for hk in hks}
            lhs_v = {hk: jnp.concatenate([vt_band[hk * hd:(hk + 1) * hd, :], ones_rows], axis=0) for hk in hks}
            st = [lax.dot_general(kk[h // SWA_GROUP], qs[:, h * hd:(h + 1) * hd], (((1,), (1,)), ((), ())),
                                  preferred_element_type=f32) + biast_ref[tab, h] for h in heads]
            m = [jnp.maximum(jnp.max(t, axis=0, keepdims=True), sink_ref[h]) for t, h in zip(st, heads)]
            pt = [jnp.exp(t - mm).astype(bf16) for t, mm in zip(st, m)]
            ov = [jnp.dot(lhs_v[h // SWA_GROUP], t, preferred_element_type=f32) for t, h in zip(pt, heads)]
            outs += [o[:hd] / (o[hd:hd + 1] + jnp.exp(sink_ref[h] - mm)) for o, mm, h in zip(ov, m, heads)]
        o_ref[pl.ds(r0, W), :] = jnp.concatenate(outs, axis=0).T.astype(o_ref.dtype)
        return carry

    lax.fori_loop(0, tq // W, qblock, 0)


def swa_branch(fat, bias_tab_t, sinks, B, S, tq=512):
    T = B * S
    ns = S // tq
    nb = S // WINDOW
    per = tq // WINDOW
    row = lambda b, s: b * ns + s
    prev = lambda b, s: b * nb + jnp.maximum(s * per - 1, 0)
    kvw = SWA_KV_HEADS * SWA_HEAD_DIM
    full = lambda shp: pl.BlockSpec(shp, lambda b, s: (0,) * len(shp))
    return pl.pallas_call(
        functools.partial(_swa_kernel, tq=tq),
        grid=(B, ns),
        in_specs=[pl.BlockSpec((tq, 1024), lambda b, s: (row(b, s), C_SQ // 1024)),
                  pl.BlockSpec((tq, kvw), lambda b, s: (row(b, s), C_SK // kvw)),
                  pl.BlockSpec((WINDOW, kvw), lambda b, s: (prev(b, s), C_SK // kvw)),
                  pl.BlockSpec((tq, kvw), lambda b, s: (row(b, s), C_SV // kvw)),
                  pl.BlockSpec((WINDOW, kvw), lambda b, s: (prev(b, s), C_SV // kvw)),
                  full((2, SWA_Q_HEADS, 2 * WINDOW, WINDOW)), pl.BlockSpec(memory_space=pltpu.SMEM)],
        out_specs=pl.BlockSpec((tq, 1024), lambda b, s: (row(b, s), 0)),
        out_shape=jax.ShapeDtypeStruct((T, 1024), bf16),
        scratch_shapes=[pltpu.VMEM((tq + WINDOW, kvw), bf16), pltpu.VMEM((per + 1, kvw, WINDOW), bf16)],
        compiler_params=_params(("arbitrary", "arbitrary")),
        name="swa",
    )(fat, fat, fat, fat, fat, bias_tab_t, sinks)


def _load_row_tiles(ref, n, lead=()):
    return jnp.concatenate([ref[lead + (pl.ds(s, n, stride=ROW_TILE), slice(None))] for s in range(ROW_TILE)], axis=1)


def _store_row_tiles(ref, val):
    n = val.shape[0]
    for s in range(ROW_TILE):
        ref[pl.ds(s, n, stride=ROW_TILE), :] = val[:, s * LANES:(s + 1) * LANES]


def _merge_kernel(oa_ref, ob_ref, oc_ref, ga_ref, gb_ref, gc_ref, x_ref, wa_ref, wb_ref, wc_ref, wo_ref,
                  g_ref, b_ref, o_ref, ot_ref, *, alpha):
    ya = jnp.dot(oa_ref[...], wa_ref[...], preferred_element_type=f32)
    yb = jnp.dot(ob_ref[...], wb_ref[...], preferred_element_type=f32)
    yc = jnp.dot(oc_ref[...], wc_ref[...], preferred_element_type=f32)
    mix = (_sigmoid(ga_ref[...].astype(f32)) * ya + _sigmoid(gb_ref[...].astype(f32)) * yb
           + _sigmoid(gc_ref[...].astype(f32)) * yc)
    y = jnp.dot(mix.astype(bf16), wo_ref[...], preferred_element_type=f32)
    x1 = _layer_norm(alpha * x_ref[...] + y, g_ref[...], b_ref[...])
    o_ref[...] = x1
    _store_row_tiles(ot_ref, x1)


def merge_ln(oa, ob, oc, fat, x2d, wa, wb, wc, wo, g, b, alpha, tm=512):
    T = x2d.shape[0]
    act = pl.BlockSpec((tm, 1024), lambda i: (i, 0))
    fatb = lambda cb: pl.BlockSpec((tm, 1024), lambda i: (i, cb))
    wsp = pl.BlockSpec((1024, 1024), lambda i: (0, 0))
    vec = pl.BlockSpec((1, 1024), lambda i: (0, 0))
    return pl.pallas_call(
        functools.partial(_merge_kernel, alpha=alpha),
        grid=(T // tm,),
        in_specs=[act, act, act, fatb(C_MA // 1024), fatb(C_MB // 1024), fatb(C_MC // 1024), act,
                  wsp, wsp, wsp, wsp, vec, vec],
        out_specs=[act, pl.BlockSpec((tm * ROW_TILE, LANES), lambda i: (i, 0))],
        out_shape=[jax.ShapeDtypeStruct((T, 1024), f32), jax.ShapeDtypeStruct((T * ROW_TILE, LANES), f32)],
        compiler_params=_params(("arbitrary",)),
        name="merge_ln",
    )(oa, ob, oc, fat, fat, fat, x2d, wa, wb, wc, wo, g, b)


def _route_kernel(x_ref, rw_ref, rb_ref, gates_ref, eidx_ref, rank_ref, cnt_ref, run_ref, *, tm):
    @pl.when(pl.program_id(0) == 0)
    def _():
        run_ref[...] = jnp.zeros_like(run_ref)

    ng = ROUTE_GROUPS
    n = tm // ng
    grp = range(ng)
    lane = lax.broadcasted_iota(i32, (n, LANES), 1)
    lane_f = lane.astype(f32)
    x = [x_ref[g * n:(g + 1) * n, :] for g in grp]
    xh = [t.astype(bf16) for t in x]
    xl = [(x[g] - xh[g].astype(f32)).astype(bf16) for g in grp]
    hw = [jnp.dot(xh[g], rw_ref[...], preferred_element_type=f32) for g in grp]
    lw = [jnp.dot(xl[g], rw_ref[:, :LANES], preferred_element_type=f32) for g in grp]
    work = [hw[g][:, :LANES] + (hw[g][:, LANES:] + lw[g]) + rb_ref[...] for g in grp]
    vals, idxs, hots = [], [], []
    for _ in range(TOP_K):
        m = [jnp.max(w, axis=-1, keepdims=True) for w in work]
        idx = [jnp.min(jnp.where(work[g] == m[g], lane_f, float(LANES)), axis=-1, keepdims=True)
               for g in grp]
        hot = [lane_f == i for i in idx]
        vals.append(m)
        idxs.append(idx)
        hots.append(hot)
        work = [jnp.where(hot[g], -jnp.inf, work[g]) for g in grp]
    ri = lax.broadcasted_iota(i32, (n, n), 0)
    ci = lax.broadcasted_iota(i32, (n, n), 1)
    tril = jnp.where(ri > ci, 1.0, 0.0).astype(bf16)
    run = run_ref[...]
    for g in grp:
        es = [jnp.exp(vals[k][g] - vals[0][g]) for k in range(TOP_K)]
        den = es[0] + es[1] + es[2] + es[3]
        sel = jnp.zeros((n, LANES), f32)
        for k in range(TOP_K):
            sel = sel + jnp.where(hots[k][g], 1.0, 0.0)
        before = jnp.dot(tril, sel.astype(bf16), preferred_element_type=f32) + run
        run = run + jnp.sum(sel, axis=0, keepdims=True)
        gates = jnp.zeros((n, LANES), f32)
        eidx = jnp.zeros((n, LANES), f32)
        rank = jnp.zeros((n, LANES), f32)
        for k in range(TOP_K):
            rk = jnp.sum(jnp.where(hots[k][g], before, 0.0), axis=-1, keepdims=True)
            gates = jnp.where(lane == k, es[k] / den, gates)
            eidx = jnp.where(lane == k, idxs[k][g], eidx)
            rank = jnp.where(lane == k, rk, rank)
        gates_ref[g * n:(g + 1) * n, :] = gates
        eidx_ref[g * n:(g + 1) * n, :] = eidx.astype(i32)
        rank_ref[g * n:(g + 1) * n, :] = rank.astype(i32)
    run_ref[...] = run
    cnt_ref[...] = run


def route(x2d, rw_pad, rb_pad, tm=512):
    T = x2d.shape[0]
    outb = pl.BlockSpec((tm, LANES), lambda i: (i, 0))
    return pl.pallas_call(
        functools.partial(_route_kernel, tm=tm),
        grid=(T // tm,),
        in_specs=[pl.BlockSpec((tm, 1024), lambda i: (i, 0)),
                  pl.BlockSpec((1024, 2 * LANES), lambda i: (0, 0)),
                  pl.BlockSpec((1, LANES), lambda i: (0, 0))],
        out_specs=[outb, outb, outb, pl.BlockSpec((1, LANES), lambda i: (0, 0))],
        out_shape=[jax.ShapeDtypeStruct((T, LANES), f32), jax.ShapeDtypeStruct((T, LANES), i32),
                   jax.ShapeDtypeStruct((T, LANES), i32), jax.ShapeDtypeStruct((1, LANES), f32)],
        scratch_shapes=[pltpu.VMEM((1, LANES), f32)],
        compiler_params=_params(("arbitrary",)),
        name="route",
    )(x2d, rw_pad, rb_pad)


GU_GROUP = 2 * LANES


def _gu_prep_tile(w_ref, o_ref):
    ri = lax.broadcasted_iota(i32, (GU_GROUP, GU_GROUP), 0)
    ci = lax.broadcasted_iota(i32, (GU_GROUP, GU_GROUP), 1)
    src = jnp.where(ci < LANES, 2 * ci, 2 * (ci - LANES) + 1)
    perm = jnp.where(ri == src, 1.0, 0.0).astype(bf16)
    for g in range(w_ref.shape[3] // GU_GROUP):
        cs = slice(g * GU_GROUP, (g + 1) * GU_GROUP)
        o_ref[0, :, cs] = jnp.dot(w_ref[0, 0, :, cs].astype(bf16), perm, preferred_element_type=f32).astype(bf16)


ROW_UNROLL = 8


def _row_copy(src_ref, src_row, dst_ref, dst_row, sem):
    tile = lambda r: pl.ds(pl.multiple_of(r * ROW_TILE, ROW_TILE), ROW_TILE)
    return pltpu.make_async_copy(src_ref.at[tile(src_row)], dst_ref.at[tile(dst_row)], sem)


def _dispatch_kernel(dest_ref, x_ref, buf_in_ref, w_ref, buf_ref, wo_ref, sem, *, tm, nt, ng):
    del buf_in_ref
    i = pl.program_id(0)

    @pl.when(i < nt)
    def _():
        def issue(g, carry):
            for j in range(ROW_UNROLL):
                r = g * ROW_UNROLL + j
                for k in range(TOP_K):
                    _row_copy(x_ref, r, buf_ref, dest_ref[r * TOP_K + k], sem).start(priority=k % 2)
            return carry

        lax.fori_loop(0, tm // ROW_UNROLL, issue, 0)

    @pl.when(i < ng)
    def _():
        _gu_prep_tile(w_ref, wo_ref)

    @pl.when(i < nt)
    def _():
        def drain(g, carry):
            for j in range(ROW_UNROLL * TOP_K):
                _row_copy(x_ref, 0, buf_ref, 0, sem).wait()
            return carry

        lax.fori_loop(0, tm // ROW_UNROLL, drain, 0)


def dispatch_prep(dest_flat, xt, buf0, w_gu_all, layer, tm=512, tk=512):
    T = xt.shape[0] // ROW_TILE
    _, E, D, N = w_gu_all.shape
    nt = T // tm
    kt = D // tk
    ng = E * kt
    tok = lambda i: jnp.minimum(i, nt - 1)
    gu = lambda i: jnp.minimum(i, ng - 1)
    return pl.pallas_call(
        functools.partial(_dispatch_kernel, tm=tm, nt=nt, ng=ng),
        grid=(max(nt, ng),),
        in_specs=[pl.BlockSpec((tm * TOP_K,), lambda i: (tok(i),), memory_space=pltpu.SMEM),
                  pl.BlockSpec((tm * ROW_TILE, LANES), lambda i: (tok(i), 0)),
                  pl.BlockSpec(memory_space=pl.ANY),
                  pl.BlockSpec((1, 1, tk, N), lambda i: (layer, gu(i) // kt, gu(i) % kt, 0))],
        out_specs=[pl.BlockSpec(memory_space=pl.ANY),
                   pl.BlockSpec((1, tk, N), lambda i: (gu(i) // kt, gu(i) % kt, 0))],
        out_shape=[jax.ShapeDtypeStruct(buf0.shape, buf0.dtype), jax.ShapeDtypeStruct((E, D, N), bf16)],
        scratch_shapes=[pltpu.SemaphoreType.DMA(())],
        input_output_aliases={2: 0},
        compiler_params=_params(("arbitrary",)),
        name="dispatch_prep",
    )(dest_flat, xt, buf0, w_gu_all)


def _expert_kernel(blk_e_ref, nused_ref, x_ref, wgu_ref, bgu_ref, wd_ref, bd_ref, o_ref, wdb_ref):
    i = pl.program_id(0)

    @pl.when(jnp.logical_or(i == 0, blk_e_ref[i] != blk_e_ref[jnp.maximum(i - 1, 0)]))
    def _():
        wdb_ref[...] = wd_ref[0, 0].astype(bf16)

    @pl.when(i < nused_ref[0])
    def _():
        de = wdb_ref.shape[0]
        xb = _load_row_tiles(x_ref, EXPERT_BLK).astype(bf16)
        hgu = jnp.dot(xb, wgu_ref[0], preferred_element_type=f32) + bgu_ref[0]
        acts = []
        for g in range(2 * de // GU_GROUP):
            gate = jnp.minimum(hgu[:, g * GU_GROUP:g * GU_GROUP + LANES], SWIGLU_LIMIT)
            lin = jnp.clip(hgu[:, g * GU_GROUP + LANES:(g + 1) * GU_GROUP], -SWIGLU_LIMIT, SWIGLU_LIMIT)
            acts.append((gate * _sigmoid(SWIGLU_ALPHA * gate) * (lin + 1.0)).astype(bf16))
        act = jnp.concatenate(acts, axis=1)
        _store_row_tiles(o_ref, jnp.dot(act, wdb_ref[...], preferred_element_type=f32) + bd_ref[0])

    @pl.when(i >= nused_ref[0])
    def _():
        o_ref[...] = jnp.zeros_like(o_ref)


def experts(blk_e, nused, buf, wgu, bgu, wd_all, layer, bd):
    D = wd_all.shape[3]
    nblk = buf.shape[0] // (EXPERT_BLK * ROW_TILE)
    de = wd_all.shape[2]
    tile_blk = (EXPERT_BLK * ROW_TILE, LANES)
    live = lambda i, be, nu: jnp.minimum(i, nu[0] - 1)
    grid_spec = pltpu.PrefetchScalarGridSpec(
        num_scalar_prefetch=2,
        grid=(nblk,),
        in_specs=[pl.BlockSpec(tile_blk, lambda i, be, nu: (live(i, be, nu), 0)),
                  pl.BlockSpec((1, D, 2 * de), lambda i, be, nu: (be[i], 0, 0)),
                  pl.BlockSpec((1, 1, 2 * de), lambda i, be, nu: (be[i], 0, 0)),
                  pl.BlockSpec((1, 1, de, D), lambda i, be, nu: (layer, be[i], 0, 0)),
                  pl.BlockSpec((1, 1, D), lambda i, be, nu: (be[i], 0, 0))],
        out_specs=pl.BlockSpec(tile_blk, lambda i, be, nu: (i, 0)),
        scratch_shapes=[pltpu.VMEM((de, D), bf16)],
    )
    return pl.pallas_call(
        _expert_kernel,
        grid_spec=grid_spec,
        out_shape=jax.ShapeDtypeStruct(buf.shape, f32),
        compiler_params=_params(("arbitrary",)),
        name="experts",
    )(blk_e, nused, buf, wgu, bgu, wd_all, bd)


def _combine_kernel(dest_ref, dest_next_ref, gates_ref, x_ref, p_ref, obuf_ref, wg_ref, wp_ref,
                    g2_ref, b2_ref, g3_ref, b3_ref, o_ref, rows_ref, sems, *, tm, alpha):
    i = pl.program_id(0)
    n = pl.num_programs(0)
    slot = i % 2

    def gather(idx_ref, s):
        def issue(g, carry):
            for j in range(ROW_UNROLL):
                r = g * ROW_UNROLL + j
                for k in range(TOP_K):
                    _row_copy(obuf_ref, idx_ref[r * TOP_K + k], rows_ref.at[s, k], r, sems.at[s]).start(priority=k % 2)
            return carry

        lax.fori_loop(0, tm // ROW_UNROLL, issue, 0)

    @pl.when(i == 0)
    def _():
        gather(dest_ref, 0)

    @pl.when(i + 1 < n)
    def _():
        gather(dest_next_ref, 1 - slot)

    def drain(g, carry):
        for j in range(ROW_UNROLL * TOP_K):
            _row_copy(obuf_ref, 0, rows_ref.at[slot, 0], 0, sems.at[slot]).wait()
        return carry

    lax.fori_loop(0, tm // ROW_UNROLL, drain, 0)

    ng = COMBINE_GROUPS
    n = tm // ng
    grp = range(ng)

    def tiles(k, g):
        return jnp.concatenate([rows_ref[slot, k, pl.ds(g * n * ROW_TILE + s, n, stride=ROW_TILE), :]
                                for s in range(ROW_TILE)], axis=1)

    gates = [gates_ref[g * n:(g + 1) * n, :] for g in grp]
    y = [gates[g][:, 0:1] * tiles(0, g) for g in grp]
    for k in range(1, TOP_K):
        y = [y[g] + gates[g][:, k:k + 1] * tiles(k, g) for g in grp]
    x2 = [_layer_norm(alpha * x_ref[g * n:(g + 1) * n, :] + y[g], g2_ref[...], b2_ref[...]) for g in grp]
    gate = [_sigmoid(jnp.dot(x2[g].astype(bf16), wg_ref[...], preferred_element_type=f32)) for g in grp]
    proj = [jnp.dot(p_ref[g * n:(g + 1) * n, :].astype(bf16), wp_ref[...], preferred_element_type=f32) for g in grp]
    for g in grp:
        o_ref[g * n:(g + 1) * n, :] = _layer_norm(alpha * x2[g] + gate[g] * proj[g], g3_ref[...], b3_ref[...])


def combine_ple(dest_flat, gates, x2d, p2d, layer, obuf, wg, wp, g2, b2, g3, b3, alpha, tm=512):
    T = x2d.shape[0]
    nt = T // tm
    act = pl.BlockSpec((tm, 1024), lambda i: (i, 0))
    vec = pl.BlockSpec((1, 1024), lambda i: (0, 0))
    return pl.pallas_call(
        functools.partial(_combine_kernel, tm=tm, alpha=alpha),
        grid=(nt,),
        in_specs=[pl.BlockSpec((tm * TOP_K,), lambda i: (i,), memory_space=pltpu.SMEM),
                  pl.BlockSpec((tm * TOP_K,), lambda i: (jnp.minimum(i + 1, nt - 1),), memory_space=pltpu.SMEM),
                  pl.BlockSpec((tm, LANES), lambda i: (i, 0)),
                  act,
                  pl.BlockSpec((tm, PLE_DIM), lambda i: (layer * nt + i, 0)),
                  pl.BlockSpec(memory_space=pl.ANY),
                  pl.BlockSpec((1024, 1024), lambda i: (0, 0)),
                  pl.BlockSpec((PLE_DIM, 1024), lambda i: (0, 0)),
                  vec, vec, vec, vec],
        out_specs=act,
        out_shape=jax.ShapeDtypeStruct((T, 1024), f32),
        scratch_shapes=[pltpu.VMEM((2, TOP_K, tm * ROW_TILE, LANES), f32), pltpu.SemaphoreType.DMA((2,))],
        compiler_params=_params(("arbitrary",)),
        name="combine_ple",
    )(dest_flat, dest_flat, gates, x2d, p2d, obuf, wg, wp, g2, b2, g3, b3)


def _t5_bucket_np(dist):
    max_exact = REL_BUCKETS // 2
    d = np.maximum(dist.astype(np.float32), np.float32(1.0))
    large = max_exact + (np.log(d / np.float32(max_exact)) / np.float32(math.log(REL_MAX_DISTANCE / max_exact))
                         * np.float32(REL_BUCKETS - max_exact)).astype(np.int32)
    large = np.minimum(large, REL_BUCKETS - 1)
    return np.where(dist < max_exact, dist, large)


def _swa_bias_table(rel_bias):
    dist = np.arange(2 * WINDOW - 1, -WINDOW, -1)
    in_window = (dist >= 0) & (dist < WINDOW)
    per_dist = rel_bias[_t5_bucket_np(np.maximum(dist, 0))].astype(f32).T
    per_dist = jnp.where(jnp.asarray(in_window)[None], per_dist, NEG_BIG)
    n = 3 * WINDOW - 1
    skew = jnp.tile(jnp.pad(per_dist, ((0, 0), (0, 1))), (1, WINDOW))[:, :WINDOW * n].reshape(-1, WINDOW, n)
    bias = jnp.transpose(skew[:, :, WINDOW - 1:], (0, 2, 1))
    first = jnp.where((jnp.arange(2 * WINDOW) >= WINDOW)[None, :, None], bias, NEG_BIG)
    return jnp.stack([bias, first])


def _pad_row(v, width=LANES, fill=0.0):
    v = v.astype(f32).reshape(1, -1)
    return jnp.pad(v, ((0, 0), (0, width - v.shape[1])), constant_values=fill)


def _wcat(w_in):
    cols = [w_in[:, 0:4096], w_in[:, 4112:6160], w_in[:, 6160:7184], w_in[:, 7696:10768],
            w_in[:, 7184:7440], w_in[:, 7440:7696], w_in[:, 4096:4112],
            jnp.zeros((w_in.shape[0], FAT_W - 10768), w_in.dtype)]
    return jnp.concatenate(cols, axis=1).astype(bf16)


def kernel(x, p, w_in, conv_qkv_w, gdn_a_log, gdn_dt_bias, gdn_norm_w, rg_conv_w, rg_conv_b, rg_w_a, rg_b_a, rg_w_x, rg_b_x, rg_lambda, attn_sinks, rel_bias, w_o_gdn, w_o_lru, w_o_swa, w_out, ln1_g, ln1_b, router_w, router_b, w_gu, b_gu, w_down, b_down, ln2_g, ln2_b, ple_w_gate, ple_w_proj, ln3_g, ln3_b):
    B, S, D = x.shape
    depth = w_in.shape[0]
    T = B * S
    A = T * TOP_K
    alpha = (2.0 * depth) ** 0.25
    P = A + N_EXPERTS * EXPERT_BLK
    nblk = P // EXPERT_BLK
    row = lambda v: v.astype(f32).reshape(1, -1)

    bias_tab = _swa_bias_table(rel_bias)
    p2d = p.reshape(depth * T, PLE_DIM)
    xc = x.reshape(T, D)
    for i in range(depth):
        fat = inproj(xc, _wcat(w_in[i]))
        o_gdn, buf0 = gdn_branch(fat, conv_qkv_w[i], _pad_row(gdn_a_log[i]), _pad_row(gdn_dt_bias[i]),
                                 row(gdn_norm_w[i]), B, S, P * ROW_TILE)
        wax = jnp.concatenate([rg_w_a[i], rg_w_x[i]], axis=-1).astype(bf16)
        o_lru = lru_branch(fat, rg_conv_w[i], row(rg_conv_b[i]), wax, row(rg_b_a[i]), row(rg_b_x[i]),
                           row(rg_lambda[i]), B, S)
        o_swa = swa_branch(fat, bias_tab, attn_sinks[i].astype(f32), B, S)
        x1, x1t = merge_ln(o_gdn, o_lru, o_swa, fat, xc, w_o_gdn[i].astype(bf16), w_o_lru[i].astype(bf16),
                           w_o_swa[i].astype(bf16), w_out[i].astype(bf16), row(ln1_g[i]), row(ln1_b[i]), alpha)

        rw_pad = jnp.pad(router_w[i].astype(f32), ((0, 0), (0, LANES - N_EXPERTS)))
        rw_hi = rw_pad.astype(bf16)
        rw_pad = jnp.concatenate([rw_hi, (rw_pad - rw_hi.astype(f32)).astype(bf16)], axis=1)
        rb_pad = _pad_row(router_b[i], fill=NEG_BIG)
        gates, eidx, rank, cnt = route(x1, rw_pad, rb_pad)
        counts = cnt[0, :N_EXPERTS].astype(i32)
        padded = ((counts + EXPERT_BLK - 1) // EXPERT_BLK) * EXPERT_BLK
        pad_ends = jnp.cumsum(padded)
        pad_starts = pad_ends - padded
        hit = eidx[:, :TOP_K, None] == jnp.arange(N_EXPERTS, dtype=i32)
        dest = (jnp.sum(jnp.where(hit, pad_starts, 0), axis=-1) + rank[:, :TOP_K]).reshape(A)
        blk_start = jnp.arange(nblk, dtype=i32) * EXPERT_BLK
        blk_e = jnp.minimum(jnp.sum((pad_ends[None, :] <= blk_start[:, None]).astype(i32), axis=1),
                            N_EXPERTS - 1).astype(i32)
        nused = (pad_ends[-1:] // EXPERT_BLK).astype(i32)

        buf, wgu = dispatch_prep(dest, x1t, buf0, w_gu, i)
        bgu = jnp.transpose(b_gu[i].reshape(N_EXPERTS, -1, LANES, 2), (0, 1, 3, 2)).reshape(N_EXPERTS, 1, -1)
        obuf = experts(blk_e, nused, buf, wgu, bgu, w_down, i, b_down[i][:, None, :])
        xc = combine_ple(dest, gates, x1, p2d, i, obuf, ple_w_gate[i].astype(bf16),
                         ple_w_proj[i].astype(bf16), row(ln2_g[i]), row(ln2_b[i]), row(ln3_g[i]), row(ln3_b[i]),
                         alpha)
    return xc.reshape(B, S, D)
```

```python
import functools
import math

import numpy as np
import jax
import jax.numpy as jnp
from jax import lax
from jax.experimental import pallas as pl
from jax.experimental.pallas import tpu as pltpu

f32 = jnp.float32
bf16 = jnp.bfloat16
i32 = jnp.int32

D_MODEL = 1024
PLE_DIM = 256
GDN_HEADS = 8
GDN_HEAD_DIM = 128
GDN_CHUNK = 64
CONV_WIDTH = 4
LRU_BLOCKS = 8
LRU_BLOCK_DIM = 128
LRU_C = 8.0
SWA_Q_HEADS = 16
SWA_KV_HEADS = 4
SWA_HEAD_DIM = 64
SWA_GROUP = 4
WINDOW = 128
REL_BUCKETS = 32
REL_MAX_DISTANCE = 128
N_EXPERTS = 32
TOP_K = 4
SWIGLU_LIMIT = 7.0
SWIGLU_ALPHA = 1.702
LN_EPS = 1e-5
NORM_EPS = 1e-6
NEG_BIG = -1e30

LANES = 128
VMEM_LIMIT = 56 * 1024 * 1024

C_GQ, C_GK, C_GV, C_GZ = 0, 1024, 2048, 3072
C_LX, C_LG = 4096, 5120
C_SQ = 6144
C_MA, C_MB, C_MC = 7168, 8192, 9216
C_SK, C_SV = 10240, 10496
C_AB = 10752
FAT_W = 10880
FAT_TN = 2176

ROW_TILE = D_MODEL // LANES

EXPERT_BLK = 512
ROUTE_GROUPS = 4
COMBINE_GROUPS = 4
CHUNKS_IN_FLIGHT = 4
SWA_KV_IN_FLIGHT = 4


def _params(sem):
    return pltpu.CompilerParams(dimension_semantics=sem, vmem_limit_bytes=VMEM_LIMIT)


def _sigmoid(x):
    return 1.0 / (1.0 + jnp.exp(-x))


def _softplus(x):
    return jnp.maximum(x, 0.0) + jnp.log(1.0 + jnp.exp(-jnp.abs(x)))


def _layer_norm(z, g, b):
    mu = jnp.mean(z, axis=-1, keepdims=True)
    zc = z - mu
    var = jnp.mean(zc * zc, axis=-1, keepdims=True)
    return zc * lax.rsqrt(var + LN_EPS) * g + b


def _inproj_kernel(x_ref, w_ref, o_ref, xb_ref):
    @pl.when(pl.program_id(1) == 0)
    def _():
        xb_ref[...] = x_ref[...].astype(bf16)

    o_ref[...] = jnp.dot(xb_ref[...], w_ref[...], preferred_element_type=f32).astype(o_ref.dtype)


def inproj(x2d, wcat, tm=2048):
    T, K = x2d.shape
    N = wcat.shape[1]
    tn = FAT_TN
    return pl.pallas_call(
        _inproj_kernel,
        grid=(T // tm, N // tn),
        in_specs=[pl.BlockSpec((tm, K), lambda i, j: (i, 0)),
                  pl.BlockSpec((K, tn), lambda i, j: (0, j))],
        out_specs=pl.BlockSpec((tm, tn), lambda i, j: (i, j)),
        out_shape=jax.ShapeDtypeStruct((T, N), bf16),
        scratch_shapes=[pltpu.VMEM((tm, K), bf16)],
        compiler_params=_params(("arbitrary", "arbitrary")),
        name="inproj",
    )(x2d, wcat)


ZERO_FILL_COPIES = 4


def _zero_fill(step, zero_ref, zbuf_ref, zsem, zero_rows):
    zr = zero_ref.shape[0]
    return [pltpu.make_async_copy(zero_ref, zbuf_ref.at[pl.ds(pl.multiple_of(step * zero_rows + c * zr, zr), zr)], zsem)
            for c in range(zero_rows // zr)]


def _shift_matrices(ts):
    ri = lax.broadcasted_iota(i32, (ts, ts), 0)
    ci = lax.broadcasted_iota(i32, (ts, ts), 1)
    return [jnp.where(ri - ci == d, 1.0, 0.0).astype(bf16) for d in (3, 2, 1)]


def _causal_conv_silu(src_ref, dst_ref, cz_ref, carry_ref, w, smats, ts, head_scale=None):
    xb = src_ref[...]
    x = xb.astype(f32)
    y = w[3:4, :] * x
    for j, sm in enumerate(smats):
        y = y + w[j:j + 1, :] * jnp.dot(sm, xb, preferred_element_type=f32)
    cz_ref[0:8, :] = carry_ref[...]
    cz_ref[8:16, :] = jnp.zeros((8, x.shape[1]), f32)
    corr = w[0:1, :] * cz_ref[5:13, :] + w[1:2, :] * cz_ref[6:14, :] + w[2:3, :] * cz_ref[7:15, :]
    carry_ref[...] = x[ts - 8:ts, :]

    def post(rows, yv):
        a = yv * _sigmoid(yv)
        if head_scale is None:
            dst_ref[rows, :] = a
            return
        for h in range(a.shape[1] // LANES):
            cs = slice(h * LANES, (h + 1) * LANES)
            ah = a[:, cs]
            dst_ref[rows, cs] = ah * (lax.rsqrt(jnp.sum(ah * ah, axis=-1, keepdims=True) + NORM_EPS) * head_scale)

    post(slice(0, 8), y[0:8] + corr)
    post(slice(8, ts), y[8:ts])


def _gdn_kernel(q_ref, k_ref, v_ref, z_ref, ab_ref, cw_ref, alog_ref, dtb_ref, nw_ref, o_ref, zbuf_ref,
                xpad_ref, qs_ref, ks_ref, vs_ref, carry_ref, state_ref, g_ref, beta_ref,
                u_s, lhs_s, intra_s, kdt_s, zero_ref, zsem, *, ts, zero_rows):
    C = GDN_CHUNK
    D = GDN_HEAD_DIM
    P = 2 * C

    step = pl.program_id(0) * pl.num_programs(1) + pl.program_id(1)

    @pl.when(step == 0)
    def _():
        zero_ref[...] = jnp.zeros_like(zero_ref)

    fills = _zero_fill(step, zero_ref, zbuf_ref, zsem, zero_rows)
    for f in fills:
        f.start()

    @pl.when(pl.program_id(1) == 0)
    def _():
        carry_ref[...] = jnp.zeros_like(carry_ref)
        state_ref[...] = jnp.zeros_like(state_ref)

    smats = _shift_matrices(ts)
    for p, (src, dst, scale) in enumerate(((q_ref, qs_ref, D ** -0.5), (k_ref, ks_ref, 1.0), (v_ref, vs_ref, None))):
        _causal_conv_silu(src, dst, xpad_ref, carry_ref.at[p], cw_ref[:, p * 1024:(p + 1) * 1024], smats, ts, scale)

    ab = ab_ref[...].astype(f32)
    g = -jnp.exp(alog_ref[...]) * _softplus(ab + dtb_ref[...])
    rin = lax.broadcasted_iota(i32, (ts, LANES), 0) & (C - 1)
    gsum = g
    for d in (1, 2, 4, 8, 16, 32):
        gsum = gsum + jnp.where(rin >= d, pltpu.roll(gsum, d, 0), 0.0)
    g_ref[...] = gsum
    beta_ref[...] = _sigmoid(ab)

    ri = lax.broadcasted_iota(i32, (P, P), 0)
    ci = lax.broadcasted_iota(i32, (P, P), 1)
    same = (ri >= C) == (ci >= C)
    eye = ri == ci
    causal = same & (ri >= ci)
    strict = same & (ri > ci)
    eye_f = jnp.where(eye, 1.0, 0.0).astype(f32)
    first_cols = ci < C
    nw = nw_ref[...]

    def stack(a, b):
        return jnp.concatenate([a, b], axis=0)

    def mm(a, b):
        return jnp.dot(a, b, preferred_element_type=f32)

    npair = GDN_HEADS // 2
    nchunk = ts // C
    hcols = [slice(h * D, (h + 1) * D) for h in range(GDN_HEADS)]

    for cg in range(0, nchunk, CHUNKS_IN_FLIGHT):
        probs = [(c, hp) for c in range(cg, cg + CHUNKS_IN_FLIGHT) for hp in range(npair)]
        qn, kn, vb, gcol, eg, egl, kb = [], [], [], [], [], [], []
        for c, hp in probs:
            rows = slice(c * C, (c + 1) * C)
            c0, c1 = hcols[2 * hp], hcols[2 * hp + 1]
            gc = g_ref[rows, :]
            bc = beta_ref[rows, :]
            qn.append(stack(qs_ref[rows, c0], qs_ref[rows, c1]))
            kn.append(stack(ks_ref[rows, c0], ks_ref[rows, c1]))
            v2 = stack(vs_ref[rows, c0], vs_ref[rows, c1])
            h0, h1 = 2 * hp, 2 * hp + 1
            gcl = stack(gc[:, h0:h0 + 1], gc[:, h1:h1 + 1])
            bcl = stack(bc[:, 8 + h0:9 + h0], bc[:, 8 + h1:9 + h1])
            glast = stack(jnp.broadcast_to(gc[C - 1:C, h0:h0 + 1], (C, 1)),
                          jnp.broadcast_to(gc[C - 1:C, h1:h1 + 1], (C, 1)))
            gcol.append(gcl)
            eg.append(jnp.exp(gcl))
            egl.append(jnp.exp(glast - gcl))
            kb.append(kn[-1] * bcl)
            vb.append(v2 * bcl)
        n = len(probs)
        a2 = [lax.dot_general(stack(kb[i], qn[i]).astype(bf16), kn[i].astype(bf16), (((1,), (1,)), ((), ())),
                              preferred_element_type=f32) for i in range(n)]
        lmat, intra = [], []
        for i in range(n):
            gm = jnp.broadcast_to(gcol[i], (P, P))
            grow = jnp.sum(jnp.where(eye, gm, 0.0), axis=0, keepdims=True)
            decay = jnp.where(causal, jnp.exp(jnp.minimum(gm - grow, 0.0)), 0.0)
            lmat.append(jnp.where(strict, a2[i][:P] * decay, 0.0))
            intra.append(a2[i][P:] * decay)
        lb = [l.astype(bf16) for l in lmat]
        xm = [eye_f - l for l in lmat]
        pm = [mm(b, b) for b in lb]
        for it in range(5):
            pb = [p.astype(bf16) for p in pm]
            xm = [x + mm(x.astype(bf16), b) for x, b in zip(xm, pb)]
            if it < 4:
                pm = [mm(b, b) for b in pb]
        uw = [mm(xm[i].astype(bf16), jnp.concatenate([vb[i], kb[i] * eg[i]], axis=1).astype(bf16))
              for i in range(n)]
        for i, (c, hp) in enumerate(probs):
            j = c * npair + hp
            w2 = uw[i][:, D:]
            qd = qn[i] * eg[i]
            kdt = (kn[i] * egl[i]).T
            u_s[j] = uw[i][:, :D]
            lhs_s[j, 0] = stack(w2[:C], qd[:C]).astype(bf16)
            lhs_s[j, 1] = stack(w2[C:], qd[C:]).astype(bf16)
            intra_s[j] = intra[i].astype(bf16)
            kdt_s[j, 0] = jnp.where(first_cols, kdt, 0.0).astype(bf16)
            kdt_s[j, 1] = jnp.where(first_cols, 0.0, kdt).astype(bf16)

    for c in range(nchunk):
        rows = slice(c * C, (c + 1) * C)
        gl = g_ref[(c + 1) * C - 1:(c + 1) * C, :]
        st = [state_ref[h] for h in range(GDN_HEADS)]
        wq = [mm(lhs_s[c * npair + h // 2, h % 2], st[h].astype(bf16)) for h in range(GDN_HEADS)]
        vnb = [(u_s[c * npair + hp] - stack(wq[2 * hp][:C], wq[2 * hp + 1][:C])).astype(bf16)
               for hp in range(npair)]
        o2 = [stack(wq[2 * hp][C:], wq[2 * hp + 1][C:]) + mm(intra_s[c * npair + hp], vnb[hp])
              for hp in range(npair)]
        for h in range(GDN_HEADS):
            state_ref[h] = st[h] * jnp.exp(gl[:, h:h + 1]) + mm(kdt_s[c * npair + h // 2, h % 2], vnb[h // 2])
        for hp in range(npair):
            c0, c1 = hcols[2 * hp], hcols[2 * hp + 1]
            z2 = stack(z_ref[rows, c0], z_ref[rows, c1]).astype(f32)
            on = (o2[hp] * lax.rsqrt(jnp.mean(o2[hp] * o2[hp], axis=-1, keepdims=True) + NORM_EPS) * nw
                  * (z2 * _sigmoid(z2))).astype(o_ref.dtype)
            o_ref[rows, c0] = on[:C]
            o_ref[rows, c1] = on[C:]
    for f in fills:
        f.wait()


def gdn_branch(fat, conv_w, a_log_row, dt_bias_row, norm_w_row, B, S, buf_rows, ts=256):
    T = B * S
    ns = S // ts
    nprob = (ts // GDN_CHUNK) * (GDN_HEADS // 2)
    zero_rows = buf_rows // (B * ns)
    assert zero_rows * B * ns == buf_rows and zero_rows % (ZERO_FILL_COPIES * ROW_TILE) == 0
    row = lambda b, s: b * ns + s
    blk = lambda cb: pl.BlockSpec((ts, 1024), lambda b, s: (row(b, s), cb))
    full = lambda shp: pl.BlockSpec(shp, lambda b, s: (0,) * len(shp))
    return pl.pallas_call(
        functools.partial(_gdn_kernel, ts=ts, zero_rows=zero_rows),
        grid=(B, ns),
        in_specs=[blk(C_GQ // 1024), blk(C_GK // 1024), blk(C_GV // 1024), blk(C_GZ // 1024),
                  pl.BlockSpec((ts, LANES), lambda b, s: (row(b, s), C_AB // LANES)),
                  full((CONV_WIDTH, 3072)), full((1, LANES)), full((1, LANES)), full((1, LANES))],
        out_specs=[pl.BlockSpec((ts, 1024), lambda b, s: (row(b, s), 0)), pl.BlockSpec(memory_space=pl.ANY)],
        out_shape=[jax.ShapeDtypeStruct((T, 1024), bf16), jax.ShapeDtypeStruct((buf_rows, LANES), f32)],
        scratch_shapes=[pltpu.VMEM((16, 1024), f32),
                        pltpu.VMEM((ts, 1024), f32), pltpu.VMEM((ts, 1024), f32), pltpu.VMEM((ts, 1024), f32),
                        pltpu.VMEM((3, 8, 1024), f32),
                        pltpu.VMEM((GDN_HEADS, GDN_HEAD_DIM, GDN_HEAD_DIM), f32),
                        pltpu.VMEM((ts, LANES), f32), pltpu.VMEM((ts, LANES), f32),
                        pltpu.VMEM((nprob, 128, GDN_HEAD_DIM), f32),
                        pltpu.VMEM((nprob, 2, 128, GDN_HEAD_DIM), bf16),
                        pltpu.VMEM((nprob, 128, 128), bf16),
                        pltpu.VMEM((nprob, 2, GDN_HEAD_DIM, 128), bf16),
                        pltpu.VMEM((zero_rows // ZERO_FILL_COPIES, LANES), f32), pltpu.SemaphoreType.DMA(())],
        compiler_params=_params(("arbitrary", "arbitrary")),
        name="gdn",
    )(fat, fat, fat, fat, fat, conv_w, a_log_row, dt_bias_row, norm_w_row)


def _lru_kernel(x_ref, gate_ref, cw_ref, cb_ref, wax_ref, ba_ref, bx_ref, lam_ref, o_ref,
                cz_ref, xc_ref, a_ref, u_ref, h_ref, carry_ref, hc_ref, *, ts):
    nb = x_ref.shape[0]

    @pl.when(pl.program_id(0) == 0)
    def _():
        carry_ref[...] = jnp.zeros_like(carry_ref)
        hc_ref[...] = jnp.zeros_like(hc_ref)

    w = cw_ref[...]
    smats = _shift_matrices(ts)
    cz_ref[8:16, :] = jnp.zeros((8, D_MODEL), f32)
    for bi in range(nb):
        xb = x_ref[bi]
        x = xb.astype(f32)
        y = w[3:4, :] * x + cb_ref[...]
        for j, sm in enumerate(smats):
            y = y + w[j:j + 1, :] * jnp.dot(sm, xb, preferred_element_type=f32)
        cz_ref[0:8, :] = carry_ref[bi]
        corr = w[0:1, :] * cz_ref[5:13, :] + w[1:2, :] * cz_ref[6:14, :] + w[2:3, :] * cz_ref[7:15, :]
        carry_ref[bi] = x[ts - 8:ts, :]
        xc_ref[bi * ts:bi * ts + 8, :] = y[0:8] + corr
        xc_ref[bi * ts + 8:(bi + 1) * ts, :] = y[8:ts]

    nsp = _softplus(-lam_ref[...])
    for blk in range(LRU_BLOCKS):
        cs = slice(blk * LRU_BLOCK_DIM, (blk + 1) * LRU_BLOCK_DIM)
        xc = xc_ref[:, cs]
        ri = jnp.dot(xc.astype(bf16), wax_ref[blk], preferred_element_type=f32)
        r = _sigmoid(ri[:, :LRU_BLOCK_DIM] + ba_ref[:, cs])
        i = _sigmoid(ri[:, LRU_BLOCK_DIM:] + bx_ref[:, cs])
        a = jnp.exp(-LRU_C * r * nsp[:, cs])
        u = jnp.sqrt(1.0 - a * a) * (i * xc)
        for bi in range(nb):
            a_ref[blk, pl.ds(bi, ts, stride=nb), :] = a[bi * ts:(bi + 1) * ts]
            u_ref[blk, pl.ds(bi, ts, stride=nb), :] = u[bi * ts:(bi + 1) * ts]

    def step(t, hs):
        view = pl.ds(pl.multiple_of(t * nb, nb), nb)
        out = []
        for blk in range(LRU_BLOCKS):
            h = a_ref[blk, view, :] * hs[blk] + u_ref[blk, view, :]
            h_ref[blk, view, :] = h
            out.append(h)
        return tuple(out)

    h0 = tuple(hc_ref[:, blk * LRU_BLOCK_DIM:(blk + 1) * LRU_BLOCK_DIM] for blk in range(LRU_BLOCKS))
    hT = lax.fori_loop(0, ts, step, h0, unroll=8)
    for blk in range(LRU_BLOCKS):
        cs = slice(blk * LRU_BLOCK_DIM, (blk + 1) * LRU_BLOCK_DIM)
        hc_ref[:, cs] = hT[blk]
        for bi in range(nb):
            hb = h_ref[blk, pl.ds(bi, ts, stride=nb), :]
            o_ref[bi, :, cs] = (hb * jax.nn.gelu(gate_ref[bi, :, cs].astype(f32))).astype(o_ref.dtype)


def lru_branch(fat, conv_w, conv_b, wax, b_a, b_x, lam, B, S, ts=128):
    assert B == 8, "the recurrence keeps the batch on the 8 sublanes of a vreg"
    T = B * S
    fat3 = fat.reshape(B, S, fat.shape[1])
    full = lambda shp: pl.BlockSpec(shp, lambda s: (0,) * len(shp))
    out = pl.pallas_call(
        functools.partial(_lru_kernel, ts=ts),
        grid=(S // ts,),
        in_specs=[pl.BlockSpec((B, ts, 1024), lambda s: (0, s, C_LX // 1024)),
                  pl.BlockSpec((B, ts, 1024), lambda s: (0, s, C_LG // 1024)),
                  full((CONV_WIDTH, 1024)), full((1, 1024)), full((LRU_BLOCKS, LRU_BLOCK_DIM, 2 * LRU_BLOCK_DIM)),
                  full((1, 1024)), full((1, 1024)), full((1, 1024))],
        out_specs=pl.BlockSpec((B, ts, 1024), lambda s: (0, s, 0)),
        out_shape=jax.ShapeDtypeStruct((B, S, 1024), bf16),
        scratch_shapes=[pltpu.VMEM((16, 1024), f32), pltpu.VMEM((B * ts, 1024), f32),
                        pltpu.VMEM((LRU_BLOCKS, B * ts, LRU_BLOCK_DIM), f32),
                        pltpu.VMEM((LRU_BLOCKS, B * ts, LRU_BLOCK_DIM), f32),
                        pltpu.VMEM((LRU_BLOCKS, B * ts, LRU_BLOCK_DIM), f32),
                        pltpu.VMEM((B, 8, 1024), f32), pltpu.VMEM((B, 1024), f32)],
        compiler_params=_params(("arbitrary",)),
        name="lru",
    )(fat3, fat3, conv_w, conv_b, wax, b_a, b_x, lam)
    return out.reshape(T, 1024)


def _swa_kernel(q_ref, kc_ref, kp_ref, vc_ref, vp_ref, biast_ref, sink_ref, o_ref, kb_ref, vt_ref, *, tq):
    W = WINDOW
    hd = SWA_HEAD_DIM
    kb_ref[0:W, :] = kp_ref[...]
    kb_ref[W:W + tq, :] = kc_ref[...]
    vt_ref[0] = vp_ref[...].astype(f32).T.astype(bf16)
    for j in range(tq // W):
        vt_ref[j + 1] = vc_ref[j * W:(j + 1) * W, :].astype(f32).T.astype(bf16)
    first_tile = pl.program_id(1) == 0
    ones_rows = jnp.ones((8, 2 * W), bf16)
    scale = hd ** -0.5

    def qblock(n, carry):
        r0 = pl.multiple_of(n * W, W)
        tab = jnp.where(jnp.logical_and(first_tile, n == 0), 1, 0)
        qs = q_ref[pl.ds(r0, W), :] * scale
        vt_band = jnp.concatenate([vt_ref[n], vt_ref[n + 1]], axis=1)
        outs = []
        for hk0 in range(0, SWA_KV_HEADS, SWA_KV_IN_FLIGHT):
            hks = range(hk0, hk0 + SWA_KV_IN_FLIGHT)
            heads = range(hk0 * SWA_GROUP, (hk0 + SWA_KV_IN_FLIGHT) * SWA_GROUP)
            kk = {hk: kb_ref[pl.ds(r0, 2 * W), hk * hd:(hk + 1) * hd] for hk in hks}
            lhs_v = {hk: jnp.concatenate([vt_band[hk * hd:(hk + 1) * hd, :], ones_rows], axis=0) for hk in hks}
            st = [lax.dot_general(kk[h // SWA_GROUP], qs[:, h * hd:(h + 1) * hd], (((1,), (1,)), ((), ())),
                                  preferred_element_type=f32) + biast_ref[tab, h] for h in heads]
            m = [jnp.maximum(jnp.max(t, axis=0, keepdims=True), sink_ref[h]) for t, h in zip(st, heads)]
            pt = [jnp.exp(t - mm).astype(bf16) for t, mm in zip(st, m)]
            ov = [jnp.dot(lhs_v[h // SWA_GROUP], t, preferred_element_type=f32) for t, h in zip(pt, heads)]
            outs += [o[:hd] / (o[hd:hd + 1] + jnp.exp(sink_ref[h] - mm)) for o, mm, h in zip(ov, m, heads)]
        o_ref[pl.ds(r0, W), :] = jnp.concatenate(outs, axis=0).T.astype(o_ref.dtype)
        return carry

    lax.fori_loop(0, tq // W, qblock, 0)


def swa_branch(fat, bias_tab_t, sinks, B, S, tq=512):
    T = B * S
    ns = S // tq
    nb = S // WINDOW
    per = tq // WINDOW
    row = lambda b, s: b * ns + s
    prev = lambda b, s: b * nb + jnp.maximum(s * per - 1, 0)
    kvw = SWA_KV_HEADS * SWA_HEAD_DIM
    full = lambda shp: pl.BlockSpec(shp, lambda b, s: (0,) * len(shp))
    return pl.pallas_call(
        functools.partial(_swa_kernel, tq=tq),
        grid=(B, ns),
        in_specs=[pl.BlockSpec((tq, 1024), lambda b, s: (row(b, s), C_SQ // 1024)),
                  pl.BlockSpec((tq, kvw), lambda b, s: (row(b, s), C_SK // kvw)),
                  pl.BlockSpec((WINDOW, kvw), lambda b, s: (prev(b, s), C_SK // kvw)),
                  pl.BlockSpec((tq, kvw), lambda b, s: (row(b, s), C_SV // kvw)),
                  pl.BlockSpec((WINDOW, kvw), lambda b, s: (prev(b, s), C_SV // kvw)),
                  full((2, SWA_Q_HEADS, 2 * WINDOW, WINDOW)), pl.BlockSpec(memory_space=pltpu.SMEM)],
        out_specs=pl.BlockSpec((tq, 1024), lambda b, s: (row(b, s), 0)),
        out_shape=jax.ShapeDtypeStruct((T, 1024), bf16),
        scratch_shapes=[pltpu.VMEM((tq + WINDOW, kvw), bf16), pltpu.VMEM((per + 1, kvw, WINDOW), bf16)],
        compiler_params=_params(("arbitrary", "arbitrary")),
        name="swa",
    )(fat, fat, fat, fat, fat, bias_tab_t, sinks)


def _load_row_tiles(ref, n, lead=()):
    return jnp.concatenate([ref[lead + (pl.ds(s, n, stride=ROW_TILE), slice(None))] for s in range(ROW_TILE)], axis=1)


def _store_row_tiles(ref, val):
    n = val.shape[0]
    for s in range(ROW_TILE):
        ref[pl.ds(s, n, stride=ROW_TILE), :] = val[:, s * LANES:(s + 1) * LANES]


def _merge_kernel(oa_ref, ob_ref, oc_ref, ga_ref, gb_ref, gc_ref, x_ref, wa_ref, wb_ref, wc_ref, wo_ref,
                  g_ref, b_ref, o_ref, ot_ref, *, alpha):
    ya = jnp.dot(oa_ref[...], wa_ref[...], preferred_element_type=f32)
    yb = jnp.dot(ob_ref[...], wb_ref[...], preferred_element_type=f32)
    yc = jnp.dot(oc_ref[...], wc_ref[...], preferred_element_type=f32)
    mix = (_sigmoid(ga_ref[...].astype(f32)) * ya + _sigmoid(gb_ref[...].astype(f32)) * yb
           + _sigmoid(gc_ref[...].astype(f32)) * yc)
    y = jnp.dot(mix.astype(bf16), wo_ref[...], preferred_element_type=f32)
    x1 = _layer_norm(alpha * x_ref[...] + y, g_ref[...], b_ref[...])
    o_ref[...] = x1
    _store_row_tiles(ot_ref, x1)


def merge_ln(oa, ob, oc, fat, x2d, wa, wb, wc, wo, g, b, alpha, tm=512):
    T = x2d.shape[0]
    act = pl.BlockSpec((tm, 1024), lambda i: (i, 0))
    fatb = lambda cb: pl.BlockSpec((tm, 1024), lambda i: (i, cb))
    wsp = pl.BlockSpec((1024, 1024), lambda i: (0, 0))
    vec = pl.BlockSpec((1, 1024), lambda i: (0, 0))
    return pl.pallas_call(
        functools.partial(_merge_kernel, alpha=alpha),
        grid=(T // tm,),
        in_specs=[act, act, act, fatb(C_MA // 1024), fatb(C_MB // 1024), fatb(C_MC // 1024), act,
                  wsp, wsp, wsp, wsp, vec, vec],
        out_specs=[act, pl.BlockSpec((tm * ROW_TILE, LANES), lambda i: (i, 0))],
        out_shape=[jax.ShapeDtypeStruct((T, 1024), f32), jax.ShapeDtypeStruct((T * ROW_TILE, LANES), f32)],
        compiler_params=_params(("arbitrary",)),
        name="merge_ln",
    )(oa, ob, oc, fat, fat, fat, x2d, wa, wb, wc, wo, g, b)


def _route_kernel(x_ref, rw_ref, rb_ref, gates_ref, eidx_ref, rank_ref, cnt_ref, run_ref, *, tm):
    @pl.when(pl.program_id(0) == 0)
    def _():
        run_ref[...] = jnp.zeros_like(run_ref)

    ng = ROUTE_GROUPS
    n = tm // ng
    grp = range(ng)
    lane = lax.broadcasted_iota(i32, (n, LANES), 1)
    lane_f = lane.astype(f32)
    x = [x_ref[g * n:(g + 1) * n, :] for g in grp]
    xh = [t.astype(bf16) for t in x]
    xl = [(x[g] - xh[g].astype(f32)).astype(bf16) for g in grp]
    hw = [jnp.dot(xh[g], rw_ref[...], preferred_element_type=f32) for g in grp]
    lw = [jnp.dot(xl[g], rw_ref[:, :LANES], preferred_element_type=f32) for g in grp]
    work = [hw[g][:, :LANES] + (hw[g][:, LANES:] + lw[g]) + rb_ref[...] for g in grp]
    vals, idxs, hots = [], [], []
    for _ in range(TOP_K):
        m = [jnp.max(w, axis=-1, keepdims=True) for w in work]
        idx = [jnp.min(jnp.where(work[g] == m[g], lane_f, float(LANES)), axis=-1, keepdims=True)
               for g in grp]
        hot = [lane_f == i for i in idx]
        vals.append(m)
        idxs.append(idx)
        hots.append(hot)
        work = [jnp.where(hot[g], -jnp.inf, work[g]) for g in grp]
    ri = lax.broadcasted_iota(i32, (n, n), 0)
    ci = lax.broadcasted_iota(i32, (n, n), 1)
    tril = jnp.where(ri > ci, 1.0, 0.0).astype(bf16)
    run = run_ref[...]
    for g in grp:
        es = [jnp.exp(vals[k][g] - vals[0][g]) for k in range(TOP_K)]
        den = es[0] + es[1] + es[2] + es[3]
        sel = jnp.zeros((n, LANES), f32)
        for k in range(TOP_K):
            sel = sel + jnp.where(hots[k][g], 1.0, 0.0)
        before = jnp.dot(tril, sel.astype(bf16), preferred_element_type=f32) + run
        run = run + jnp.sum(sel, axis=0, keepdims=True)
        gates = jnp.zeros((n, LANES), f32)
        eidx = jnp.zeros((n, LANES), f32)
        rank = jnp.zeros((n, LANES), f32)
        for k in range(TOP_K):
            rk = jnp.sum(jnp.where(hots[k][g], before, 0.0), axis=-1, keepdims=True)
            gates = jnp.where(lane == k, es[k] / den, gates)
            eidx = jnp.where(lane == k, idxs[k][g], eidx)
            rank = jnp.where(lane == k, rk, rank)
        gates_ref[g * n:(g + 1) * n, :] = gates
        eidx_ref[g * n:(g + 1) * n, :] = eidx.astype(i32)
        rank_ref[g * n:(g + 1) * n, :] = rank.astype(i32)
    run_ref[...] = run
    cnt_ref[...] = run


def route(x2d, rw_pad, rb_pad, tm=512):
    T = x2d.shape[0]
    outb = pl.BlockSpec((tm, LANES), lambda i: (i, 0))
    return pl.pallas_call(
        functools.partial(_route_kernel, tm=tm),
        grid=(T // tm,),
        in_specs=[pl.BlockSpec((tm, 1024), lambda i: (i, 0)),
                  pl.BlockSpec((1024, 2 * LANES), lambda i: (0, 0)),
                  pl.BlockSpec((1, LANES), lambda i: (0, 0))],
        out_specs=[outb, outb, outb, pl.BlockSpec((1, LANES), lambda i: (0, 0))],
        out_shape=[jax.ShapeDtypeStruct((T, LANES), f32), jax.ShapeDtypeStruct((T, LANES), i32),
                   jax.ShapeDtypeStruct((T, LANES), i32), jax.ShapeDtypeStruct((1, LANES), f32)],
        scratch_shapes=[pltpu.VMEM((1, LANES), f32)],
        compiler_params=_params(("arbitrary",)),
        name="route",
    )(x2d, rw_pad, rb_pad)


GU_GROUP = 2 * LANES


def _gu_prep_tile(w_ref, o_ref):
    ri = lax.broadcasted_iota(i32, (GU_GROUP, GU_GROUP), 0)
    ci = lax.broadcasted_iota(i32, (GU_GROUP, GU_GROUP), 1)
    src = jnp.where(ci < LANES, 2 * ci, 2 * (ci - LANES) + 1)
    perm = jnp.where(ri == src, 1.0, 0.0).astype(bf16)
    for g in range(w_ref.shape[3] // GU_GROUP):
        cs = slice(g * GU_GROUP, (g + 1) * GU_GROUP)
        o_ref[0, :, cs] = jnp.dot(w_ref[0, 0, :, cs].astype(bf16), perm, preferred_element_type=f32).astype(bf16)


ROW_UNROLL = 8


def _row_copy(src_ref, src_row, dst_ref, dst_row, sem):
    tile = lambda r: pl.ds(pl.multiple_of(r * ROW_TILE, ROW_TILE), ROW_TILE)
    return pltpu.make_async_copy(src_ref.at[tile(src_row)], dst_ref.at[tile(dst_row)], sem)


def _dispatch_kernel(dest_ref, x_ref, buf_in_ref, w_ref, buf_ref, wo_ref, sem, *, tm, nt, ng):
    del buf_in_ref
    i = pl.program_id(0)

    @pl.when(i < nt)
    def _():
        def issue(g, carry):
            for j in range(ROW_UNROLL):
                r = g * ROW_UNROLL + j
                for k in range(TOP_K):
                    _row_copy(x_ref, r, buf_ref, dest_ref[r * TOP_K + k], sem).start(priority=k % 2)
            return carry

        lax.fori_loop(0, tm // ROW_UNROLL, issue, 0)

    @pl.when(i < ng)
    def _():
        _gu_prep_tile(w_ref, wo_ref)

    @pl.when(i < nt)
    def _():
        def drain(g, carry):
            for j in range(ROW_UNROLL * TOP_K):
                _row_copy(x_ref, 0, buf_ref, 0, sem).wait()
            return carry

        lax.fori_loop(0, tm // ROW_UNROLL, drain, 0)


def dispatch_prep(dest_flat, xt, buf0, w_gu_all, layer, tm=512, tk=512):
    T = xt.shape[0] // ROW_TILE
    _, E, D, N = w_gu_all.shape
    nt = T // tm
    kt = D // tk
    ng = E * kt
    tok = lambda i: jnp.minimum(i, nt - 1)
    gu = lambda i: jnp.minimum(i, ng - 1)
    return pl.pallas_call(
        functools.partial(_dispatch_kernel, tm=tm, nt=nt, ng=ng),
        grid=(max(nt, ng),),
        in_specs=[pl.BlockSpec((tm * TOP_K,), lambda i: (tok(i),), memory_space=pltpu.SMEM),
                  pl.BlockSpec((tm * ROW_TILE, LANES), lambda i: (tok(i), 0)),
                  pl.BlockSpec(memory_space=pl.ANY),
                  pl.BlockSpec((1, 1, tk, N), lambda i: (layer, gu(i) // kt, gu(i) % kt, 0))],
        out_specs=[pl.BlockSpec(memory_space=pl.ANY),
                   pl.BlockSpec((1, tk, N), lambda i: (gu(i) // kt, gu(i) % kt, 0))],
        out_shape=[jax.ShapeDtypeStruct(buf0.shape, buf0.dtype), jax.ShapeDtypeStruct((E, D, N), bf16)],
        scratch_shapes=[pltpu.SemaphoreType.DMA(())],
        input_output_aliases={2: 0},
        compiler_params=_params(("arbitrary",)),
        name="dispatch_prep",
    )(dest_flat, xt, buf0, w_gu_all)


def _expert_kernel(blk_e_ref, nused_ref, x_ref, wgu_ref, bgu_ref, wd_ref, bd_ref, o_ref, wdb_ref):
    i = pl.program_id(0)

    @pl.when(jnp.logical_or(i == 0, blk_e_ref[i] != blk_e_ref[jnp.maximum(i - 1, 0)]))
    def _():
        wdb_ref[...] = wd_ref[0, 0].astype(bf16)

    @pl.when(i < nused_ref[0])
    def _():
        de = wdb_ref.shape[0]
        xb = _load_row_tiles(x_ref, EXPERT_BLK).astype(bf16)
        hgu = jnp.dot(xb, wgu_ref[0], preferred_element_type=f32) + bgu_ref[0]
        acts = []
        for g in range(2 * de // GU_GROUP):
            gate = jnp.minimum(hgu[:, g * GU_GROUP:g * GU_GROUP + LANES], SWIGLU_LIMIT)
            lin = jnp.clip(hgu[:, g * GU_GROUP + LANES:(g + 1) * GU_GROUP], -SWIGLU_LIMIT, SWIGLU_LIMIT)
            acts.append((gate * _sigmoid(SWIGLU_ALPHA * gate) * (lin + 1.0)).astype(bf16))
        act = jnp.concatenate(acts, axis=1)
        _store_row_tiles(o_ref, jnp.dot(act, wdb_ref[...], preferred_element_type=f32) + bd_ref[0])

    @pl.when(i >= nused_ref[0])
    def _():
        o_ref[...] = jnp.zeros_like(o_ref)


def experts(blk_e, nused, buf, wgu, bgu, wd_all, layer, bd):
    D = wd_all.shape[3]
    nblk = buf.shape[0] // (EXPERT_BLK * ROW_TILE)
    de = wd_all.shape[2]
    tile_blk = (EXPERT_BLK * ROW_TILE, LANES)
    live = lambda i, be, nu: jnp.minimum(i, nu[0] - 1)
    grid_spec = pltpu.PrefetchScalarGridSpec(
        num_scalar_prefetch=2,
        grid=(nblk,),
        in_specs=[pl.BlockSpec(tile_blk, lambda i, be, nu: (live(i, be, nu), 0)),
                  pl.BlockSpec((1, D, 2 * de), lambda i, be, nu: (be[i], 0, 0)),
                  pl.BlockSpec((1, 1, 2 * de), lambda i, be, nu: (be[i], 0, 0)),
                  pl.BlockSpec((1, 1, de, D), lambda i, be, nu: (layer, be[i], 0, 0)),
                  pl.BlockSpec((1, 1, D), lambda i, be, nu: (be[i], 0, 0))],
        out_specs=pl.BlockSpec(tile_blk, lambda i, be, nu: (i, 0)),
        scratch_shapes=[pltpu.VMEM((de, D), bf16)],
    )
    return pl.pallas_call(
        _expert_kernel,
        grid_spec=grid_spec,
        out_shape=jax.ShapeDtypeStruct(buf.shape, f32),
        compiler_params=_params(("arbitrary",)),
        name="experts",
    )(blk_e, nused, buf, wgu, bgu, wd_all, bd)


def _combine_kernel(dest_ref, dest_next_ref, gates_ref, x_ref, p_ref, obuf_ref, wg_ref, wp_ref,
                    g2_ref, b2_ref, g3_ref, b3_ref, o_ref, rows_ref, sems, *, tm, alpha):
    i = pl.program_id(0)
    n = pl.num_programs(0)
    slot = i % 2

    def gather(idx_ref, s):
        def issue(g, carry):
            for j in range(ROW_UNROLL):
                r = g * ROW_UNROLL + j
                for k in range(TOP_K):
                    _row_copy(obuf_ref, idx_ref[r * TOP_K + k], rows_ref.at[s, k], r, sems.at[s]).start(priority=k % 2)
            return carry

        lax.fori_loop(0, tm // ROW_UNROLL, issue, 0)

    @pl.when(i == 0)
    def _():
        gather(dest_ref, 0)

    @pl.when(i + 1 < n)
    def _():
        gather(dest_next_ref, 1 - slot)

    def drain(g, carry):
        for j in range(ROW_UNROLL * TOP_K):
            _row_copy(obuf_ref, 0, rows_ref.at[slot, 0], 0, sems.at[slot]).wait()
        return carry

    lax.fori_loop(0, tm // ROW_UNROLL, drain, 0)

    ng = COMBINE_GROUPS
    n = tm // ng
    grp = range(ng)

    def tiles(k, g):
        return jnp.concatenate([rows_ref[slot, k, pl.ds(g * n * ROW_TILE + s, n, stride=ROW_TILE), :]
                                for s in range(ROW_TILE)], axis=1)

    gates = [gates_ref[g * n:(g + 1) * n, :] for g in grp]
    y = [gates[g][:, 0:1] * tiles(0, g) for g in grp]
    for k in range(1, TOP_K):
        y = [y[g] + gates[g][:, k:k + 1] * tiles(k, g) for g in grp]
    x2 = [_layer_norm(alpha * x_ref[g * n:(g + 1) * n, :] + y[g], g2_ref[...], b2_ref[...]) for g in grp]
    gate = [_sigmoid(jnp.dot(x2[g].astype(bf16), wg_ref[...], preferred_element_type=f32)) for g in grp]
    proj = [jnp.dot(p_ref[g * n:(g + 1) * n, :].astype(bf16), wp_ref[...], preferred_element_type=f32) for g in grp]
    for g in grp:
        o_ref[g * n:(g + 1) * n, :] = _layer_norm(alpha * x2[g] + gate[g] * proj[g], g3_ref[...], b3_ref[...])


def combine_ple(dest_flat, gates, x2d, p2d, layer, obuf, wg, wp, g2, b2, g3, b3, alpha, tm=512):
    T = x2d.shape[0]
    nt = T // tm
    act = pl.BlockSpec((tm, 1024), lambda i: (i, 0))
    vec = pl.BlockSpec((1, 1024), lambda i: (0, 0))
    return pl.pallas_call(
        functools.partial(_combine_kernel, tm=tm, alpha=alpha),
        grid=(nt,),
        in_specs=[pl.BlockSpec((tm * TOP_K,), lambda i: (i,), memory_space=pltpu.SMEM),
                  pl.BlockSpec((tm * TOP_K,), lambda i: (jnp.minimum(i + 1, nt - 1),), memory_space=pltpu.SMEM),
                  pl.BlockSpec((tm, LANES), lambda i: (i, 0)),
                  act,
                  pl.BlockSpec((tm, PLE_DIM), lambda i: (layer * nt + i, 0)),
                  pl.BlockSpec(memory_space=pl.ANY),
                  pl.BlockSpec((1024, 1024), lambda i: (0, 0)),
                  pl.BlockSpec((PLE_DIM, 1024), lambda i: (0, 0)),
                  vec, vec, vec, vec],
        out_specs=act,
        out_shape=jax.ShapeDtypeStruct((T, 1024), f32),
        scratch_shapes=[pltpu.VMEM((2, TOP_K, tm * ROW_TILE, LANES), f32), pltpu.SemaphoreType.DMA((2,))],
        compiler_params=_params(("arbitrary",)),
        name="combine_ple",
    )(dest_flat, dest_flat, gates, x2d, p2d, obuf, wg, wp, g2, b2, g3, b3)


def _t5_bucket_np(dist):
    max_exact = REL_BUCKETS // 2
    d = np.maximum(dist.astype(np.float32), np.float32(1.0))
    large = max_exact + (np.log(d / np.float32(max_exact)) / np.float32(math.log(REL_MAX_DISTANCE / max_exact))
                         * np.float32(REL_BUCKETS - max_exact)).astype(np.int32)
    large = np.minimum(large, REL_BUCKETS - 1)
    return np.where(dist < max_exact, dist, large)


def _swa_bias_table(rel_bias):
    dist = np.arange(2 * WINDOW - 1, -WINDOW, -1)
    in_window = (dist >= 0) & (dist < WINDOW)
    per_dist = rel_bias[_t5_bucket_np(np.maximum(dist, 0))].astype(f32).T
    per_dist = jnp.where(jnp.asarray(in_window)[None], per_dist, NEG_BIG)
    n = 3 * WINDOW - 1
    skew = jnp.tile(jnp.pad(per_dist, ((0, 0), (0, 1))), (1, WINDOW))[:, :WINDOW * n].reshape(-1, WINDOW, n)
    bias = jnp.transpose(skew[:, :, WINDOW - 1:], (0, 2, 1))
    first = jnp.where((jnp.arange(2 * WINDOW) >= WINDOW)[None, :, None], bias, NEG_BIG)
    return jnp.stack([bias, first])


def _pad_row(v, width=LANES, fill=0.0):
    v = v.astype(f32).reshape(1, -1)
    return jnp.pad(v, ((0, 0), (0, width - v.shape[1])), constant_values=fill)


def _wcat(w_in):
    cols = [w_in[:, 0:4096], w_in[:, 4112:6160], w_in[:, 6160:7184], w_in[:, 7696:10768],
            w_in[:, 7184:7440], w_in[:, 7440:7696], w_in[:, 4096:4112],
            jnp.zeros((w_in.shape[0], FAT_W - 10768), w_in.dtype)]
    return jnp.concatenate(cols, axis=1).astype(bf16)


def kernel(x, p, w_in, conv_qkv_w, gdn_a_log, gdn_dt_bias, gdn_norm_w, rg_conv_w, rg_conv_b, rg_w_a, rg_b_a, rg_w_x, rg_b_x, rg_lambda, attn_sinks, rel_bias, w_o_gdn, w_o_lru, w_o_swa, w_out, ln1_g, ln1_b, router_w, router_b, w_gu, b_gu, w_down, b_down, ln2_g, ln2_b, ple_w_gate, ple_w_proj, ln3_g, ln3_b):
    B, S, D = x.shape
    depth = w_in.shape[0]
    T = B * S
    A = T * TOP_K
    alpha = (2.0 * depth) ** 0.25
    P = A + N_EXPERTS * EXPERT_BLK
    nblk = P // EXPERT_BLK
    row = lambda v: v.astype(f32).reshape(1, -1)

    bias_tab = _swa_bias_table(rel_bias)
    p2d = p.reshape(depth * T, PLE_DIM)
    xc = x.reshape(T, D)
    for i in range(depth):
        fat = inproj(xc, _wcat(w_in[i]))
        o_gdn, buf0 = gdn_branch(fat, conv_qkv_w[i], _pad_row(gdn_a_log[i]), _pad_row(gdn_dt_bias[i]),
                                 row(gdn_norm_w[i]), B, S, P * ROW_TILE)
        wax = jnp.concatenate([rg_w_a[i], rg_w_x[i]], axis=-1).astype(bf16)
        o_lru = lru_branch(fat, rg_conv_w[i], row(rg_conv_b[i]), wax, row(rg_b_a[i]), row(rg_b_x[i]),
                           row(rg_lambda[i]), B, S)
        o_swa = swa_branch(fat, bias_tab, attn_sinks[i].astype(f32), B, S)
        x1, x1t = merge_ln(o_gdn, o_lru, o_swa, fat, xc, w_o_gdn[i].astype(bf16), w_o_lru[i].astype(bf16),
                           w_o_swa[i].astype(bf16), w_out[i].astype(bf16), row(ln1_g[i]), row(ln1_b[i]), alpha)

        rw_pad = jnp.pad(router_w[i].astype(f32), ((0, 0), (0, LANES - N_EXPERTS)))
        rw_hi = rw_pad.astype(bf16)
        rw_pad = jnp.concatenate([rw_hi, (rw_pad - rw_hi.astype(f32)).astype(bf16)], axis=1)
        rb_pad = _pad_row(router_b[i], fill=NEG_BIG)
        gates, eidx, rank, cnt = route(x1, rw_pad, rb_pad)
        counts = cnt[0, :N_EXPERTS].astype(i32)
        padded = ((counts + EXPERT_BLK - 1) // EXPERT_BLK) * EXPERT_BLK
        pad_ends = jnp.cumsum(padded)
        pad_starts = pad_ends - padded
        hit = eidx[:, :TOP_K, None] == jnp.arange(N_EXPERTS, dtype=i32)
        dest = (jnp.sum(jnp.where(hit, pad_starts, 0), axis=-1) + rank[:, :TOP_K]).reshape(A)
        blk_start = jnp.arange(nblk, dtype=i32) * EXPERT_BLK
        blk_e = jnp.minimum(jnp.sum((pad_ends[None, :] <= blk_start[:, None]).astype(i32), axis=1),
                            N_EXPERTS - 1).astype(i32)
        nused = (pad_ends[-1:] // EXPERT_BLK).astype(i32)

        buf, wgu = dispatch_prep(dest, x1t, buf0, w_gu, i)
        bgu = jnp.transpose(b_gu[i].reshape(N_EXPERTS, -1, LANES, 2), (0, 1, 3, 2)).reshape(N_EXPERTS, 1, -1)
        obuf = experts(blk_e, nused, buf, wgu, bgu, w_down, i, b_down[i][:, None, :])
        xc = combine_ple(dest, gates, x1, p2d, i, obuf, ple_w_gate[i].astype(bf16),
                         ple_w_proj[i].astype(bf16), row(ln2_g[i]), row(ln2_b[i]), row(ln3_g[i]), row(ln3_b[i]),
                         alpha)
    return xc.reshape(B, S, D)
```

```python
import functools
import math

import numpy as np
import jax
import jax.numpy as jnp
from jax import lax
from jax.experimental import pallas as pl
from jax.experimental.pallas import tpu as pltpu

f32 = jnp.float32
bf16 = jnp.bfloat16
i32 = jnp.int32

D_MODEL = 1024
PLE_DIM = 256
GDN_HEADS = 8
GDN_HEAD_DIM = 128
GDN_CHUNK = 64
CONV_WIDTH = 4
LRU_BLOCKS = 8
LRU_BLOCK_DIM = 128
LRU_C = 8.0
SWA_Q_HEADS = 16
SWA_KV_HEADS = 4
SWA_HEAD_DIM = 64
SWA_GROUP = 4
WINDOW = 128
REL_BUCKETS = 32
REL_MAX_DISTANCE = 128
N_EXPERTS = 32
TOP_K = 4
SWIGLU_LIMIT = 7.0
SWIGLU_ALPHA = 1.702
LN_EPS = 1e-5
NORM_EPS = 1e-6
NEG_BIG = -1e30

LANES = 128
VMEM_LIMIT = 56 * 1024 * 1024

C_GQ, C_GK, C_GV, C_GZ = 0, 1024, 2048, 3072
C_LX, C_LG = 4096, 5120
C_SQ = 6144
C_MA, C_MB, C_MC = 7168, 8192, 9216
C_SK, C_SV = 10240, 10496
C_AB = 10752
FAT_W = 10880
FAT_TN = 2176

ROW_TILE = D_MODEL // LANES

EXPERT_BLK = 512
MERGE_GROUPS = 2
ROUTE_GROUPS = 4
COMBINE_GROUPS = 4
CHUNKS_IN_FLIGHT = 4
SWA_KV_IN_FLIGHT = 4


def _params(sem):
    return pltpu.CompilerParams(dimension_semantics=sem, vmem_limit_bytes=VMEM_LIMIT)


def _sigmoid(x):
    return 1.0 / (1.0 + jnp.exp(-x))


def _softplus(x):
    return jnp.maximum(x, 0.0) + jnp.log(1.0 + jnp.exp(-jnp.abs(x)))


def _layer_norm(z, g, b):
    mu = jnp.mean(z, axis=-1, keepdims=True)
    zc = z - mu
    var = jnp.mean(zc * zc, axis=-1, keepdims=True)
    return zc * lax.rsqrt(var + LN_EPS) * g + b


def _inproj_kernel(x_ref, w_ref, o_ref, xb_ref):
    @pl.when(pl.program_id(1) == 0)
    def _():
        xb_ref[...] = x_ref[...].astype(bf16)

    o_ref[...] = jnp.dot(xb_ref[...], w_ref[...], preferred_element_type=f32).astype(o_ref.dtype)


def inproj(x2d, wcat, tm=2048):
    T, K = x2d.shape
    N = wcat.shape[1]
    tn = FAT_TN
    return pl.pallas_call(
        _inproj_kernel,
        grid=(T // tm, N // tn),
        in_specs=[pl.BlockSpec((tm, K), lambda i, j: (i, 0)),
                  pl.BlockSpec((K, tn), lambda i, j: (0, j))],
        out_specs=pl.BlockSpec((tm, tn), lambda i, j: (i, j)),
        out_shape=jax.ShapeDtypeStruct((T, N), bf16),
        scratch_shapes=[pltpu.VMEM((tm, K), bf16)],
        compiler_params=_params(("arbitrary", "arbitrary")),
        name="inproj",
    )(x2d, wcat)


ZERO_FILL_COPIES = 4


def _zero_fill(step, zero_ref, zbuf_ref, zsem, zero_rows):
    zr = zero_ref.shape[0]
    return [pltpu.make_async_copy(zero_ref, zbuf_ref.at[pl.ds(pl.multiple_of(step * zero_rows + c * zr, zr), zr)], zsem)
            for c in range(zero_rows // zr)]


def _shift_matrices(ts):
    ri = lax.broadcasted_iota(i32, (ts, ts), 0)
    ci = lax.broadcasted_iota(i32, (ts, ts), 1)
    return [jnp.where(ri - ci == d, 1.0, 0.0).astype(bf16) for d in (3, 2, 1)]


def _causal_conv_silu(src_ref, dst_ref, cz_ref, carry_ref, w, smats, ts, head_scale=None):
    xb = src_ref[...]
    x = xb.astype(f32)
    y = w[3:4, :] * x
    for j, sm in enumerate(smats):
        y = y + w[j:j + 1, :] * jnp.dot(sm, xb, preferred_element_type=f32)
    cz_ref[0:8, :] = carry_ref[...]
    cz_ref[8:16, :] = jnp.zeros((8, x.shape[1]), f32)
    corr = w[0:1, :] * cz_ref[5:13, :] + w[1:2, :] * cz_ref[6:14, :] + w[2:3, :] * cz_ref[7:15, :]
    carry_ref[...] = x[ts - 8:ts, :]

    def post(rows, yv):
        a = yv * _sigmoid(yv)
        if head_scale is None:
            dst_ref[rows, :] = a
            return
        for h in range(a.shape[1] // LANES):
            cs = slice(h * LANES, (h + 1) * LANES)
            ah = a[:, cs]
            dst_ref[rows, cs] = ah * (lax.rsqrt(jnp.sum(ah * ah, axis=-1, keepdims=True) + NORM_EPS) * head_scale)

    post(slice(0, 8), y[0:8] + corr)
    post(slice(8, ts), y[8:ts])


def _gdn_kernel(q_ref, k_ref, v_ref, z_ref, ab_ref, cw_ref, alog_ref, dtb_ref, nw_ref, o_ref, zbuf_ref,
                xpad_ref, qs_ref, ks_ref, vs_ref, carry_ref, state_ref, g_ref, beta_ref,
                u_s, lhs_s, intra_s, kdt_s, zero_ref, zsem, *, ts, zero_rows):
    C = GDN_CHUNK
    D = GDN_HEAD_DIM
    P = 2 * C

    step = pl.program_id(0) * pl.num_programs(1) + pl.program_id(1)

    @pl.when(step == 0)
    def _():
        zero_ref[...] = jnp.zeros_like(zero_ref)

    fills = _zero_fill(step, zero_ref, zbuf_ref, zsem, zero_rows)
    for f in fills:
        f.start()

    @pl.when(pl.program_id(1) == 0)
    def _():
        carry_ref[...] = jnp.zeros_like(carry_ref)
        state_ref[...] = jnp.zeros_like(state_ref)

    smats = _shift_matrices(ts)
    for p, (src, dst, scale) in enumerate(((q_ref, qs_ref, D ** -0.5), (k_ref, ks_ref, 1.0), (v_ref, vs_ref, None))):
        _causal_conv_silu(src, dst, xpad_ref, carry_ref.at[p], cw_ref[:, p * 1024:(p + 1) * 1024], smats, ts, scale)

    ab = ab_ref[...].astype(f32)
    g = -jnp.exp(alog_ref[...]) * _softplus(ab + dtb_ref[...])
    rin = lax.broadcasted_iota(i32, (ts, LANES), 0) & (C - 1)
    gsum = g
    for d in (1, 2, 4, 8, 16, 32):
        gsum = gsum + jnp.where(rin >= d, pltpu.roll(gsum, d, 0), 0.0)
    g_ref[...] = gsum
    beta_ref[...] = _sigmoid(ab)

    ri = lax.broadcasted_iota(i32, (P, P), 0)
    ci = lax.broadcasted_iota(i32, (P, P), 1)
    same = (ri >= C) == (ci >= C)
    eye = ri == ci
    causal = same & (ri >= ci)
    strict = same & (ri > ci)
    eye_f = jnp.where(eye, 1.0, 0.0).astype(f32)
    first_cols = ci < C
    nw = nw_ref[...]

    def stack(a, b):
        return jnp.concatenate([a, b], axis=0)

    def mm(a, b):
        return jnp.dot(a, b, preferred_element_type=f32)

    npair = GDN_HEADS // 2
    nchunk = ts // C
    hcols = [slice(h * D, (h + 1) * D) for h in range(GDN_HEADS)]

    for cg in range(0, nchunk, CHUNKS_IN_FLIGHT):
        probs = [(c, hp) for c in range(cg, cg + CHUNKS_IN_FLIGHT) for hp in range(npair)]
        qn, kn, vb, gcol, eg, egl, kb = [], [], [], [], [], [], []
        for c, hp in probs:
            rows = slice(c * C, (c + 1) * C)
            c0, c1 = hcols[2 * hp], hcols[2 * hp + 1]
            gc = g_ref[rows, :]
            bc = beta_ref[rows, :]
            qn.append(stack(qs_ref[rows, c0], qs_ref[rows, c1]))
            kn.append(stack(ks_ref[rows, c0], ks_ref[rows, c1]))
            v2 = stack(vs_ref[rows, c0], vs_ref[rows, c1])
            h0, h1 = 2 * hp, 2 * hp + 1
            gcl = stack(gc[:, h0:h0 + 1], gc[:, h1:h1 + 1])
            bcl = stack(bc[:, 8 + h0:9 + h0], bc[:, 8 + h1:9 + h1])
            glast = stack(jnp.broadcast_to(gc[C - 1:C, h0:h0 + 1], (C, 1)),
                          jnp.broadcast_to(gc[C - 1:C, h1:h1 + 1], (C, 1)))
            gcol.append(gcl)
            eg.append(jnp.exp(gcl))
            egl.append(jnp.exp(glast - gcl))
            kb.append(kn[-1] * bcl)
            vb.append(v2 * bcl)
        n = len(probs)
        a2 = [lax.dot_general(stack(kb[i], qn[i]).astype(bf16), kn[i].astype(bf16), (((1,), (1,)), ((), ())),
                              preferred_element_type=f32) for i in range(n)]
        lmat, intra = [], []
        for i in range(n):
            gm = jnp.broadcast_to(gcol[i], (P, P))
            grow = jnp.sum(jnp.where(eye, gm, 0.0), axis=0, keepdims=True)
            decay = jnp.where(causal, jnp.exp(jnp.minimum(gm - grow, 0.0)), 0.0)
            lmat.append(jnp.where(strict, a2[i][:P] * decay, 0.0))
            intra.append(a2[i][P:] * decay)
        lb = [l.astype(bf16) for l in lmat]
        xm = [eye_f - l for l in lmat]
        pm = [mm(b, b) for b in lb]
        for it in range(5):
            pb = [p.astype(bf16) for p in pm]
            xm = [x + mm(x.astype(bf16), b) for x, b in zip(xm, pb)]
            if it < 4:
                pm = [mm(b, b) for b in pb]
        uw = [mm(xm[i].astype(bf16), jnp.concatenate([vb[i], kb[i] * eg[i]], axis=1).astype(bf16))
              for i in range(n)]
        for i, (c, hp) in enumerate(probs):
            j = c * npair + hp
            w2 = uw[i][:, D:]
            qd = qn[i] * eg[i]
            kdt = (kn[i] * egl[i]).T
            u_s[j] = uw[i][:, :D]
            lhs_s[j, 0] = stack(w2[:C], qd[:C]).astype(bf16)
            lhs_s[j, 1] = stack(w2[C:], qd[C:]).astype(bf16)
            intra_s[j] = intra[i].astype(bf16)
            kdt_s[j, 0] = jnp.where(first_cols, kdt, 0.0).astype(bf16)
            kdt_s[j, 1] = jnp.where(first_cols, 0.0, kdt).astype(bf16)

    for c in range(nchunk):
        rows = slice(c * C, (c + 1) * C)
        gl = g_ref[(c + 1) * C - 1:(c + 1) * C, :]
        st = [state_ref[h] for h in range(GDN_HEADS)]
        wq = [mm(lhs_s[c * npair + h // 2, h % 2], st[h].astype(bf16)) for h in range(GDN_HEADS)]
        vnb = [(u_s[c * npair + hp] - stack(wq[2 * hp][:C], wq[2 * hp + 1][:C])).astype(bf16)
               for hp in range(npair)]
        o2 = [stack(wq[2 * hp][C:], wq[2 * hp + 1][C:]) + mm(intra_s[c * npair + hp], vnb[hp])
              for hp in range(npair)]
        for h in range(GDN_HEADS):
            state_ref[h] = st[h] * jnp.exp(gl[:, h:h + 1]) + mm(kdt_s[c * npair + h // 2, h % 2], vnb[h // 2])
        for hp in range(npair):
            c0, c1 = hcols[2 * hp], hcols[2 * hp + 1]
            z2 = stack(z_ref[rows, c0], z_ref[rows, c1]).astype(f32)
            on = (o2[hp] * lax.rsqrt(jnp.mean(o2[hp] * o2[hp], axis=-1, keepdims=True) + NORM_EPS) * nw
                  * (z2 * _sigmoid(z2))).astype(o_ref.dtype)
            o_ref[rows, c0] = on[:C]
            o_ref[rows, c1] = on[C:]
    for f in fills:
        f.wait()


def gdn_branch(fat, conv_w, a_log_row, dt_bias_row, norm_w_row, B, S, buf_rows, ts=256):
    T = B * S
    ns = S // ts
    nprob = (ts // GDN_CHUNK) * (GDN_HEADS // 2)
    zero_rows = buf_rows // (B * ns)
    assert zero_rows * B * ns == buf_rows and zero_rows % (ZERO_FILL_COPIES * ROW_TILE) == 0
    row = lambda b, s: b * ns + s
    blk = lambda cb: pl.BlockSpec((ts, 1024), lambda b, s: (row(b, s), cb))
    full = lambda shp: pl.BlockSpec(shp, lambda b, s: (0,) * len(shp))
    return pl.pallas_call(
        functools.partial(_gdn_kernel, ts=ts, zero_rows=zero_rows),
        grid=(B, ns),
        in_specs=[blk(C_GQ // 1024), blk(C_GK // 1024), blk(C_GV // 1024), blk(C_GZ // 1024),
                  pl.BlockSpec((ts, LANES), lambda b, s: (row(b, s), C_AB // LANES)),
                  full((CONV_WIDTH, 3072)), full((1, LANES)), full((1, LANES)), full((1, LANES))],
        out_specs=[pl.BlockSpec((ts, 1024), lambda b, s: (row(b, s), 0)), pl.BlockSpec(memory_space=pl.ANY)],
        out_shape=[jax.ShapeDtypeStruct((T, 1024), bf16), jax.ShapeDtypeStruct((buf_rows, LANES), f32)],
        scratch_shapes=[pltpu.VMEM((16, 1024), f32),
                        pltpu.VMEM((ts, 1024), f32), pltpu.VMEM((ts, 1024), f32), pltpu.VMEM((ts, 1024), f32),
                        pltpu.VMEM((3, 8, 1024), f32),
                        pltpu.VMEM((GDN_HEADS, GDN_HEAD_DIM, GDN_HEAD_DIM), f32),
                        pltpu.VMEM((ts, LANES), f32), pltpu.VMEM((ts, LANES), f32),
                        pltpu.VMEM((nprob, 128, GDN_HEAD_DIM), f32),
                        pltpu.VMEM((nprob, 2, 128, GDN_HEAD_DIM), bf16),
                        pltpu.VMEM((nprob, 128, 128), bf16),
                        pltpu.VMEM((nprob, 2, GDN_HEAD_DIM, 128), bf16),
                        pltpu.VMEM((zero_rows // ZERO_FILL_COPIES, LANES), f32), pltpu.SemaphoreType.DMA(())],
        compiler_params=_params(("arbitrary", "arbitrary")),
        name="gdn",
    )(fat, fat, fat, fat, fat, conv_w, a_log_row, dt_bias_row, norm_w_row)


def _lru_kernel(x_ref, gate_ref, cw_ref, cb_ref, wax_ref, ba_ref, bx_ref, lam_ref, o_ref,
                cz_ref, xc_ref, a_ref, u_ref, h_ref, carry_ref, hc_ref, *, ts):
    nb = x_ref.shape[0]

    @pl.when(pl.program_id(0) == 0)
    def _():
        carry_ref[...] = jnp.zeros_like(carry_ref)
        hc_ref[...] = jnp.zeros_like(hc_ref)

    w = cw_ref[...]
    smats = _shift_matrices(ts)
    cz_ref[8:16, :] = jnp.zeros((8, D_MODEL), f32)
    for bi in range(nb):
        xb = x_ref[bi]
        x = xb.astype(f32)
        y = w[3:4, :] * x + cb_ref[...]
        for j, sm in enumerate(smats):
            y = y + w[j:j + 1, :] * jnp.dot(sm, xb, preferred_element_type=f32)
        cz_ref[0:8, :] = carry_ref[bi]
        corr = w[0:1, :] * cz_ref[5:13, :] + w[1:2, :] * cz_ref[6:14, :] + w[2:3, :] * cz_ref[7:15, :]
        carry_ref[bi] = x[ts - 8:ts, :]
        xc_ref[bi * ts:bi * ts + 8, :] = y[0:8] + corr
        xc_ref[bi * ts + 8:(bi + 1) * ts, :] = y[8:ts]

    la2 = (-LRU_C / math.log(2.0)) * _softplus(-lam_ref[...])
    for blk in range(LRU_BLOCKS):
        cs = slice(blk * LRU_BLOCK_DIM, (blk + 1) * LRU_BLOCK_DIM)
        xc = xc_ref[:, cs]
        ri = jnp.dot(xc.astype(bf16), wax_ref[blk], preferred_element_type=f32)
        r = _sigmoid(ri[:, :LRU_BLOCK_DIM] + ba_ref[:, cs])
        i = _sigmoid(ri[:, LRU_BLOCK_DIM:] + bx_ref[:, cs])
        a = jnp.exp2(r * la2[:, cs])
        u = jnp.sqrt(1.0 - a * a) * (i * xc)
        for bi in range(nb):
            a_ref[blk, pl.ds(bi, ts, stride=nb), :] = a[bi * ts:(bi + 1) * ts]
            u_ref[blk, pl.ds(bi, ts, stride=nb), :] = u[bi * ts:(bi + 1) * ts]

    def step(t, hs):
        view = pl.ds(pl.multiple_of(t * nb, nb), nb)
        out = []
        for blk in range(LRU_BLOCKS):
            h = a_ref[blk, view, :] * hs[blk] + u_ref[blk, view, :]
            h_ref[blk, view, :] = h
            out.append(h)
        return tuple(out)

    h0 = tuple(hc_ref[:, blk * LRU_BLOCK_DIM:(blk + 1) * LRU_BLOCK_DIM] for blk in range(LRU_BLOCKS))
    hT = lax.fori_loop(0, ts, step, h0, unroll=8)
    for blk in range(LRU_BLOCKS):
        cs = slice(blk * LRU_BLOCK_DIM, (blk + 1) * LRU_BLOCK_DIM)
        hc_ref[:, cs] = hT[blk]
        for bi in range(nb):
            hb = h_ref[blk, pl.ds(bi, ts, stride=nb), :]
            o_ref[bi, :, cs] = (hb * jax.nn.gelu(gate_ref[bi, :, cs].astype(f32))).astype(o_ref.dtype)


def lru_branch(fat, conv_w, conv_b, wax, b_a, b_x, lam, B, S, ts=128):
    assert B == 8, "the recurrence keeps the batch on the 8 sublanes of a vreg"
    T = B * S
    fat3 = fat.reshape(B, S, fat.shape[1])
    full = lambda shp: pl.BlockSpec(shp, lambda s: (0,) * len(shp))
    out = pl.pallas_call(
        functools.partial(_lru_kernel, ts=ts),
        grid=(S // ts,),
        in_specs=[pl.BlockSpec((B, ts, 1024), lambda s: (0, s, C_LX // 1024)),
                  pl.BlockSpec((B, ts, 1024), lambda s: (0, s, C_LG // 1024)),
                  full((CONV_WIDTH, 1024)), full((1, 1024)), full((LRU_BLOCKS, LRU_BLOCK_DIM, 2 * LRU_BLOCK_DIM)),
                  full((1, 1024)), full((1, 1024)), full((1, 1024))],
        out_specs=pl.BlockSpec((B, ts, 1024), lambda s: (0, s, 0)),
        out_shape=jax.ShapeDtypeStruct((B, S, 1024), bf16),
        scratch_shapes=[pltpu.VMEM((16, 1024), f32), pltpu.VMEM((B * ts, 1024), f32),
                        pltpu.VMEM((LRU_BLOCKS, B * ts, LRU_BLOCK_DIM), f32),
                        pltpu.VMEM((LRU_BLOCKS, B * ts, LRU_BLOCK_DIM), f32),
                        pltpu.VMEM((LRU_BLOCKS, B * ts, LRU_BLOCK_DIM), f32),
                        pltpu.VMEM((B, 8, 1024), f32), pltpu.VMEM((B, 1024), f32)],
        compiler_params=_params(("arbitrary",)),
        name="lru",
    )(fat3, fat3, conv_w, conv_b, wax, b_a, b_x, lam)
    return out.reshape(T, 1024)


def _swa_kernel(q_ref, kc_ref, kp_ref, vc_ref, vp_ref, biast_ref, sink_ref, o_ref, kb_ref, vt_ref, *, tq):
    W = WINDOW
    hd = SWA_HEAD_DIM
    kb_ref[0:W, :] = kp_ref[...]
    kb_ref[W:W + tq, :] = kc_ref[...]
    vt_ref[0] = vp_ref[...].astype(f32).T.astype(bf16)
    for j in range(tq // W):
        vt_ref[j + 1] = vc_ref[j * W:(j + 1) * W, :].astype(f32).T.astype(bf16)
    first_tile = pl.program_id(1) == 0
    ones_rows = jnp.ones((8, 2 * W), bf16)
    scale = hd ** -0.5

    def qblock(n, carry):
        r0 = pl.multiple_of(n * W, W)
        tab = jnp.where(jnp.logical_and(first_tile, n == 0), 1, 0)
        qs = q_ref[pl.ds(r0, W), :] * scale
        vt_band = jnp.concatenate([vt_ref[n], vt_ref[n + 1]], axis=1)
        outs = []
        for hk0 in range(0, SWA_KV_HEADS, SWA_KV_IN_FLIGHT):
            hks = range(hk0, hk0 + SWA_KV_IN_FLIGHT)
            heads = range(hk0 * SWA_GROUP, (hk0 + SWA_KV_IN_FLIGHT) * SWA_GROUP)
            kk = {hk: kb_ref[pl.ds(r0, 2 * W), hk * hd:(hk + 1) * hd] for hk in hks}
            lhs_v = {hk: jnp.concatenate([vt_band[hk * hd:(hk + 1) * hd, :], ones_rows], axis=0) for hk in hks}
            st = [lax.dot_general(kk[h // SWA_GROUP], qs[:, h * hd:(h + 1) * hd], (((1,), (1,)), ((), ())),
                                  preferred_element_type=f32) + biast_ref[tab, h] for h in heads]
            m = [jnp.maximum(jnp.max(t, axis=0, keepdims=True), sink_ref[h]) for t, h in zip(st, heads)]
            pt = [jnp.exp(t - mm).astype(bf16) for t, mm in zip(st, m)]
            ov = [jnp.dot(lhs_v[h // SWA_GROUP], t, preferred_element_type=f32) for t, h in zip(pt, heads)]
            outs += [o[:hd] / (o[hd:hd + 1] + jnp.exp(sink_ref[h] - mm)) for o, mm, h in zip(ov, m, heads)]
        o_ref[pl.ds(r0, W), :] = jnp.concatenate(outs, axis=0).T.astype(o_ref.dtype)
        return carry

    lax.fori_loop(0, tq // W, qblock, 0)


def swa_branch(fat, bias_tab_t, sinks, B, S, tq=1024):
    T = B * S
    ns = S // tq
    nb = S // WINDOW
    per = tq // WINDOW
    row = lambda b, s: b * ns + s
    prev = lambda b, s: b * nb + jnp.maximum(s * per - 1, 0)
    kvw = SWA_KV_HEADS * SWA_HEAD_DIM
    full = lambda shp: pl.BlockSpec(shp, lambda b, s: (0,) * len(shp))
    return pl.pallas_call(
        functools.partial(_swa_kernel, tq=tq),
        grid=(B, ns),
        in_specs=[pl.BlockSpec((tq, 1024), lambda b, s: (row(b, s), C_SQ // 1024)),
                  pl.BlockSpec((tq, kvw), lambda b, s: (row(b, s), C_SK // kvw)),
                  pl.BlockSpec((WINDOW, kvw), lambda b, s: (prev(b, s), C_SK // kvw)),
                  pl.BlockSpec((tq, kvw), lambda b, s: (row(b, s), C_SV // kvw)),
                  pl.BlockSpec((WINDOW, kvw), lambda b, s: (prev(b, s), C_SV // kvw)),
                  full((2, SWA_Q_HEADS, 2 * WINDOW, WINDOW)), pl.BlockSpec(memory_space=pltpu.SMEM)],
        out_specs=pl.BlockSpec((tq, 1024), lambda b, s: (row(b, s), 0)),
        out_shape=jax.ShapeDtypeStruct((T, 1024), bf16),
        scratch_shapes=[pltpu.VMEM((tq + WINDOW, kvw), bf16), pltpu.VMEM((per + 1, kvw, WINDOW), bf16)],
        compiler_params=_params(("arbitrary", "arbitrary")),
        name="swa",
    )(fat, fat, fat, fat, fat, bias_tab_t, sinks)


def _load_row_tiles(ref, n, lead=()):
    return jnp.concatenate([ref[lead + (pl.ds(s, n, stride=ROW_TILE), slice(None))] for s in range(ROW_TILE)], axis=1)


def _store_row_tiles(ref, val):
    n = val.shape[0]
    for s in range(ROW_TILE):
        ref[pl.ds(s, n, stride=ROW_TILE), :] = val[:, s * LANES:(s + 1) * LANES]


def _merge_kernel(oa_ref, ob_ref, oc_ref, ga_ref, gb_ref, gc_ref, x_ref, wa_ref, wb_ref, wc_ref, wo_ref,
                  g_ref, b_ref, o_ref, ot_ref, *, alpha):
    n = o_ref.shape[0] // MERGE_GROUPS
    grp = [slice(g * n, (g + 1) * n) for g in range(MERGE_GROUPS)]
    ya = [jnp.dot(oa_ref[r, :], wa_ref[...], preferred_element_type=f32) for r in grp]
    yb = [jnp.dot(ob_ref[r, :], wb_ref[...], preferred_element_type=f32) for r in grp]
    yc = [jnp.dot(oc_ref[r, :], wc_ref[...], preferred_element_type=f32) for r in grp]
    mix = [(_sigmoid(ga_ref[r, :].astype(f32)) * ya[g] + _sigmoid(gb_ref[r, :].astype(f32)) * yb[g]
            + _sigmoid(gc_ref[r, :].astype(f32)) * yc[g]).astype(bf16) for g, r in enumerate(grp)]
    y = [jnp.dot(m, wo_ref[...], preferred_element_type=f32) for m in mix]
    for g, r in enumerate(grp):
        x1 = _layer_norm(alpha * x_ref[r, :] + y[g], g_ref[...], b_ref[...])
        o_ref[r, :] = x1
        for s in range(ROW_TILE):
            ot_ref[pl.ds(g * n * ROW_TILE + s, n, stride=ROW_TILE), :] = x1[:, s * LANES:(s + 1) * LANES]


def merge_ln(oa, ob, oc, fat, x2d, wa, wb, wc, wo, g, b, alpha, tm=512):
    T = x2d.shape[0]
    act = pl.BlockSpec((tm, 1024), lambda i: (i, 0))
    fatb = lambda cb: pl.BlockSpec((tm, 1024), lambda i: (i, cb))
    wsp = pl.BlockSpec((1024, 1024), lambda i: (0, 0))
    vec = pl.BlockSpec((1, 1024), lambda i: (0, 0))
    return pl.pallas_call(
        functools.partial(_merge_kernel, alpha=alpha),
        grid=(T // tm,),
        in_specs=[act, act, act, fatb(C_MA // 1024), fatb(C_MB // 1024), fatb(C_MC // 1024), act,
                  wsp, wsp, wsp, wsp, vec, vec],
        out_specs=[act, pl.BlockSpec((tm * ROW_TILE, LANES), lambda i: (i, 0))],
        out_shape=[jax.ShapeDtypeStruct((T, 1024), f32), jax.ShapeDtypeStruct((T * ROW_TILE, LANES), f32)],
        compiler_params=_params(("arbitrary",)),
        name="merge_ln",
    )(oa, ob, oc, fat, fat, fat, x2d, wa, wb, wc, wo, g, b)


def _route_kernel(x_ref, rw_ref, rb_ref, gates_ref, eidx_ref, rank_ref, cnt_ref, run_ref, *, tm):
    @pl.when(pl.program_id(0) == 0)
    def _():
        run_ref[...] = jnp.zeros_like(run_ref)

    ng = ROUTE_GROUPS
    n = tm // ng
    grp = range(ng)
    lane = lax.broadcasted_iota(i32, (n, LANES), 1)
    lane_f = lane.astype(f32)
    x = [x_ref[g * n:(g + 1) * n, :] for g in grp]
    xh = [t.astype(bf16) for t in x]
    xl = [(x[g] - xh[g].astype(f32)).astype(bf16) for g in grp]
    hw = [jnp.dot(xh[g], rw_ref[...], preferred_element_type=f32) for g in grp]
    lw = [jnp.dot(xl[g], rw_ref[:, :LANES], preferred_element_type=f32) for g in grp]
    work = [hw[g][:, :LANES] + (hw[g][:, LANES:] + lw[g]) + rb_ref[...] for g in grp]
    vals, idxs, hots = [], [], []
    for _ in range(TOP_K):
        m = [jnp.max(w, axis=-1, keepdims=True) for w in work]
        idx = [jnp.min(jnp.where(work[g] == m[g], lane_f, float(LANES)), axis=-1, keepdims=True)
               for g in grp]
        hot = [lane_f == i for i in idx]
        vals.append(m)
        idxs.append(idx)
        hots.append(hot)
        work = [jnp.where(hot[g], -jnp.inf, work[g]) for g in grp]
    ri = lax.broadcasted_iota(i32, (n, n), 0)
    ci = lax.broadcasted_iota(i32, (n, n), 1)
    tril = jnp.where(ri > ci, 1.0, 0.0).astype(bf16)
    run = run_ref[...]
    for g in grp:
        es = [jnp.exp(vals[k][g] - vals[0][g]) for k in range(TOP_K)]
        den = es[0] + es[1] + es[2] + es[3]
        sel = jnp.zeros((n, LANES), f32)
        for k in range(TOP_K):
            sel = sel + jnp.where(hots[k][g], 1.0, 0.0)
        before = jnp.dot(tril, sel.astype(bf16), preferred_element_type=f32) + run
        run = run + jnp.sum(sel, axis=0, keepdims=True)
        gates = jnp.zeros((n, LANES), f32)
        eidx = jnp.zeros((n, LANES), f32)
        rank = jnp.zeros((n, LANES), f32)
        for k in range(TOP_K):
            rk = jnp.sum(jnp.where(hots[k][g], before, 0.0), axis=-1, keepdims=True)
            gates = jnp.where(lane == k, es[k] / den, gates)
            eidx = jnp.where(lane == k, idxs[k][g], eidx)
            rank = jnp.where(lane == k, rk, rank)
        gates_ref[g * n:(g + 1) * n, :] = gates
        eidx_ref[g * n:(g + 1) * n, :] = eidx.astype(i32)
        rank_ref[g * n:(g + 1) * n, :] = rank.astype(i32)
    run_ref[...] = run
    cnt_ref[...] = run


def route(x2d, rw_pad, rb_pad, tm=512):
    T = x2d.shape[0]
    outb = pl.BlockSpec((tm, LANES), lambda i: (i, 0))
    return pl.pallas_call(
        functools.partial(_route_kernel, tm=tm),
        grid=(T // tm,),
        in_specs=[pl.BlockSpec((tm, 1024), lambda i: (i, 0)),
                  pl.BlockSpec((1024, 2 * LANES), lambda i: (0, 0)),
                  pl.BlockSpec((1, LANES), lambda i: (0, 0))],
        out_specs=[outb, outb, outb, pl.BlockSpec((1, LANES), lambda i: (0, 0))],
        out_shape=[jax.ShapeDtypeStruct((T, LANES), f32), jax.ShapeDtypeStruct((T, LANES), i32),
                   jax.ShapeDtypeStruct((T, LANES), i32), jax.ShapeDtypeStruct((1, LANES), f32)],
        scratch_shapes=[pltpu.VMEM((1, LANES), f32)],
        compiler_params=_params(("arbitrary",)),
        name="route",
    )(x2d, rw_pad, rb_pad)


GU_GROUP = 2 * LANES


def _gu_prep_tile(w_ref, o_ref):
    ri = lax.broadcasted_iota(i32, (GU_GROUP, GU_GROUP), 0)
    ci = lax.broadcasted_iota(i32, (GU_GROUP, GU_GROUP), 1)
    src = jnp.where(ci < LANES, 2 * ci, 2 * (ci - LANES) + 1)
    perm = jnp.where(ri == src, 1.0, 0.0).astype(bf16)
    for g in range(w_ref.shape[3] // GU_GROUP):
        cs = slice(g * GU_GROUP, (g + 1) * GU_GROUP)
        o_ref[0, :, cs] = jnp.dot(w_ref[0, 0, :, cs].astype(bf16), perm, preferred_element_type=f32).astype(bf16)


ROW_UNROLL = 8


def _row_copy(src_ref, src_row, dst_ref, dst_row, sem):
    tile = lambda r: pl.ds(pl.multiple_of(r * ROW_TILE, ROW_TILE), ROW_TILE)
    return pltpu.make_async_copy(src_ref.at[tile(src_row)], dst_ref.at[tile(dst_row)], sem)


def _dispatch_kernel(dest_ref, x_ref, buf_in_ref, w_ref, buf_ref, wo_ref, sem, *, tm, nt, ng):
    del buf_in_ref
    i = pl.program_id(0)

    @pl.when(i < nt)
    def _():
        def issue(g, carry):
            for j in range(ROW_UNROLL):
                r = g * ROW_UNROLL + j
                for k in range(TOP_K):
                    _row_copy(x_ref, r, buf_ref, dest_ref[r * TOP_K + k], sem).start(priority=k % 2)
            return carry

        lax.fori_loop(0, tm // ROW_UNROLL, issue, 0)

    @pl.when(i < ng)
    def _():
        _gu_prep_tile(w_ref, wo_ref)

    @pl.when(i < nt)
    def _():
        def drain(g, carry):
            for j in range(ROW_UNROLL * TOP_K):
                _row_copy(x_ref, 0, buf_ref, 0, sem).wait()
            return carry

        lax.fori_loop(0, tm // ROW_UNROLL, drain, 0)


def dispatch_prep(dest_flat, xt, buf0, w_gu_all, layer, tm=512, tk=512):
    T = xt.shape[0] // ROW_TILE
    _, E, D, N = w_gu_all.shape
    nt = T // tm
    kt = D // tk
    ng = E * kt
    tok = lambda i: jnp.minimum(i, nt - 1)
    gu = lambda i: jnp.minimum(i, ng - 1)
    return pl.pallas_call(
        functools.partial(_dispatch_kernel, tm=tm, nt=nt, ng=ng),
        grid=(max(nt, ng),),
        in_specs=[pl.BlockSpec((tm * TOP_K,), lambda i: (tok(i),), memory_space=pltpu.SMEM),
                  pl.BlockSpec((tm * ROW_TILE, LANES), lambda i: (tok(i), 0)),
                  pl.BlockSpec(memory_space=pl.ANY),
                  pl.BlockSpec((1, 1, tk, N), lambda i: (layer, gu(i) // kt, gu(i) % kt, 0))],
        out_specs=[pl.BlockSpec(memory_space=pl.ANY),
                   pl.BlockSpec((1, tk, N), lambda i: (gu(i) // kt, gu(i) % kt, 0))],
        out_shape=[jax.ShapeDtypeStruct(buf0.shape, buf0.dtype), jax.ShapeDtypeStruct((E, D, N), bf16)],
        scratch_shapes=[pltpu.SemaphoreType.DMA(())],
        input_output_aliases={2: 0},
        compiler_params=_params(("arbitrary",)),
        name="dispatch_prep",
    )(dest_flat, xt, buf0, w_gu_all)


def _expert_kernel(blk_e_ref, nused_ref, x_ref, wgu_ref, bgu_ref, wd_ref, bd_ref, o_ref, wdb_ref):
    i = pl.program_id(0)

    @pl.when(jnp.logical_or(i == 0, blk_e_ref[i] != blk_e_ref[jnp.maximum(i - 1, 0)]))
    def _():
        wdb_ref[...] = wd_ref[0, 0].astype(bf16)

    @pl.when(i < nused_ref[0])
    def _():
        de = wdb_ref.shape[0]
        xb = _load_row_tiles(x_ref, EXPERT_BLK).astype(bf16)
        hgu = jnp.dot(xb, wgu_ref[0], preferred_element_type=f32) + bgu_ref[0]
        acts = []
        for g in range(2 * de // GU_GROUP):
            gate = jnp.minimum(hgu[:, g * GU_GROUP:g * GU_GROUP + LANES], SWIGLU_LIMIT)
            lin = jnp.clip(hgu[:, g * GU_GROUP + LANES:(g + 1) * GU_GROUP], -SWIGLU_LIMIT, SWIGLU_LIMIT)
            acts.append((gate * _sigmoid(SWIGLU_ALPHA * gate) * (lin + 1.0)).astype(bf16))
        act = jnp.concatenate(acts, axis=1)
        _store_row_tiles(o_ref, jnp.dot(act, wdb_ref[...], preferred_element_type=f32) + bd_ref[0])

    @pl.when(i >= nused_ref[0])
    def _():
        o_ref[...] = jnp.zeros_like(o_ref)


def experts(blk_e, nused, buf, wgu, bgu, wd_all, layer, bd):
    D = wd_all.shape[3]
    nblk = buf.shape[0] // (EXPERT_BLK * ROW_TILE)
    de = wd_all.shape[2]
    tile_blk = (EXPERT_BLK * ROW_TILE, LANES)
    live = lambda i, be, nu: jnp.minimum(i, nu[0] - 1)
    grid_spec = pltpu.PrefetchScalarGridSpec(
        num_scalar_prefetch=2,
        grid=(nblk,),
        in_specs=[pl.BlockSpec(tile_blk, lambda i, be, nu: (live(i, be, nu), 0)),
                  pl.BlockSpec((1, D, 2 * de), lambda i, be, nu: (be[i], 0, 0)),
                  pl.BlockSpec((1, 1, 2 * de), lambda i, be, nu: (be[i], 0, 0)),
                  pl.BlockSpec((1, 1, de, D), lambda i, be, nu: (layer, be[i], 0, 0)),
                  pl.BlockSpec((1, 1, D), lambda i, be, nu: (be[i], 0, 0))],
        out_specs=pl.BlockSpec(tile_blk, lambda i, be, nu: (i, 0)),
        scratch_shapes=[pltpu.VMEM((de, D), bf16)],
    )
    return pl.pallas_call(
        _expert_kernel,
        grid_spec=grid_spec,
        out_shape=jax.ShapeDtypeStruct(buf.shape, f32),
        compiler_params=_params(("arbitrary",)),
        name="experts",
    )(blk_e, nused, buf, wgu, bgu, wd_all, bd)


def _combine_kernel(dest_ref, dest_next_ref, gates_ref, x_ref, p_ref, obuf_ref, wg_ref, wp_ref,
                    g2_ref, b2_ref, g3_ref, b3_ref, o_ref, rows_ref, sems, *, tm, alpha):
    i = pl.program_id(0)
    n = pl.num_programs(0)
    slot = i % 2

    def gather(idx_ref, s):
        def issue(g, carry):
            for j in range(ROW_UNROLL):
                r = g * ROW_UNROLL + j
                for k in range(TOP_K):
                    _row_copy(obuf_ref, idx_ref[r * TOP_K + k], rows_ref.at[s, k], r, sems.at[s]).start(priority=k % 2)
            return carry

        lax.fori_loop(0, tm // ROW_UNROLL, issue, 0)

    @pl.when(i == 0)
    def _():
        gather(dest_ref, 0)

    @pl.when(i + 1 < n)
    def _():
        gather(dest_next_ref, 1 - slot)

    def drain(g, carry):
        for j in range(ROW_UNROLL * TOP_K):
            _row_copy(obuf_ref, 0, rows_ref.at[slot, 0], 0, sems.at[slot]).wait()
        return carry

    lax.fori_loop(0, tm // ROW_UNROLL, drain, 0)

    ng = COMBINE_GROUPS
    n = tm // ng
    grp = range(ng)

    def tiles(k, g):
        return jnp.concatenate([rows_ref[slot, k, pl.ds(g * n * ROW_TILE + s, n, stride=ROW_TILE), :]
                                for s in range(ROW_TILE)], axis=1)

    gates = [gates_ref[g * n:(g + 1) * n, :] for g in grp]
    y = [gates[g][:, 0:1] * tiles(0, g) for g in grp]
    for k in range(1, TOP_K):
        y = [y[g] + gates[g][:, k:k + 1] * tiles(k, g) for g in grp]
    x2 = [_layer_norm(alpha * x_ref[g * n:(g + 1) * n, :] + y[g], g2_ref[...], b2_ref[...]) for g in grp]
    gate = [_sigmoid(jnp.dot(x2[g].astype(bf16), wg_ref[...], preferred_element_type=f32)) for g in grp]
    proj = [jnp.dot(p_ref[g * n:(g + 1) * n, :].astype(bf16), wp_ref[...], preferred_element_type=f32) for g in grp]
    for g in grp:
        o_ref[g * n:(g + 1) * n, :] = _layer_norm(alpha * x2[g] + gate[g] * proj[g], g3_ref[...], b3_ref[...])


def combine_ple(dest_flat, gates, x2d, p2d, layer, obuf, wg, wp, g2, b2, g3, b3, alpha, tm=512):
    T = x2d.shape[0]
    nt = T // tm
    act = pl.BlockSpec((tm, 1024), lambda i: (i, 0))
    vec = pl.BlockSpec((1, 1024), lambda i: (0, 0))
    return pl.pallas_call(
        functools.partial(_combine_kernel, tm=tm, alpha=alpha),
        grid=(nt,),
        in_specs=[pl.BlockSpec((tm * TOP_K,), lambda i: (i,), memory_space=pltpu.SMEM),
                  pl.BlockSpec((tm * TOP_K,), lambda i: (jnp.minimum(i + 1, nt - 1),), memory_space=pltpu.SMEM),
                  pl.BlockSpec((tm, LANES), lambda i: (i, 0)),
                  act,
                  pl.BlockSpec((tm, PLE_DIM), lambda i: (layer * nt + i, 0)),
                  pl.BlockSpec(memory_space=pl.ANY),
                  pl.BlockSpec((1024, 1024), lambda i: (0, 0)),
                  pl.BlockSpec((PLE_DIM, 1024), lambda i: (0, 0)),
                  vec, vec, vec, vec],
        out_specs=act,
        out_shape=jax.ShapeDtypeStruct((T, 1024), f32),
        scratch_shapes=[pltpu.VMEM((2, TOP_K, tm * ROW_TILE, LANES), f32), pltpu.SemaphoreType.DMA((2,))],
        compiler_params=_params(("arbitrary",)),
        name="combine_ple",
    )(dest_flat, dest_flat, gates, x2d, p2d, obuf, wg, wp, g2, b2, g3, b3)


def _t5_bucket_np(dist):
    max_exact = REL_BUCKETS // 2
    d = np.maximum(dist.astype(np.float32), np.float32(1.0))
    large = max_exact + (np.log(d / np.float32(max_exact)) / np.float32(math.log(REL_MAX_DISTANCE / max_exact))
                         * np.float32(REL_BUCKETS - max_exact)).astype(np.int32)
    large = np.minimum(large, REL_BUCKETS - 1)
    return np.where(dist < max_exact, dist, large)


def _swa_bias_table(rel_bias):
    dist = np.arange(2 * WINDOW - 1, -WINDOW, -1)
    in_window = (dist >= 0) & (dist < WINDOW)
    per_dist = rel_bias[_t5_bucket_np(np.maximum(dist, 0))].astype(f32).T
    per_dist = jnp.where(jnp.asarray(in_window)[None], per_dist, NEG_BIG)
    n = 3 * WINDOW - 1
    skew = jnp.tile(jnp.pad(per_dist, ((0, 0), (0, 1))), (1, WINDOW))[:, :WINDOW * n].reshape(-1, WINDOW, n)
    bias = jnp.transpose(skew[:, :, WINDOW - 1:], (0, 2, 1))
    first = jnp.where((jnp.arange(2 * WINDOW) >= WINDOW)[None, :, None], bias, NEG_BIG)
    return jnp.stack([bias, first])


def _pad_row(v, width=LANES, fill=0.0):
    v = v.astype(f32).reshape(1, -1)
    return jnp.pad(v, ((0, 0), (0, width - v.shape[1])), constant_values=fill)


def _wcat(w_in):
    cols = [w_in[:, 0:4096], w_in[:, 4112:6160], w_in[:, 6160:7184], w_in[:, 7696:10768],
            w_in[:, 7184:7440], w_in[:, 7440:7696], w_in[:, 4096:4112],
            jnp.zeros((w_in.shape[0], FAT_W - 10768), w_in.dtype)]
    return jnp.concatenate(cols, axis=1).astype(bf16)


def kernel(x, p, w_in, conv_qkv_w, gdn_a_log, gdn_dt_bias, gdn_norm_w, rg_conv_w, rg_conv_b, rg_w_a, rg_b_a, rg_w_x, rg_b_x, rg_lambda, attn_sinks, rel_bias, w_o_gdn, w_o_lru, w_o_swa, w_out, ln1_g, ln1_b, router_w, router_b, w_gu, b_gu, w_down, b_down, ln2_g, ln2_b, ple_w_gate, ple_w_proj, ln3_g, ln3_b):
    B, S, D = x.shape
    depth = w_in.shape[0]
    T = B * S
    A = T * TOP_K
    alpha = (2.0 * depth) ** 0.25
    P = A + N_EXPERTS * EXPERT_BLK
    nblk = P // EXPERT_BLK
    row = lambda v: v.astype(f32).reshape(1, -1)

    bias_tab = _swa_bias_table(rel_bias)
    p2d = p.reshape(depth * T, PLE_DIM)
    xc = x.reshape(T, D)
    for i in range(depth):
        fat = inproj(xc, _wcat(w_in[i]))
        o_gdn, buf0 = gdn_branch(fat, conv_qkv_w[i], _pad_row(gdn_a_log[i]), _pad_row(gdn_dt_bias[i]),
                                 row(gdn_norm_w[i]), B, S, P * ROW_TILE)
        wax = jnp.concatenate([rg_w_a[i], rg_w_x[i]], axis=-1).astype(bf16)
        o_lru = lru_branch(fat, rg_conv_w[i], row(rg_conv_b[i]), wax, row(rg_b_a[i]), row(rg_b_x[i]),
                           row(rg_lambda[i]), B, S)
        o_swa = swa_branch(fat, bias_tab, attn_sinks[i].astype(f32), B, S)
        x1, x1t = merge_ln(o_gdn, o_lru, o_swa, fat, xc, w_o_gdn[i].astype(bf16), w_o_lru[i].astype(bf16),
                           w_o_swa[i].astype(bf16), w_out[i].astype(bf16), row(ln1_g[i]), row(ln1_b[i]), alpha)

        rw_pad = jnp.pad(router_w[i].astype(f32), ((0, 0), (0, LANES - N_EXPERTS)))
        rw_hi = rw_pad.astype(bf16)
        rw_pad = jnp.concatenate([rw_hi, (rw_pad - rw_hi.astype(f32)).astype(bf16)], axis=1)
        rb_pad = _pad_row(router_b[i], fill=NEG_BIG)
        gates, eidx, rank, cnt = route(x1, rw_pad, rb_pad)
        counts = cnt[0, :N_EXPERTS].astype(i32)
        padded = ((counts + EXPERT_BLK - 1) // EXPERT_BLK) * EXPERT_BLK
        pad_ends = jnp.cumsum(padded)
        pad_starts = pad_ends - padded
        hit = eidx[:, :TOP_K, None] == jnp.arange(N_EXPERTS, dtype=i32)
        dest = (jnp.sum(jnp.where(hit, pad_starts, 0), axis=-1) + rank[:, :TOP_K]).reshape(A)
        blk_start = jnp.arange(nblk, dtype=i32) * EXPERT_BLK
        blk_e = jnp.minimum(jnp.sum((pad_ends[None, :] <= blk_start[:, None]).astype(i32), axis=1),
                            N_EXPERTS - 1).astype(i32)
        nused = (pad_ends[-1:] // EXPERT_BLK).astype(i32)

        buf, wgu = dispatch_prep(dest, x1t, buf0, w_gu, i)
        bgu = jnp.transpose(b_gu[i].reshape(N_EXPERTS, -1, LANES, 2), (0, 1, 3, 2)).reshape(N_EXPERTS, 1, -1)
        obuf = experts(blk_e, nused, buf, wgu, bgu, w_down, i, b_down[i][:, None, :])
        xc = combine_ple(dest, gates, x1, p2d, i, obuf, ple_w_gate[i].astype(bf16),
                         ple_w_proj[i].astype(bf16), row(ln2_g[i]), row(ln2_b[i]), row(ln3_g[i]), row(ln3_b[i]),
                         alpha)
    return xc.reshape(B, S, D)
```

```python
import functools
import math

import numpy as np
import jax
import jax.numpy as jnp
from jax import lax
from jax.experimental import pallas as pl
from jax.experimental.pallas import tpu as pltpu

f32 = jnp.float32
bf16 = jnp.bfloat16
i32 = jnp.int32

D_MODEL = 1024
PLE_DIM = 256
GDN_HEADS = 8
GDN_HEAD_DIM = 128
GDN_CHUNK = 64
CONV_WIDTH = 4
LRU_BLOCKS = 8
LRU_BLOCK_DIM = 128
LRU_C = 8.0
SWA_Q_HEADS = 16
SWA_KV_HEADS = 4
SWA_HEAD_DIM = 64
SWA_GROUP = 4
WINDOW = 128
REL_BUCKETS = 32
REL_MAX_DISTANCE = 128
N_EXPERTS = 32
TOP_K = 4
SWIGLU_LIMIT = 7.0
SWIGLU_ALPHA = 1.702
LN_EPS = 1e-5
NORM_EPS = 1e-6
NEG_BIG = -1e30

LANES = 128
VMEM_LIMIT = 56 * 1024 * 1024

C_GQ, C_GK, C_GV, C_GZ = 0, 1024, 2048, 3072
C_LX, C_LG = 4096, 5120
C_SQ = 6144
C_MA, C_MB, C_MC = 7168, 8192, 9216
C_SK, C_SV = 10240, 10496
C_AB = 10752
FAT_W = 10880
FAT_TN = 2176

ROW_TILE = D_MODEL // LANES

EXPERT_BLK = 512
MERGE_GROUPS = 2
ROUTE_GROUPS = 4
COMBINE_GROUPS = 4
CHUNKS_IN_FLIGHT = 4
SWA_KV_IN_FLIGHT = 4


def _params(sem):
    return pltpu.CompilerParams(dimension_semantics=sem, vmem_limit_bytes=VMEM_LIMIT)


def _sigmoid(x):
    return 1.0 / (1.0 + jnp.exp(-x))


def _softplus(x):
    return jnp.maximum(x, 0.0) + jnp.log(1.0 + jnp.exp(-jnp.abs(x)))


def _layer_norm(z, g, b):
    mu = jnp.mean(z, axis=-1, keepdims=True)
    zc = z - mu
    var = jnp.mean(zc * zc, axis=-1, keepdims=True)
    return zc * lax.rsqrt(var + LN_EPS) * g + b


def _inproj_kernel(x_ref, w_ref, o_ref, xb_ref):
    @pl.when(pl.program_id(1) == 0)
    def _():
        xb_ref[...] = x_ref[...].astype(bf16)

    o_ref[...] = jnp.dot(xb_ref[...], w_ref[...], preferred_element_type=f32).astype(o_ref.dtype)


def inproj(x2d, wcat, tm=2048):
    T, K = x2d.shape
    N = wcat.shape[1]
    tn = FAT_TN
    return pl.pallas_call(
        _inproj_kernel,
        grid=(T // tm, N // tn),
        in_specs=[pl.BlockSpec((tm, K), lambda i, j: (i, 0)),
                  pl.BlockSpec((K, tn), lambda i, j: (0, j))],
        out_specs=pl.BlockSpec((tm, tn), lambda i, j: (i, j)),
        out_shape=jax.ShapeDtypeStruct((T, N), bf16),
        scratch_shapes=[pltpu.VMEM((tm, K), bf16)],
        compiler_params=_params(("arbitrary", "arbitrary")),
        name="inproj",
    )(x2d, wcat)


ZERO_FILL_COPIES = 4


def _zero_fill(step, zero_ref, zbuf_ref, zsem, zero_rows):
    zr = zero_ref.shape[0]
    return [pltpu.make_async_copy(zero_ref, zbuf_ref.at[pl.ds(pl.multiple_of(step * zero_rows + c * zr, zr), zr)], zsem)
            for c in range(zero_rows // zr)]


def _shift_matrices(ts):
    ri = lax.broadcasted_iota(i32, (ts, ts), 0)
    ci = lax.broadcasted_iota(i32, (ts, ts), 1)
    return [jnp.where(ri - ci == d, 1.0, 0.0).astype(bf16) for d in (3, 2, 1)]


def _causal_conv_silu(src_ref, dst_ref, cz_ref, carry_ref, w, smats, ts, head_scale=None):
    xb = src_ref[...]
    x = xb.astype(f32)
    y = w[3:4, :] * x
    for j, sm in enumerate(smats):
        y = y + w[j:j + 1, :] * jnp.dot(sm, xb, preferred_element_type=f32)
    cz_ref[0:8, :] = carry_ref[...]
    cz_ref[8:16, :] = jnp.zeros((8, x.shape[1]), f32)
    corr = w[0:1, :] * cz_ref[5:13, :] + w[1:2, :] * cz_ref[6:14, :] + w[2:3, :] * cz_ref[7:15, :]
    carry_ref[...] = x[ts - 8:ts, :]

    def post(rows, yv):
        a = yv * _sigmoid(yv)
        if head_scale is None:
            dst_ref[rows, :] = a
            return
        for h in range(a.shape[1] // LANES):
            cs = slice(h * LANES, (h + 1) * LANES)
            ah = a[:, cs]
            dst_ref[rows, cs] = ah * (lax.rsqrt(jnp.sum(ah * ah, axis=-1, keepdims=True) + NORM_EPS) * head_scale)

    post(slice(0, 8), y[0:8] + corr)
    post(slice(8, ts), y[8:ts])


def _gdn_kernel(q_ref, k_ref, v_ref, z_ref, ab_ref, cw_ref, alog_ref, dtb_ref, nw_ref, o_ref, zbuf_ref,
                xpad_ref, qs_ref, ks_ref, vs_ref, carry_ref, state_ref, g_ref, beta_ref,
                u_s, lhs_s, intra_s, kdt_s, zero_ref, zsem, *, ts, zero_rows):
    C = GDN_CHUNK
    D = GDN_HEAD_DIM
    P = 2 * C

    step = pl.program_id(0) * pl.num_programs(1) + pl.program_id(1)

    @pl.when(step == 0)
    def _():
        zero_ref[...] = jnp.zeros_like(zero_ref)

    fills = _zero_fill(step, zero_ref, zbuf_ref, zsem, zero_rows)
    for f in fills:
        f.start()

    @pl.when(pl.program_id(1) == 0)
    def _():
        carry_ref[...] = jnp.zeros_like(carry_ref)
        state_ref[...] = jnp.zeros_like(state_ref)

    smats = _shift_matrices(ts)
    for p, (src, dst, scale) in enumerate(((q_ref, qs_ref, D ** -0.5), (k_ref, ks_ref, 1.0), (v_ref, vs_ref, None))):
        _causal_conv_silu(src, dst, xpad_ref, carry_ref.at[p], cw_ref[:, p * 1024:(p + 1) * 1024], smats, ts, scale)

    ab = ab_ref[...].astype(f32)
    g = -jnp.exp(alog_ref[...]) * _softplus(ab + dtb_ref[...])
    rin = lax.broadcasted_iota(i32, (ts, LANES), 0) & (C - 1)
    gsum = g
    for d in (1, 2, 4, 8, 16, 32):
        gsum = gsum + jnp.where(rin >= d, pltpu.roll(gsum, d, 0), 0.0)
    g_ref[...] = gsum
    beta_ref[...] = _sigmoid(ab)

    ri = lax.broadcasted_iota(i32, (P, P), 0)
    ci = lax.broadcasted_iota(i32, (P, P), 1)
    same = (ri >= C) == (ci >= C)
    eye = ri == ci
    causal = same & (ri >= ci)
    strict = same & (ri > ci)
    eye_f = jnp.where(eye, 1.0, 0.0).astype(f32)
    first_cols = ci < C
    nw = nw_ref[...]

    def stack(a, b):
        return jnp.concatenate([a, b], axis=0)

    def mm(a, b):
        return jnp.dot(a, b, preferred_element_type=f32)

    npair = GDN_HEADS // 2
    nchunk = ts // C
    hcols = [slice(h * D, (h + 1) * D) for h in range(GDN_HEADS)]

    for cg in range(0, nchunk, CHUNKS_IN_FLIGHT):
        probs = [(c, hp) for c in range(cg, cg + CHUNKS_IN_FLIGHT) for hp in range(npair)]
        qn, kn, vb, gcol, eg, egl, kb = [], [], [], [], [], [], []
        for c, hp in probs:
            rows = slice(c * C, (c + 1) * C)
            c0, c1 = hcols[2 * hp], hcols[2 * hp + 1]
            gc = g_ref[rows, :]
            bc = beta_ref[rows, :]
            qn.append(stack(qs_ref[rows, c0], qs_ref[rows, c1]))
            kn.append(stack(ks_ref[rows, c0], ks_ref[rows, c1]))
            v2 = stack(vs_ref[rows, c0], vs_ref[rows, c1])
            h0, h1 = 2 * hp, 2 * hp + 1
            gcl = stack(gc[:, h0:h0 + 1], gc[:, h1:h1 + 1])
            bcl = stack(bc[:, 8 + h0:9 + h0], bc[:, 8 + h1:9 + h1])
            glast = stack(jnp.broadcast_to(gc[C - 1:C, h0:h0 + 1], (C, 1)),
                          jnp.broadcast_to(gc[C - 1:C, h1:h1 + 1], (C, 1)))
            gcol.append(gcl)
            eg.append(jnp.exp(gcl))
            egl.append(jnp.exp(glast - gcl))
            kb.append(kn[-1] * bcl)
            vb.append(v2 * bcl)
        n = len(probs)
        a2 = [lax.dot_general(stack(kb[i], qn[i]).astype(bf16), kn[i].astype(bf16), (((1,), (1,)), ((), ())),
                              preferred_element_type=f32) for i in range(n)]
        lmat, intra = [], []
        for i in range(n):
            gm = jnp.broadcast_to(gcol[i], (P, P))
            grow = jnp.sum(jnp.where(eye, gm, 0.0), axis=0, keepdims=True)
            decay = jnp.where(causal, jnp.exp(jnp.minimum(gm - grow, 0.0)), 0.0)
            lmat.append(jnp.where(strict, a2[i][:P] * decay, 0.0))
            intra.append(a2[i][P:] * decay)
        lb = [l.astype(bf16) for l in lmat]
        xm = [eye_f - l for l in lmat]
        pm = [mm(b, b) for b in lb]
        for it in range(5):
            pb = [p.astype(bf16) for p in pm]
            xm = [x + mm(x.astype(bf16), b) for x, b in zip(xm, pb)]
            if it < 4:
                pm = [mm(b, b) for b in pb]
        uw = [mm(xm[i].astype(bf16), jnp.concatenate([vb[i], kb[i] * eg[i]], axis=1).astype(bf16))
              for i in range(n)]
        for i, (c, hp) in enumerate(probs):
            j = c * npair + hp
            w2 = uw[i][:, D:]
            qd = qn[i] * eg[i]
            kdt = (kn[i] * egl[i]).T
            u_s[j] = uw[i][:, :D]
            lhs_s[j, 0] = stack(w2[:C], qd[:C]).astype(bf16)
            lhs_s[j, 1] = stack(w2[C:], qd[C:]).astype(bf16)
            intra_s[j] = intra[i].astype(bf16)
            kdt_s[j, 0] = jnp.where(first_cols, kdt, 0.0).astype(bf16)
            kdt_s[j, 1] = jnp.where(first_cols, 0.0, kdt).astype(bf16)

    for c in range(nchunk):
        rows = slice(c * C, (c + 1) * C)
        gl = g_ref[(c + 1) * C - 1:(c + 1) * C, :]
        st = [state_ref[h] for h in range(GDN_HEADS)]
        wq = [mm(lhs_s[c * npair + h // 2, h % 2], st[h].astype(bf16)) for h in range(GDN_HEADS)]
        vnb = [(u_s[c * npair + hp] - stack(wq[2 * hp][:C], wq[2 * hp + 1][:C])).astype(bf16)
               for hp in range(npair)]
        o2 = [stack(wq[2 * hp][C:], wq[2 * hp + 1][C:]) + mm(intra_s[c * npair + hp], vnb[hp])
              for hp in range(npair)]
        for h in range(GDN_HEADS):
            state_ref[h] = st[h] * jnp.exp(gl[:, h:h + 1]) + mm(kdt_s[c * npair + h // 2, h % 2], vnb[h // 2])
        for hp in range(npair):
            c0, c1 = hcols[2 * hp], hcols[2 * hp + 1]
            z2 = stack(z_ref[rows, c0], z_ref[rows, c1]).astype(f32)
            on = (o2[hp] * lax.rsqrt(jnp.mean(o2[hp] * o2[hp], axis=-1, keepdims=True) + NORM_EPS) * nw
                  * (z2 * _sigmoid(z2))).astype(o_ref.dtype)
            o_ref[rows, c0] = on[:C]
            o_ref[rows, c1] = on[C:]
    for f in fills:
        f.wait()


def gdn_branch(fat, conv_w, a_log_row, dt_bias_row, norm_w_row, B, S, buf_rows, ts=256):
    T = B * S
    ns = S // ts
    nprob = (ts // GDN_CHUNK) * (GDN_HEADS // 2)
    zero_rows = buf_rows // (B * ns)
    assert zero_rows * B * ns == buf_rows and zero_rows % (ZERO_FILL_COPIES * ROW_TILE) == 0
    row = lambda b, s: b * ns + s
    blk = lambda cb: pl.BlockSpec((ts, 1024), lambda b, s: (row(b, s), cb))
    full = lambda shp: pl.BlockSpec(shp, lambda b, s: (0,) * len(shp))
    return pl.pallas_call(
        functools.partial(_gdn_kernel, ts=ts, zero_rows=zero_rows),
        grid=(B, ns),
        in_specs=[blk(C_GQ // 1024), blk(C_GK // 1024), blk(C_GV // 1024), blk(C_GZ // 1024),
                  pl.BlockSpec((ts, LANES), lambda b, s: (row(b, s), C_AB // LANES)),
                  full((CONV_WIDTH, 3072)), full((1, LANES)), full((1, LANES)), full((1, LANES))],
        out_specs=[pl.BlockSpec((ts, 1024), lambda b, s: (row(b, s), 0)), pl.BlockSpec(memory_space=pl.ANY)],
        out_shape=[jax.ShapeDtypeStruct((T, 1024), bf16), jax.ShapeDtypeStruct((buf_rows, LANES), f32)],
        scratch_shapes=[pltpu.VMEM((16, 1024), f32),
                        pltpu.VMEM((ts, 1024), f32), pltpu.VMEM((ts, 1024), f32), pltpu.VMEM((ts, 1024), f32),
                        pltpu.VMEM((3, 8, 1024), f32),
                        pltpu.VMEM((GDN_HEADS, GDN_HEAD_DIM, GDN_HEAD_DIM), f32),
                        pltpu.VMEM((ts, LANES), f32), pltpu.VMEM((ts, LANES), f32),
                        pltpu.VMEM((nprob, 128, GDN_HEAD_DIM), f32),
                        pltpu.VMEM((nprob, 2, 128, GDN_HEAD_DIM), bf16),
                        pltpu.VMEM((nprob, 128, 128), bf16),
                        pltpu.VMEM((nprob, 2, GDN_HEAD_DIM, 128), bf16),
                        pltpu.VMEM((zero_rows // ZERO_FILL_COPIES, LANES), f32), pltpu.SemaphoreType.DMA(())],
        compiler_params=_params(("arbitrary", "arbitrary")),
        name="gdn",
    )(fat, fat, fat, fat, fat, conv_w, a_log_row, dt_bias_row, norm_w_row)


def _lru_kernel(x_ref, gate_ref, cw_ref, cb_ref, wax_ref, ba_ref, bx_ref, lam_ref, o_ref,
                cz_ref, xc_ref, a_ref, u_ref, h_ref, carry_ref, hc_ref, *, ts):
    nb = x_ref.shape[0]

    @pl.when(pl.program_id(0) == 0)
    def _():
        carry_ref[...] = jnp.zeros_like(carry_ref)
        hc_ref[...] = jnp.zeros_like(hc_ref)

    w = cw_ref[...]
    smats = _shift_matrices(ts)
    cz_ref[8:16, :] = jnp.zeros((8, D_MODEL), f32)
    for bi in range(nb):
        xb = x_ref[bi]
        x = xb.astype(f32)
        y = w[3:4, :] * x + cb_ref[...]
        for j, sm in enumerate(smats):
            y = y + w[j:j + 1, :] * jnp.dot(sm, xb, preferred_element_type=f32)
        cz_ref[0:8, :] = carry_ref[bi]
        corr = w[0:1, :] * cz_ref[5:13, :] + w[1:2, :] * cz_ref[6:14, :] + w[2:3, :] * cz_ref[7:15, :]
        carry_ref[bi] = x[ts - 8:ts, :]
        xc_ref[bi * ts:bi * ts + 8, :] = y[0:8] + corr
        xc_ref[bi * ts + 8:(bi + 1) * ts, :] = y[8:ts]

    la2 = (-LRU_C / math.log(2.0)) * _softplus(-lam_ref[...])
    for blk in range(LRU_BLOCKS):
        cs = slice(blk * LRU_BLOCK_DIM, (blk + 1) * LRU_BLOCK_DIM)
        xc = xc_ref[:, cs]
        ri = jnp.dot(xc.astype(bf16), wax_ref[blk], preferred_element_type=f32)
        r = _sigmoid(ri[:, :LRU_BLOCK_DIM] + ba_ref[:, cs])
        i = _sigmoid(ri[:, LRU_BLOCK_DIM:] + bx_ref[:, cs])
        a = jnp.exp2(r * la2[:, cs])
        u = jnp.sqrt(1.0 - a * a) * (i * xc)
        for bi in range(nb):
            a_ref[blk, pl.ds(bi, ts, stride=nb), :] = a[bi * ts:(bi + 1) * ts]
            u_ref[blk, pl.ds(bi, ts, stride=nb), :] = u[bi * ts:(bi + 1) * ts]

    def step(t, hs):
        view = pl.ds(pl.multiple_of(t * nb, nb), nb)
        out = []
        for blk in range(LRU_BLOCKS):
            h = a_ref[blk, view, :] * hs[blk] + u_ref[blk, view, :]
            h_ref[blk, view, :] = h
            out.append(h)
        return tuple(out)

    h0 = tuple(hc_ref[:, blk * LRU_BLOCK_DIM:(blk + 1) * LRU_BLOCK_DIM] for blk in range(LRU_BLOCKS))
    hT = lax.fori_loop(0, ts, step, h0, unroll=8)
    for blk in range(LRU_BLOCKS):
        cs = slice(blk * LRU_BLOCK_DIM, (blk + 1) * LRU_BLOCK_DIM)
        hc_ref[:, cs] = hT[blk]
        for bi in range(nb):
            hb = h_ref[blk, pl.ds(bi, ts, stride=nb), :]
            o_ref[bi, :, cs] = (hb * jax.nn.gelu(gate_ref[bi, :, cs].astype(f32))).astype(o_ref.dtype)


def lru_branch(fat, conv_w, conv_b, wax, b_a, b_x, lam, B, S, ts=128):
    assert B == 8, "the recurrence keeps the batch on the 8 sublanes of a vreg"
    T = B * S
    fat3 = fat.reshape(B, S, fat.shape[1])
    full = lambda shp: pl.BlockSpec(shp, lambda s: (0,) * len(shp))
    out = pl.pallas_call(
        functools.partial(_lru_kernel, ts=ts),
        grid=(S // ts,),
        in_specs=[pl.BlockSpec((B, ts, 1024), lambda s: (0, s, C_LX // 1024)),
                  pl.BlockSpec((B, ts, 1024), lambda s: (0, s, C_LG // 1024)),
                  full((CONV_WIDTH, 1024)), full((1, 1024)), full((LRU_BLOCKS, LRU_BLOCK_DIM, 2 * LRU_BLOCK_DIM)),
                  full((1, 1024)), full((1, 1024)), full((1, 1024))],
        out_specs=pl.BlockSpec((B, ts, 1024), lambda s: (0, s, 0)),
        out_shape=jax.ShapeDtypeStruct((B, S, 1024), bf16),
        scratch_shapes=[pltpu.VMEM((16, 1024), f32), pltpu.VMEM((B * ts, 1024), f32),
                        pltpu.VMEM((LRU_BLOCKS, B * ts, LRU_BLOCK_DIM), f32),
                        pltpu.VMEM((LRU_BLOCKS, B * ts, LRU_BLOCK_DIM), f32),
                        pltpu.VMEM((LRU_BLOCKS, B * ts, LRU_BLOCK_DIM), f32),
                        pltpu.VMEM((B, 8, 1024), f32), pltpu.VMEM((B, 1024), f32)],
        compiler_params=_params(("arbitrary",)),
        name="lru",
    )(fat3, fat3, conv_w, conv_b, wax, b_a, b_x, lam)
    return out.reshape(T, 1024)


def _swa_kernel(q_ref, kc_ref, kp_ref, vc_ref, vp_ref, biast_ref, sink_ref, o_ref, kb_ref, vt_ref, *, tq):
    W = WINDOW
    hd = SWA_HEAD_DIM
    kb_ref[0:W, :] = kp_ref[...]
    kb_ref[W:W + tq, :] = kc_ref[...]
    vt_ref[0] = vp_ref[...].astype(f32).T.astype(bf16)
    for j in range(tq // W):
        vt_ref[j + 1] = vc_ref[j * W:(j + 1) * W, :].astype(f32).T.astype(bf16)
    first_tile = pl.program_id(1) == 0
    ones_rows = jnp.ones((8, 2 * W), bf16)
    scale = hd ** -0.5

    def qblock(n, carry):
        r0 = pl.multiple_of(n * W, W)
        tab = jnp.where(jnp.logical_and(first_tile, n == 0), 1, 0)
        qs = q_ref[pl.ds(r0, W), :] * scale
        vt_band = jnp.concatenate([vt_ref[n], vt_ref[n + 1]], axis=1)
        outs = []
        for hk0 in range(0, SWA_KV_HEADS, SWA_KV_IN_FLIGHT):
            hks = range(hk0, hk0 + SWA_KV_IN_FLIGHT)
            heads = range(hk0 * SWA_GROUP, (hk0 + SWA_KV_IN_FLIGHT) * SWA_GROUP)
            kk = {hk: kb_ref[pl.ds(r0, 2 * W), hk * hd:(hk + 1) * hd] for hk in hks}
            lhs_v = {hk: jnp.concatenate([vt_band[hk * hd:(hk + 1) * hd, :], ones_rows], axis=0) for hk in hks}
            st = [lax.dot_general(kk[h // SWA_GROUP], qs[:, h * hd:(h + 1) * hd], (((1,), (1,)), ((), ())),
                                  preferred_element_type=f32) + biast_ref[tab, h] for h in heads]
            m = [jnp.maximum(jnp.max(t, axis=0, keepdims=True), sink_ref[h]) for t, h in zip(st, heads)]
            pt = [jnp.exp(t - mm).astype(bf16) for t, mm in zip(st, m)]
            ov = [jnp.dot(lhs_v[h // SWA_GROUP], t, preferred_element_type=f32) for t, h in zip(pt, heads)]
            outs += [o[:hd] / (o[hd:hd + 1] + jnp.exp(sink_ref[h] - mm)) for o, mm, h in zip(ov, m, heads)]
        o_ref[pl.ds(r0, W), :] = jnp.concatenate(outs, axis=0).T.astype(o_ref.dtype)
        return carry

    lax.fori_loop(0, tq // W, qblock, 0)


def swa_branch(fat, bias_tab_t, sinks, B, S, tq=1024):
    T = B * S
    ns = S // tq
    nb = S // WINDOW
    per = tq // WINDOW
    row = lambda b, s: b * ns + s
    prev = lambda b, s: b * nb + jnp.maximum(s * per - 1, 0)
    kvw = SWA_KV_HEADS * SWA_HEAD_DIM
    full = lambda shp: pl.BlockSpec(shp, lambda b, s: (0,) * len(shp))
    return pl.pallas_call(
        functools.partial(_swa_kernel, tq=tq),
        grid=(B, ns),
        in_specs=[pl.BlockSpec((tq, 1024), lambda b, s: (row(b, s), C_SQ // 1024)),
                  pl.BlockSpec((tq, kvw), lambda b, s: (row(b, s), C_SK // kvw)),
                  pl.BlockSpec((WINDOW, kvw), lambda b, s: (prev(b, s), C_SK // kvw)),
                  pl.BlockSpec((tq, kvw), lambda b, s: (row(b, s), C_SV // kvw)),
                  pl.BlockSpec((WINDOW, kvw), lambda b, s: (prev(b, s), C_SV // kvw)),
                  full((2, SWA_Q_HEADS, 2 * WINDOW, WINDOW)), pl.BlockSpec(memory_space=pltpu.SMEM)],
        out_specs=pl.BlockSpec((tq, 1024), lambda b, s: (row(b, s), 0)),
        out_shape=jax.ShapeDtypeStruct((T, 1024), bf16),
        scratch_shapes=[pltpu.VMEM((tq + WINDOW, kvw), bf16), pltpu.VMEM((per + 1, kvw, WINDOW), bf16)],
        compiler_params=_params(("arbitrary", "arbitrary")),
        name="swa",
    )(fat, fat, fat, fat, fat, bias_tab_t, sinks)


def _load_row_tiles(ref, n, lead=()):
    return jnp.concatenate([ref[lead + (pl.ds(s, n, stride=ROW_TILE), slice(None))] for s in range(ROW_TILE)], axis=1)


def _store_row_tiles(ref, val):
    n = val.shape[0]
    for s in range(ROW_TILE):
        ref[pl.ds(s, n, stride=ROW_TILE), :] = val[:, s * LANES:(s + 1) * LANES]


def _merge_kernel(oa_ref, ob_ref, oc_ref, ga_ref, gb_ref, gc_ref, x_ref, wa_ref, wb_ref, wc_ref, wo_ref,
                  g_ref, b_ref, ot_ref, *, alpha):
    n = x_ref.shape[0] // MERGE_GROUPS
    grp = [slice(g * n, (g + 1) * n) for g in range(MERGE_GROUPS)]
    ya = [jnp.dot(oa_ref[r, :], wa_ref[...], preferred_element_type=f32) for r in grp]
    yb = [jnp.dot(ob_ref[r, :], wb_ref[...], preferred_element_type=f32) for r in grp]
    yc = [jnp.dot(oc_ref[r, :], wc_ref[...], preferred_element_type=f32) for r in grp]
    mix = [(_sigmoid(ga_ref[r, :].astype(f32)) * ya[g] + _sigmoid(gb_ref[r, :].astype(f32)) * yb[g]
            + _sigmoid(gc_ref[r, :].astype(f32)) * yc[g]).astype(bf16) for g, r in enumerate(grp)]
    y = [jnp.dot(m, wo_ref[...], preferred_element_type=f32) for m in mix]
    for g, r in enumerate(grp):
        x1 = _layer_norm(alpha * x_ref[r, :] + y[g], g_ref[...], b_ref[...])
        for s in range(ROW_TILE):
            ot_ref[pl.ds(g * n * ROW_TILE + s, n, stride=ROW_TILE), :] = x1[:, s * LANES:(s + 1) * LANES]


def merge_ln(oa, ob, oc, fat, x2d, wa, wb, wc, wo, g, b, alpha, tm=512):
    T = x2d.shape[0]
    act = pl.BlockSpec((tm, 1024), lambda i: (i, 0))
    fatb = lambda cb: pl.BlockSpec((tm, 1024), lambda i: (i, cb))
    wsp = pl.BlockSpec((1024, 1024), lambda i: (0, 0))
    vec = pl.BlockSpec((1, 1024), lambda i: (0, 0))
    return pl.pallas_call(
        functools.partial(_merge_kernel, alpha=alpha),
        grid=(T // tm,),
        in_specs=[act, act, act, fatb(C_MA // 1024), fatb(C_MB // 1024), fatb(C_MC // 1024), act,
                  wsp, wsp, wsp, wsp, vec, vec],
        out_specs=pl.BlockSpec((tm * ROW_TILE, LANES), lambda i: (i, 0)),
        out_shape=jax.ShapeDtypeStruct((T * ROW_TILE, LANES), f32),
        compiler_params=_params(("arbitrary",)),
        name="merge_ln",
    )(oa, ob, oc, fat, fat, fat, x2d, wa, wb, wc, wo, g, b)


def _route_kernel(x_ref, rw_ref, rb_ref, gates_ref, eidx_ref, rank_ref, cnt_ref, run_ref, *, tm):
    @pl.when(pl.program_id(0) == 0)
    def _():
        run_ref[...] = jnp.zeros_like(run_ref)

    ng = ROUTE_GROUPS
    n = tm // ng
    grp = range(ng)
    lane = lax.broadcasted_iota(i32, (n, LANES), 1)
    lane_f = lane.astype(f32)
    x = [jnp.concatenate([x_ref[pl.ds(g * n * ROW_TILE + s, n, stride=ROW_TILE), :] for s in range(ROW_TILE)], axis=1)
         for g in grp]
    xh = [t.astype(bf16) for t in x]
    xl = [(x[g] - xh[g].astype(f32)).astype(bf16) for g in grp]
    hw = [jnp.dot(xh[g], rw_ref[...], preferred_element_type=f32) for g in grp]
    lw = [jnp.dot(xl[g], rw_ref[:, :LANES], preferred_element_type=f32) for g in grp]
    work = [hw[g][:, :LANES] + (hw[g][:, LANES:] + lw[g]) + rb_ref[...] for g in grp]
    vals, idxs, hots = [], [], []
    for _ in range(TOP_K):
        m = [jnp.max(w, axis=-1, keepdims=True) for w in work]
        idx = [jnp.min(jnp.where(work[g] == m[g], lane_f, float(LANES)), axis=-1, keepdims=True)
               for g in grp]
        hot = [lane_f == i for i in idx]
        vals.append(m)
        idxs.append(idx)
        hots.append(hot)
        work = [jnp.where(hot[g], -jnp.inf, work[g]) for g in grp]
    ri = lax.broadcasted_iota(i32, (n, n), 0)
    ci = lax.broadcasted_iota(i32, (n, n), 1)
    tril = jnp.where(ri > ci, 1.0, 0.0).astype(bf16)
    run = run_ref[...]
    for g in grp:
        es = [jnp.exp(vals[k][g] - vals[0][g]) for k in range(TOP_K)]
        den = es[0] + es[1] + es[2] + es[3]
        sel = jnp.zeros((n, LANES), f32)
        for k in range(TOP_K):
            sel = sel + jnp.where(hots[k][g], 1.0, 0.0)
        before = jnp.dot(tril, sel.astype(bf16), preferred_element_type=f32) + run
        run = run + jnp.sum(sel, axis=0, keepdims=True)
        gates = jnp.zeros((n, LANES), f32)
        eidx = jnp.zeros((n, LANES), f32)
        rank = jnp.zeros((n, LANES), f32)
        for k in range(TOP_K):
            rk = jnp.sum(jnp.where(hots[k][g], before, 0.0), axis=-1, keepdims=True)
            gates = jnp.where(lane == k, es[k] / den, gates)
            eidx = jnp.where(lane == k, idxs[k][g], eidx)
            rank = jnp.where(lane == k, rk, rank)
        gates_ref[g * n:(g + 1) * n, :] = gates
        eidx_ref[g * n:(g + 1) * n, :] = eidx.astype(i32)
        rank_ref[g * n:(g + 1) * n, :] = rank.astype(i32)
    run_ref[...] = run
    cnt_ref[...] = run


def route(xt, rw_pad, rb_pad, tm=512):
    T = xt.shape[0] // ROW_TILE
    outb = pl.BlockSpec((tm, LANES), lambda i: (i, 0))
    return pl.pallas_call(
        functools.partial(_route_kernel, tm=tm),
        grid=(T // tm,),
        in_specs=[pl.BlockSpec((tm * ROW_TILE, LANES), lambda i: (i, 0)),
                  pl.BlockSpec((1024, 2 * LANES), lambda i: (0, 0)),
                  pl.BlockSpec((1, LANES), lambda i: (0, 0))],
        out_specs=[outb, outb, outb, pl.BlockSpec((1, LANES), lambda i: (0, 0))],
        out_shape=[jax.ShapeDtypeStruct((T, LANES), f32), jax.ShapeDtypeStruct((T, LANES), i32),
                   jax.ShapeDtypeStruct((T, LANES), i32), jax.ShapeDtypeStruct((1, LANES), f32)],
        scratch_shapes=[pltpu.VMEM((1, LANES), f32)],
        compiler_params=_params(("arbitrary",)),
        name="route",
    )(xt, rw_pad, rb_pad)


GU_GROUP = 2 * LANES


def _gu_prep_tile(w_ref, o_ref):
    ri = lax.broadcasted_iota(i32, (GU_GROUP, GU_GROUP), 0)
    ci = lax.broadcasted_iota(i32, (GU_GROUP, GU_GROUP), 1)
    src = jnp.where(ci < LANES, 2 * ci, 2 * (ci - LANES) + 1)
    perm = jnp.where(ri == src, 1.0, 0.0).astype(bf16)
    for g in range(w_ref.shape[3] // GU_GROUP):
        cs = slice(g * GU_GROUP, (g + 1) * GU_GROUP)
        o_ref[0, :, cs] = jnp.dot(w_ref[0, 0, :, cs].astype(bf16), perm, preferred_element_type=f32).astype(bf16)


ROW_UNROLL = 8


def _row_copy(src_ref, src_row, dst_ref, dst_row, sem):
    tile = lambda r: pl.ds(pl.multiple_of(r * ROW_TILE, ROW_TILE), ROW_TILE)
    return pltpu.make_async_copy(src_ref.at[tile(src_row)], dst_ref.at[tile(dst_row)], sem)


def _dispatch_kernel(dest_ref, x_ref, buf_in_ref, w_ref, buf_ref, wo_ref, sem, *, tm, nt, ng):
    del buf_in_ref
    i = pl.program_id(0)

    @pl.when(i < nt)
    def _():
        def issue(g, carry):
            for j in range(ROW_UNROLL):
                r = g * ROW_UNROLL + j
                for k in range(TOP_K):
                    _row_copy(x_ref, r, buf_ref, dest_ref[r * TOP_K + k], sem).start(priority=k % 2)
            return carry

        lax.fori_loop(0, tm // ROW_UNROLL, issue, 0)

    @pl.when(i < ng)
    def _():
        _gu_prep_tile(w_ref, wo_ref)

    @pl.when(i < nt)
    def _():
        def drain(g, carry):
            for j in range(ROW_UNROLL * TOP_K):
                _row_copy(x_ref, 0, buf_ref, 0, sem).wait()
            return carry

        lax.fori_loop(0, tm // ROW_UNROLL, drain, 0)


def dispatch_prep(dest_flat, xt, buf0, w_gu_all, layer, tm=512, tk=512):
    T = xt.shape[0] // ROW_TILE
    _, E, D, N = w_gu_all.shape
    nt = T // tm
    kt = D // tk
    ng = E * kt
    tok = lambda i: jnp.minimum(i, nt - 1)
    gu = lambda i: jnp.minimum(i, ng - 1)
    return pl.pallas_call(
        functools.partial(_dispatch_kernel, tm=tm, nt=nt, ng=ng),
        grid=(max(nt, ng),),
        in_specs=[pl.BlockSpec((tm * TOP_K,), lambda i: (tok(i),), memory_space=pltpu.SMEM),
                  pl.BlockSpec((tm * ROW_TILE, LANES), lambda i: (tok(i), 0)),
                  pl.BlockSpec(memory_space=pl.ANY),
                  pl.BlockSpec((1, 1, tk, N), lambda i: (layer, gu(i) // kt, gu(i) % kt, 0))],
        out_specs=[pl.BlockSpec(memory_space=pl.ANY),
                   pl.BlockSpec((1, tk, N), lambda i: (gu(i) // kt, gu(i) % kt, 0))],
        out_shape=[jax.ShapeDtypeStruct(buf0.shape, buf0.dtype), jax.ShapeDtypeStruct((E, D, N), bf16)],
        scratch_shapes=[pltpu.SemaphoreType.DMA(())],
        input_output_aliases={2: 0},
        compiler_params=_params(("arbitrary",)),
        name="dispatch_prep",
    )(dest_flat, xt, buf0, w_gu_all)


def _expert_kernel(blk_e_ref, nused_ref, x_ref, wgu_ref, bgu_ref, wd_ref, bd_ref, o_ref, wdb_ref):
    i = pl.program_id(0)

    @pl.when(jnp.logical_or(i == 0, blk_e_ref[i] != blk_e_ref[jnp.maximum(i - 1, 0)]))
    def _():
        wdb_ref[...] = wd_ref[0, 0].astype(bf16)

    @pl.when(i < nused_ref[0])
    def _():
        de = wdb_ref.shape[0]
        xb = _load_row_tiles(x_ref, EXPERT_BLK).astype(bf16)
        hgu = jnp.dot(xb, wgu_ref[0], preferred_element_type=f32) + bgu_ref[0]
        acts = []
        for g in range(2 * de // GU_GROUP):
            gate = jnp.minimum(hgu[:, g * GU_GROUP:g * GU_GROUP + LANES], SWIGLU_LIMIT)
            lin = jnp.clip(hgu[:, g * GU_GROUP + LANES:(g + 1) * GU_GROUP], -SWIGLU_LIMIT, SWIGLU_LIMIT)
            acts.append((gate * _sigmoid(SWIGLU_ALPHA * gate) * (lin + 1.0)).astype(bf16))
        act = jnp.concatenate(acts, axis=1)
        _store_row_tiles(o_ref, jnp.dot(act, wdb_ref[...], preferred_element_type=f32) + bd_ref[0])

    @pl.when(i >= nused_ref[0])
    def _():
        o_ref[...] = jnp.zeros_like(o_ref)


def experts(blk_e, nused, buf, wgu, bgu, wd_all, layer, bd):
    D = wd_all.shape[3]
    nblk = buf.shape[0] // (EXPERT_BLK * ROW_TILE)
    de = wd_all.shape[2]
    tile_blk = (EXPERT_BLK * ROW_TILE, LANES)
    live = lambda i, be, nu: jnp.minimum(i, nu[0] - 1)
    grid_spec = pltpu.PrefetchScalarGridSpec(
        num_scalar_prefetch=2,
        grid=(nblk,),
        in_specs=[pl.BlockSpec(tile_blk, lambda i, be, nu: (live(i, be, nu), 0)),
                  pl.BlockSpec((1, D, 2 * de), lambda i, be, nu: (be[i], 0, 0)),
                  pl.BlockSpec((1, 1, 2 * de), lambda i, be, nu: (be[i], 0, 0)),
                  pl.BlockSpec((1, 1, de, D), lambda i, be, nu: (layer, be[i], 0, 0)),
                  pl.BlockSpec((1, 1, D), lambda i, be, nu: (be[i], 0, 0))],
        out_specs=pl.BlockSpec(tile_blk, lambda i, be, nu: (i, 0)),
        scratch_shapes=[pltpu.VMEM((de, D), bf16)],
    )
    return pl.pallas_call(
        _expert_kernel,
        grid_spec=grid_spec,
        out_shape=jax.ShapeDtypeStruct(buf.shape, f32),
        compiler_params=_params(("arbitrary",)),
        name="experts",
    )(blk_e, nused, buf, wgu, bgu, wd_all, bd)


def _combine_kernel(dest_ref, dest_next_ref, gates_ref, x_ref, p_ref, obuf_ref, wg_ref, wp_ref,
                    g2_ref, b2_ref, g3_ref, b3_ref, o_ref, rows_ref, sems, *, tm, alpha):
    i = pl.program_id(0)
    n = pl.num_programs(0)
    slot = i % 2

    def gather(idx_ref, s):
        def issue(g, carry):
            for j in range(ROW_UNROLL):
                r = g * ROW_UNROLL + j
                for k in range(TOP_K):
                    _row_copy(obuf_ref, idx_ref[r * TOP_K + k], rows_ref.at[s, k], r, sems.at[s]).start(priority=k % 2)
            return carry

        lax.fori_loop(0, tm // ROW_UNROLL, issue, 0)

    @pl.when(i == 0)
    def _():
        gather(dest_ref, 0)

    @pl.when(i + 1 < n)
    def _():
        gather(dest_next_ref, 1 - slot)

    def drain(g, carry):
        for j in range(ROW_UNROLL * TOP_K):
            _row_copy(obuf_ref, 0, rows_ref.at[slot, 0], 0, sems.at[slot]).wait()
        return carry

    lax.fori_loop(0, tm // ROW_UNROLL, drain, 0)

    ng = COMBINE_GROUPS
    n = tm // ng
    grp = range(ng)

    def tiles(k, g):
        return jnp.concatenate([rows_ref[slot, k, pl.ds(g * n * ROW_TILE + s, n, stride=ROW_TILE), :]
                                for s in range(ROW_TILE)], axis=1)

    gates = [gates_ref[g * n:(g + 1) * n, :] for g in grp]
    y = [gates[g][:, 0:1] * tiles(0, g) for g in grp]
    for k in range(1, TOP_K):
        y = [y[g] + gates[g][:, k:k + 1] * tiles(k, g) for g in grp]
    xres = [jnp.concatenate([x_ref[pl.ds(g * n * ROW_TILE + s, n, stride=ROW_TILE), :] for s in range(ROW_TILE)], axis=1)
            for g in grp]
    x2 = [_layer_norm(alpha * xres[g] + y[g], g2_ref[...], b2_ref[...]) for g in grp]
    gate = [_sigmoid(jnp.dot(x2[g].astype(bf16), wg_ref[...], preferred_element_type=f32)) for g in grp]
    proj = [jnp.dot(p_ref[g * n:(g + 1) * n, :].astype(bf16), wp_ref[...], preferred_element_type=f32) for g in grp]
    for g in grp:
        o_ref[g * n:(g + 1) * n, :] = _layer_norm(alpha * x2[g] + gate[g] * proj[g], g3_ref[...], b3_ref[...])


def combine_ple(dest_flat, gates, xt, p2d, layer, obuf, wg, wp, g2, b2, g3, b3, alpha, tm=512):
    T = xt.shape[0] // ROW_TILE
    nt = T // tm
    act = pl.BlockSpec((tm, 1024), lambda i: (i, 0))
    vec = pl.BlockSpec((1, 1024), lambda i: (0, 0))
    return pl.pallas_call(
        functools.partial(_combine_kernel, tm=tm, alpha=alpha),
        grid=(nt,),
        in_specs=[pl.BlockSpec((tm * TOP_K,), lambda i: (i,), memory_space=pltpu.SMEM),
                  pl.BlockSpec((tm * TOP_K,), lambda i: (jnp.minimum(i + 1, nt - 1),), memory_space=pltpu.SMEM),
                  pl.BlockSpec((tm, LANES), lambda i: (i, 0)),
                  pl.BlockSpec((tm * ROW_TILE, LANES), lambda i: (i, 0)),
                  pl.BlockSpec((tm, PLE_DIM), lambda i: (layer * nt + i, 0)),
                  pl.BlockSpec(memory_space=pl.ANY),
                  pl.BlockSpec((1024, 1024), lambda i: (0, 0)),
                  pl.BlockSpec((PLE_DIM, 1024), lambda i: (0, 0)),
                  vec, vec, vec, vec],
        out_specs=act,
        out_shape=jax.ShapeDtypeStruct((T, 1024), f32),
        scratch_shapes=[pltpu.VMEM((2, TOP_K, tm * ROW_TILE, LANES), f32), pltpu.SemaphoreType.DMA((2,))],
        compiler_params=_params(("arbitrary",)),
        name="combine_ple",
    )(dest_flat, dest_flat, gates, xt, p2d, obuf, wg, wp, g2, b2, g3, b3)


def _t5_bucket_np(dist):
    max_exact = REL_BUCKETS // 2
    d = np.maximum(dist.astype(np.float32), np.float32(1.0))
    large = max_exact + (np.log(d / np.float32(max_exact)) / np.float32(math.log(REL_MAX_DISTANCE / max_exact))
                         * np.float32(REL_BUCKETS - max_exact)).astype(np.int32)
    large = np.minimum(large, REL_BUCKETS - 1)
    return np.where(dist < max_exact, dist, large)


def _swa_bias_table(rel_bias):
    dist = np.arange(2 * WINDOW - 1, -WINDOW, -1)
    in_window = (dist >= 0) & (dist < WINDOW)
    per_dist = rel_bias[_t5_bucket_np(np.maximum(dist, 0))].astype(f32).T
    per_dist = jnp.where(jnp.asarray(in_window)[None], per_dist, NEG_BIG)
    n = 3 * WINDOW - 1
    skew = jnp.tile(jnp.pad(per_dist, ((0, 0), (0, 1))), (1, WINDOW))[:, :WINDOW * n].reshape(-1, WINDOW, n)
    bias = jnp.transpose(skew[:, :, WINDOW - 1:], (0, 2, 1))
    first = jnp.where((jnp.arange(2 * WINDOW) >= WINDOW)[None, :, None], bias, NEG_BIG)
    return jnp.stack([bias, first])


def _pad_row(v, width=LANES, fill=0.0):
    v = v.astype(f32).reshape(1, -1)
    return jnp.pad(v, ((0, 0), (0, width - v.shape[1])), constant_values=fill)


def _wcat(w_in):
    cols = [w_in[:, 0:4096], w_in[:, 4112:6160], w_in[:, 6160:7184], w_in[:, 7696:10768],
            w_in[:, 7184:7440], w_in[:, 7440:7696], w_in[:, 4096:4112],
            jnp.zeros((w_in.shape[0], FAT_W - 10768), w_in.dtype)]
    return jnp.concatenate(cols, axis=1).astype(bf16)


def kernel(x, p, w_in, conv_qkv_w, gdn_a_log, gdn_dt_bias, gdn_norm_w, rg_conv_w, rg_conv_b, rg_w_a, rg_b_a, rg_w_x, rg_b_x, rg_lambda, attn_sinks, rel_bias, w_o_gdn, w_o_lru, w_o_swa, w_out, ln1_g, ln1_b, router_w, router_b, w_gu, b_gu, w_down, b_down, ln2_g, ln2_b, ple_w_gate, ple_w_proj, ln3_g, ln3_b):
    B, S, D = x.shape
    depth = w_in.shape[0]
    T = B * S
    A = T * TOP_K
    alpha = (2.0 * depth) ** 0.25
    P = A + N_EXPERTS * EXPERT_BLK
    nblk = P // EXPERT_BLK
    row = lambda v: v.astype(f32).reshape(1, -1)

    bias_tab = _swa_bias_table(rel_bias)
    p2d = p.reshape(depth * T, PLE_DIM)
    xc = x.reshape(T, D)
    for i in range(depth):
        fat = inproj(xc, _wcat(w_in[i]))
        o_gdn, buf0 = gdn_branch(fat, conv_qkv_w[i], _pad_row(gdn_a_log[i]), _pad_row(gdn_dt_bias[i]),
                                 row(gdn_norm_w[i]), B, S, P * ROW_TILE)
        wax = jnp.concatenate([rg_w_a[i], rg_w_x[i]], axis=-1).astype(bf16)
        o_lru = lru_branch(fat, rg_conv_w[i], row(rg_conv_b[i]), wax, row(rg_b_a[i]), row(rg_b_x[i]),
                           row(rg_lambda[i]), B, S)
        o_swa = swa_branch(fat, bias_tab, attn_sinks[i].astype(f32), B, S)
        x1t = merge_ln(o_gdn, o_lru, o_swa, fat, xc, w_o_gdn[i].astype(bf16), w_o_lru[i].astype(bf16),
                           w_o_swa[i].astype(bf16), w_out[i].astype(bf16), row(ln1_g[i]), row(ln1_b[i]), alpha)

        rw_pad = jnp.pad(router_w[i].astype(f32), ((0, 0), (0, LANES - N_EXPERTS)))
        rw_hi = rw_pad.astype(bf16)
        rw_pad = jnp.concatenate([rw_hi, (rw_pad - rw_hi.astype(f32)).astype(bf16)], axis=1)
        rb_pad = _pad_row(router_b[i], fill=NEG_BIG)
        gates, eidx, rank, cnt = route(x1t, rw_pad, rb_pad)
        counts = cnt[0, :N_EXPERTS].astype(i32)
        padded = ((counts + EXPERT_BLK - 1) // EXPERT_BLK) * EXPERT_BLK
        pad_ends = jnp.cumsum(padded)
        pad_starts = pad_ends - padded
        hit = eidx[:, :TOP_K, None] == jnp.arange(N_EXPERTS, dtype=i32)
        dest = (jnp.sum(jnp.where(hit, pad_starts, 0), axis=-1) + rank[:, :TOP_K]).reshape(A)
        blk_start = jnp.arange(nblk, dtype=i32) * EXPERT_BLK
        blk_e = jnp.minimum(jnp.sum((pad_ends[None, :] <= blk_start[:, None]).astype(i32), axis=1),
                            N_EXPERTS - 1).astype(i32)
        nused = (pad_ends[-1:] // EXPERT_BLK).astype(i32)

        buf, wgu = dispatch_prep(dest, x1t, buf0, w_gu, i)
        bgu = jnp.transpose(b_gu[i].reshape(N_EXPERTS, -1, LANES, 2), (0, 1, 3, 2)).reshape(N_EXPERTS, 1, -1)
        obuf = experts(blk_e, nused, buf, wgu, bgu, w_down, i, b_down[i][:, None, :])
        xc = combine_ple(dest, gates, x1t, p2d, i, obuf, ple_w_gate[i].astype(bf16),
                         ple_w_proj[i].astype(bf16), row(ln2_g[i]), row(ln2_b[i]), row(ln3_g[i]), row(ln3_b[i]),
                         alpha)
    return xc.reshape(B, S, D)
```

```python
import functools
import math

import numpy as np
import jax
import jax.numpy as jnp
from jax import lax
from jax.experimental import pallas as pl
from jax.experimental.pallas import tpu as pltpu

f32 = jnp.float32
bf16 = jnp.bfloat16
i32 = jnp.int32

D_MODEL = 1024
PLE_DIM = 256
GDN_HEADS = 8
GDN_HEAD_DIM = 128
GDN_CHUNK = 64
CONV_WIDTH = 4
LRU_BLOCKS = 8
LRU_BLOCK_DIM = 128
LRU_C = 8.0
SWA_Q_HEADS = 16
SWA_KV_HEADS = 4
SWA_HEAD_DIM = 64
SWA_GROUP = 4
WINDOW = 128
REL_BUCKETS = 32
REL_MAX_DISTANCE = 128
N_EXPERTS = 32
TOP_K = 4
SWIGLU_LIMIT = 7.0
SWIGLU_ALPHA = 1.702
LN_EPS = 1e-5
NORM_EPS = 1e-6
NEG_BIG = -1e30

LANES = 128
VMEM_LIMIT = 56 * 1024 * 1024

C_GQ, C_GK, C_GV, C_GZ = 0, 1024, 2048, 3072
C_LX, C_LG = 4096, 5120
C_SQ = 6144
C_MA, C_MB, C_MC = 7168, 8192, 9216
C_SK, C_SV = 10240, 10496
C_AB = 10752
FAT_W = 10880
FAT_TN = 2176

ROW_TILE = D_MODEL // LANES

EXPERT_BLK = 512
MERGE_GROUPS = 2
ROUTE_GROUPS = 4
COMBINE_GROUPS = 4
CHUNKS_IN_FLIGHT = 4
SWA_KV_IN_FLIGHT = 4


def _params(sem):
    return pltpu.CompilerParams(dimension_semantics=sem, vmem_limit_bytes=VMEM_LIMIT)


def _sigmoid(x):
    return 1.0 / (1.0 + jnp.exp(-x))


def _softplus(x):
    return jnp.maximum(x, 0.0) + jnp.log(1.0 + jnp.exp(-jnp.abs(x)))


def _layer_norm(z, g, b):
    mu = jnp.mean(z, axis=-1, keepdims=True)
    zc = z - mu
    var = jnp.mean(zc * zc, axis=-1, keepdims=True)
    return zc * lax.rsqrt(var + LN_EPS) * g + b


def _inproj_kernel(x_ref, w_ref, o_ref, xb_ref):
    @pl.when(pl.program_id(1) == 0)
    def _():
        xb_ref[...] = x_ref[...].astype(bf16)

    o_ref[...] = jnp.dot(xb_ref[...], w_ref[...], preferred_element_type=f32).astype(o_ref.dtype)


def inproj(x2d, wcat, tm=2048):
    T, K = x2d.shape
    N = wcat.shape[1]
    tn = FAT_TN
    return pl.pallas_call(
        _inproj_kernel,
        grid=(T // tm, N // tn),
        in_specs=[pl.BlockSpec((tm, K), lambda i, j: (i, 0)),
                  pl.BlockSpec((K, tn), lambda i, j: (0, j))],
        out_specs=pl.BlockSpec((tm, tn), lambda i, j: (i, j)),
        out_shape=jax.ShapeDtypeStruct((T, N), bf16),
        scratch_shapes=[pltpu.VMEM((tm, K), bf16)],
        compiler_params=_params(("arbitrary", "arbitrary")),
        name="inproj",
    )(x2d, wcat)


ZERO_FILL_COPIES = 4


def _zero_fill(step, zero_ref, zbuf_ref, zsem, zero_rows):
    zr = zero_ref.shape[0]
    return [pltpu.make_async_copy(zero_ref, zbuf_ref.at[pl.ds(pl.multiple_of(step * zero_rows + c * zr, zr), zr)], zsem)
            for c in range(zero_rows // zr)]


def _shift_matrices(ts):
    ri = lax.broadcasted_iota(i32, (ts, ts), 0)
    ci = lax.broadcasted_iota(i32, (ts, ts), 1)
    return [jnp.where(ri - ci == d, 1.0, 0.0).astype(bf16) for d in (3, 2, 1)]


def _causal_conv_silu(src_ref, dst_ref, cz_ref, carry_ref, w, smats, ts, head_scale=None):
    xb = src_ref[...]
    x = xb.astype(f32)
    y = w[3:4, :] * x
    for j, sm in enumerate(smats):
        y = y + w[j:j + 1, :] * jnp.dot(sm, xb, preferred_element_type=f32)
    cz_ref[0:8, :] = carry_ref[...]
    cz_ref[8:16, :] = jnp.zeros((8, x.shape[1]), f32)
    corr = w[0:1, :] * cz_ref[5:13, :] + w[1:2, :] * cz_ref[6:14, :] + w[2:3, :] * cz_ref[7:15, :]
    carry_ref[...] = x[ts - 8:ts, :]

    def post(rows, yv):
        a = yv * _sigmoid(yv)
        if head_scale is None:
            dst_ref[rows, :] = a
            return
        for h in range(a.shape[1] // LANES):
            cs = slice(h * LANES, (h + 1) * LANES)
            ah = a[:, cs]
            dst_ref[rows, cs] = ah * (lax.rsqrt(jnp.sum(ah * ah, axis=-1, keepdims=True) + NORM_EPS) * head_scale)

    post(slice(0, 8), y[0:8] + corr)
    post(slice(8, ts), y[8:ts])


def _gdn_kernel(q_ref, k_ref, v_ref, z_ref, ab_ref, cw_ref, alog_ref, dtb_ref, nw_ref, o_ref, zbuf_ref,
                xpad_ref, qs_ref, ks_ref, vs_ref, carry_ref, state_ref, g_ref, beta_ref,
                u_s, lhs_s, intra_s, kdt_s, zero_ref, zsem, *, ts, zero_rows):
    C = GDN_CHUNK
    D = GDN_HEAD_DIM
    P = 2 * C

    step = pl.program_id(0) * pl.num_programs(1) + pl.program_id(1)

    @pl.when(step == 0)
    def _():
        zero_ref[...] = jnp.zeros_like(zero_ref)

    fills = _zero_fill(step, zero_ref, zbuf_ref, zsem, zero_rows)
    for f in fills:
        f.start()

    @pl.when(pl.program_id(1) == 0)
    def _():
        carry_ref[...] = jnp.zeros_like(carry_ref)
        state_ref[...] = jnp.zeros_like(state_ref)

    smats = _shift_matrices(ts)
    for p, (src, dst, scale) in enumerate(((q_ref, qs_ref, D ** -0.5), (k_ref, ks_ref, 1.0), (v_ref, vs_ref, None))):
        _causal_conv_silu(src, dst, xpad_ref, carry_ref.at[p], cw_ref[:, p * 1024:(p + 1) * 1024], smats, ts, scale)

    ab = ab_ref[...].astype(f32)
    g = -jnp.exp(alog_ref[...]) * _softplus(ab + dtb_ref[...])
    rin = lax.broadcasted_iota(i32, (ts, LANES), 0) & (C - 1)
    gsum = g
    for d in (1, 2, 4, 8, 16, 32):
        gsum = gsum + jnp.where(rin >= d, pltpu.roll(gsum, d, 0), 0.0)
    g_ref[...] = gsum
    beta_ref[...] = _sigmoid(ab)

    ri = lax.broadcasted_iota(i32, (P, P), 0)
    ci = lax.broadcasted_iota(i32, (P, P), 1)
    same = (ri >= C) == (ci >= C)
    eye = ri == ci
    causal = same & (ri >= ci)
    strict = same & (ri > ci)
    eye_f = jnp.where(eye, 1.0, 0.0).astype(f32)
    first_cols = ci < C
    nw = nw_ref[...]

    def stack(a, b):
        return jnp.concatenate([a, b], axis=0)

    def mm(a, b):
        return jnp.dot(a, b, preferred_element_type=f32)

    npair = GDN_HEADS // 2
    nchunk = ts // C
    hcols = [slice(h * D, (h + 1) * D) for h in range(GDN_HEADS)]

    for cg in range(0, nchunk, CHUNKS_IN_FLIGHT):
        probs = [(c, hp) for c in range(cg, cg + CHUNKS_IN_FLIGHT) for hp in range(npair)]
        qn, kn, vb, gcol, eg, egl, kb = [], [], [], [], [], [], []
        for c, hp in probs:
            rows = slice(c * C, (c + 1) * C)
            c0, c1 = hcols[2 * hp], hcols[2 * hp + 1]
            gc = g_ref[rows, :]
            bc = beta_ref[rows, :]
            qn.append(stack(qs_ref[rows, c0], qs_ref[rows, c1]))
            kn.append(stack(ks_ref[rows, c0], ks_ref[rows, c1]))
            v2 = stack(vs_ref[rows, c0], vs_ref[rows, c1])
            h0, h1 = 2 * hp, 2 * hp + 1
            gcl = stack(gc[:, h0:h0 + 1], gc[:, h1:h1 + 1])
            bcl = stack(bc[:, 8 + h0:9 + h0], bc[:, 8 + h1:9 + h1])
            glast = stack(jnp.broadcast_to(gc[C - 1:C, h0:h0 + 1], (C, 1)),
                          jnp.broadcast_to(gc[C - 1:C, h1:h1 + 1], (C, 1)))
            gcol.append(gcl)
            eg.append(jnp.exp(gcl))
            egl.append(jnp.exp(glast - gcl))
            kb.append(kn[-1] * bcl)
            vb.append(v2 * bcl)
        n = len(probs)
        a2 = [lax.dot_general(stack(kb[i], qn[i]).astype(bf16), kn[i].astype(bf16), (((1,), (1,)), ((), ())),
                              preferred_element_type=f32) for i in range(n)]
        lmat, intra = [], []
        for i in range(n):
            gm = jnp.broadcast_to(gcol[i], (P, P))
            grow = jnp.sum(jnp.where(eye, gm, 0.0), axis=0, keepdims=True)
            decay = jnp.where(causal, jnp.exp(jnp.minimum(gm - grow, 0.0)), 0.0)
            lmat.append(jnp.where(strict, a2[i][:P] * decay, 0.0))
            intra.append(a2[i][P:] * decay)
        lb = [l.astype(bf16) for l in lmat]
        xm = [eye_f - l for l in lmat]
        pm = [mm(b, b) for b in lb]
        for it in range(5):
            pb = [p.astype(bf16) for p in pm]
            xm = [x + mm(x.astype(bf16), b) for x, b in zip(xm, pb)]
            if it < 4:
                pm = [mm(b, b) for b in pb]
        uw = [mm(xm[i].astype(bf16), jnp.concatenate([vb[i], kb[i] * eg[i]], axis=1).astype(bf16))
              for i in range(n)]
        for i, (c, hp) in enumerate(probs):
            j = c * npair + hp
            w2 = uw[i][:, D:]
            qd = qn[i] * eg[i]
            kdt = (kn[i] * egl[i]).T
            u_s[j] = uw[i][:, :D]
            lhs_s[j, 0] = stack(w2[:C], qd[:C]).astype(bf16)
            lhs_s[j, 1] = stack(w2[C:], qd[C:]).astype(bf16)
            intra_s[j] = intra[i].astype(bf16)
            kdt_s[j, 0] = jnp.where(first_cols, kdt, 0.0).astype(bf16)
            kdt_s[j, 1] = jnp.where(first_cols, 0.0, kdt).astype(bf16)

    for c in range(nchunk):
        rows = slice(c * C, (c + 1) * C)
        gl = g_ref[(c + 1) * C - 1:(c + 1) * C, :]
        st = [state_ref[h] for h in range(GDN_HEADS)]
        wq = [mm(lhs_s[c * npair + h // 2, h % 2], st[h].astype(bf16)) for h in range(GDN_HEADS)]
        vnb = [(u_s[c * npair + hp] - stack(wq[2 * hp][:C], wq[2 * hp + 1][:C])).astype(bf16)
               for hp in range(npair)]
        o2 = [stack(wq[2 * hp][C:], wq[2 * hp + 1][C:]) + mm(intra_s[c * npair + hp], vnb[hp])
              for hp in range(npair)]
        for h in range(GDN_HEADS):
            state_ref[h] = st[h] * jnp.exp(gl[:, h:h + 1]) + mm(kdt_s[c * npair + h // 2, h % 2], vnb[h // 2])
        for hp in range(npair):
            c0, c1 = hcols[2 * hp], hcols[2 * hp + 1]
            z2 = stack(z_ref[rows, c0], z_ref[rows, c1]).astype(f32)
            on = (o2[hp] * lax.rsqrt(jnp.mean(o2[hp] * o2[hp], axis=-1, keepdims=True) + NORM_EPS) * nw
                  * (z2 * _sigmoid(z2))).astype(o_ref.dtype)
            o_ref[rows, c0] = on[:C]
            o_ref[rows, c1] = on[C:]
    for f in fills:
        f.wait()


def gdn_branch(fat, conv_w, a_log_row, dt_bias_row, norm_w_row, B, S, buf_rows, ts=256):
    T = B * S
    ns = S // ts
    nprob = (ts // GDN_CHUNK) * (GDN_HEADS // 2)
    zero_rows = buf_rows // (B * ns)
    assert zero_rows * B * ns == buf_rows and zero_rows % (ZERO_FILL_COPIES * ROW_TILE) == 0
    row = lambda b, s: b * ns + s
    blk = lambda cb: pl.BlockSpec((ts, 1024), lambda b, s: (row(b, s), cb))
    full = lambda shp: pl.BlockSpec(shp, lambda b, s: (0,) * len(shp))
    return pl.pallas_call(
        functools.partial(_gdn_kernel, ts=ts, zero_rows=zero_rows),
        grid=(B, ns),
        in_specs=[blk(C_GQ // 1024), blk(C_GK // 1024), blk(C_GV // 1024), blk(C_GZ // 1024),
                  pl.BlockSpec((ts, LANES), lambda b, s: (row(b, s), C_AB // LANES)),
                  full((CONV_WIDTH, 3072)), full((1, LANES)), full((1, LANES)), full((1, LANES))],
        out_specs=[pl.BlockSpec((ts, 1024), lambda b, s: (row(b, s), 0)), pl.BlockSpec(memory_space=pl.ANY)],
        out_shape=[jax.ShapeDtypeStruct((T, 1024), bf16), jax.ShapeDtypeStruct((buf_rows, LANES), f32)],
        scratch_shapes=[pltpu.VMEM((16, 1024), f32),
                        pltpu.VMEM((ts, 1024), f32), pltpu.VMEM((ts, 1024), f32), pltpu.VMEM((ts, 1024), f32),
                        pltpu.VMEM((3, 8, 1024), f32),
                        pltpu.VMEM((GDN_HEADS, GDN_HEAD_DIM, GDN_HEAD_DIM), f32),
                        pltpu.VMEM((ts, LANES), f32), pltpu.VMEM((ts, LANES), f32),
                        pltpu.VMEM((nprob, 128, GDN_HEAD_DIM), f32),
                        pltpu.VMEM((nprob, 2, 128, GDN_HEAD_DIM), bf16),
                        pltpu.VMEM((nprob, 128, 128), bf16),
                        pltpu.VMEM((nprob, 2, GDN_HEAD_DIM, 128), bf16),
                        pltpu.VMEM((zero_rows // ZERO_FILL_COPIES, LANES), f32), pltpu.SemaphoreType.DMA(())],
        compiler_params=_params(("arbitrary", "arbitrary")),
        name="gdn",
    )(fat, fat, fat, fat, fat, conv_w, a_log_row, dt_bias_row, norm_w_row)


def _lru_kernel(x_ref, gate_ref, cw_ref, cb_ref, wax_ref, ba_ref, bx_ref, lam_ref, o_ref,
                cz_ref, xc_ref, a_ref, u_ref, h_ref, carry_ref, hc_ref, *, ts):
    nb = x_ref.shape[0]

    @pl.when(pl.program_id(0) == 0)
    def _():
        carry_ref[...] = jnp.zeros_like(carry_ref)
        hc_ref[...] = jnp.zeros_like(hc_ref)

    w = cw_ref[...]
    smats = _shift_matrices(ts)
    cz_ref[8:16, :] = jnp.zeros((8, D_MODEL), f32)
    for bi in range(nb):
        xb = x_ref[bi]
        x = xb.astype(f32)
        y = w[3:4, :] * x + cb_ref[...]
        for j, sm in enumerate(smats):
            y = y + w[j:j + 1, :] * jnp.dot(sm, xb, preferred_element_type=f32)
        cz_ref[0:8, :] = carry_ref[bi]
        corr = w[0:1, :] * cz_ref[5:13, :] + w[1:2, :] * cz_ref[6:14, :] + w[2:3, :] * cz_ref[7:15, :]
        carry_ref[bi] = x[ts - 8:ts, :]
        xc_ref[bi * ts:bi * ts + 8, :] = y[0:8] + corr
        xc_ref[bi * ts + 8:(bi + 1) * ts, :] = y[8:ts]

    la2 = (-LRU_C / math.log(2.0)) * _softplus(-lam_ref[...])
    for blk in range(LRU_BLOCKS):
        cs = slice(blk * LRU_BLOCK_DIM, (blk + 1) * LRU_BLOCK_DIM)
        xc = xc_ref[:, cs]
        ri = jnp.dot(xc.astype(bf16), wax_ref[blk], preferred_element_type=f32)
        r = _sigmoid(ri[:, :LRU_BLOCK_DIM] + ba_ref[:, cs])
        i = _sigmoid(ri[:, LRU_BLOCK_DIM:] + bx_ref[:, cs])
        a = jnp.exp2(r * la2[:, cs])
        u = jnp.sqrt(1.0 - a * a) * (i * xc)
        for bi in range(nb):
            a_ref[blk, pl.ds(bi, ts, stride=nb), :] = a[bi * ts:(bi + 1) * ts]
            u_ref[blk, pl.ds(bi, ts, stride=nb), :] = u[bi * ts:(bi + 1) * ts]

    def step(t, hs):
        view = pl.ds(pl.multiple_of(t * nb, nb), nb)
        out = []
        for blk in range(LRU_BLOCKS):
            h = a_ref[blk, view, :] * hs[blk] + u_ref[blk, view, :]
            h_ref[blk, view, :] = h
            out.append(h)
        return tuple(out)

    h0 = tuple(hc_ref[:, blk * LRU_BLOCK_DIM:(blk + 1) * LRU_BLOCK_DIM] for blk in range(LRU_BLOCKS))
    hT = lax.fori_loop(0, ts, step, h0, unroll=8)
    for blk in range(LRU_BLOCKS):
        cs = slice(blk * LRU_BLOCK_DIM, (blk + 1) * LRU_BLOCK_DIM)
        hc_ref[:, cs] = hT[blk]
        for bi in range(nb):
            hb = h_ref[blk, pl.ds(bi, ts, stride=nb), :]
            o_ref[bi, :, cs] = (hb * jax.nn.gelu(gate_ref[bi, :, cs].astype(f32))).astype(o_ref.dtype)


def lru_branch(fat, conv_w, conv_b, wax, b_a, b_x, lam, B, S, ts=128):
    assert B == 8, "the recurrence keeps the batch on the 8 sublanes of a vreg"
    T = B * S
    fat3 = fat.reshape(B, S, fat.shape[1])
    full = lambda shp: pl.BlockSpec(shp, lambda s: (0,) * len(shp))
    out = pl.pallas_call(
        functools.partial(_lru_kernel, ts=ts),
        grid=(S // ts,),
        in_specs=[pl.BlockSpec((B, ts, 1024), lambda s: (0, s, C_LX // 1024)),
                  pl.BlockSpec((B, ts, 1024), lambda s: (0, s, C_LG // 1024)),
                  full((CONV_WIDTH, 1024)), full((1, 1024)), full((LRU_BLOCKS, LRU_BLOCK_DIM, 2 * LRU_BLOCK_DIM)),
                  full((1, 1024)), full((1, 1024)), full((1, 1024))],
        out_specs=pl.BlockSpec((B, ts, 1024), lambda s: (0, s, 0)),
        out_shape=jax.ShapeDtypeStruct((B, S, 1024), bf16),
        scratch_shapes=[pltpu.VMEM((16, 1024), f32), pltpu.VMEM((B * ts, 1024), f32),
                        pltpu.VMEM((LRU_BLOCKS, B * ts, LRU_BLOCK_DIM), f32),
                        pltpu.VMEM((LRU_BLOCKS, B * ts, LRU_BLOCK_DIM), f32),
                        pltpu.VMEM((LRU_BLOCKS, B * ts, LRU_BLOCK_DIM), f32),
                        pltpu.VMEM((B, 8, 1024), f32), pltpu.VMEM((B, 1024), f32)],
        compiler_params=_params(("arbitrary",)),
        name="lru",
    )(fat3, fat3, conv_w, conv_b, wax, b_a, b_x, lam)
    return out.reshape(T, 1024)


def _swa_kernel(q_ref, kc_ref, kp_ref, vc_ref, vp_ref, biast_ref, sink_ref, o_ref, kb_ref, vt_ref, *, tq):
    W = WINDOW
    hd = SWA_HEAD_DIM
    kb_ref[0:W, :] = kp_ref[...]
    kb_ref[W:W + tq, :] = kc_ref[...]
    vt_ref[0] = vp_ref[...].astype(f32).T.astype(bf16)
    for j in range(tq // W):
        vt_ref[j + 1] = vc_ref[j * W:(j + 1) * W, :].astype(f32).T.astype(bf16)
    first_tile = pl.program_id(1) == 0
    ones_rows = jnp.ones((8, 2 * W), bf16)
    scale = hd ** -0.5

    def qblock(n, carry):
        r0 = pl.multiple_of(n * W, W)
        tab = jnp.where(jnp.logical_and(first_tile, n == 0), 1, 0)
        qs = q_ref[pl.ds(r0, W), :] * scale
        vt_band = jnp.concatenate([vt_ref[n], vt_ref[n + 1]], axis=1)
        outs = []
        for hk0 in range(0, SWA_KV_HEADS, SWA_KV_IN_FLIGHT):
            hks = range(hk0, hk0 + SWA_KV_IN_FLIGHT)
            heads = range(hk0 * SWA_GROUP, (hk0 + SWA_KV_IN_FLIGHT) * SWA_GROUP)
            kk = {hk: kb_ref[pl.ds(r0, 2 * W), hk * hd:(hk + 1) * hd] for hk in hks}
            lhs_v = {hk: jnp.concatenate([vt_band[hk * hd:(hk + 1) * hd, :], ones_rows], axis=0) for hk in hks}
            st = [lax.dot_general(kk[h // SWA_GROUP], qs[:, h * hd:(h + 1) * hd], (((1,), (1,)), ((), ())),
                                  preferred_element_type=f32) + biast_ref[tab, h] for h in heads]
            m = [jnp.maximum(jnp.max(t, axis=0, keepdims=True), sink_ref[h]) for t, h in zip(st, heads)]
            pt = [jnp.exp(t - mm).astype(bf16) for t, mm in zip(st, m)]
            ov = [jnp.dot(lhs_v[h // SWA_GROUP], t, preferred_element_type=f32) for t, h in zip(pt, heads)]
            outs += [o[:hd] / (o[hd:hd + 1] + jnp.exp(sink_ref[h] - mm)) for o, mm, h in zip(ov, m, heads)]
        o_ref[pl.ds(r0, W), :] = jnp.concatenate(outs, axis=0).T.astype(o_ref.dtype)
        return carry

    lax.fori_loop(0, tq // W, qblock, 0)


def swa_branch(fat, bias_tab_t, sinks, B, S, tq=1024):
    T = B * S
    ns = S // tq
    nb = S // WINDOW
    per = tq // WINDOW
    row = lambda b, s: b * ns + s
    prev = lambda b, s: b * nb + jnp.maximum(s * per - 1, 0)
    kvw = SWA_KV_HEADS * SWA_HEAD_DIM
    full = lambda shp: pl.BlockSpec(shp, lambda b, s: (0,) * len(shp))
    return pl.pallas_call(
        functools.partial(_swa_kernel, tq=tq),
        grid=(B, ns),
        in_specs=[pl.BlockSpec((tq, 1024), lambda b, s: (row(b, s), C_SQ // 1024)),
                  pl.BlockSpec((tq, kvw), lambda b, s: (row(b, s), C_SK // kvw)),
                  pl.BlockSpec((WINDOW, kvw), lambda b, s: (prev(b, s), C_SK // kvw)),
                  pl.BlockSpec((tq, kvw), lambda b, s: (row(b, s), C_SV // kvw)),
                  pl.BlockSpec((WINDOW, kvw), lambda b, s: (prev(b, s), C_SV // kvw)),
                  full((2, SWA_Q_HEADS, 2 * WINDOW, WINDOW)), pl.BlockSpec(memory_space=pltpu.SMEM)],
        out_specs=pl.BlockSpec((tq, 1024), lambda b, s: (row(b, s), 0)),
        out_shape=jax.ShapeDtypeStruct((T, 1024), bf16),
        scratch_shapes=[pltpu.VMEM((tq + WINDOW, kvw), bf16), pltpu.VMEM((per + 1, kvw, WINDOW), bf16)],
        compiler_params=_params(("arbitrary", "arbitrary")),
        name="swa",
    )(fat, fat, fat, fat, fat, bias_tab_t, sinks)


def _load_row_tiles(ref, n, lead=()):
    return jnp.concatenate([ref[lead + (pl.ds(s, n, stride=ROW_TILE), slice(None))] for s in range(ROW_TILE)], axis=1)


def _store_row_tiles(ref, val):
    n = val.shape[0]
    for s in range(ROW_TILE):
        ref[pl.ds(s, n, stride=ROW_TILE), :] = val[:, s * LANES:(s + 1) * LANES]


MERGE_RING = 3


def _merge_kernel(oa_hbm, ob_hbm, oc_hbm, fat_hbm, x_hbm, wa_ref, wb_ref, wc_ref, wo_ref, g_ref, b_ref,
                  o_ref, ot_ref, act_buf, x_buf, sems, *, alpha, tm):
    i = pl.program_id(0)
    nsteps = pl.num_programs(0)
    streams = [(oa_hbm, 0), (ob_hbm, 0), (oc_hbm, 0), (fat_hbm, C_MA), (fat_hbm, C_MB), (fat_hbm, C_MC)]

    def copies(step, slot):
        rows = pl.ds(pl.multiple_of(step * tm, tm), tm)
        cps = [pltpu.make_async_copy(src.at[rows, pl.ds(col, D_MODEL)], act_buf.at[k, slot], sems.at[k, slot])
               for k, (src, col) in enumerate(streams)]
        cps.append(pltpu.make_async_copy(x_hbm.at[rows, :], x_buf.at[slot], sems.at[len(streams), slot]))
        return cps

    @pl.when(i == 0)
    def _():
        for step in range(MERGE_RING - 1):
            @pl.when(step < nsteps)
            def _():
                for cp in copies(step, step):
                    cp.start()

    @pl.when(i + MERGE_RING - 1 < nsteps)
    def _():
        for cp in copies(i + MERGE_RING - 1, (i + MERGE_RING - 1) % MERGE_RING):
            cp.start()

    slot = i % MERGE_RING
    for cp in copies(i, slot):
        cp.wait()

    n = tm // MERGE_GROUPS
    grp = [pl.ds(g * n, n) for g in range(MERGE_GROUPS)]
    ya = [jnp.dot(act_buf[0, slot, r, :], wa_ref[...], preferred_element_type=f32) for r in grp]
    yb = [jnp.dot(act_buf[1, slot, r, :], wb_ref[...], preferred_element_type=f32) for r in grp]
    yc = [jnp.dot(act_buf[2, slot, r, :], wc_ref[...], preferred_element_type=f32) for r in grp]
    mix = [(_sigmoid(act_buf[3, slot, r, :].astype(f32)) * ya[g] + _sigmoid(act_buf[4, slot, r, :].astype(f32)) * yb[g]
            + _sigmoid(act_buf[5, slot, r, :].astype(f32)) * yc[g]).astype(bf16) for g, r in enumerate(grp)]
    y = [jnp.dot(m, wo_ref[...], preferred_element_type=f32) for m in mix]
    for g, r in enumerate(grp):
        x1 = _layer_norm(alpha * x_buf[slot, r, :] + y[g], g_ref[...], b_ref[...])
        o_ref[r, :] = x1
        for s in range(ROW_TILE):
            ot_ref[pl.ds(g * n * ROW_TILE + s, n, stride=ROW_TILE), :] = x1[:, s * LANES:(s + 1) * LANES]


def merge_ln(oa, ob, oc, fat, x2d, wa, wb, wc, wo, g, b, alpha, tm=512):
    T = x2d.shape[0]
    act = pl.BlockSpec((tm, 1024), lambda i: (i, 0))
    hbm = pl.BlockSpec(memory_space=pl.ANY)
    wsp = pl.BlockSpec((1024, 1024), lambda i: (0, 0))
    vec = pl.BlockSpec((1, 1024), lambda i: (0, 0))
    return pl.pallas_call(
        functools.partial(_merge_kernel, alpha=alpha, tm=tm),
        grid=(T // tm,),
        in_specs=[hbm, hbm, hbm, hbm, hbm, wsp, wsp, wsp, wsp, vec, vec],
        out_specs=[act, pl.BlockSpec((tm * ROW_TILE, LANES), lambda i: (i, 0))],
        out_shape=[jax.ShapeDtypeStruct((T, 1024), f32), jax.ShapeDtypeStruct((T * ROW_TILE, LANES), f32)],
        scratch_shapes=[pltpu.VMEM((6, MERGE_RING, tm, D_MODEL), bf16), pltpu.VMEM((MERGE_RING, tm, D_MODEL), f32),
                        pltpu.SemaphoreType.DMA((7, MERGE_RING))],
        compiler_params=_params(("arbitrary",)),
        name="merge_ln",
    )(oa, ob, oc, fat, x2d, wa, wb, wc, wo, g, b)


def _route_kernel(x_ref, rw_ref, rb_ref, gates_ref, eidx_ref, rank_ref, cnt_ref, run_ref, *, tm):
    @pl.when(pl.program_id(0) == 0)
    def _():
        run_ref[...] = jnp.zeros_like(run_ref)

    ng = ROUTE_GROUPS
    n = tm // ng
    grp = range(ng)
    lane = lax.broadcasted_iota(i32, (n, LANES), 1)
    lane_f = lane.astype(f32)
    x = [x_ref[g * n:(g + 1) * n, :] for g in grp]
    xh = [t.astype(bf16) for t in x]
    xl = [(x[g] - xh[g].astype(f32)).astype(bf16) for g in grp]
    hw = [jnp.dot(xh[g], rw_ref[...], preferred_element_type=f32) for g in grp]
    lw = [jnp.dot(xl[g], rw_ref[:, :LANES], preferred_element_type=f32) for g in grp]
    work = [hw[g][:, :LANES] + (hw[g][:, LANES:] + lw[g]) + rb_ref[...] for g in grp]
    vals, idxs, hots = [], [], []
    for _ in range(TOP_K):
        m = [jnp.max(w, axis=-1, keepdims=True) for w in work]
        idx = [jnp.min(jnp.where(work[g] == m[g], lane_f, float(LANES)), axis=-1, keepdims=True)
               for g in grp]
        hot = [lane_f == i for i in idx]
        vals.append(m)
        idxs.append(idx)
        hots.append(hot)
        work = [jnp.where(hot[g], -jnp.inf, work[g]) for g in grp]
    ri = lax.broadcasted_iota(i32, (n, n), 0)
    ci = lax.broadcasted_iota(i32, (n, n), 1)
    tril = jnp.where(ri > ci, 1.0, 0.0).astype(bf16)
    run = run_ref[...]
    for g in grp:
        es = [jnp.exp(vals[k][g] - vals[0][g]) for k in range(TOP_K)]
        den = es[0] + es[1] + es[2] + es[3]
        sel = jnp.zeros((n, LANES), f32)
        for k in range(TOP_K):
            sel = sel + jnp.where(hots[k][g], 1.0, 0.0)
        before = jnp.dot(tril, sel.astype(bf16), preferred_element_type=f32) + run
        run = run + jnp.sum(sel, axis=0, keepdims=True)
        gates = jnp.zeros((n, LANES), f32)
        eidx = jnp.zeros((n, LANES), f32)
        rank = jnp.zeros((n, LANES), f32)
        for k in range(TOP_K):
            rk = jnp.sum(jnp.where(hots[k][g], before, 0.0), axis=-1, keepdims=True)
            gates = jnp.where(lane == k, es[k] / den, gates)
            eidx = jnp.where(lane == k, idxs[k][g], eidx)
            rank = jnp.where(lane == k, rk, rank)
        gates_ref[g * n:(g + 1) * n, :] = gates
        eidx_ref[g * n:(g + 1) * n, :] = eidx.astype(i32)
        rank_ref[g * n:(g + 1) * n, :] = rank.astype(i32)
    run_ref[...] = run
    cnt_ref[...] = run


def route(x2d, rw_pad, rb_pad, tm=512):
    T = x2d.shape[0]
    outb = pl.BlockSpec((tm, LANES), lambda i: (i, 0))
    return pl.pallas_call(
        functools.partial(_route_kernel, tm=tm),
        grid=(T // tm,),
        in_specs=[pl.BlockSpec((tm, 1024), lambda i: (i, 0)),
                  pl.BlockSpec((1024, 2 * LANES), lambda i: (0, 0)),
                  pl.BlockSpec((1, LANES), lambda i: (0, 0))],
        out_specs=[outb, outb, outb, pl.BlockSpec((1, LANES), lambda i: (0, 0))],
        out_shape=[jax.ShapeDtypeStruct((T, LANES), f32), jax.ShapeDtypeStruct((T, LANES), i32),
                   jax.ShapeDtypeStruct((T, LANES), i32), jax.ShapeDtypeStruct((1, LANES), f32)],
        scratch_shapes=[pltpu.VMEM((1, LANES), f32)],
        compiler_params=_params(("arbitrary",)),
        name="route",
    )(x2d, rw_pad, rb_pad)


GU_GROUP = 2 * LANES


def _gu_prep_tile(w_ref, o_ref):
    ri = lax.broadcasted_iota(i32, (GU_GROUP, GU_GROUP), 0)
    ci = lax.broadcasted_iota(i32, (GU_GROUP, GU_GROUP), 1)
    src = jnp.where(ci < LANES, 2 * ci, 2 * (ci - LANES) + 1)
    perm = jnp.where(ri == src, 1.0, 0.0).astype(bf16)
    for g in range(w_ref.shape[3] // GU_GROUP):
        cs = slice(g * GU_GROUP, (g + 1) * GU_GROUP)
        o_ref[0, :, cs] = jnp.dot(w_ref[0, 0, :, cs].astype(bf16), perm, preferred_element_type=f32).astype(bf16)


ROW_UNROLL = 8


def _row_copy(src_ref, src_row, dst_ref, dst_row, sem):
    tile = lambda r: pl.ds(pl.multiple_of(r * ROW_TILE, ROW_TILE), ROW_TILE)
    return pltpu.make_async_copy(src_ref.at[tile(src_row)], dst_ref.at[tile(dst_row)], sem)


def _dispatch_kernel(dest_ref, x_ref, buf_in_ref, w_ref, buf_ref, wo_ref, sem, *, tm, nt, ng):
    del buf_in_ref
    i = pl.program_id(0)

    @pl.when(i < nt)
    def _():
        def issue(g, carry):
            for j in range(ROW_UNROLL):
                r = g * ROW_UNROLL + j
                for k in range(TOP_K):
                    _row_copy(x_ref, r, buf_ref, dest_ref[r * TOP_K + k], sem).start(priority=k % 2)
            return carry

        lax.fori_loop(0, tm // ROW_UNROLL, issue, 0)

    @pl.when(i < ng)
    def _():
        _gu_prep_tile(w_ref, wo_ref)

    @pl.when(i < nt)
    def _():
        def drain(g, carry):
            for j in range(ROW_UNROLL * TOP_K):
                _row_copy(x_ref, 0, buf_ref, 0, sem).wait()
            return carry

        lax.fori_loop(0, tm // ROW_UNROLL, drain, 0)


def dispatch_prep(dest_flat, xt, buf0, w_gu_all, layer, tm=512, tk=512):
    T = xt.shape[0] // ROW_TILE
    _, E, D, N = w_gu_all.shape
    nt = T // tm
    kt = D // tk
    ng = E * kt
    tok = lambda i: jnp.minimum(i, nt - 1)
    gu = lambda i: jnp.minimum(i, ng - 1)
    return pl.pallas_call(
        functools.partial(_dispatch_kernel, tm=tm, nt=nt, ng=ng),
        grid=(max(nt, ng),),
        in_specs=[pl.BlockSpec((tm * TOP_K,), lambda i: (tok(i),), memory_space=pltpu.SMEM),
                  pl.BlockSpec((tm * ROW_TILE, LANES), lambda i: (tok(i), 0)),
                  pl.BlockSpec(memory_space=pl.ANY),
                  pl.BlockSpec((1, 1, tk, N), lambda i: (layer, gu(i) // kt, gu(i) % kt, 0))],
        out_specs=[pl.BlockSpec(memory_space=pl.ANY),
                   pl.BlockSpec((1, tk, N), lambda i: (gu(i) // kt, gu(i) % kt, 0))],
        out_shape=[jax.ShapeDtypeStruct(buf0.shape, buf0.dtype), jax.ShapeDtypeStruct((E, D, N), bf16)],
        scratch_shapes=[pltpu.SemaphoreType.DMA(())],
        input_output_aliases={2: 0},
        compiler_params=_params(("arbitrary",)),
        name="dispatch_prep",
    )(dest_flat, xt, buf0, w_gu_all)


def _expert_kernel(blk_e_ref, nused_ref, x_ref, wgu_ref, bgu_ref, wd_ref, bd_ref, o_ref, wdb_ref):
    i = pl.program_id(0)

    @pl.when(jnp.logical_or(i == 0, blk_e_ref[i] != blk_e_ref[jnp.maximum(i - 1, 0)]))
    def _():
        wdb_ref[...] = wd_ref[0, 0].astype(bf16)

    @pl.when(i < nused_ref[0])
    def _():
        de = wdb_ref.shape[0]
        xb = _load_row_tiles(x_ref, EXPERT_BLK).astype(bf16)
        hgu = jnp.dot(xb, wgu_ref[0], preferred_element_type=f32) + bgu_ref[0]
        acts = []
        for g in range(2 * de // GU_GROUP):
            gate = jnp.minimum(hgu[:, g * GU_GROUP:g * GU_GROUP + LANES], SWIGLU_LIMIT)
            lin = jnp.clip(hgu[:, g * GU_GROUP + LANES:(g + 1) * GU_GROUP], -SWIGLU_LIMIT, SWIGLU_LIMIT)
            acts.append((gate * _sigmoid(SWIGLU_ALPHA * gate) * (lin + 1.0)).astype(bf16))
        act = jnp.concatenate(acts, axis=1)
        _store_row_tiles(o_ref, jnp.dot(act, wdb_ref[...], preferred_element_type=f32) + bd_ref[0])

    @pl.when(i >= nused_ref[0])
    def _():
        o_ref[...] = jnp.zeros_like(o_ref)


def experts(blk_e, nused, buf, wgu, bgu, wd_all, layer, bd):
    D = wd_all.shape[3]
    nblk = buf.shape[0] // (EXPERT_BLK * ROW_TILE)
    de = wd_all.shape[2]
    tile_blk = (EXPERT_BLK * ROW_TILE, LANES)
    live = lambda i, be, nu: jnp.minimum(i, nu[0] - 1)
    grid_spec = pltpu.PrefetchScalarGridSpec(
        num_scalar_prefetch=2,
        grid=(nblk,),
        in_specs=[pl.BlockSpec(tile_blk, lambda i, be, nu: (live(i, be, nu), 0)),
                  pl.BlockSpec((1, D, 2 * de), lambda i, be, nu: (be[i], 0, 0)),
                  pl.BlockSpec((1, 1, 2 * de), lambda i, be, nu: (be[i], 0, 0)),
                  pl.BlockSpec((1, 1, de, D), lambda i, be, nu: (layer, be[i], 0, 0)),
                  pl.BlockSpec((1, 1, D), lambda i, be, nu: (be[i], 0, 0))],
        out_specs=pl.BlockSpec(tile_blk, lambda i, be, nu: (i, 0)),
        scratch_shapes=[pltpu.VMEM((de, D), bf16)],
    )
    return pl.pallas_call(
        _expert_kernel,
        grid_spec=grid_spec,
        out_shape=jax.ShapeDtypeStruct(buf.shape, f32),
        compiler_params=_params(("arbitrary",)),
        name="experts",
    )(blk_e, nused, buf, wgu, bgu, wd_all, bd)


def _combine_kernel(dest_ref, dest_next_ref, gates_ref, x_ref, p_ref, obuf_ref, wg_ref, wp_ref,
                    g2_ref, b2_ref, g3_ref, b3_ref, o_ref, rows_ref, sems, *, tm, alpha):
    i = pl.program_id(0)
    n = pl.num_programs(0)
    slot = i % 2

    def gather(idx_ref, s):
        def issue(g, carry):
            for j in range(ROW_UNROLL):
                r = g * ROW_UNROLL + j
                for k in range(TOP_K):
                    _row_copy(obuf_ref, idx_ref[r * TOP_K + k], rows_ref.at[s, k], r, sems.at[s]).start(priority=k % 2)
            return carry

        lax.fori_loop(0, tm // ROW_UNROLL, issue, 0)

    @pl.when(i == 0)
    def _():
        gather(dest_ref, 0)

    @pl.when(i + 1 < n)
    def _():
        gather(dest_next_ref, 1 - slot)

    def drain(g, carry):
        for j in range(ROW_UNROLL * TOP_K):
            _row_copy(obuf_ref, 0, rows_ref.at[slot, 0], 0, sems.at[slot]).wait()
        return carry

    lax.fori_loop(0, tm // ROW_UNROLL, drain, 0)

    ng = COMBINE_GROUPS
    n = tm // ng
    grp = range(ng)

    def tiles(k, g):
        return jnp.concatenate([rows_ref[slot, k, pl.ds(g * n * ROW_TILE + s, n, stride=ROW_TILE), :]
                                for s in range(ROW_TILE)], axis=1)

    gates = [gates_ref[g * n:(g + 1) * n, :] for g in grp]
    y = [gates[g][:, 0:1] * tiles(0, g) for g in grp]
    for k in range(1, TOP_K):
        y = [y[g] + gates[g][:, k:k + 1] * tiles(k, g) for g in grp]
    x2 = [_layer_norm(alpha * x_ref[g * n:(g + 1) * n, :] + y[g], g2_ref[...], b2_ref[...]) for g in grp]
    gate = [_sigmoid(jnp.dot(x2[g].astype(bf16), wg_ref[...], preferred_element_type=f32)) for g in grp]
    proj = [jnp.dot(p_ref[g * n:(g + 1) * n, :].astype(bf16), wp_ref[...], preferred_element_type=f32) for g in grp]
    for g in grp:
        o_ref[g * n:(g + 1) * n, :] = _layer_norm(alpha * x2[g] + gate[g] * proj[g], g3_ref[...], b3_ref[...])


def combine_ple(dest_flat, gates, x2d, p2d, layer, obuf, wg, wp, g2, b2, g3, b3, alpha, tm=512):
    T = x2d.shape[0]
    nt = T // tm
    act = pl.BlockSpec((tm, 1024), lambda i: (i, 0))
    vec = pl.BlockSpec((1, 1024), lambda i: (0, 0))
    return pl.pallas_call(
        functools.partial(_combine_kernel, tm=tm, alpha=alpha),
        grid=(nt,),
        in_specs=[pl.BlockSpec((tm * TOP_K,), lambda i: (i,), memory_space=pltpu.SMEM),
                  pl.BlockSpec((tm * TOP_K,), lambda i: (jnp.minimum(i + 1, nt - 1),), memory_space=pltpu.SMEM),
                  pl.BlockSpec((tm, LANES), lambda i: (i, 0)),
                  act,
                  pl.BlockSpec((tm, PLE_DIM), lambda i: (layer * nt + i, 0)),
                  pl.BlockSpec(memory_space=pl.ANY),
                  pl.BlockSpec((1024, 1024), lambda i: (0, 0)),
                  pl.BlockSpec((PLE_DIM, 1024), lambda i: (0, 0)),
                  vec, vec, vec, vec],
        out_specs=act,
        out_shape=jax.ShapeDtypeStruct((T, 1024), f32),
        scratch_shapes=[pltpu.VMEM((2, TOP_K, tm * ROW_TILE, LANES), f32), pltpu.SemaphoreType.DMA((2,))],
        compiler_params=_params(("arbitrary",)),
        name="combine_ple",
    )(dest_flat, dest_flat, gates, x2d, p2d, obuf, wg, wp, g2, b2, g3, b3)


def _t5_bucket_np(dist):
    max_exact = REL_BUCKETS // 2
    d = np.maximum(dist.astype(np.float32), np.float32(1.0))
    large = max_exact + (np.log(d / np.float32(max_exact)) / np.float32(math.log(REL_MAX_DISTANCE / max_exact))
                         * np.float32(REL_BUCKETS - max_exact)).astype(np.int32)
    large = np.minimum(large, REL_BUCKETS - 1)
    return np.where(dist < max_exact, dist, large)


def _swa_bias_table(rel_bias):
    dist = np.arange(2 * WINDOW - 1, -WINDOW, -1)
    in_window = (dist >= 0) & (dist < WINDOW)
    per_dist = rel_bias[_t5_bucket_np(np.maximum(dist, 0))].astype(f32).T
    per_dist = jnp.where(jnp.asarray(in_window)[None], per_dist, NEG_BIG)
    n = 3 * WINDOW - 1
    skew = jnp.tile(jnp.pad(per_dist, ((0, 0), (0, 1))), (1, WINDOW))[:, :WINDOW * n].reshape(-1, WINDOW, n)
    bias = jnp.transpose(skew[:, :, WINDOW - 1:], (0, 2, 1))
    first = jnp.where((jnp.arange(2 * WINDOW) >= WINDOW)[None, :, None], bias, NEG_BIG)
    return jnp.stack([bias, first])


def _pad_row(v, width=LANES, fill=0.0):
    v = v.astype(f32).reshape(1, -1)
    return jnp.pad(v, ((0, 0), (0, width - v.shape[1])), constant_values=fill)


def _wcat(w_in):
    cols = [w_in[:, 0:4096], w_in[:, 4112:6160], w_in[:, 6160:7184], w_in[:, 7696:10768],
            w_in[:, 7184:7440], w_in[:, 7440:7696], w_in[:, 4096:4112],
            jnp.zeros((w_in.shape[0], FAT_W - 10768), w_in.dtype)]
    return jnp.concatenate(cols, axis=1).astype(bf16)


def kernel(x, p, w_in, conv_qkv_w, gdn_a_log, gdn_dt_bias, gdn_norm_w, rg_conv_w, rg_conv_b, rg_w_a, rg_b_a, rg_w_x, rg_b_x, rg_lambda, attn_sinks, rel_bias, w_o_gdn, w_o_lru, w_o_swa, w_out, ln1_g, ln1_b, router_w, router_b, w_gu, b_gu, w_down, b_down, ln2_g, ln2_b, ple_w_gate, ple_w_proj, ln3_g, ln3_b):
    B, S, D = x.shape
    depth = w_in.shape[0]
    T = B * S
    A = T * TOP_K
    alpha = (2.0 * depth) ** 0.25
    P = A + N_EXPERTS * EXPERT_BLK
    nblk = P // EXPERT_BLK
    row = lambda v: v.astype(f32).reshape(1, -1)

    bias_tab = _swa_bias_table(rel_bias)
    p2d = p.reshape(depth * T, PLE_DIM)
    xc = x.reshape(T, D)
    for i in range(depth):
        fat = inproj(xc, _wcat(w_in[i]))
        o_gdn, buf0 = gdn_branch(fat, conv_qkv_w[i], _pad_row(gdn_a_log[i]), _pad_row(gdn_dt_bias[i]),
                                 row(gdn_norm_w[i]), B, S, P * ROW_TILE)
        wax = jnp.concatenate([rg_w_a[i], rg_w_x[i]], axis=-1).astype(bf16)
        o_lru = lru_branch(fat, rg_conv_w[i], row(rg_conv_b[i]), wax, row(rg_b_a[i]), row(rg_b_x[i]),
                           row(rg_lambda[i]), B, S)
        o_swa = swa_branch(fat, bias_tab, attn_sinks[i].astype(f32), B, S)
        x1, x1t = merge_ln(o_gdn, o_lru, o_swa, fat, xc, w_o_gdn[i].astype(bf16), w_o_lru[i].astype(bf16),
                           w_o_swa[i].astype(bf16), w_out[i].astype(bf16), row(ln1_g[i]), row(ln1_b[i]), alpha)

        rw_pad = jnp.pad(router_w[i].astype(f32), ((0, 0), (0, LANES - N_EXPERTS)))
        rw_hi = rw_pad.astype(bf16)
        rw_pad = jnp.concatenate([rw_hi, (rw_pad - rw_hi.astype(f32)).astype(bf16)], axis=1)
        rb_pad = _pad_row(router_b[i], fill=NEG_BIG)
        gates, eidx, rank, cnt = route(x1, rw_pad, rb_pad)
        counts = cnt[0, :N_EXPERTS].astype(i32)
        padded = ((counts + EXPERT_BLK - 1) // EXPERT_BLK) * EXPERT_BLK
        pad_ends = jnp.cumsum(padded)
        pad_starts = pad_ends - padded
        hit = eidx[:, :TOP_K, None] == jnp.arange(N_EXPERTS, dtype=i32)
        dest = (jnp.sum(jnp.where(hit, pad_starts, 0), axis=-1) + rank[:, :TOP_K]).reshape(A)
        blk_start = jnp.arange(nblk, dtype=i32) * EXPERT_BLK
        blk_e = jnp.minimum(jnp.sum((pad_ends[None, :] <= blk_start[:, None]).astype(i32), axis=1),
                            N_EXPERTS - 1).astype(i32)
        nused = (pad_ends[-1:] // EXPERT_BLK).astype(i32)

        buf, wgu = dispatch_prep(dest, x1t, buf0, w_gu, i)
        bgu = jnp.transpose(b_gu[i].reshape(N_EXPERTS, -1, LANES, 2), (0, 1, 3, 2)).reshape(N_EXPERTS, 1, -1)
        obuf = experts(blk_e, nused, buf, wgu, bgu, w_down, i, b_down[i][:, None, :])
        xc = combine_ple(dest, gates, x1, p2d, i, obuf, ple_w_gate[i].astype(bf16),
                         ple_w_proj[i].astype(bf16), row(ln2_g[i]), row(ln2_b[i]), row(ln3_g[i]), row(ln3_b[i]),
                         alpha)
    return xc.reshape(B, S, D)
```
